```python
import jax, jax.numpy as jnp
from jax import lax
import numpy as np

D_MODEL = 1024
BATCH = 32
SEQ = 2048
DEPTH = 1

GRID_W = 64
WIN_R = 8
WIN_C = 16
NA_HEADS = 8
NA_HEAD_DIM = 64
NA_WIDTH = NA_HEADS * NA_HEAD_DIM
RET_HEADS = 4
RET_HEAD_DIM = 128
RET_WIDTH = RET_HEADS * RET_HEAD_DIM
RET_CHUNK = 128
ROPE_BASE = 10000.0
D_MIX = NA_WIDTH + RET_WIDTH
D_IN_PROJ = 3 * NA_WIDTH + 4 * RET_WIDTH
N_EXPERTS = 256
TOP_K = 8
N_GROUPS = 8
TOPK_GROUPS = 4
D_EXPERT = 256
D_SHARED = 256
ROUTED_SCALE = 2.5
EXPERT_BLOCK = 128
BLOCKS_PER_STEP = 32
LN_EPS = 1e-5
GN_EPS = 1e-6
DEEPNORM_ALPHA = (2.0 * DEPTH) ** 0.25
DEEPNORM_BETA = (8.0 * DEPTH) ** -0.25

kernel_name = 'hybrid_natten_retnet_moe_block'


def _layer_norm(x, g, b):
    xf = x.astype(jnp.float32)
    mu = jnp.mean(xf, -1, keepdims=True)
    var = jnp.mean(jnp.square(xf - mu), -1, keepdims=True)
    y = (xf - mu) * lax.rsqrt(var + LN_EPS) * g.astype(jnp.float32) + b.astype(jnp.float32)
    return y.astype(x.dtype)


def _rope(t, cos, sin):
    half = t.shape[-1] // 2
    t1, t2 = t[..., :half], t[..., half:]
    return jnp.concatenate([t1 * cos - t2 * sin, t2 * cos + t1 * sin], axis=-1).astype(t.dtype)


def _neighbourhood_attention(q, k, v, rpb, rows):
    b, s, _ = q.shape
    w, h, d = GRID_W, NA_HEADS, NA_HEAD_DIM
    kr = min(WIN_R, rows)
    grid = lambda t: t.reshape(b, rows, w, h, d)
    q_rows = grid(q * d ** -0.5).transpose(1, 0, 3, 2, 4)
    k_grid = grid(k).transpose(0, 3, 1, 2, 4)
    v_grid = grid(v).transpose(0, 3, 1, 2, 4)
    cq = jnp.arange(w)
    cs = jnp.clip(cq - WIN_C // 2, 0, w - WIN_C)
    ck = jnp.arange(w)
    col_in = (ck[None, :] >= cs[:, None]) & (ck[None, :] < cs[:, None] + WIN_C)
    dc_idx = jnp.clip(ck[None, :] - cq[:, None] + WIN_C - 1, 0, 2 * WIN_C - 2)
    rpb_cols = rpb[:, :, dc_idx]

    def row_block(args):
        r, q_row = args
        rs = jnp.clip(r - kr // 2, 0, rows - kr)
        k_blk = lax.dynamic_slice_in_dim(k_grid, rs, kr, axis=2)
        v_blk = lax.dynamic_slice_in_dim(v_grid, rs, kr, axis=2)
        dr_idx = rs + jnp.arange(kr) - r + WIN_R - 1
        bias = rpb_cols[:, dr_idx].transpose(0, 2, 1, 3)
        sc = jnp.einsum('bhqd,bhrkd->bhqrk', q_row, k_blk).astype(jnp.float32)
        sc = sc + bias.astype(jnp.float32)[None]
        sc = jnp.where(col_in[:, None, :], sc, -jnp.inf)
        p = jax.nn.softmax(sc, axis=(-2, -1)).astype(v_blk.dtype)
        return jnp.einsum('bhqrk,bhrkd->bhqd', p, v_blk)

    out = lax.map(row_block, (jnp.arange(rows, dtype=jnp.int32), q_rows))
    return out.transpose(1, 0, 3, 2, 4).reshape(b, s, h * d)


def _retention_chunkwise(q, k, v, log_gamma, strict):
    b, h, s, dh = q.shape
    n = s // RET_CHUNK
    qc = q.reshape(b, h, n, RET_CHUNK, dh)
    kc = k.reshape(b, h, n, RET_CHUNK, dh)
    vc = v.reshape(b, h, n, RET_CHUNK, dh)
    lg = log_gamma.astype(jnp.float32)[:, None]
    i = jnp.arange(RET_CHUNK, dtype=jnp.float32)
    diff = i[:, None] - i[None, :]
    mask = (diff > 0) if strict else (diff >= 0)
    decay_in = jnp.where(mask, jnp.exp(jnp.maximum(diff, 0.0) * lg[:, :, None]), 0.0)
    sc = jnp.einsum('bhncd,bhnmd->bhncm', qc, kc) * decay_in[None, :, None]
    intra = jnp.einsum('bhncm,bhnme->bhnce', sc, vc)
    k_decay = jnp.exp((RET_CHUNK - 1 - i)[None, :] * lg)
    q_decay = jnp.exp((i + 1)[None, :] * lg)
    kv = jnp.einsum('bhncd,bhnce->nbhde', kc * k_decay[None, :, None, :, None], vc)
    chunk_decay = jnp.exp(RET_CHUNK * lg)[None, :, :, None]

    def step(state, kv_n):
        return chunk_decay * state + kv_n, state

    _, r_prev = lax.scan(step, jnp.zeros_like(kv[0]), kv)
    cross = jnp.einsum('bhncd,nbhde->bhnce', qc * q_decay[None, :, None, :, None], r_prev)
    return (intra + cross).reshape(b, h, s, dh)


def _bidirectional_retention(q, k, v, g, log_decay, gn_g, cos, sin):
    b, s, _ = q.shape
    h, dh = RET_HEADS, RET_HEAD_DIM
    qh = _rope(q.reshape(b, s, h, dh), cos, sin).transpose(0, 2, 1, 3)
    kh = (_rope(k.reshape(b, s, h, dh), cos, sin) * dh ** -0.5).transpose(0, 2, 1, 3)
    vh = v.reshape(b, s, h, dh).transpose(0, 2, 1, 3)
    y_fwd = _retention_chunkwise(qh, kh, vh, log_decay[0], False)
    y_bwd = jnp.flip(_retention_chunkwise(jnp.flip(qh, 2), jnp.flip(kh, 2), jnp.flip(vh, 2),
                                          log_decay[1], True), 2)
    y = (y_fwd + y_bwd).astype(jnp.float32)
    mu = jnp.mean(y, -1, keepdims=True)
    var = jnp.mean(jnp.square(y - mu), -1, keepdims=True)
    yn = ((y - mu) * lax.rsqrt(var + GN_EPS)).transpose(0, 2, 1, 3).reshape(b, s, h * dh)
    yn = yn * gn_g.astype(jnp.float32)
    return (jax.nn.silu(g.astype(jnp.float32)) * yn).astype(g.dtype)


def _token_mixer(h, w_in, w_out, rpb, log_decay, gn_g, cos, sin, rows):
    proj = jnp.einsum('bsd,de->bse', h, w_in)
    cuts = [NA_WIDTH, 2 * NA_WIDTH, 3 * NA_WIDTH, 3 * NA_WIDTH + RET_WIDTH,
            3 * NA_WIDTH + 2 * RET_WIDTH, 3 * NA_WIDTH + 3 * RET_WIDTH]
    q_na, k_na, v_na, q_r, k_r, v_r, g_r = jnp.split(proj, cuts, axis=-1)
    y_na = _neighbourhood_attention(q_na, k_na, v_na, rpb, rows)
    y_r = _bidirectional_retention(q_r, k_r, v_r, g_r, log_decay, gn_g, cos, sin)
    return jnp.einsum('bse,ed->bsd', jnp.concatenate([y_na, y_r], axis=-1), w_out)


def _route(xf, w_router, router_bias):
    n = xf.shape[0]
    scores = jax.nn.sigmoid(jnp.einsum('nd,de->ne', xf, w_router).astype(jnp.float32))
    sel = scores + router_bias.astype(jnp.float32)[None]
    grp = sel.reshape(n, N_GROUPS, N_EXPERTS // N_GROUPS)
    group_score = lax.top_k(grp, 2)[0].sum(-1)
    _, gidx = lax.top_k(group_score, TOPK_GROUPS)
    gmask = jax.nn.one_hot(gidx, N_GROUPS, dtype=jnp.float32).sum(1) > 0
    masked = jnp.where(gmask[:, :, None], grp, -jnp.inf).reshape(n, N_EXPERTS)
    _, topk_idx = lax.top_k(masked, TOP_K)
    w = jnp.take_along_axis(scores, topk_idx, axis=1)
    w = w / jnp.sum(w, -1, keepdims=True) * ROUTED_SCALE
    return topk_idx.astype(jnp.int32), w


def _routed_experts(xf, topk_idx, topk_w, w_gate, w_up, w_down):
    n, d = xf.shape
    a = n * TOP_K
    flat_e = topk_idx.reshape(a)
    flat_w = topk_w.reshape(a)
    flat_tok = jnp.arange(a, dtype=jnp.int32) // TOP_K
    order = jnp.argsort(flat_e)
    e_sorted = flat_e[order]
    counts = jnp.bincount(flat_e, length=N_EXPERTS).astype(jnp.int32)
    starts = jnp.cumsum(counts) - counts
    padded = (counts + EXPERT_BLOCK - 1) // EXPERT_BLOCK * EXPERT_BLOCK
    pends = jnp.cumsum(padded)
    pstarts = pends - padded
    dest = pstarts[e_sorted] + (jnp.arange(a, dtype=jnp.int32) - starts[e_sorted])
    n_blocks = -(-(a + N_EXPERTS * (EXPERT_BLOCK - 1)) // EXPERT_BLOCK)
    n_blocks = -(-n_blocks // BLOCKS_PER_STEP) * BLOCKS_PER_STEP
    p = n_blocks * EXPERT_BLOCK
    slot_tok = jnp.full((p,), n, jnp.int32).at[dest].set(flat_tok[order])
    slot_w = jnp.zeros((p,), flat_w.dtype).at[dest].set(flat_w[order])
    block_e = jnp.searchsorted(pends, jnp.arange(n_blocks, dtype=jnp.int32) * EXPERT_BLOCK, side='right')
    block_e = jnp.minimum(block_e, N_EXPERTS - 1).astype(jnp.int32)
    n_steps = n_blocks // BLOCKS_PER_STEP
    slot_tok = slot_tok.reshape(n_steps, BLOCKS_PER_STEP, EXPERT_BLOCK)
    slot_w = slot_w.reshape(n_steps, BLOCKS_PER_STEP, EXPERT_BLOCK)
    block_e = block_e.reshape(n_steps, BLOCKS_PER_STEP)
    x_pad = jnp.concatenate([xf, jnp.zeros((1, d), xf.dtype)], axis=0)

    def step(acc, inp):
        tok, wt, e = inp
        xb = x_pad[tok]
        hg = jnp.einsum('sbd,sdf->sbf', xb, w_gate[e])
        hu = jnp.einsum('sbd,sdf->sbf', xb, w_up[e])
        y = jnp.einsum('sbf,sfd->sbd', jax.nn.silu(hg) * hu, w_down[e]) * wt[..., None]
        acc = acc.at[tok.reshape(-1)].add(y.reshape(-1, d).astype(acc.dtype))
        return acc, None

    acc, _ = lax.scan(step, jnp.zeros((n + 1, d), xf.dtype), (slot_tok, slot_w, block_e))
    return acc[:n]


def _moe_ffn(xf, w_router, router_bias, w_gate, w_up, w_down, ws_gate, ws_up, ws_down):
    shared = jnp.einsum('nf,fd->nd', jax.nn.silu(xf @ ws_gate) * (xf @ ws_up), ws_down)
    topk_idx, topk_w = _route(xf, w_router, router_bias)
    return shared + _routed_experts(xf, topk_idx, topk_w, w_gate, w_up, w_down)


def setup_inputs(seed: int = 0) -> dict:
    key = jax.random.key(seed)
    ks = jax.random.split(key, 24)
    L, D, E = DEPTH, D_MODEL, N_EXPERTS
    nrm = lambda k, shape, scale: jax.random.normal(k, shape, jnp.float32) * scale
    beta = DEEPNORM_BETA
    col_scale = np.concatenate([np.ones(2 * NA_WIDTH), np.full(NA_WIDTH, beta), np.ones(2 * RET_WIDTH),
                                np.full(RET_WIDTH, beta), np.ones(RET_WIDTH)]).astype(np.float32)
    base_decay = np.log(1.0 - 2.0 ** (-5.0 - np.arange(RET_HEADS))).astype(np.float32)
    ret_log_decay = jnp.asarray(base_decay)[None, None, :] * jnp.exp(nrm(ks[7], (L, 2, RET_HEADS), 0.1))
    return {
        'x': nrm(ks[0], (BATCH, SEQ, D), 1.0),
        'c': nrm(ks[1], (BATCH, D), 1.0),
        'w_ada': nrm(ks[2], (L, D, 6 * D), D ** -0.5),
        'b_ada': nrm(ks[3], (L, 6 * D), 0.02),
        'w_in': nrm(ks[4], (L, D, D_IN_PROJ), D ** -0.5) * jnp.asarray(col_scale),
        'w_out': nrm(ks[5], (L, D_MIX, D), beta * D_MIX ** -0.5),
        'na_rpb': nrm(ks[6], (L, NA_HEADS, 2 * WIN_R - 1, 2 * WIN_C - 1), 0.1),
        'ret_log_decay': ret_log_decay,
        'ret_gn_g': 1.0 + nrm(ks[8], (L, RET_WIDTH), 0.02),
        'ln1_g': 1.0 + nrm(ks[9], (L, D), 0.02),
        'ln1_b': nrm(ks[10], (L, D), 0.02),
        'ln2_g': 1.0 + nrm(ks[11], (L, D), 0.02),
        'ln2_b': nrm(ks[12], (L, D), 0.02),
        'w_router': nrm(ks[13], (L, D, E), D ** -0.5),
        'router_bias': nrm(ks[14], (L, E), 0.01),
        'w_gate': nrm(ks[15], (L, E, D, D_EXPERT), beta * D ** -0.5),
        'w_up': nrm(ks[16], (L, E, D, D_EXPERT), beta * D ** -0.5),
        'w_down': nrm(ks[17], (L, E, D_EXPERT, D), beta * D_EXPERT ** -0.5),
        'ws_gate': nrm(ks[18], (L, D, D_SHARED), beta * D ** -0.5),
        'ws_up': nrm(ks[19], (L, D, D_SHARED), beta * D ** -0.5),
        'ws_down': nrm(ks[20], (L, D_SHARED, D), beta * D_SHARED ** -0.5),
    }


def reference(x, c, w_ada, b_ada, w_in, w_out, na_rpb, ret_log_decay, ret_gn_g, ln1_g, ln1_b,
              ln2_g, ln2_b, w_router, router_bias, w_gate, w_up, w_down, ws_gate, ws_up, ws_down):
    b, s, d = x.shape
    rows = s // GRID_W
    t = jnp.arange(s, dtype=jnp.float32)
    inv_freq = ROPE_BASE ** (-jnp.arange(0, RET_HEAD_DIM, 2, dtype=jnp.float32) / RET_HEAD_DIM)
    ang = t[:, None] * inv_freq[None, :]
    cos, sin = jnp.cos(ang)[:, None, :], jnp.sin(ang)[:, None, :]
    cond = jax.nn.silu(c)
    for l in range(DEPTH):
        mod = jnp.einsum('bd,de->be', cond, w_ada[l]) + b_ada[l]
        shift_a, scale_a, gate_a, shift_f, scale_f, gate_f = [m[:, None, :] for m in jnp.split(mod, 6, axis=-1)]
        h = x * (1.0 + scale_a) + shift_a
        mix = _token_mixer(h, w_in[l], w_out[l], na_rpb[l], ret_log_decay[l], ret_gn_g[l], cos, sin, rows)
        x = _layer_norm(DEEPNORM_ALPHA * x + gate_a * mix, ln1_g[l], ln1_b[l])
        hf = (x * (1.0 + scale_f) + shift_f).reshape(b * s, d)
        ffn = _moe_ffn(hf, w_router[l], router_bias[l], w_gate[l], w_up[l], w_down[l],
                       ws_gate[l], ws_up[l], ws_down[l]).reshape(b, s, d)
        x = _layer_norm(DEEPNORM_ALPHA * x + gate_f * ffn, ln2_g[l], ln2_b[l])
    return x
```

```python
import functools

import jax
import jax.numpy as jnp
import numpy as np
from jax import lax
from jax.experimental import pallas as pl
from jax.experimental.pallas import tpu as pltpu

F32 = jnp.float32
BF16 = jnp.bfloat16
U32 = jnp.uint32
I32 = jnp.int32

GRID_W = 64
WIN_R = 8
WIN_C = 16
NA_HEADS = 8
NA_HEAD_DIM = 64
NA_WIDTH = NA_HEADS * NA_HEAD_DIM
RET_HEADS = 4
RET_HEAD_DIM = 128
RET_WIDTH = RET_HEADS * RET_HEAD_DIM
RET_CHUNK = 128
ROPE_BASE = 10000.0
N_EXPERTS = 256
TOP_K = 8
N_GROUPS = 8
TOPK_GROUPS = 4
GROUP_SIZE = N_EXPERTS // N_GROUPS
ROUTED_SCALE = 2.5
LN_EPS = 1e-5
GN_EPS = 1e-6

LANES = 128
VMEM_LIMIT = 56 * 1024 * 1024

TM_PROJ = 512
T_ROUTE = 512
T_DISP = 256
BLK_E = 256
T_COMB = 128
PACK_W = 4


def _cparams(sem, vmem=VMEM_LIMIT):
    return pltpu.CompilerParams(dimension_semantics=sem, vmem_limit_bytes=vmem)


def _silu(v):
    return v * jax.nn.sigmoid(v)


def _layer_norm(z, g, b):
    mu = jnp.mean(z, -1, keepdims=True)
    zc = z - mu
    var = jnp.mean(zc * zc, -1, keepdims=True)
    return zc * lax.rsqrt(var + LN_EPS) * g + b


def _pack_rows(v):
    half = v.shape[1] // 2
    vb = v.astype(BF16)
    lo = lax.bitcast_convert_type(vb[:, :half].astype(F32), U32) >> 16
    hi = lax.bitcast_convert_type(vb[:, half:].astype(F32), U32)
    w = hi | lo
    return [w[:, j * LANES:(j + 1) * LANES] for j in range(half // LANES)]


def _unpack_words(words):
    lo = [lax.bitcast_convert_type(w << 16, F32) for w in words]
    hi = [lax.bitcast_convert_type(w & jnp.uint32(0xFFFF0000), F32) for w in words]
    return jnp.concatenate(lo + hi, axis=-1)


def _mod_kernel(c_ref, w_ref, b_ref, o_ref):
    cond = _silu(c_ref[...])
    o_ref[0] = jnp.dot(cond, w_ref[...], precision=lax.Precision.HIGHEST,
                       preferred_element_type=F32) + b_ref[0]


def _mod(c, w_ada, b_ada):
    b, d = c.shape
    n6 = w_ada.shape[1] // d
    out = pl.pallas_call(
        _mod_kernel,
        grid=(n6,),
        in_specs=[pl.BlockSpec((b, d), lambda j: (0, 0)),
                  pl.BlockSpec((d, d), lambda j: (0, j)),
                  pl.BlockSpec((1, 1, d), lambda j: (j, 0, 0))],
        out_specs=pl.BlockSpec((1, b, d), lambda j: (j, 0, 0)),
        out_shape=jax.ShapeDtypeStruct((n6, b, d), F32),
        compiler_params=_cparams(("arbitrary",)),
        name="mod",
    )(c, w_ada, b_ada.reshape(n6, 1, d))
    return out.reshape(n6, b, 1, d)


def _mod_spec(which, d):
    return pl.BlockSpec((None, None, 1, d), lambda b, i, which=which: (which, b, 0, 0))


def _inproj_kernel(x_ref, sc_ref, sh_ref, w_ref, o_ref, *, chunk, q_scale):
    h = (x_ref[...] * (1.0 + sc_ref[...]) + sh_ref[...]).astype(BF16)
    for j in range(o_ref.shape[1] // chunk):
        acc = jnp.dot(h, w_ref[:, j * chunk:(j + 1) * chunk], preferred_element_type=F32)
        if j == 0:
            acc = acc * q_scale
        o_ref[:, j * chunk:(j + 1) * chunk] = acc.astype(o_ref.dtype)


def _inproj(x, mod, w_in_bf):
    b, s, d = x.shape
    e = w_in_bf.shape[1]
    tm = min(TM_PROJ, s)
    return pl.pallas_call(
        functools.partial(_inproj_kernel, chunk=NA_WIDTH, q_scale=NA_HEAD_DIM ** -0.5),
        grid=(b, s // tm),
        in_specs=[pl.BlockSpec((None, tm, d), lambda bi, i: (bi, i, 0)),
                  _mod_spec(1, d), _mod_spec(0, d),
                  pl.BlockSpec((d, e), lambda bi, i: (0, 0))],
        out_specs=pl.BlockSpec((None, tm, e), lambda bi, i: (bi, i, 0)),
        out_shape=jax.ShapeDtypeStruct((b, s, e), BF16),
        compiler_params=_cparams(("parallel", "parallel")),
        name="inproj",
    )(x, mod, mod, w_in_bf)


def _natten_kernel(q_ref, k_ref, v_ref, bias_ref, o_ref, *, rows):
    kspan = WIN_R * GRID_W

    def row_body(r, carry):
        rs = jnp.clip(r - WIN_R // 2, 0, rows - WIN_R)
        vi = r - rs
        q = q_ref[pl.ds(pl.multiple_of(r * GRID_W, GRID_W), GRID_W), :]
        k = k_ref[pl.ds(pl.multiple_of(rs * GRID_W, GRID_W), kspan), :]
        v = v_ref[pl.ds(pl.multiple_of(rs * GRID_W, GRID_W), kspan), :]
        outs = []
        for j in range(LANES // NA_HEAD_DIM):
            sl = slice(j * NA_HEAD_DIM, (j + 1) * NA_HEAD_DIM)
            s = lax.dot_general(q[:, sl], k[:, sl], (((1,), (1,)), ((), ())),
                                preferred_element_type=F32)
            s = s + bias_ref[j, vi]
            m = jnp.max(s, axis=-1, keepdims=True)
            p = jnp.exp(s - m)
            l = jnp.sum(p, axis=-1, keepdims=True)
            o = jnp.dot(p.astype(BF16), v[:, sl], preferred_element_type=F32)
            outs.append(o / l)
        o_ref[pl.ds(pl.multiple_of(r * GRID_W, GRID_W), GRID_W), :] = (
            jnp.concatenate(outs, axis=-1).astype(o_ref.dtype))
        return carry

    lax.fori_loop(0, rows, row_body, 0)


def _na_bias_table(rpb):
    w = GRID_W
    cq = jnp.arange(w)
    cs = jnp.clip(cq - WIN_C // 2, 0, w - WIN_C)
    ck = jnp.arange(w)
    col_in = (ck[None, :] >= cs[:, None]) & (ck[None, :] < cs[:, None] + WIN_C)
    dc_idx = jnp.clip(ck[None, :] - cq[:, None] + WIN_C - 1, 0, 2 * WIN_C - 2)
    t = rpb[:, :, dc_idx]
    t = jnp.where(col_in[None, None], t, -jnp.inf)
    vi = jnp.arange(WIN_R)
    kr = jnp.arange(WIN_R)
    dr = kr[None, :] - vi[:, None] + WIN_R - 1
    tb = t[:, dr]
    return tb.transpose(0, 1, 3, 2, 4).reshape(rpb.shape[0], WIN_R, w, WIN_R * w).astype(F32)


def _natten(proj, bias_tab):
    b, s, _ = proj.shape
    rows = s // GRID_W
    hp = LANES // NA_HEAD_DIM
    npair = NA_HEADS // hp
    blk = lambda off: pl.BlockSpec((None, s, LANES), lambda bi, p, off=off: (bi, 0, off + p))
    return pl.pallas_call(
        functools.partial(_natten_kernel, rows=rows),
        grid=(b, npair),
        in_specs=[blk(0), blk(npair), blk(2 * npair),
                  pl.BlockSpec((hp, WIN_R, GRID_W, WIN_R * GRID_W), lambda bi, p: (p, 0, 0, 0))],
        out_specs=pl.BlockSpec((None, s, LANES), lambda bi, p: (bi, 0, p)),
        out_shape=jax.ShapeDtypeStruct((b, s, NA_WIDTH), BF16),
        compiler_params=_cparams(("parallel", "parallel")),
        name="natten",
    )(proj, proj, proj, bias_tab)


def _retent_kernel(ld_ref, q_ref, k_ref, v_ref, g_ref, cos_ref, sin_ref, gn_ref, o_ref,
                   qs_ref, ks_ref, sb_ref, *, nchunk):
    c = RET_CHUNK
    dh = RET_HEAD_DIM
    h = pl.program_id(1)
    lgf = ld_ref[0, h]
    lgb = ld_ref[1, h]

    cos2 = cos_ref[...]
    sin2 = sin_ref[...]
    qf = q_ref[...].astype(F32)
    qs_ref[...] = qf * cos2 + pltpu.roll(qf, dh // 2, 1) * sin2
    kf = k_ref[...].astype(F32)
    ks_ref[...] = (kf * cos2 + pltpu.roll(kf, dh // 2, 1) * sin2) * (dh ** -0.5)

    ic = lax.broadcasted_iota(I32, (c, 1), 0).astype(F32)
    ir = lax.broadcasted_iota(I32, (1, c), 1).astype(F32)
    diff = ic - ir
    dmat = jnp.where(diff >= 0, jnp.exp(jnp.maximum(diff, 0.0) * lgf),
                     jnp.exp(jnp.maximum(-diff, 0.0) * lgb))
    kdec_f = jnp.exp((c - 1 - ic) * lgf)
    qdec_f = jnp.exp((ic + 1) * lgf)
    kdec_b = jnp.exp(ic * lgb)
    qdec_b = jnp.exp((c - ic) * lgb)
    one = jnp.ones((1, 1), F32)
    cdec_f = jnp.exp(one * (c * lgf))
    cdec_b = jnp.exp(one * (c * lgb))
    tn = (((0,), (0,)), ((), ()))

    def bwd_body(i, sb):
        n = nchunk - 1 - i
        sb_ref[n] = sb
        rows = pl.ds(pl.multiple_of(n * c, c), c)
        kd = (ks_ref[rows, :] * kdec_b).astype(BF16)
        kv = lax.dot_general(kd, v_ref[rows, :], tn, preferred_element_type=F32)
        return cdec_b * sb + kv

    lax.fori_loop(0, nchunk, bwd_body, jnp.zeros((dh, dh), F32))

    gn = gn_ref[...]

    def fwd_body(n, sf):
        rows = pl.ds(pl.multiple_of(n * c, c), c)
        qn = qs_ref[rows, :]
        kn = ks_ref[rows, :]
        vn = v_ref[rows, :]
        sc = lax.dot_general(qn.astype(BF16), kn.astype(BF16), (((1,), (1,)), ((), ())),
                             preferred_element_type=F32) * dmat
        y = jnp.dot(sc.astype(BF16), vn, preferred_element_type=F32)
        y = y + jnp.dot((qn * qdec_f).astype(BF16), sf.astype(BF16), preferred_element_type=F32)
        y = y + jnp.dot((qn * qdec_b).astype(BF16), sb_ref[n].astype(BF16), preferred_element_type=F32)
        mu = jnp.mean(y, -1, keepdims=True)
        yc = y - mu
        var = jnp.mean(yc * yc, -1, keepdims=True)
        yn = yc * lax.rsqrt(var + GN_EPS) * gn
        o_ref[rows, :] = (_silu(g_ref[rows, :].astype(F32)) * yn).astype(o_ref.dtype)
        kv = lax.dot_general((kn * kdec_f).astype(BF16), vn, tn, preferred_element_type=F32)
        return cdec_f * sf + kv

    lax.fori_loop(0, nchunk, fwd_body, jnp.zeros((dh, dh), F32))


def _retention(proj, log_decay, gn_g, cos2, sin2):
    b, s, _ = proj.shape
    dh = RET_HEAD_DIM
    nchunk = s // RET_CHUNK
    base = 3 * NA_WIDTH // dh
    blk = lambda off: pl.BlockSpec((None, s, dh), lambda bi, h, off=off: (bi, 0, base + off + h))
    full = pl.BlockSpec((s, dh), lambda bi, h: (0, 0))
    return pl.pallas_call(
        functools.partial(_retent_kernel, nchunk=nchunk),
        grid=(b, RET_HEADS),
        in_specs=[pl.BlockSpec(memory_space=pltpu.SMEM),
                  blk(0), blk(RET_HEADS), blk(2 * RET_HEADS), blk(3 * RET_HEADS),
                  full, full,
                  pl.BlockSpec((1, dh), lambda bi, h: (0, h))],
        out_specs=pl.BlockSpec((None, s, dh), lambda bi, h: (bi, 0, h)),
        out_shape=jax.ShapeDtypeStruct((b, s, RET_WIDTH), BF16),
        scratch_shapes=[pltpu.VMEM((s, dh), F32), pltpu.VMEM((s, dh), F32),
                        pltpu.VMEM((nchunk, dh, dh), F32)],
        compiler_params=_cparams(("parallel", "parallel")),
        name="retent",
    )(log_decay, proj, proj, proj, proj, cos2, sin2, gn_g.reshape(1, RET_WIDTH))


def _outproj_kernel(yna_ref, yr_ref, x_ref, ga_ref, sf_ref, shf_ref, wo1_ref, wo2_ref, g_ref, b_ref,
                    wrh_ref, wrl_ref, x1_ref, hfp_ref, lg_ref, *, alpha):
    mix = jnp.dot(yna_ref[...], wo1_ref[...], preferred_element_type=F32)
    mix = mix + jnp.dot(yr_ref[...], wo2_ref[...], preferred_element_type=F32)
    x1 = _layer_norm(alpha * x_ref[...] + ga_ref[...] * mix, g_ref[...], b_ref[...])
    x1_ref[...] = x1
    hf = x1 * (1.0 + sf_ref[...]) + shf_ref[...]
    for j, w in enumerate(_pack_rows(hf)):
        hfp_ref[:, j, :] = w
    hb = hf.astype(BF16)
    hl = (hf - hb.astype(F32)).astype(BF16)
    nt = (((1,), (1,)), ((), ()))
    lg = lax.dot_general(wrh_ref[...], hb, nt, preferred_element_type=F32)
    lg = lg + lax.dot_general(wrh_ref[...], hl, nt, preferred_element_type=F32)
    lg = lg + lax.dot_general(wrl_ref[...], hb, nt, preferred_element_type=F32)
    lg_ref[...] = lg


def _outproj(y_na, y_r, x, mod, w_out_bf, ln_g, ln_b, wr_hi, wr_lo, alpha):
    b, s, d = x.shape
    tm = min(TM_PROJ, s)
    nt = s // tm
    ne = wr_hi.shape[0]
    const = lambda shape: pl.BlockSpec(shape, lambda bi, i: tuple(0 for _ in shape))
    x1, hfp, lg = pl.pallas_call(
        functools.partial(_outproj_kernel, alpha=alpha),
        grid=(b, nt),
        in_specs=[pl.BlockSpec((None, tm, NA_WIDTH), lambda bi, i: (bi, i, 0)),
                  pl.BlockSpec((None, tm, RET_WIDTH), lambda bi, i: (bi, i, 0)),
                  pl.BlockSpec((None, tm, d), lambda bi, i: (bi, i, 0)),
                  _mod_spec(2, d), _mod_spec(4, d), _mod_spec(3, d),
                  pl.BlockSpec((NA_WIDTH, d), lambda bi, i: (0, 0)),
                  pl.BlockSpec((RET_WIDTH, d), lambda bi, i: (1, 0)),
                  const((1, d)), const((1, d)), const((ne, d)), const((ne, d))],
        out_specs=[pl.BlockSpec((None, tm, d), lambda bi, i: (bi, i, 0)),
                   pl.BlockSpec((tm, PACK_W, LANES), lambda bi, i: (bi * nt + i, 0, 0)),
                   pl.BlockSpec((ne, tm), lambda bi, i: (0, bi * nt + i))],
        out_shape=[jax.ShapeDtypeStruct((b, s, d), F32),
                   jax.ShapeDtypeStruct((b * s, PACK_W, LANES), U32),
                   jax.ShapeDtypeStruct((ne, b * s), F32)],
        compiler_params=_cparams(("parallel", "parallel")),
        name="outproj",
    )(y_na, y_r, x, mod, mod, mod, w_out_bf, w_out_bf, ln_g.reshape(1, d), ln_b.reshape(1, d), wr_hi, wr_lo)
    return x1, hfp, lg


def _route_kernel(lg_ref, rb_ref, idx_ref, w_ref, rank_ref, cnt_ref):
    t = lg_ref.shape[1]
    ninf = -jnp.inf

    @pl.when(pl.program_id(0) == 0)
    def _():
        cnt_ref[...] = jnp.zeros_like(cnt_ref)

    scores = jax.nn.sigmoid(lg_ref[...])
    sel = scores + rb_ref[...]

    io_g = lax.broadcasted_iota(I32, (GROUP_SIZE, t), 0)
    gs_rows = []
    for g in range(N_GROUPS):
        blk = sel[g * GROUP_SIZE:(g + 1) * GROUP_SIZE, :]
        m1 = jnp.max(blk, axis=0, keepdims=True)
        i1 = jnp.min(jnp.where(blk == m1, io_g, GROUP_SIZE), axis=0, keepdims=True)
        m2 = jnp.max(jnp.where(io_g == i1, ninf, blk), axis=0, keepdims=True)
        gs_rows.append(m1 + m2)
    gs = jnp.concatenate(gs_rows, axis=0)

    io8 = lax.broadcasted_iota(I32, (N_GROUPS, t), 0)
    gsel = jnp.zeros((N_GROUPS, t), F32)
    for _ in range(TOPK_GROUPS):
        m = jnp.max(gs, axis=0, keepdims=True)
        gi = jnp.min(jnp.where(gs == m, io8, N_GROUPS), axis=0, keepdims=True)
        hit = io8 == gi
        gsel = jnp.where(hit, 1.0, gsel)
        gs = jnp.where(hit, ninf, gs)

    masked = jnp.concatenate(
        [jnp.where(gsel[g:g + 1, :] > 0.0, sel[g * GROUP_SIZE:(g + 1) * GROUP_SIZE, :], ninf)
         for g in range(N_GROUPS)], axis=0)

    io_e = lax.broadcasted_iota(I32, (N_EXPERTS, t), 0)
    chosen = jnp.zeros((N_EXPERTS, t), F32)
    idx_rows, w_rows = [], []
    for _ in range(TOP_K):
        m = jnp.max(masked, axis=0, keepdims=True)
        ei = jnp.min(jnp.where(masked == m, io_e, N_EXPERTS), axis=0, keepdims=True)
        hit = io_e == ei
        w_rows.append(jnp.sum(jnp.where(hit, scores, 0.0), axis=0, keepdims=True))
        idx_rows.append(ei)
        chosen = jnp.where(hit, 1.0, chosen)
        masked = jnp.where(hit, ninf, masked)
    wk = jnp.concatenate(w_rows, axis=0)
    w_ref[...] = wk / jnp.sum(wk, axis=0, keepdims=True) * ROUTED_SCALE
    idx_ref[...] = jnp.concatenate(idx_rows, axis=0)

    upper = (lax.broadcasted_iota(I32, (t, t), 0) < lax.broadcasted_iota(I32, (t, t), 1))
    prefix = jnp.dot(chosen.astype(BF16), upper.astype(BF16), preferred_element_type=F32)
    rank_full = prefix + cnt_ref[...]
    rank_rows = [jnp.sum(jnp.where(io_e == ei, rank_full, 0.0), axis=0, keepdims=True) for ei in idx_rows]
    rank_ref[...] = jnp.concatenate(rank_rows, axis=0).astype(I32)
    cnt_ref[...] += jnp.sum(chosen, axis=1, keepdims=True)


def _route(logits_t, router_bias):
    ne, n = logits_t.shape
    t = min(T_ROUTE, n)
    kspec = pl.BlockSpec((TOP_K, t), lambda i: (0, i))
    return pl.pallas_call(
        _route_kernel,
        grid=(n // t,),
        in_specs=[pl.BlockSpec((ne, t), lambda i: (0, i)),
                  pl.BlockSpec((ne, 1), lambda i: (0, 0))],
        out_specs=[kspec, kspec, kspec, pl.BlockSpec((ne, 1), lambda i: (0, 0))],
        out_shape=[jax.ShapeDtypeStruct((TOP_K, n), I32),
                   jax.ShapeDtypeStruct((TOP_K, n), F32),
                   jax.ShapeDtypeStruct((TOP_K, n), I32),
                   jax.ShapeDtypeStruct((ne, 1), F32)],
        compiler_params=_cparams(("arbitrary",)),
        name="route",
    )(logits_t, router_bias.reshape(ne, 1))


def _row_copy(src_ref, dst_ref, sem):
    return pltpu.make_async_copy(src_ref, dst_ref, sem)


def _dispatch_kernel(dest_ref, hfp_ref, xs_in_ref, xs_ref, sem):
    del xs_in_ref
    td = hfp_ref.shape[0]

    def issue(t, carry):
        for k in range(TOP_K):
            _row_copy(hfp_ref.at[t], xs_ref.at[dest_ref[k, t]], sem).start()
        return carry

    lax.fori_loop(0, td, issue, 0)

    def drain(t, carry):
        for k in range(TOP_K):
            _row_copy(hfp_ref.at[0], xs_ref.at[0], sem).wait()
        return carry

    lax.fori_loop(0, td, drain, 0)


def _dispatch(dest, hfp, n_slots):
    n = hfp.shape[0]
    td = min(T_DISP, n)
    xs0 = jnp.zeros((n_slots, PACK_W, LANES), U32)
    return pl.pallas_call(
        _dispatch_kernel,
        grid=(n // td,),
        in_specs=[pl.BlockSpec((TOP_K, td), lambda i: (0, i), memory_space=pltpu.SMEM),
                  pl.BlockSpec((td, PACK_W, LANES), lambda i: (i, 0, 0)),
                  pl.BlockSpec(memory_space=pl.ANY)],
        out_specs=pl.BlockSpec(memory_space=pl.ANY),
        out_shape=jax.ShapeDtypeStruct((n_slots, PACK_W, LANES), U32),
        scratch_shapes=[pltpu.SemaphoreType.DMA(())],
        input_output_aliases={2: 0},
        compiler_params=_cparams(("arbitrary",)),
        name="dispatch",
    )(dest, hfp, xs0)


def _experts_kernel(be_ref, nb_ref, xs_ref, wg_ref, wu_ref, wd_ref, ys_ref, wgb_ref, wub_ref, wdb_ref):
    i = pl.program_id(0)

    @pl.when(i < nb_ref[0])
    def _():
        prev = be_ref[jnp.maximum(i - 1, 0)]

        @pl.when((i == 0) | (be_ref[i] != prev))
        def _():
            wgb_ref[...] = wg_ref[...].astype(BF16)
            wub_ref[...] = wu_ref[...].astype(BF16)
            wdb_ref[...] = wd_ref[...].astype(BF16)

        xb = _unpack_words([xs_ref[:, j, :] for j in range(PACK_W)]).astype(BF16)
        hg = jnp.dot(xb, wgb_ref[...], preferred_element_type=F32)
        hu = jnp.dot(xb, wub_ref[...], preferred_element_type=F32)
        act = (_silu(hg) * hu).astype(BF16)
        y = jnp.dot(act, wdb_ref[...], preferred_element_type=F32)
        for j, w in enumerate(_pack_rows(y)):
            ys_ref[:, j, :] = w

    @pl.when(i >= nb_ref[0])
    def _():
        ys_ref[...] = jnp.zeros_like(ys_ref)


def _experts(block_e, n_blocks, xs, w_gate, w_up, w_down):
    p = xs.shape[0]
    nb = p // BLK_E
    _, d, f = w_gate.shape
    row_blk = lambda i, be, nbr: (i, 0, 0)
    grid_spec = pltpu.PrefetchScalarGridSpec(
        num_scalar_prefetch=2,
        grid=(nb,),
        in_specs=[pl.BlockSpec((BLK_E, PACK_W, LANES), row_blk),
                  pl.BlockSpec((None, d, f), lambda i, be, nbr: (be[i], 0, 0)),
                  pl.BlockSpec((None, d, f), lambda i, be, nbr: (be[i], 0, 0)),
                  pl.BlockSpec((None, f, d), lambda i, be, nbr: (be[i], 0, 0))],
        out_specs=pl.BlockSpec((BLK_E, PACK_W, LANES), row_blk),
        scratch_shapes=[pltpu.VMEM((d, f), BF16), pltpu.VMEM((d, f), BF16), pltpu.VMEM((f, d), BF16)],
    )
    return pl.pallas_call(
        _experts_kernel,
        grid_spec=grid_spec,
        out_shape=jax.ShapeDtypeStruct((p, PACK_W, LANES), U32),
        compiler_params=_cparams(("arbitrary",)),
        name="experts",
    )(block_e, n_blocks, xs, w_gate, w_up, w_down)


def _combine_kernel(dest_ref, wt_ref, hfp_ref, x1_ref, gf_ref, wsg_ref, wsu_ref, wsd_ref, g_ref, b_ref,
                    ys_ref, o_ref, buf_ref, sem, *, alpha):
    tc = hfp_ref.shape[0]

    def issue(t, carry):
        for k in range(TOP_K):
            _row_copy(ys_ref.at[dest_ref[k, t]], buf_ref.at[k, t], sem).start()
        return carry

    lax.fori_loop(0, tc, issue, 0)

    hb = _unpack_words([hfp_ref[:, j, :] for j in range(PACK_W)]).astype(BF16)
    sg = jnp.dot(hb, wsg_ref[...], preferred_element_type=F32)
    su = jnp.dot(hb, wsu_ref[...], preferred_element_type=F32)
    ffn = jnp.dot((_silu(sg) * su).astype(BF16), wsd_ref[...], preferred_element_type=F32)

    def drain(t, carry):
        for k in range(TOP_K):
            _row_copy(ys_ref.at[0], buf_ref.at[0, 0], sem).wait()
        return carry

    lax.fori_loop(0, tc, drain, 0)

    wt = wt_ref[...]
    for k in range(TOP_K):
        yk = _unpack_words([buf_ref[k, :, j, :] for j in range(PACK_W)])
        ffn = ffn + wt[:, k:k + 1] * yk
    o_ref[...] = _layer_norm(alpha * x1_ref[...] + gf_ref[...] * ffn, g_ref[...], b_ref[...])


def _combine(dest, w_tok, hfp, x1, mod, ws_gate_bf, ws_up_bf, ws_down_bf, ln_g, ln_b, ys, alpha):
    b, s, d = x1.shape
    n = b * s
    tc = min(T_COMB, s)
    nt = s // tc
    fs = ws_gate_bf.shape[1]
    const = lambda shape: pl.BlockSpec(shape, lambda bi, i: tuple(0 for _ in shape))
    return pl.pallas_call(
        functools.partial(_combine_kernel, alpha=alpha),
        grid=(b, nt),
        in_specs=[pl.BlockSpec((TOP_K, tc), lambda bi, i: (0, bi * nt + i), memory_space=pltpu.SMEM),
                  pl.BlockSpec((tc, TOP_K), lambda bi, i: (bi * nt + i, 0)),
                  pl.BlockSpec((tc, PACK_W, LANES), lambda bi, i: (bi * nt + i, 0, 0)),
                  pl.BlockSpec((None, tc, d), lambda bi, i: (bi, i, 0)),
                  _mod_spec(5, d),
                  const((d, fs)), const((d, fs)), const((fs, d)), const((1, d)), const((1, d)),
                  pl.BlockSpec(memory_space=pl.ANY)],
        out_specs=pl.BlockSpec((None, tc, d), lambda bi, i: (bi, i, 0)),
        out_shape=jax.ShapeDtypeStruct((b, s, d), F32),
        scratch_shapes=[pltpu.VMEM((TOP_K, tc, PACK_W, LANES), U32), pltpu.SemaphoreType.DMA(())],
        compiler_params=_cparams(("arbitrary", "arbitrary")),
        name="combine",
    )(dest, w_tok, hfp, x1, mod, ws_gate_bf, ws_up_bf, ws_down_bf, ln_g.reshape(1, d), ln_b.reshape(1, d), ys)


def _slot_layout(idx, rank, counts, n_assign):
    cnt = counts[:, 0].astype(I32)
    padded = (cnt + BLK_E - 1) // BLK_E * BLK_E
    pends = jnp.cumsum(padded)
    pstarts = pends - padded
    n_blocks_max = (n_assign + N_EXPERTS * (BLK_E - 1)) // BLK_E
    block_e = jnp.searchsorted(pends, jnp.arange(n_blocks_max, dtype=I32) * BLK_E, side='right')
    block_e = jnp.minimum(block_e, N_EXPERTS - 1).astype(I32)
    n_blocks = (pends[-1] // BLK_E).astype(I32).reshape(1)
    dest = pstarts[idx] + rank
    return dest, block_e, n_blocks, n_blocks_max * BLK_E


def kernel(x, c, w_ada, b_ada, w_in, w_out, na_rpb, ret_log_decay, ret_gn_g, ln1_g, ln1_b, ln2_g, ln2_b,
           w_router, router_bias, w_gate, w_up, w_down, ws_gate, ws_up, ws_down):
    b, s, d = x.shape
    depth = w_ada.shape[0]
    alpha = (2.0 * depth) ** 0.25
    t = jnp.arange(s, dtype=F32)
    inv_freq = ROPE_BASE ** (-jnp.arange(0, RET_HEAD_DIM, 2, dtype=F32) / RET_HEAD_DIM)
    ang = t[:, None] * inv_freq[None, :]
    cos, sin = jnp.cos(ang), jnp.sin(ang)
    cos2 = jnp.concatenate([cos, cos], axis=-1)
    sin2 = jnp.concatenate([-sin, sin], axis=-1)
    for l in range(depth):
        mod = _mod(c, w_ada[l], b_ada[l])
        proj = _inproj(x, mod, w_in[l].astype(BF16))
        y_na = _natten(proj, _na_bias_table(na_rpb[l]))
        y_r = _retention(proj, ret_log_decay[l], ret_gn_g[l], cos2, sin2)
        wr_t = w_router[l].T
        wr_hi = wr_t.astype(BF16)
        wr_lo = (wr_t - wr_hi.astype(F32)).astype(BF16)
        x1, hfp, logits_t = _outproj(y_na, y_r, x, mod, w_out[l].astype(BF16), ln1_g[l], ln1_b[l],
                                     wr_hi, wr_lo, alpha)
        idx, wts, rank, counts = _route(logits_t, router_bias[l])
        dest, block_e, n_blocks, n_slots = _slot_layout(idx, rank, counts, b * s * TOP_K)
        xs = _dispatch(dest, hfp, n_slots)
        ys = _experts(block_e, n_blocks, xs, w_gate[l], w_up[l], w_down[l])
        x = _combine(dest, wts.T, hfp, x1, mod, ws_gate[l].astype(BF16), ws_up[l].astype(BF16),
                     ws_down[l].astype(BF16), ln2_g[l], ln2_b[l], ys, alpha)
    return x
```

```python
import functools

import jax
import jax.numpy as jnp
import numpy as np
from jax import lax
from jax.experimental import pallas as pl
from jax.experimental.pallas import tpu as pltpu

F32 = jnp.float32
BF16 = jnp.bfloat16
U32 = jnp.uint32
I32 = jnp.int32

GRID_W = 64
WIN_R = 8
WIN_C = 16
NA_HEADS = 8
NA_HEAD_DIM = 64
NA_WIDTH = NA_HEADS * NA_HEAD_DIM
RET_HEADS = 4
RET_HEAD_DIM = 128
RET_WIDTH = RET_HEADS * RET_HEAD_DIM
RET_CHUNK = 128
ROPE_BASE = 10000.0
N_EXPERTS = 256
TOP_K = 8
N_GROUPS = 8
TOPK_GROUPS = 4
GROUP_SIZE = N_EXPERTS // N_GROUPS
ROUTED_SCALE = 2.5
LN_EPS = 1e-5
GN_EPS = 1e-6

LANES = 128
VMEM_LIMIT = 56 * 1024 * 1024

TM_PROJ = 512
T_ROUTE = 512
T_DISP = 256
BLK_E = 256
T_COMB = 256
PACK_W = 4
NA_ROWS_PER_ITER = 4
RET_UNROLL = 2


def _cparams(sem, vmem=VMEM_LIMIT):
    return pltpu.CompilerParams(dimension_semantics=sem, vmem_limit_bytes=vmem)


def _silu(v):
    return v * jax.nn.sigmoid(v)


def _layer_norm(z, g, b):
    mu = jnp.mean(z, -1, keepdims=True)
    zc = z - mu
    var = jnp.mean(zc * zc, -1, keepdims=True)
    return zc * lax.rsqrt(var + LN_EPS) * g + b


def _pack_rows(v):
    half = v.shape[1] // 2
    vb = v.astype(BF16)
    lo = lax.bitcast_convert_type(vb[:, :half].astype(F32), U32) >> 16
    hi = lax.bitcast_convert_type(vb[:, half:].astype(F32), U32)
    w = hi | lo
    return [w[:, j * LANES:(j + 1) * LANES] for j in range(half // LANES)]


def _unpack_words(words):
    lo = [lax.bitcast_convert_type(w << 16, F32) for w in words]
    hi = [lax.bitcast_convert_type(w & jnp.uint32(0xFFFF0000), F32) for w in words]
    return jnp.concatenate(lo + hi, axis=-1)


def _mod_kernel(c_ref, w_ref, b_ref, o_ref):
    cond = _silu(c_ref[...])
    o_ref[0] = jnp.dot(cond, w_ref[...], precision=lax.Precision.HIGHEST,
                       preferred_element_type=F32) + b_ref[0]


def _mod(c, w_ada, b_ada):
    b, d = c.shape
    n6 = w_ada.shape[1] // d
    out = pl.pallas_call(
        _mod_kernel,
        grid=(n6,),
        in_specs=[pl.BlockSpec((b, d), lambda j: (0, 0)),
                  pl.BlockSpec((d, d), lambda j: (0, j)),
                  pl.BlockSpec((1, 1, d), lambda j: (j, 0, 0))],
        out_specs=pl.BlockSpec((1, b, d), lambda j: (j, 0, 0)),
        out_shape=jax.ShapeDtypeStruct((n6, b, d), F32),
        compiler_params=_cparams(("arbitrary",)),
        name="mod",
    )(c, w_ada, b_ada.reshape(n6, 1, d))
    return out.reshape(n6, b, 1, d)


def _mod_spec(which, d):
    return pl.BlockSpec((None, None, 1, d), lambda b, i, which=which: (which, b, 0, 0))


def _inproj_kernel(x_ref, sc_ref, sh_ref, w_ref, o_ref, *, chunk, q_scale):
    h = (x_ref[...] * (1.0 + sc_ref[...]) + sh_ref[...]).astype(BF16)
    for j in range(o_ref.shape[1] // chunk):
        acc = jnp.dot(h, w_ref[:, j * chunk:(j + 1) * chunk], preferred_element_type=F32)
        if j == 0:
            acc = acc * q_scale
        o_ref[:, j * chunk:(j + 1) * chunk] = acc.astype(o_ref.dtype)


def _inproj(x, mod, w_in_bf):
    b, s, d = x.shape
    e = w_in_bf.shape[1]
    tm = min(TM_PROJ, s)
    return pl.pallas_call(
        functools.partial(_inproj_kernel, chunk=NA_WIDTH, q_scale=NA_HEAD_DIM ** -0.5),
        grid=(b, s // tm),
        in_specs=[pl.BlockSpec((None, tm, d), lambda bi, i: (bi, i, 0)),
                  _mod_spec(1, d), _mod_spec(0, d),
                  pl.BlockSpec((d, e), lambda bi, i: (0, 0))],
        out_specs=pl.BlockSpec((None, tm, e), lambda bi, i: (bi, i, 0)),
        out_shape=jax.ShapeDtypeStruct((b, s, e), BF16),
        compiler_params=_cparams(("parallel", "parallel")),
        name="inproj",
    )(x, mod, mod, w_in_bf)


def _natten_kernel(q_ref, k_ref, v_ref, bias_ref, o_ref, *, rows):
    kspan = WIN_R * GRID_W

    def one_row(r):
        rs = jnp.clip(r - WIN_R // 2, 0, rows - WIN_R)
        vi = r - rs
        q = q_ref[pl.ds(pl.multiple_of(r * GRID_W, GRID_W), GRID_W), :]
        k = k_ref[pl.ds(pl.multiple_of(rs * GRID_W, GRID_W), kspan), :]
        v = v_ref[pl.ds(pl.multiple_of(rs * GRID_W, GRID_W), kspan), :]
        outs = []
        for j in range(LANES // NA_HEAD_DIM):
            sl = slice(j * NA_HEAD_DIM, (j + 1) * NA_HEAD_DIM)
            s = lax.dot_general(q[:, sl], k[:, sl], (((1,), (1,)), ((), ())),
                                preferred_element_type=F32)
            s = s + bias_ref[j, vi]
            m = jnp.max(s, axis=-1, keepdims=True)
            p = jnp.exp(s - m)
            l = jnp.sum(p, axis=-1, keepdims=True)
            o = jnp.dot(p.astype(BF16), v[:, sl], preferred_element_type=F32)
            outs.append(o / l)
        o_ref[pl.ds(pl.multiple_of(r * GRID_W, GRID_W), GRID_W), :] = (
            jnp.concatenate(outs, axis=-1).astype(o_ref.dtype))

    def rows_body(i, carry):
        for u in range(NA_ROWS_PER_ITER):
            one_row(i * NA_ROWS_PER_ITER + u)
        return carry

    lax.fori_loop(0, rows // NA_ROWS_PER_ITER, rows_body, 0)


def _na_bias_table(rpb):
    w = GRID_W
    cq = jnp.arange(w)
    cs = jnp.clip(cq - WIN_C // 2, 0, w - WIN_C)
    ck = jnp.arange(w)
    col_in = (ck[None, :] >= cs[:, None]) & (ck[None, :] < cs[:, None] + WIN_C)
    dc_idx = jnp.clip(ck[None, :] - cq[:, None] + WIN_C - 1, 0, 2 * WIN_C - 2)
    t = rpb[:, :, dc_idx]
    t = jnp.where(col_in[None, None], t, -jnp.inf)
    vi = jnp.arange(WIN_R)
    kr = jnp.arange(WIN_R)
    dr = kr[None, :] - vi[:, None] + WIN_R - 1
    tb = t[:, dr]
    return tb.transpose(0, 1, 3, 2, 4).reshape(rpb.shape[0], WIN_R, w, WIN_R * w).astype(F32)


def _natten(proj, bias_tab):
    b, s, _ = proj.shape
    rows = s // GRID_W
    hp = LANES // NA_HEAD_DIM
    npair = NA_HEADS // hp
    blk = lambda off: pl.BlockSpec((None, s, LANES), lambda bi, p, off=off: (bi, 0, off + p))
    return pl.pallas_call(
        functools.partial(_natten_kernel, rows=rows),
        grid=(b, npair),
        in_specs=[blk(0), blk(npair), blk(2 * npair),
                  pl.BlockSpec((hp, WIN_R, GRID_W, WIN_R * GRID_W), lambda bi, p: (p, 0, 0, 0))],
        out_specs=pl.BlockSpec((None, s, LANES), lambda bi, p: (bi, 0, p)),
        out_shape=jax.ShapeDtypeStruct((b, s, NA_WIDTH), BF16),
        compiler_params=_cparams(("parallel", "parallel")),
        name="natten",
    )(proj, proj, proj, bias_tab)


def _retent_kernel(ld_ref, q_ref, k_ref, v_ref, g_ref, cos_ref, sin_ref, gn_ref, o_ref,
                   qs_ref, ks_ref, sb_ref, *, nchunk):
    c = RET_CHUNK
    dh = RET_HEAD_DIM
    h = pl.program_id(1)
    lgf = ld_ref[0, h]
    lgb = ld_ref[1, h]

    cos2 = cos_ref[...]
    sin2 = sin_ref[...]
    qf = q_ref[...].astype(F32)
    qs_ref[...] = qf * cos2 + pltpu.roll(qf, dh // 2, 1) * sin2
    kf = k_ref[...].astype(F32)
    ks_ref[...] = (kf * cos2 + pltpu.roll(kf, dh // 2, 1) * sin2) * (dh ** -0.5)

    ic = lax.broadcasted_iota(I32, (c, 1), 0).astype(F32)
    ir = lax.broadcasted_iota(I32, (1, c), 1).astype(F32)
    diff = ic - ir
    dmat = jnp.where(diff >= 0, jnp.exp(jnp.maximum(diff, 0.0) * lgf),
                     jnp.exp(jnp.maximum(-diff, 0.0) * lgb))
    kdec_f = jnp.exp((c - 1 - ic) * lgf)
    qdec_f = jnp.exp((ic + 1) * lgf)
    kdec_b = jnp.exp(ic * lgb)
    qdec_b = jnp.exp((c - ic) * lgb)
    one = jnp.ones((1, 1), F32)
    cdec_f = jnp.exp(one * (c * lgf))
    cdec_b = jnp.exp(one * (c * lgb))
    tn = (((0,), (0,)), ((), ()))

    def bwd_body(i, sb):
        n = nchunk - 1 - i
        sb_ref[n] = sb
        rows = pl.ds(pl.multiple_of(n * c, c), c)
        kd = (ks_ref[rows, :] * kdec_b).astype(BF16)
        kv = lax.dot_general(kd, v_ref[rows, :], tn, preferred_element_type=F32)
        return cdec_b * sb + kv

    lax.fori_loop(0, nchunk, bwd_body, jnp.zeros((dh, dh), F32), unroll=RET_UNROLL)

    gn = gn_ref[...]

    def fwd_body(n, sf):
        rows = pl.ds(pl.multiple_of(n * c, c), c)
        qn = qs_ref[rows, :]
        kn = ks_ref[rows, :]
        vn = v_ref[rows, :]
        sc = lax.dot_general(qn.astype(BF16), kn.astype(BF16), (((1,), (1,)), ((), ())),
                             preferred_element_type=F32) * dmat
        y = jnp.dot(sc.astype(BF16), vn, preferred_element_type=F32)
        y = y + jnp.dot((qn * qdec_f).astype(BF16), sf.astype(BF16), preferred_element_type=F32)
        y = y + jnp.dot((qn * qdec_b).astype(BF16), sb_ref[n].astype(BF16), preferred_element_type=F32)
        mu = jnp.mean(y, -1, keepdims=True)
        yc = y - mu
        var = jnp.mean(yc * yc, -1, keepdims=True)
        yn = yc * lax.rsqrt(var + GN_EPS) * gn
        o_ref[rows, :] = (_silu(g_ref[rows, :].astype(F32)) * yn).astype(o_ref.dtype)
        kv = lax.dot_general((kn * kdec_f).astype(BF16), vn, tn, preferred_element_type=F32)
        return cdec_f * sf + kv

    lax.fori_loop(0, nchunk, fwd_body, jnp.zeros((dh, dh), F32), unroll=RET_UNROLL)


def _retention(proj, log_decay, gn_g, cos2, sin2):
    b, s, _ = proj.shape
    dh = RET_HEAD_DIM
    nchunk = s // RET_CHUNK
    base = 3 * NA_WIDTH // dh
    blk = lambda off: pl.BlockSpec((None, s, dh), lambda bi, h, off=off: (bi, 0, base + off + h))
    full = pl.BlockSpec((s, dh), lambda bi, h: (0, 0))
    return pl.pallas_call(
        functools.partial(_retent_kernel, nchunk=nchunk),
        grid=(b, RET_HEADS),
        in_specs=[pl.BlockSpec(memory_space=pltpu.SMEM),
                  blk(0), blk(RET_HEADS), blk(2 * RET_HEADS), blk(3 * RET_HEADS),
                  full, full,
                  pl.BlockSpec((1, dh), lambda bi, h: (0, h))],
        out_specs=pl.BlockSpec((None, s, dh), lambda bi, h: (bi, 0, h)),
        out_shape=jax.ShapeDtypeStruct((b, s, RET_WIDTH), BF16),
        scratch_shapes=[pltpu.VMEM((s, dh), F32), pltpu.VMEM((s, dh), F32),
                        pltpu.VMEM((nchunk, dh, dh), F32)],
        compiler_params=_cparams(("parallel", "parallel")),
        name="retent",
    )(log_decay, proj, proj, proj, proj, cos2, sin2, gn_g.reshape(1, RET_WIDTH))


def _outproj_kernel(yna_ref, yr_ref, x_ref, ga_ref, sf_ref, shf_ref, wo1_ref, wo2_ref, g_ref, b_ref,
                    wrh_ref, wrl_ref, x1_ref, hfp_ref, lg_ref, *, alpha):
    mix = jnp.dot(yna_ref[...], wo1_ref[...], preferred_element_type=F32)
    mix = mix + jnp.dot(yr_ref[...], wo2_ref[...], preferred_element_type=F32)
    x1 = _layer_norm(alpha * x_ref[...] + ga_ref[...] * mix, g_ref[...], b_ref[...])
    x1_ref[...] = x1
    hf = x1 * (1.0 + sf_ref[...]) + shf_ref[...]
    for j, w in enumerate(_pack_rows(hf)):
        hfp_ref[j] = w
    hb = hf.astype(BF16)
    hl = (hf - hb.astype(F32)).astype(BF16)
    nt = (((1,), (1,)), ((), ()))
    lg = lax.dot_general(wrh_ref[...], hb, nt, preferred_element_type=F32)
    lg = lg + lax.dot_general(wrh_ref[...], hl, nt, preferred_element_type=F32)
    lg = lg + lax.dot_general(wrl_ref[...], hb, nt, preferred_element_type=F32)
    lg_ref[...] = lg


def _outproj(y_na, y_r, x, mod, w_out_bf, ln_g, ln_b, wr_hi, wr_lo, alpha):
    b, s, d = x.shape
    tm = min(TM_PROJ, s)
    nt = s // tm
    ne = wr_hi.shape[0]
    const = lambda shape: pl.BlockSpec(shape, lambda bi, i: tuple(0 for _ in shape))
    x1, hfp, lg = pl.pallas_call(
        functools.partial(_outproj_kernel, alpha=alpha),
        grid=(b, nt),
        in_specs=[pl.BlockSpec((None, tm, NA_WIDTH), lambda bi, i: (bi, i, 0)),
                  pl.BlockSpec((None, tm, RET_WIDTH), lambda bi, i: (bi, i, 0)),
                  pl.BlockSpec((None, tm, d), lambda bi, i: (bi, i, 0)),
                  _mod_spec(2, d), _mod_spec(4, d), _mod_spec(3, d),
                  pl.BlockSpec((NA_WIDTH, d), lambda bi, i: (0, 0)),
                  pl.BlockSpec((RET_WIDTH, d), lambda bi, i: (1, 0)),
                  const((1, d)), const((1, d)), const((ne, d)), const((ne, d))],
        out_specs=[pl.BlockSpec((None, tm, d), lambda bi, i: (bi, i, 0)),
                   pl.BlockSpec((PACK_W, tm, LANES), lambda bi, i: (0, bi * nt + i, 0)),
                   pl.BlockSpec((ne, tm), lambda bi, i: (0, bi * nt + i))],
        out_shape=[jax.ShapeDtypeStruct((b, s, d), F32),
                   jax.ShapeDtypeStruct((PACK_W, b * s, LANES), U32),
                   jax.ShapeDtypeStruct((ne, b * s), F32)],
        compiler_params=_cparams(("parallel", "parallel")),
        name="outproj",
    )(y_na, y_r, x, mod, mod, mod, w_out_bf, w_out_bf, ln_g.reshape(1, d), ln_b.reshape(1, d), wr_hi, wr_lo)
    return x1, hfp, lg


def _route_kernel(lg_ref, rb_ref, idx_ref, w_ref, rank_ref, cnt_ref):
    t = lg_ref.shape[1]
    ninf = -jnp.inf

    @pl.when(pl.program_id(0) == 0)
    def _():
        cnt_ref[...] = jnp.zeros_like(cnt_ref)

    scores = jax.nn.sigmoid(lg_ref[...])
    sel = scores + rb_ref[...]

    io_g = lax.broadcasted_iota(I32, (GROUP_SIZE, t), 0)
    gs_rows = []
    for g in range(N_GROUPS):
        blk = sel[g * GROUP_SIZE:(g + 1) * GROUP_SIZE, :]
        m1 = jnp.max(blk, axis=0, keepdims=True)
        i1 = jnp.min(jnp.where(blk == m1, io_g, GROUP_SIZE), axis=0, keepdims=True)
        m2 = jnp.max(jnp.where(io_g == i1, ninf, blk), axis=0, keepdims=True)
        gs_rows.append(m1 + m2)
    gs = jnp.concatenate(gs_rows, axis=0)

    io8 = lax.broadcasted_iota(I32, (N_GROUPS, t), 0)
    gsel = jnp.zeros((N_GROUPS, t), F32)
    for _ in range(TOPK_GROUPS):
        m = jnp.max(gs, axis=0, keepdims=True)
        gi = jnp.min(jnp.where(gs == m, io8, N_GROUPS), axis=0, keepdims=True)
        hit = io8 == gi
        gsel = jnp.where(hit, 1.0, gsel)
        gs = jnp.where(hit, ninf, gs)

    masked = jnp.concatenate(
        [jnp.where(gsel[g:g + 1, :] > 0.0, sel[g * GROUP_SIZE:(g + 1) * GROUP_SIZE, :], ninf)
         for g in range(N_GROUPS)], axis=0)

    io_e = lax.broadcasted_iota(I32, (N_EXPERTS, t), 0)
    chosen = jnp.zeros((N_EXPERTS, t), F32)
    idx_rows, w_rows = [], []
    for _ in range(TOP_K):
        m = jnp.max(masked, axis=0, keepdims=True)
        ei = jnp.min(jnp.where(masked == m, io_e, N_EXPERTS), axis=0, keepdims=True)
        hit = io_e == ei
        w_rows.append(jnp.sum(jnp.where(hit, scores, 0.0), axis=0, keepdims=True))
        idx_rows.append(ei)
        chosen = jnp.where(hit, 1.0, chosen)
        masked = jnp.where(hit, ninf, masked)
    wk = jnp.concatenate(w_rows, axis=0)
    w_ref[...] = wk / jnp.sum(wk, axis=0, keepdims=True) * ROUTED_SCALE
    idx_ref[...] = jnp.concatenate(idx_rows, axis=0)

    upper = (lax.broadcasted_iota(I32, (t, t), 0) < lax.broadcasted_iota(I32, (t, t), 1))
    prefix = jnp.dot(chosen.astype(BF16), upper.astype(BF16), preferred_element_type=F32)
    rank_full = prefix + cnt_ref[...]
    rank_rows = [jnp.sum(jnp.where(io_e == ei, rank_full, 0.0), axis=0, keepdims=True) for ei in idx_rows]
    rank_ref[...] = jnp.concatenate(rank_rows, axis=0).astype(I32)
    cnt_ref[...] += jnp.sum(chosen, axis=1, keepdims=True)


def _route(logits_t, router_bias):
    ne, n = logits_t.shape
    t = min(T_ROUTE, n)
    kspec = pl.BlockSpec((TOP_K, t), lambda i: (0, i))
    return pl.pallas_call(
        _route_kernel,
        grid=(n // t,),
        in_specs=[pl.BlockSpec((ne, t), lambda i: (0, i)),
                  pl.BlockSpec((ne, 1), lambda i: (0, 0))],
        out_specs=[kspec, kspec, kspec, pl.BlockSpec((ne, 1), lambda i: (0, 0))],
        out_shape=[jax.ShapeDtypeStruct((TOP_K, n), I32),
                   jax.ShapeDtypeStruct((TOP_K, n), F32),
                   jax.ShapeDtypeStruct((TOP_K, n), I32),
                   jax.ShapeDtypeStruct((ne, 1), F32)],
        compiler_params=_cparams(("arbitrary",)),
        name="route",
    )(logits_t, router_bias.reshape(ne, 1))


def _row(ref, r):
    return ref.at[:, pl.ds(r, 1), :]


def _rows_wait(ref, n_rows, sem):
    span = ref.at[:, pl.ds(0, n_rows), :]
    pltpu.make_async_copy(span, span, sem).wait()


def _dispatch_kernel(idx_ref, rank_ref, pstart_ref, hfp_ref, xs_in_ref, xs_ref, sem):
    del xs_in_ref
    td = hfp_ref.shape[1]

    def issue(t, carry):
        for k in range(TOP_K):
            d = pstart_ref[idx_ref[k, t]] + rank_ref[k, t]
            pltpu.make_async_copy(_row(hfp_ref, t), _row(xs_ref, d), sem).start()
        return carry

    lax.fori_loop(0, td, issue, 0, unroll=2)
    _rows_wait(xs_ref, TOP_K * td, sem)


def _dispatch(idx, rank, pstarts, hfp, n_slots):
    n = hfp.shape[1]
    td = min(T_DISP, n)
    xs0 = jnp.zeros((PACK_W, n_slots, LANES), U32)
    kspec = pl.BlockSpec((TOP_K, td), lambda i: (0, i), memory_space=pltpu.SMEM)
    return pl.pallas_call(
        _dispatch_kernel,
        grid=(n // td,),
        in_specs=[kspec, kspec, pl.BlockSpec(memory_space=pltpu.SMEM),
                  pl.BlockSpec((PACK_W, td, LANES), lambda i: (0, i, 0)),
                  pl.BlockSpec(memory_space=pl.ANY)],
        out_specs=pl.BlockSpec(memory_space=pl.ANY),
        out_shape=jax.ShapeDtypeStruct((PACK_W, n_slots, LANES), U32),
        scratch_shapes=[pltpu.SemaphoreType.DMA(())],
        input_output_aliases={4: 0},
        compiler_params=_cparams(("arbitrary",)),
        name="dispatch",
    )(idx, rank, pstarts, hfp, xs0)


def _experts_kernel(be_ref, nb_ref, xs_ref, wg_ref, wu_ref, wd_ref, ys_ref, wgb_ref, wub_ref, wdb_ref):
    i = pl.program_id(0)

    @pl.when(i < nb_ref[0])
    def _():
        prev = be_ref[jnp.maximum(i - 1, 0)]

        @pl.when((i == 0) | (be_ref[i] != prev))
        def _():
            wgb_ref[...] = wg_ref[...].astype(BF16)
            wub_ref[...] = wu_ref[...].astype(BF16)
            wdb_ref[...] = wd_ref[...].astype(BF16)

        xb = _unpack_words([xs_ref[j] for j in range(PACK_W)]).astype(BF16)
        hg = jnp.dot(xb, wgb_ref[...], preferred_element_type=F32)
        hu = jnp.dot(xb, wub_ref[...], preferred_element_type=F32)
        act = (_silu(hg) * hu).astype(BF16)
        y = jnp.dot(act, wdb_ref[...], preferred_element_type=F32)
        for j, w in enumerate(_pack_rows(y)):
            ys_ref[j] = w

    @pl.when(i >= nb_ref[0])
    def _():
        ys_ref[...] = jnp.zeros_like(ys_ref)


def _experts(block_e, n_blocks, xs, w_gate, w_up, w_down):
    p = xs.shape[1]
    nb = p // BLK_E
    _, d, f = w_gate.shape
    row_blk = lambda i, be, nbr: (0, i, 0)
    grid_spec = pltpu.PrefetchScalarGridSpec(
        num_scalar_prefetch=2,
        grid=(nb,),
        in_specs=[pl.BlockSpec((PACK_W, BLK_E, LANES), row_blk),
                  pl.BlockSpec((None, d, f), lambda i, be, nbr: (be[i], 0, 0)),
                  pl.BlockSpec((None, d, f), lambda i, be, nbr: (be[i], 0, 0)),
                  pl.BlockSpec((None, f, d), lambda i, be, nbr: (be[i], 0, 0))],
        out_specs=pl.BlockSpec((PACK_W, BLK_E, LANES), row_blk),
        scratch_shapes=[pltpu.VMEM((d, f), BF16), pltpu.VMEM((d, f), BF16), pltpu.VMEM((f, d), BF16)],
    )
    return pl.pallas_call(
        _experts_kernel,
        grid_spec=grid_spec,
        out_shape=jax.ShapeDtypeStruct((PACK_W, p, LANES), U32),
        compiler_params=_cparams(("arbitrary",)),
        name="experts",
    )(block_e, n_blocks, xs, w_gate, w_up, w_down)


def _combine_kernel(idx_ref, rank_ref, idxn_ref, rankn_ref, pstart_ref, wt_ref, hfp_ref, x1_ref, gf_ref,
                    wsg_ref, wsu_ref, wsd_ref, g_ref, b_ref, ys_ref, o_ref, buf_ref, sems, *, alpha):
    tc = hfp_ref.shape[1]
    i = pl.program_id(0)
    slot = i % 2

    def gather(iref, rref, s):
        def issue(t, carry):
            for k in range(TOP_K):
                d = pstart_ref[iref[k, t]] + rref[k, t]
                pltpu.make_async_copy(_row(ys_ref, d), buf_ref.at[s, k, :, pl.ds(t, 1), :], sems.at[s]).start()
            return carry

        lax.fori_loop(0, tc, issue, 0, unroll=2)

    @pl.when(i == 0)
    def _():
        gather(idx_ref, rank_ref, 0)

    @pl.when(i + 1 < pl.num_programs(0))
    def _():
        gather(idxn_ref, rankn_ref, 1 - slot)

    hb = _unpack_words([hfp_ref[j] for j in range(PACK_W)]).astype(BF16)
    sg = jnp.dot(hb, wsg_ref[...], preferred_element_type=F32)
    su = jnp.dot(hb, wsu_ref[...], preferred_element_type=F32)
    ffn = jnp.dot((_silu(sg) * su).astype(BF16), wsd_ref[...], preferred_element_type=F32)

    cur = buf_ref.at[slot]
    pltpu.make_async_copy(cur, cur, sems.at[slot]).wait()

    wt = wt_ref[...]
    for k in range(TOP_K):
        yk = _unpack_words([buf_ref[slot, k, j] for j in range(PACK_W)])
        ffn = ffn + wt[:, k:k + 1] * yk
    o_ref[...] = _layer_norm(alpha * x1_ref[...] + gf_ref[...] * ffn, g_ref[...], b_ref[...])


def _combine(idx, rank, pstarts, w_tok, hfp, x1, mod, ws_gate_bf, ws_up_bf, ws_down_bf, ln_g, ln_b, ys, alpha):
    b, s, d = x1.shape
    n = b * s
    tc = min(T_COMB, s)
    nt = s // tc
    ntile = n // tc
    fs = ws_gate_bf.shape[1]
    const = lambda shape: pl.BlockSpec(shape, lambda i: tuple(0 for _ in shape))
    kcur = pl.BlockSpec((TOP_K, tc), lambda i: (0, i), memory_space=pltpu.SMEM)
    knext = pl.BlockSpec((TOP_K, tc), lambda i: (0, jnp.minimum(i + 1, ntile - 1)), memory_space=pltpu.SMEM)
    out = pl.pallas_call(
        functools.partial(_combine_kernel, alpha=alpha),
        grid=(ntile,),
        in_specs=[kcur, kcur, knext, knext, pl.BlockSpec(memory_space=pltpu.SMEM),
                  pl.BlockSpec((tc, TOP_K), lambda i: (i, 0)),
                  pl.BlockSpec((PACK_W, tc, LANES), lambda i: (0, i, 0)),
                  pl.BlockSpec((tc, d), lambda i: (i, 0)),
                  pl.BlockSpec((None, None, 1, d), lambda i: (5, i // nt, 0, 0)),
                  const((d, fs)), const((d, fs)), const((fs, d)), const((1, d)), const((1, d)),
                  pl.BlockSpec(memory_space=pl.ANY)],
        out_specs=pl.BlockSpec((tc, d), lambda i: (i, 0)),
        out_shape=jax.ShapeDtypeStruct((n, d), F32),
        scratch_shapes=[pltpu.VMEM((2, TOP_K, PACK_W, tc, LANES), U32), pltpu.SemaphoreType.DMA((2,))],
        compiler_params=_cparams(("arbitrary",)),
        name="combine",
    )(idx, rank, idx, rank, pstarts, w_tok, hfp, x1.reshape(n, d), mod, ws_gate_bf, ws_up_bf, ws_down_bf,
      ln_g.reshape(1, d), ln_b.reshape(1, d), ys)
    return out.reshape(b, s, d)


def _slot_layout(counts, n_assign):
    cnt = counts[:, 0].astype(I32)
    padded = (cnt + BLK_E - 1) // BLK_E * BLK_E
    pends = jnp.cumsum(padded)
    pstarts = pends - padded
    n_blocks_max = (n_assign + N_EXPERTS * (BLK_E - 1)) // BLK_E
    first_row = jnp.arange(n_blocks_max, dtype=I32) * BLK_E
    block_e = jnp.sum((pends[None, :] <= first_row[:, None]).astype(I32), axis=1)
    block_e = jnp.minimum(block_e, N_EXPERTS - 1).astype(I32)
    n_blocks = (pends[-1] // BLK_E).astype(I32).reshape(1)
    return pstarts, block_e, n_blocks, n_blocks_max * BLK_E


def kernel(x, c, w_ada, b_ada, w_in, w_out, na_rpb, ret_log_decay, ret_gn_g, ln1_g, ln1_b, ln2_g, ln2_b,
           w_router, router_bias, w_gate, w_up, w_down, ws_gate, ws_up, ws_down):
    b, s, d = x.shape
    depth = w_ada.shape[0]
    alpha = (2.0 * depth) ** 0.25
    t = jnp.arange(s, dtype=F32)
    inv_freq = ROPE_BASE ** (-jnp.arange(0, RET_HEAD_DIM, 2, dtype=F32) / RET_HEAD_DIM)
    ang = t[:, None] * inv_freq[None, :]
    cos, sin = jnp.cos(ang), jnp.sin(ang)
    cos2 = jnp.concatenate([cos, cos], axis=-1)
    sin2 = jnp.concatenate([-sin, sin], axis=-1)
    for l in range(depth):
        mod = _mod(c, w_ada[l], b_ada[l])
        proj = _inproj(x, mod, w_in[l].astype(BF16))
        y_na = _natten(proj, _na_bias_table(na_rpb[l]))
        y_r = _retention(proj, ret_log_decay[l], ret_gn_g[l], cos2, sin2)
        wr_t = w_router[l].T
        wr_hi = wr_t.astype(BF16)
        wr_lo = (wr_t - wr_hi.astype(F32)).astype(BF16)
        x1, hfp, logits_t = _outproj(y_na, y_r, x, mod, w_out[l].astype(BF16), ln1_g[l], ln1_b[l],
                                     wr_hi, wr_lo, alpha)
        idx, wts, rank, counts = _route(logits_t, router_bias[l])
        pstarts, block_e, n_blocks, n_slots = _slot_layout(counts, b * s * TOP_K)
        xs = _dispatch(idx, rank, pstarts, hfp, n_slots)
        ys = _experts(block_e, n_blocks, xs, w_gate[l], w_up[l], w_down[l])
        x = _combine(idx, rank, pstarts, wts.T, hfp, x1, mod, ws_gate[l].astype(BF16), ws_up[l].astype(BF16),
                     ws_down[l].astype(BF16), ln2_g[l], ln2_b[l], ys, alpha)
    return x
```

```python
import functools

import jax
import jax.numpy as jnp
import numpy as np
from jax import lax
from jax.experimental import pallas as pl
from jax.experimental.pallas import tpu as pltpu

F32 = jnp.float32
BF16 = jnp.bfloat16
U32 = jnp.uint32
I32 = jnp.int32

GRID_W = 64
WIN_R = 8
WIN_C = 16
NA_HEADS = 8
NA_HEAD_DIM = 64
NA_WIDTH = NA_HEADS * NA_HEAD_DIM
RET_HEADS = 4
RET_HEAD_DIM = 128
RET_WIDTH = RET_HEADS * RET_HEAD_DIM
RET_CHUNK = 128
ROPE_BASE = 10000.0
N_EXPERTS = 256
TOP_K = 8
N_GROUPS = 8
TOPK_GROUPS = 4
GROUP_SIZE = N_EXPERTS // N_GROUPS
ROUTED_SCALE = 2.5
LN_EPS = 1e-5
GN_EPS = 1e-6

LANES = 128
VMEM_LIMIT = 56 * 1024 * 1024

TM_PROJ = 512
T_ROUTE = 512
T_DISP = 256
BLK_E = 256
T_COMB = 256
PACK_W = 4
NA_ROWS_PER_ITER = 8
RET_UNROLL = 2


def _cparams(sem, vmem=VMEM_LIMIT):
    return pltpu.CompilerParams(dimension_semantics=sem, vmem_limit_bytes=vmem)


def _silu(v):
    return v * jax.nn.sigmoid(v)


def _layer_norm(z, g, b):
    mu = jnp.mean(z, -1, keepdims=True)
    zc = z - mu
    var = jnp.mean(zc * zc, -1, keepdims=True)
    return zc * lax.rsqrt(var + LN_EPS) * g + b


def _pack_rows(v):
    half = v.shape[1] // 2
    vb = v.astype(BF16)
    lo = lax.bitcast_convert_type(vb[:, :half].astype(F32), U32) >> 16
    hi = lax.bitcast_convert_type(vb[:, half:].astype(F32), U32)
    w = hi | lo
    return [w[:, j * LANES:(j + 1) * LANES] for j in range(half // LANES)]


def _unpack_words(words):
    lo = [lax.bitcast_convert_type(w << 16, F32) for w in words]
    hi = [lax.bitcast_convert_type(w & jnp.uint32(0xFFFF0000), F32) for w in words]
    return jnp.concatenate(lo + hi, axis=-1)


def _mod_kernel(c_ref, w_ref, b_ref, o_ref):
    cond = _silu(c_ref[...])
    o_ref[0] = jnp.dot(cond, w_ref[...], precision=lax.Precision.HIGHEST,
                       preferred_element_type=F32) + b_ref[0]


def _mod(c, w_ada, b_ada):
    b, d = c.shape
    n6 = w_ada.shape[1] // d
    out = pl.pallas_call(
        _mod_kernel,
        grid=(n6,),
        in_specs=[pl.BlockSpec((b, d), lambda j: (0, 0)),
                  pl.BlockSpec((d, d), lambda j: (0, j)),
                  pl.BlockSpec((1, 1, d), lambda j: (j, 0, 0))],
        out_specs=pl.BlockSpec((1, b, d), lambda j: (j, 0, 0)),
        out_shape=jax.ShapeDtypeStruct((n6, b, d), F32),
        compiler_params=_cparams(("arbitrary",)),
        name="mod",
    )(c, w_ada, b_ada.reshape(n6, 1, d))
    return out.reshape(n6, b, 1, d)


def _mod_spec(which, d):
    return pl.BlockSpec((None, None, 1, d), lambda b, i, which=which: (which, b, 0, 0))


def _inproj_kernel(x_ref, sc_ref, sh_ref, w_ref, o_ref, *, chunk, q_scale):
    h = (x_ref[...] * (1.0 + sc_ref[...]) + sh_ref[...]).astype(BF16)
    for j in range(o_ref.shape[1] // chunk):
        acc = jnp.dot(h, w_ref[:, j * chunk:(j + 1) * chunk], preferred_element_type=F32)
        if j == 0:
            acc = acc * q_scale
        o_ref[:, j * chunk:(j + 1) * chunk] = acc.astype(o_ref.dtype)


def _inproj(x, mod, w_in_bf):
    b, s, d = x.shape
    e = w_in_bf.shape[1]
    tm = min(TM_PROJ, s)
    return pl.pallas_call(
        functools.partial(_inproj_kernel, chunk=NA_WIDTH, q_scale=NA_HEAD_DIM ** -0.5),
        grid=(b, s // tm),
        in_specs=[pl.BlockSpec((None, tm, d), lambda bi, i: (bi, i, 0)),
                  _mod_spec(1, d), _mod_spec(0, d),
                  pl.BlockSpec((d, e), lambda bi, i: (0, 0))],
        out_specs=pl.BlockSpec((None, tm, e), lambda bi, i: (bi, i, 0)),
        out_shape=jax.ShapeDtypeStruct((b, s, e), BF16),
        compiler_params=_cparams(("parallel", "parallel")),
        name="inproj",
    )(x, mod, mod, w_in_bf)


def _natten_kernel(q_ref, k_ref, v_ref, bias_ref, o_ref, *, rows):
    kspan = WIN_R * GRID_W

    first = lax.broadcasted_iota(I32, (1, LANES), 1) < NA_HEAD_DIM
    zero = jnp.zeros((), BF16)
    one = jnp.ones((), BF16)

    def rows_body(i, carry):
        qrows, krows, scores = [], [], []
        for u in range(NA_ROWS_PER_ITER):
            r = i * NA_ROWS_PER_ITER + u
            rs = jnp.clip(r - WIN_R // 2, 0, rows - WIN_R)
            vi = r - rs
            qrows.append(pl.ds(pl.multiple_of(r * GRID_W, GRID_W), GRID_W))
            krows.append(pl.ds(pl.multiple_of(rs * GRID_W, GRID_W), kspan))
            q = q_ref[qrows[u], :]
            k = k_ref[krows[u], :]
            for j, qh in enumerate((jnp.where(first, q, zero), jnp.where(first, zero, q))):
                s = lax.dot_general(qh, k, (((1,), (1,)), ((), ())), preferred_element_type=F32)
                scores.append(s + bias_ref[j, vi])
        probs = []
        for s in scores:
            m = jnp.max(s, axis=-1, keepdims=True)
            probs.append(jnp.exp(s - m).astype(BF16))
        for u in range(NA_ROWS_PER_ITER):
            v = v_ref[krows[u], :]
            a0 = jnp.dot(probs[2 * u], jnp.where(first, v, one), preferred_element_type=F32)
            a1 = jnp.dot(probs[2 * u + 1], jnp.where(first, one, v), preferred_element_type=F32)
            num = jnp.where(first, a0, a1)
            den = pltpu.roll(jnp.where(first, a1, a0), NA_HEAD_DIM, 1)
            o_ref[qrows[u], :] = (num / den).astype(o_ref.dtype)
        return carry

    lax.fori_loop(0, rows // NA_ROWS_PER_ITER, rows_body, 0)


def _na_bias_table(rpb):
    w = GRID_W
    cq = jnp.arange(w)
    cs = jnp.clip(cq - WIN_C // 2, 0, w - WIN_C)
    ck = jnp.arange(w)
    col_in = (ck[None, :] >= cs[:, None]) & (ck[None, :] < cs[:, None] + WIN_C)
    dc_idx = jnp.clip(ck[None, :] - cq[:, None] + WIN_C - 1, 0, 2 * WIN_C - 2)
    t = rpb[:, :, dc_idx]
    t = jnp.where(col_in[None, None], t, -jnp.inf)
    vi = jnp.arange(WIN_R)
    kr = jnp.arange(WIN_R)
    dr = kr[None, :] - vi[:, None] + WIN_R - 1
    tb = t[:, dr]
    return tb.transpose(0, 1, 3, 2, 4).reshape(rpb.shape[0], WIN_R, w, WIN_R * w).astype(F32)


def _natten(proj, bias_tab):
    b, s, _ = proj.shape
    rows = s // GRID_W
    hp = LANES // NA_HEAD_DIM
    npair = NA_HEADS // hp
    blk = lambda off: pl.BlockSpec((None, s, LANES), lambda bi, p, off=off: (bi, 0, off + p))
    return pl.pallas_call(
        functools.partial(_natten_kernel, rows=rows),
        grid=(b, npair),
        in_specs=[blk(0), blk(npair), blk(2 * npair),
                  pl.BlockSpec((hp, WIN_R, GRID_W, WIN_R * GRID_W), lambda bi, p: (p, 0, 0, 0))],
        out_specs=pl.BlockSpec((None, s, LANES), lambda bi, p: (bi, 0, p)),
        out_shape=jax.ShapeDtypeStruct((b, s, NA_WIDTH), BF16),
        compiler_params=_cparams(("parallel", "parallel")),
        name="natten",
    )(proj, proj, proj, bias_tab)


def _retent_kernel(ld_ref, q_ref, k_ref, v_ref, g_ref, cos_ref, sin_ref, gn_ref, o_ref,
                   qs_ref, ks_ref, sb_ref, *, nchunk):
    c = RET_CHUNK
    dh = RET_HEAD_DIM
    h = pl.program_id(1)
    lgf = ld_ref[0, h]
    lgb = ld_ref[1, h]

    cos2 = cos_ref[...]
    sin2 = sin_ref[...]
    qf = q_ref[...].astype(F32)
    qs_ref[...] = qf * cos2 + pltpu.roll(qf, dh // 2, 1) * sin2
    kf = k_ref[...].astype(F32)
    ks_ref[...] = (kf * cos2 + pltpu.roll(kf, dh // 2, 1) * sin2) * (dh ** -0.5)

    ic = lax.broadcasted_iota(I32, (c, 1), 0).astype(F32)
    ir = lax.broadcasted_iota(I32, (1, c), 1).astype(F32)
    diff = ic - ir
    dmat = jnp.where(diff >= 0, jnp.exp(jnp.maximum(diff, 0.0) * lgf),
                     jnp.exp(jnp.maximum(-diff, 0.0) * lgb))
    kdec_f = jnp.exp((c - 1 - ic) * lgf)
    qdec_f = jnp.exp((ic + 1) * lgf)
    kdec_b = jnp.exp(ic * lgb)
    qdec_b = jnp.exp((c - ic) * lgb)
    one = jnp.ones((1, 1), F32)
    cdec_f = jnp.exp(one * (c * lgf))
    cdec_b = jnp.exp(one * (c * lgb))
    tn = (((0,), (0,)), ((), ()))

    def bwd_body(i, sb):
        n = nchunk - 1 - i
        sb_ref[n] = sb
        rows = pl.ds(pl.multiple_of(n * c, c), c)
        kd = (ks_ref[rows, :] * kdec_b).astype(BF16)
        kv = lax.dot_general(kd, v_ref[rows, :], tn, preferred_element_type=F32)
        return cdec_b * sb + kv

    lax.fori_loop(0, nchunk, bwd_body, jnp.zeros((dh, dh), F32), unroll=RET_UNROLL)

    gn = gn_ref[...]

    def fwd_body(n, sf):
        rows = pl.ds(pl.multiple_of(n * c, c), c)
        qn = qs_ref[rows, :]
        kn = ks_ref[rows, :]
        vn = v_ref[rows, :]
        sc = lax.dot_general(qn.astype(BF16), kn.astype(BF16), (((1,), (1,)), ((), ())),
                             preferred_element_type=F32) * dmat
        y = jnp.dot(sc.astype(BF16), vn, preferred_element_type=F32)
        y = y + jnp.dot((qn * qdec_f).astype(BF16), sf.astype(BF16), preferred_element_type=F32)
        y = y + jnp.dot((qn * qdec_b).astype(BF16), sb_ref[n].astype(BF16), preferred_element_type=F32)
        mu = jnp.mean(y, -1, keepdims=True)
        yc = y - mu
        var = jnp.mean(yc * yc, -1, keepdims=True)
        yn = yc * lax.rsqrt(var + GN_EPS) * gn
        o_ref[rows, :] = (_silu(g_ref[rows, :].astype(F32)) * yn).astype(o_ref.dtype)
        kv = lax.dot_general((kn * kdec_f).astype(BF16), vn, tn, preferred_element_type=F32)
        return cdec_f * sf + kv

    lax.fori_loop(0, nchunk, fwd_body, jnp.zeros((dh, dh), F32), unroll=RET_UNROLL)


def _retention(proj, log_decay, gn_g, cos2, sin2):
    b, s, _ = proj.shape
    dh = RET_HEAD_DIM
    nchunk = s // RET_CHUNK
    base = 3 * NA_WIDTH // dh
    blk = lambda off: pl.BlockSpec((None, s, dh), lambda bi, h, off=off: (bi, 0, base + off + h))
    full = pl.BlockSpec((s, dh), lambda bi, h: (0, 0))
    return pl.pallas_call(
        functools.partial(_retent_kernel, nchunk=nchunk),
        grid=(b, RET_HEADS),
        in_specs=[pl.BlockSpec(memory_space=pltpu.SMEM),
                  blk(0), blk(RET_HEADS), blk(2 * RET_HEADS), blk(3 * RET_HEADS),
                  full, full,
                  pl.BlockSpec((1, dh), lambda bi, h: (0, h))],
        out_specs=pl.BlockSpec((None, s, dh), lambda bi, h: (bi, 0, h)),
        out_shape=jax.ShapeDtypeStruct((b, s, RET_WIDTH), BF16),
        scratch_shapes=[pltpu.VMEM((s, dh), F32), pltpu.VMEM((s, dh), F32),
                        pltpu.VMEM((nchunk, dh, dh), F32)],
        compiler_params=_cparams(("parallel", "parallel")),
        name="retent",
    )(log_decay, proj, proj, proj, proj, cos2, sin2, gn_g.reshape(1, RET_WIDTH))


def _outproj_kernel(yna_ref, yr_ref, x_ref, ga_ref, sf_ref, shf_ref, wo1_ref, wo2_ref, g_ref, b_ref,
                    wrh_ref, wrl_ref, x1_ref, hfp_ref, lg_ref, *, alpha):
    mix = jnp.dot(yna_ref[...], wo1_ref[...], preferred_element_type=F32)
    mix = mix + jnp.dot(yr_ref[...], wo2_ref[...], preferred_element_type=F32)
    x1 = _layer_norm(alpha * x_ref[...] + ga_ref[...] * mix, g_ref[...], b_ref[...])
    x1_ref[...] = x1
    hf = x1 * (1.0 + sf_ref[...]) + shf_ref[...]
    for j, w in enumerate(_pack_rows(hf)):
        hfp_ref[j] = w
    hb = hf.astype(BF16)
    hl = (hf - hb.astype(F32)).astype(BF16)
    nt = (((1,), (1,)), ((), ()))
    lg = lax.dot_general(wrh_ref[...], hb, nt, preferred_element_type=F32)
    lg = lg + lax.dot_general(wrh_ref[...], hl, nt, preferred_element_type=F32)
    lg = lg + lax.dot_general(wrl_ref[...], hb, nt, preferred_element_type=F32)
    lg_ref[...] = lg


def _outproj(y_na, y_r, x, mod, w_out_bf, ln_g, ln_b, wr_hi, wr_lo, alpha):
    b, s, d = x.shape
    tm = min(TM_PROJ, s)
    nt = s // tm
    ne = wr_hi.shape[0]
    const = lambda shape: pl.BlockSpec(shape, lambda bi, i: tuple(0 for _ in shape))
    x1, hfp, lg = pl.pallas_call(
        functools.partial(_outproj_kernel, alpha=alpha),
        grid=(b, nt),
        in_specs=[pl.BlockSpec((None, tm, NA_WIDTH), lambda bi, i: (bi, i, 0)),
                  pl.BlockSpec((None, tm, RET_WIDTH), lambda bi, i: (bi, i, 0)),
                  pl.BlockSpec((None, tm, d), lambda bi, i: (bi, i, 0)),
                  _mod_spec(2, d), _mod_spec(4, d), _mod_spec(3, d),
                  pl.BlockSpec((NA_WIDTH, d), lambda bi, i: (0, 0)),
                  pl.BlockSpec((RET_WIDTH, d), lambda bi, i: (1, 0)),
                  const((1, d)), const((1, d)), const((ne, d)), const((ne, d))],
        out_specs=[pl.BlockSpec((None, tm, d), lambda bi, i: (bi, i, 0)),
                   pl.BlockSpec((PACK_W, tm, LANES), lambda bi, i: (0, bi * nt + i, 0)),
                   pl.BlockSpec((ne, tm), lambda bi, i: (0, bi * nt + i))],
        out_shape=[jax.ShapeDtypeStruct((b, s, d), F32),
                   jax.ShapeDtypeStruct((PACK_W, b * s, LANES), U32),
                   jax.ShapeDtypeStruct((ne, b * s), F32)],
        compiler_params=_cparams(("parallel", "parallel")),
        name="outproj",
    )(y_na, y_r, x, mod, mod, mod, w_out_bf, w_out_bf, ln_g.reshape(1, d), ln_b.reshape(1, d), wr_hi, wr_lo)
    return x1, hfp, lg


def _route_kernel(lg_ref, rb_ref, idx_ref, w_ref, rank_ref, cnt_ref):
    t = lg_ref.shape[1]
    ninf = -jnp.inf

    @pl.when(pl.program_id(0) == 0)
    def _():
        cnt_ref[...] = jnp.zeros_like(cnt_ref)

    scores = jax.nn.sigmoid(lg_ref[...])
    sel = scores + rb_ref[...]

    io_g = lax.broadcasted_iota(I32, (GROUP_SIZE, t), 0)
    gs_rows = []
    for g in range(N_GROUPS):
        blk = sel[g * GROUP_SIZE:(g + 1) * GROUP_SIZE, :]
        m1 = jnp.max(blk, axis=0, keepdims=True)
        i1 = jnp.min(jnp.where(blk == m1, io_g, GROUP_SIZE), axis=0, keepdims=True)
        m2 = jnp.max(jnp.where(io_g == i1, ninf, blk), axis=0, keepdims=True)
        gs_rows.append(m1 + m2)
    gs = jnp.concatenate(gs_rows, axis=0)

    io8 = lax.broadcasted_iota(I32, (N_GROUPS, t), 0)
    gsel = jnp.zeros((N_GROUPS, t), F32)
    for _ in range(TOPK_GROUPS):
        m = jnp.max(gs, axis=0, keepdims=True)
        gi = jnp.min(jnp.where(gs == m, io8, N_GROUPS), axis=0, keepdims=True)
        hit = io8 == gi
        gsel = jnp.where(hit, 1.0, gsel)
        gs = jnp.where(hit, ninf, gs)

    masked = jnp.concatenate(
        [jnp.where(gsel[g:g + 1, :] > 0.0, sel[g * GROUP_SIZE:(g + 1) * GROUP_SIZE, :], ninf)
         for g in range(N_GROUPS)], axis=0)

    io_e = lax.broadcasted_iota(I32, (N_EXPERTS, t), 0)
    chosen = jnp.zeros((N_EXPERTS, t), F32)
    idx_rows, w_rows = [], []
    for _ in range(TOP_K):
        m = jnp.max(masked, axis=0, keepdims=True)
        ei = jnp.min(jnp.where(masked == m, io_e, N_EXPERTS), axis=0, keepdims=True)
        hit = io_e == ei
        w_rows.append(jnp.sum(jnp.where(hit, scores, 0.0), axis=0, keepdims=True))
        idx_rows.append(ei)
        chosen = jnp.where(hit, 1.0, chosen)
        masked = jnp.where(hit, ninf, masked)
    wk = jnp.concatenate(w_rows, axis=0)
    w_ref[...] = wk / jnp.sum(wk, axis=0, keepdims=True) * ROUTED_SCALE
    idx_ref[...] = jnp.concatenate(idx_rows, axis=0)

    upper = (lax.broadcasted_iota(I32, (t, t), 0) < lax.broadcasted_iota(I32, (t, t), 1))
    prefix = jnp.dot(chosen.astype(BF16), upper.astype(BF16), preferred_element_type=F32)
    rank_full = prefix + cnt_ref[...]
    rank_rows = [jnp.sum(jnp.where(io_e == ei, rank_full, 0.0), axis=0, keepdims=True) for ei in idx_rows]
    rank_ref[...] = jnp.concatenate(rank_rows, axis=0).astype(I32)
    cnt_ref[...] += jnp.sum(chosen, axis=1, keepdims=True)


def _route(logits_t, router_bias):
    ne, n = logits_t.shape
    t = min(T_ROUTE, n)
    kspec = pl.BlockSpec((TOP_K, t), lambda i: (0, i))
    return pl.pallas_call(
        _route_kernel,
        grid=(n // t,),
        in_specs=[pl.BlockSpec((ne, t), lambda i: (0, i)),
                  pl.BlockSpec((ne, 1), lambda i: (0, 0))],
        out_specs=[kspec, kspec, kspec, pl.BlockSpec((ne, 1), lambda i: (0, 0))],
        out_shape=[jax.ShapeDtypeStruct((TOP_K, n), I32),
                   jax.ShapeDtypeStruct((TOP_K, n), F32),
                   jax.ShapeDtypeStruct((TOP_K, n), I32),
                   jax.ShapeDtypeStruct((ne, 1), F32)],
        compiler_params=_cparams(("arbitrary",)),
        name="route",
    )(logits_t, router_bias.reshape(ne, 1))


def _row(ref, r):
    return ref.at[:, pl.ds(r, 1), :]


def _rows_wait(ref, n_rows, sem):
    span = ref.at[:, pl.ds(0, n_rows), :]
    pltpu.make_async_copy(span, span, sem).wait()


def _block(ref, g):
    return ref.at[:, pl.ds(pl.multiple_of(g * BLK_E, BLK_E), BLK_E), :]


def _dispatch_kernel(idx_ref, rank_ref, pstart_ref, pend_ref, nb_ref, hfp_ref, xs_ref, zero_ref, sem, zsem):
    td = hfp_ref.shape[1]
    nb_max = xs_ref.shape[1] // BLK_E

    @pl.when(pl.program_id(0) == 0)
    def _():
        zero_ref[...] = jnp.zeros_like(zero_ref)

        def fill_expert(e, cnt):
            nonempty = pend_ref[e] > pstart_ref[e]

            @pl.when(nonempty)
            def _():
                pltpu.make_async_copy(zero_ref, _block(xs_ref, pend_ref[e] // BLK_E - 1), zsem).start()

            return cnt + nonempty.astype(I32)

        n_fill = lax.fori_loop(0, N_EXPERTS, fill_expert, 0)

        def fill_tail(g, carry):
            pltpu.make_async_copy(zero_ref, _block(xs_ref, g), zsem).start()
            return carry

        lax.fori_loop(nb_ref[0], nb_max, fill_tail, 0)

        def drain(g, carry):
            pltpu.make_async_copy(zero_ref, _block(xs_ref, 0), zsem).wait()
            return carry

        lax.fori_loop(0, n_fill + nb_max - nb_ref[0], drain, 0)

    def issue(t, carry):
        for k in range(TOP_K):
            d = pstart_ref[idx_ref[k, t]] + rank_ref[k, t]
            pltpu.make_async_copy(_row(hfp_ref, t), _row(xs_ref, d), sem).start()
        return carry

    lax.fori_loop(0, td, issue, 0, unroll=2)
    _rows_wait(xs_ref, TOP_K * td, sem)


def _dispatch(idx, rank, pstarts, pends, n_blocks, hfp, n_slots):
    n = hfp.shape[1]
    td = min(T_DISP, n)
    kspec = pl.BlockSpec((TOP_K, td), lambda i: (0, i), memory_space=pltpu.SMEM)
    smem = pl.BlockSpec(memory_space=pltpu.SMEM)
    return pl.pallas_call(
        _dispatch_kernel,
        grid=(n // td,),
        in_specs=[kspec, kspec, smem, smem, smem,
                  pl.BlockSpec((PACK_W, td, LANES), lambda i: (0, i, 0))],
        out_specs=pl.BlockSpec(memory_space=pl.ANY),
        out_shape=jax.ShapeDtypeStruct((PACK_W, n_slots, LANES), U32),
        scratch_shapes=[pltpu.VMEM((PACK_W, BLK_E, LANES), U32),
                        pltpu.SemaphoreType.DMA(()), pltpu.SemaphoreType.DMA(())],
        compiler_params=_cparams(("arbitrary",)),
        name="dispatch",
    )(idx, rank, pstarts, pends, n_blocks, hfp)


def _experts_kernel(nblk_ref, gstart_ref, nb_ref, xs_ref, wg_ref, wu_ref, wd_ref, ys_ref,
                    xbuf_ref, ybuf_ref, wgb_ref, wub_ref, wdb_ref, xsem, ysem):
    e = pl.program_id(0)
    total = nb_ref[0]
    nb_max = ys_ref.shape[1] // BLK_E

    def x_copy(g, slot):
        return pltpu.make_async_copy(_block(xs_ref, g), xbuf_ref.at[slot], xsem.at[slot])

    def y_copy(g):
        return pltpu.make_async_copy(ybuf_ref, _block(ys_ref, g), ysem)

    @pl.when(e == 0)
    def _():
        x_copy(0, 0).start()

    @pl.when(nblk_ref[e] > 0)
    def _():
        wgb_ref[...] = wg_ref[...].astype(BF16)
        wub_ref[...] = wu_ref[...].astype(BF16)
        wdb_ref[...] = wd_ref[...].astype(BF16)

    def block(b, carry):
        g = gstart_ref[e] + b
        slot = g % 2
        x_copy(g, slot).wait()

        @pl.when(g + 1 < total)
        def _():
            x_copy(g + 1, 1 - slot).start()

        xb = _unpack_words([xbuf_ref[slot, j] for j in range(PACK_W)]).astype(BF16)
        hg = jnp.dot(xb, wgb_ref[...], preferred_element_type=F32)
        hu = jnp.dot(xb, wub_ref[...], preferred_element_type=F32)
        act = (_silu(hg) * hu).astype(BF16)
        y = jnp.dot(act, wdb_ref[...], preferred_element_type=F32)
        words = _pack_rows(y)

        @pl.when(g > 0)
        def _():
            y_copy(g - 1).wait()

        for j, w in enumerate(words):
            ybuf_ref[j] = w
        y_copy(g).start()
        return carry

    lax.fori_loop(0, nblk_ref[e], block, 0)

    @pl.when(e == pl.num_programs(0) - 1)
    def _():
        y_copy(total - 1).wait()
        ybuf_ref[...] = jnp.zeros_like(ybuf_ref)

        def fill(g, carry):
            y_copy(g).start()
            return carry

        lax.fori_loop(total, nb_max, fill, 0)

        def drain(g, carry):
            y_copy(g).wait()
            return carry

        lax.fori_loop(total, nb_max, drain, 0)


def _experts(nblk, gstart, n_blocks, xs, w_gate, w_up, w_down):
    p = xs.shape[1]
    ne, d, f = w_gate.shape
    wspec = lambda shape: pl.BlockSpec((None,) + shape, lambda e, *_: (e, 0, 0))
    grid_spec = pltpu.PrefetchScalarGridSpec(
        num_scalar_prefetch=3,
        grid=(ne,),
        in_specs=[pl.BlockSpec(memory_space=pl.ANY), wspec((d, f)), wspec((d, f)), wspec((f, d))],
        out_specs=pl.BlockSpec(memory_space=pl.ANY),
        scratch_shapes=[pltpu.VMEM((2, PACK_W, BLK_E, LANES), U32), pltpu.VMEM((PACK_W, BLK_E, LANES), U32),
                        pltpu.VMEM((d, f), BF16), pltpu.VMEM((d, f), BF16), pltpu.VMEM((f, d), BF16),
                        pltpu.SemaphoreType.DMA((2,)), pltpu.SemaphoreType.DMA(())],
    )
    return pl.pallas_call(
        _experts_kernel,
        grid_spec=grid_spec,
        out_shape=jax.ShapeDtypeStruct((PACK_W, p, LANES), U32),
        compiler_params=_cparams(("arbitrary",)),
        name="experts",
    )(nblk, gstart, n_blocks, xs, w_gate, w_up, w_down)


def _combine_kernel(idx_ref, rank_ref, idxn_ref, rankn_ref, pstart_ref, wt_ref, hfp_ref, x1_ref, gf_ref,
                    wsg_ref, wsu_ref, wsd_ref, g_ref, b_ref, ys_ref, o_ref, buf_ref, sems, *, alpha):
    tc = hfp_ref.shape[1]
    i = pl.program_id(0)
    slot = i % 2

    def gather(iref, rref, s):
        def issue(t, carry):
            for k in range(TOP_K):
                d = pstart_ref[iref[k, t]] + rref[k, t]
                pltpu.make_async_copy(_row(ys_ref, d), buf_ref.at[s, k, :, pl.ds(t, 1), :], sems.at[s]).start()
            return carry

        lax.fori_loop(0, tc, issue, 0, unroll=2)

    @pl.when(i == 0)
    def _():
        gather(idx_ref, rank_ref, 0)

    @pl.when(i + 1 < pl.num_programs(0))
    def _():
        gather(idxn_ref, rankn_ref, 1 - slot)

    hb = _unpack_words([hfp_ref[j] for j in range(PACK_W)]).astype(BF16)
    sg = jnp.dot(hb, wsg_ref[...], preferred_element_type=F32)
    su = jnp.dot(hb, wsu_ref[...], preferred_element_type=F32)
    ffn = jnp.dot((_silu(sg) * su).astype(BF16), wsd_ref[...], preferred_element_type=F32)

    cur = buf_ref.at[slot]
    pltpu.make_async_copy(cur, cur, sems.at[slot]).wait()

    wt = wt_ref[...]
    for k in range(TOP_K):
        yk = _unpack_words([buf_ref[slot, k, j] for j in range(PACK_W)])
        ffn = ffn + wt[:, k:k + 1] * yk
    o_ref[...] = _layer_norm(alpha * x1_ref[...] + gf_ref[...] * ffn, g_ref[...], b_ref[...])


def _combine(idx, rank, pstarts, w_tok, hfp, x1, mod, ws_gate_bf, ws_up_bf, ws_down_bf, ln_g, ln_b, ys, alpha):
    b, s, d = x1.shape
    n = b * s
    tc = min(T_COMB, s)
    nt = s // tc
    ntile = n // tc
    fs = ws_gate_bf.shape[1]
    const = lambda shape: pl.BlockSpec(shape, lambda i: tuple(0 for _ in shape))
    kcur = pl.BlockSpec((TOP_K, tc), lambda i: (0, i), memory_space=pltpu.SMEM)
    knext = pl.BlockSpec((TOP_K, tc), lambda i: (0, jnp.minimum(i + 1, ntile - 1)), memory_space=pltpu.SMEM)
    out = pl.pallas_call(
        functools.partial(_combine_kernel, alpha=alpha),
        grid=(ntile,),
        in_specs=[kcur, kcur, knext, knext, pl.BlockSpec(memory_space=pltpu.SMEM),
                  pl.BlockSpec((tc, TOP_K), lambda i: (i, 0)),
                  pl.BlockSpec((PACK_W, tc, LANES), lambda i: (0, i, 0)),
                  pl.BlockSpec((tc, d), lambda i: (i, 0)),
                  pl.BlockSpec((None, None, 1, d), lambda i: (5, i // nt, 0, 0)),
                  const((d, fs)), const((d, fs)), const((fs, d)), const((1, d)), const((1, d)),
                  pl.BlockSpec(memory_space=pl.ANY)],
        out_specs=pl.BlockSpec((tc, d), lambda i: (i, 0)),
        out_shape=jax.ShapeDtypeStruct((n, d), F32),
        scratch_shapes=[pltpu.VMEM((2, TOP_K, PACK_W, tc, LANES), U32), pltpu.SemaphoreType.DMA((2,))],
        compiler_params=_cparams(("arbitrary",)),
        name="combine",
    )(idx, rank, idx, rank, pstarts, w_tok, hfp, x1.reshape(n, d), mod, ws_gate_bf, ws_up_bf, ws_down_bf,
      ln_g.reshape(1, d), ln_b.reshape(1, d), ys)
    return out.reshape(b, s, d)


def _slot_layout(counts, n_assign):
    cnt = counts[:, 0].astype(I32)
    padded = (cnt + BLK_E - 1) // BLK_E * BLK_E
    pends = jnp.cumsum(padded)
    pstarts = pends - padded
    n_blocks_max = (n_assign + N_EXPERTS * (BLK_E - 1)) // BLK_E
    n_blocks = (pends[-1] // BLK_E).astype(I32).reshape(1)
    return pstarts, pends, padded // BLK_E, pstarts // BLK_E, n_blocks, n_blocks_max * BLK_E


def kernel(x, c, w_ada, b_ada, w_in, w_out, na_rpb, ret_log_decay, ret_gn_g, ln1_g, ln1_b, ln2_g, ln2_b,
           w_router, router_bias, w_gate, w_up, w_down, ws_gate, ws_up, ws_down):
    b, s, d = x.shape
    depth = w_ada.shape[0]
    alpha = (2.0 * depth) ** 0.25
    t = jnp.arange(s, dtype=F32)
    inv_freq = ROPE_BASE ** (-jnp.arange(0, RET_HEAD_DIM, 2, dtype=F32) / RET_HEAD_DIM)
    ang = t[:, None] * inv_freq[None, :]
    cos, sin = jnp.cos(ang), jnp.sin(ang)
    cos2 = jnp.concatenate([cos, cos], axis=-1)
    sin2 = jnp.concatenate([-sin, sin], axis=-1)
    for l in range(depth):
        mod = _mod(c, w_ada[l], b_ada[l])
        proj = _inproj(x, mod, w_in[l].astype(BF16))
        y_na = _natten(proj, _na_bias_table(na_rpb[l]))
        y_r = _retention(proj, ret_log_decay[l], ret_gn_g[l], cos2, sin2)
        wr_t = w_router[l].T
        wr_hi = wr_t.astype(BF16)
        wr_lo = (wr_t - wr_hi.astype(F32)).astype(BF16)
        x1, hfp, logits_t = _outproj(y_na, y_r, x, mod, w_out[l].astype(BF16), ln1_g[l], ln1_b[l],
                                     wr_hi, wr_lo, alpha)
        idx, wts, rank, counts = _route(logits_t, router_bias[l])
        pstarts, pends, nblk, gstart, n_blocks, n_slots = _slot_layout(counts, b * s * TOP_K)
        xs = _dispatch(idx, rank, pstarts, pends, n_blocks, hfp, n_slots)
        ys = _experts(nblk, gstart, n_blocks, xs, w_gate[l], w_up[l], w_down[l])
        x = _combine(idx, rank, pstarts, wts.T, hfp, x1, mod, ws_gate[l].astype(BF16), ws_up[l].astype(BF16),
                     ws_down[l].astype(BF16), ln2_g[l], ln2_b[l], ys, alpha)
    return x
```

```python
import functools

import jax
import jax.numpy as jnp
import numpy as np
from jax import lax
from jax.experimental import pallas as pl
from jax.experimental.pallas import tpu as pltpu

F32 = jnp.float32
BF16 = jnp.bfloat16
U32 = jnp.uint32
I32 = jnp.int32

GRID_W = 64
WIN_R = 8
WIN_C = 16
NA_HEADS = 8
NA_HEAD_DIM = 64
NA_WIDTH = NA_HEADS * NA_HEAD_DIM
RET_HEADS = 4
RET_HEAD_DIM = 128
RET_WIDTH = RET_HEADS * RET_HEAD_DIM
RET_CHUNK = 128
ROPE_BASE = 10000.0
N_EXPERTS = 256
TOP_K = 8
N_GROUPS = 8
TOPK_GROUPS = 4
GROUP_SIZE = N_EXPERTS // N_GROUPS
ROUTED_SCALE = 2.5
LN_EPS = 1e-5
GN_EPS = 1e-6

LANES = 128
VMEM_LIMIT = 56 * 1024 * 1024

TM_PROJ = 512
T_ROUTE = 512
T_DISP = 256
BLK_E = 256
T_COMB = 256
PACK_W = 4
NA_ROWS_PER_ITER = 8
RET_UNROLL = 2
X_RING = 4
Y_RING = 3


def _cparams(sem, vmem=VMEM_LIMIT):
    return pltpu.CompilerParams(dimension_semantics=sem, vmem_limit_bytes=vmem)


def _silu(v):
    return v * jax.nn.sigmoid(v)


def _layer_norm(z, g, b):
    mu = jnp.mean(z, -1, keepdims=True)
    zc = z - mu
    var = jnp.mean(zc * zc, -1, keepdims=True)
    return zc * lax.rsqrt(var + LN_EPS) * g + b


def _pack_rows(v):
    half = v.shape[1] // 2
    vb = v.astype(BF16)
    lo = lax.bitcast_convert_type(vb[:, :half].astype(F32), U32) >> 16
    hi = lax.bitcast_convert_type(vb[:, half:].astype(F32), U32)
    w = hi | lo
    return [w[:, j * LANES:(j + 1) * LANES] for j in range(half // LANES)]


def _unpack_words(words):
    lo = [lax.bitcast_convert_type(w << 16, F32) for w in words]
    hi = [lax.bitcast_convert_type(w & jnp.uint32(0xFFFF0000), F32) for w in words]
    return jnp.concatenate(lo + hi, axis=-1)


def _mod_kernel(c_ref, w_ref, b_ref, o_ref):
    cond = _silu(c_ref[...])
    o_ref[0] = jnp.dot(cond, w_ref[...], precision=lax.Precision.HIGHEST,
                       preferred_element_type=F32) + b_ref[0]


def _mod(c, w_ada, b_ada):
    b, d = c.shape
    n6 = w_ada.shape[1] // d
    out = pl.pallas_call(
        _mod_kernel,
        grid=(n6,),
        in_specs=[pl.BlockSpec((b, d), lambda j: (0, 0)),
                  pl.BlockSpec((d, d), lambda j: (0, j)),
                  pl.BlockSpec((1, 1, d), lambda j: (j, 0, 0))],
        out_specs=pl.BlockSpec((1, b, d), lambda j: (j, 0, 0)),
        out_shape=jax.ShapeDtypeStruct((n6, b, d), F32),
        compiler_params=_cparams(("arbitrary",)),
        name="mod",
    )(c, w_ada, b_ada.reshape(n6, 1, d))
    return out.reshape(n6, b, 1, d)


def _mod_spec(which, d):
    return pl.BlockSpec((None, None, 1, d), lambda b, i, which=which: (which, b, 0, 0))


def _inproj_kernel(x_ref, sc_ref, sh_ref, w_ref, o_ref, *, chunk, q_scale):
    h = (x_ref[...] * (1.0 + sc_ref[...]) + sh_ref[...]).astype(BF16)
    for j in range(o_ref.shape[1] // chunk):
        acc = jnp.dot(h, w_ref[:, j * chunk:(j + 1) * chunk], preferred_element_type=F32)
        if j == 0:
            acc = acc * q_scale
        o_ref[:, j * chunk:(j + 1) * chunk] = acc.astype(o_ref.dtype)


def _inproj(x, mod, w_in_bf):
    b, s, d = x.shape
    e = w_in_bf.shape[1]
    tm = min(TM_PROJ, s)
    return pl.pallas_call(
        functools.partial(_inproj_kernel, chunk=NA_WIDTH, q_scale=NA_HEAD_DIM ** -0.5),
        grid=(b, s // tm),
        in_specs=[pl.BlockSpec((None, tm, d), lambda bi, i: (bi, i, 0)),
                  _mod_spec(1, d), _mod_spec(0, d),
                  pl.BlockSpec((d, e), lambda bi, i: (0, 0))],
        out_specs=pl.BlockSpec((None, tm, e), lambda bi, i: (bi, i, 0)),
        out_shape=jax.ShapeDtypeStruct((b, s, e), BF16),
        compiler_params=_cparams(("parallel", "parallel")),
        name="inproj",
    )(x, mod, mod, w_in_bf)


def _natten_kernel(q_ref, k_ref, v_ref, bias_ref, o_ref, *, rows):
    kspan = WIN_R * GRID_W

    first = lax.broadcasted_iota(I32, (1, LANES), 1) < NA_HEAD_DIM
    zero = jnp.zeros((), BF16)
    one = jnp.ones((), BF16)

    def rows_body(i, carry):
        qrows, krows, scores = [], [], []
        for u in range(NA_ROWS_PER_ITER):
            r = i * NA_ROWS_PER_ITER + u
            rs = jnp.clip(r - WIN_R // 2, 0, rows - WIN_R)
            vi = r - rs
            qrows.append(pl.ds(pl.multiple_of(r * GRID_W, GRID_W), GRID_W))
            krows.append(pl.ds(pl.multiple_of(rs * GRID_W, GRID_W), kspan))
            q = q_ref[qrows[u], :]
            k = k_ref[krows[u], :]
            for j, qh in enumerate((jnp.where(first, q, zero), jnp.where(first, zero, q))):
                s = lax.dot_general(qh, k, (((1,), (1,)), ((), ())), preferred_element_type=F32)
                scores.append(s + bias_ref[j, vi])
        probs = []
        for s in scores:
            m = jnp.max(s, axis=-1, keepdims=True)
            probs.append(jnp.exp(s - m).astype(BF16))
        for u in range(NA_ROWS_PER_ITER):
            v = v_ref[krows[u], :]
            a0 = jnp.dot(probs[2 * u], jnp.where(first, v, one), preferred_element_type=F32)
            a1 = jnp.dot(probs[2 * u + 1], jnp.where(first, one, v), preferred_element_type=F32)
            num = jnp.where(first, a0, a1)
            den = pltpu.roll(jnp.where(first, a1, a0), NA_HEAD_DIM, 1)
            o_ref[qrows[u], :] = (num / den).astype(o_ref.dtype)
        return carry

    lax.fori_loop(0, rows // NA_ROWS_PER_ITER, rows_body, 0)


def _na_bias_table(rpb):
    w = GRID_W
    cq = jnp.arange(w)
    cs = jnp.clip(cq - WIN_C // 2, 0, w - WIN_C)
    ck = jnp.arange(w)
    col_in = (ck[None, :] >= cs[:, None]) & (ck[None, :] < cs[:, None] + WIN_C)
    dc_idx = jnp.clip(ck[None, :] - cq[:, None] + WIN_C - 1, 0, 2 * WIN_C - 2)
    t = rpb[:, :, dc_idx]
    t = jnp.where(col_in[None, None], t, -jnp.inf)
    vi = jnp.arange(WIN_R)
    kr = jnp.arange(WIN_R)
    dr = kr[None, :] - vi[:, None] + WIN_R - 1
    tb = t[:, dr]
    return tb.transpose(0, 1, 3, 2, 4).reshape(rpb.shape[0], WIN_R, w, WIN_R * w).astype(F32)


def _natten(proj, bias_tab):
    b, s, _ = proj.shape
    rows = s // GRID_W
    hp = LANES // NA_HEAD_DIM
    npair = NA_HEADS // hp
    blk = lambda off: pl.BlockSpec((None, s, LANES), lambda bi, p, off=off: (bi, 0, off + p))
    return pl.pallas_call(
        functools.partial(_natten_kernel, rows=rows),
        grid=(b, npair),
        in_specs=[blk(0), blk(npair), blk(2 * npair),
                  pl.BlockSpec((hp, WIN_R, GRID_W, WIN_R * GRID_W), lambda bi, p: (p, 0, 0, 0))],
        out_specs=pl.BlockSpec((None, s, LANES), lambda bi, p: (bi, 0, p)),
        out_shape=jax.ShapeDtypeStruct((b, s, NA_WIDTH), BF16),
        compiler_params=_cparams(("parallel", "parallel")),
        name="natten",
    )(proj, proj, proj, bias_tab)


def _retent_kernel(ld_ref, q_ref, k_ref, v_ref, g_ref, cos_ref, sin_ref, gn_ref, o_ref,
                   qs_ref, ks_ref, sb_ref, *, nchunk):
    c = RET_CHUNK
    dh = RET_HEAD_DIM
    h = pl.program_id(1)
    lgf = ld_ref[0, h]
    lgb = ld_ref[1, h]

    cos2 = cos_ref[...]
    sin2 = sin_ref[...]
    qf = q_ref[...].astype(F32)
    qs_ref[...] = qf * cos2 + pltpu.roll(qf, dh // 2, 1) * sin2
    kf = k_ref[...].astype(F32)
    ks_ref[...] = (kf * cos2 + pltpu.roll(kf, dh // 2, 1) * sin2) * (dh ** -0.5)

    ic = lax.broadcasted_iota(I32, (c, 1), 0).astype(F32)
    ir = lax.broadcasted_iota(I32, (1, c), 1).astype(F32)
    diff = ic - ir
    dmat = jnp.where(diff >= 0, jnp.exp(jnp.maximum(diff, 0.0) * lgf),
                     jnp.exp(jnp.maximum(-diff, 0.0) * lgb))
    kdec_f = jnp.exp((c - 1 - ic) * lgf)
    qdec_f = jnp.exp((ic + 1) * lgf)
    kdec_b = jnp.exp(ic * lgb)
    qdec_b = jnp.exp((c - ic) * lgb)
    one = jnp.ones((1, 1), F32)
    cdec_f = jnp.exp(one * (c * lgf))
    cdec_b = jnp.exp(one * (c * lgb))
    tn = (((0,), (0,)), ((), ()))

    def bwd_body(i, sb):
        n = nchunk - 1 - i
        sb_ref[n] = sb
        rows = pl.ds(pl.multiple_of(n * c, c), c)
        kd = (ks_ref[rows, :] * kdec_b).astype(BF16)
        kv = lax.dot_general(kd, v_ref[rows, :], tn, preferred_element_type=F32)
        return cdec_b * sb + kv

    lax.fori_loop(0, nchunk, bwd_body, jnp.zeros((dh, dh), F32), unroll=RET_UNROLL)

    gn = gn_ref[...]

    def fwd_body(n, sf):
        rows = pl.ds(pl.multiple_of(n * c, c), c)
        qn = qs_ref[rows, :]
        kn = ks_ref[rows, :]
        vn = v_ref[rows, :]
        sc = lax.dot_general(qn.astype(BF16), kn.astype(BF16), (((1,), (1,)), ((), ())),
                             preferred_element_type=F32) * dmat
        y = jnp.dot(sc.astype(BF16), vn, preferred_element_type=F32)
        y = y + jnp.dot((qn * qdec_f).astype(BF16), sf.astype(BF16), preferred_element_type=F32)
        y = y + jnp.dot((qn * qdec_b).astype(BF16), sb_ref[n].astype(BF16), preferred_element_type=F32)
        mu = jnp.mean(y, -1, keepdims=True)
        yc = y - mu
        var = jnp.mean(yc * yc, -1, keepdims=True)
        yn = yc * lax.rsqrt(var + GN_EPS) * gn
        o_ref[rows, :] = (_silu(g_ref[rows, :].astype(F32)) * yn).astype(o_ref.dtype)
        kv = lax.dot_general((kn * kdec_f).astype(BF16), vn, tn, preferred_element_type=F32)
        return cdec_f * sf + kv

    lax.fori_loop(0, nchunk, fwd_body, jnp.zeros((dh, dh), F32), unroll=RET_UNROLL)


def _retention(proj, log_decay, gn_g, cos2, sin2):
    b, s, _ = proj.shape
    dh = RET_HEAD_DIM
    nchunk = s // RET_CHUNK
    base = 3 * NA_WIDTH // dh
    blk = lambda off: pl.BlockSpec((None, s, dh), lambda bi, h, off=off: (bi, 0, base + off + h))
    full = pl.BlockSpec((s, dh), lambda bi, h: (0, 0))
    return pl.pallas_call(
        functools.partial(_retent_kernel, nchunk=nchunk),
        grid=(b, RET_HEADS),
        in_specs=[pl.BlockSpec(memory_space=pltpu.SMEM),
                  blk(0), blk(RET_HEADS), blk(2 * RET_HEADS), blk(3 * RET_HEADS),
                  full, full,
                  pl.BlockSpec((1, dh), lambda bi, h: (0, h))],
        out_specs=pl.BlockSpec((None, s, dh), lambda bi, h: (bi, 0, h)),
        out_shape=jax.ShapeDtypeStruct((b, s, RET_WIDTH), BF16),
        scratch_shapes=[pltpu.VMEM((s, dh), F32), pltpu.VMEM((s, dh), F32),
                        pltpu.VMEM((nchunk, dh, dh), F32)],
        compiler_params=_cparams(("parallel", "parallel")),
        name="retent",
    )(log_decay, proj, proj, proj, proj, cos2, sin2, gn_g.reshape(1, RET_WIDTH))


def _outproj_kernel(yna_ref, yr_ref, x_ref, ga_ref, sf_ref, shf_ref, wo1_ref, wo2_ref, g_ref, b_ref,
                    wrh_ref, wrl_ref, x1_ref, hfp_ref, lg_ref, *, alpha):
    mix = jnp.dot(yna_ref[...], wo1_ref[...], preferred_element_type=F32)
    mix = mix + jnp.dot(yr_ref[...], wo2_ref[...], preferred_element_type=F32)
    x1 = _layer_norm(alpha * x_ref[...] + ga_ref[...] * mix, g_ref[...], b_ref[...])
    x1_ref[...] = x1
    hf = x1 * (1.0 + sf_ref[...]) + shf_ref[...]
    for j, w in enumerate(_pack_rows(hf)):
        hfp_ref[j] = w
    hb = hf.astype(BF16)
    hl = (hf - hb.astype(F32)).astype(BF16)
    nt = (((1,), (1,)), ((), ()))
    lg = lax.dot_general(wrh_ref[...], hb, nt, preferred_element_type=F32)
    lg = lg + lax.dot_general(wrh_ref[...], hl, nt, preferred_element_type=F32)
    lg = lg + lax.dot_general(wrl_ref[...], hb, nt, preferred_element_type=F32)
    lg_ref[...] = lg


def _outproj(y_na, y_r, x, mod, w_out_bf, ln_g, ln_b, wr_hi, wr_lo, alpha):
    b, s, d = x.shape
    tm = min(TM_PROJ, s)
    nt = s // tm
    ne = wr_hi.shape[0]
    const = lambda shape: pl.BlockSpec(shape, lambda bi, i: tuple(0 for _ in shape))
    x1, hfp, lg = pl.pallas_call(
        functools.partial(_outproj_kernel, alpha=alpha),
        grid=(b, nt),
        in_specs=[pl.BlockSpec((None, tm, NA_WIDTH), lambda bi, i: (bi, i, 0)),
                  pl.BlockSpec((None, tm, RET_WIDTH), lambda bi, i: (bi, i, 0)),
                  pl.BlockSpec((None, tm, d), lambda bi, i: (bi, i, 0)),
                  _mod_spec(2, d), _mod_spec(4, d), _mod_spec(3, d),
                  pl.BlockSpec((NA_WIDTH, d), lambda bi, i: (0, 0)),
                  pl.BlockSpec((RET_WIDTH, d), lambda bi, i: (1, 0)),
                  const((1, d)), const((1, d)), const((ne, d)), const((ne, d))],
        out_specs=[pl.BlockSpec((None, tm, d), lambda bi, i: (bi, i, 0)),
                   pl.BlockSpec((PACK_W, tm, LANES), lambda bi, i: (0, bi * nt + i, 0)),
                   pl.BlockSpec((ne, tm), lambda bi, i: (0, bi * nt + i))],
        out_shape=[jax.ShapeDtypeStruct((b, s, d), F32),
                   jax.ShapeDtypeStruct((PACK_W, b * s, LANES), U32),
                   jax.ShapeDtypeStruct((ne, b * s), F32)],
        compiler_params=_cparams(("parallel", "parallel")),
        name="outproj",
    )(y_na, y_r, x, mod, mod, mod, w_out_bf, w_out_bf, ln_g.reshape(1, d), ln_b.reshape(1, d), wr_hi, wr_lo)
    return x1, hfp, lg


def _route_kernel(lg_ref, rb_ref, idx_ref, w_ref, rank_ref, cnt_ref):
    t = lg_ref.shape[1]
    ninf = -jnp.inf

    @pl.when(pl.program_id(0) == 0)
    def _():
        cnt_ref[...] = jnp.zeros_like(cnt_ref)

    scores = jax.nn.sigmoid(lg_ref[...])
    sel = scores + rb_ref[...]

    io_g = lax.broadcasted_iota(I32, (GROUP_SIZE, t), 0)
    gs_rows = []
    for g in range(N_GROUPS):
        blk = sel[g * GROUP_SIZE:(g + 1) * GROUP_SIZE, :]
        m1 = jnp.max(blk, axis=0, keepdims=True)
        i1 = jnp.min(jnp.where(blk == m1, io_g, GROUP_SIZE), axis=0, keepdims=True)
        m2 = jnp.max(jnp.where(io_g == i1, ninf, blk), axis=0, keepdims=True)
        gs_rows.append(m1 + m2)
    gs = jnp.concatenate(gs_rows, axis=0)

    io8 = lax.broadcasted_iota(I32, (N_GROUPS, t), 0)
    gsel = jnp.zeros((N_GROUPS, t), F32)
    for _ in range(TOPK_GROUPS):
        m = jnp.max(gs, axis=0, keepdims=True)
        gi = jnp.min(jnp.where(gs == m, io8, N_GROUPS), axis=0, keepdims=True)
        hit = io8 == gi
        gsel = jnp.where(hit, 1.0, gsel)
        gs = jnp.where(hit, ninf, gs)

    masked = jnp.concatenate(
        [jnp.where(gsel[g:g + 1, :] > 0.0, sel[g * GROUP_SIZE:(g + 1) * GROUP_SIZE, :], ninf)
         for g in range(N_GROUPS)], axis=0)

    io_e = lax.broadcasted_iota(I32, (N_EXPERTS, t), 0)
    chosen = jnp.zeros((N_EXPERTS, t), F32)
    idx_rows, w_rows = [], []
    for _ in range(TOP_K):
        m = jnp.max(masked, axis=0, keepdims=True)
        ei = jnp.min(jnp.where(masked == m, io_e, N_EXPERTS), axis=0, keepdims=True)
        hit = io_e == ei
        w_rows.append(jnp.sum(jnp.where(hit, scores, 0.0), axis=0, keepdims=True))
        idx_rows.append(ei)
        chosen = jnp.where(hit, 1.0, chosen)
        masked = jnp.where(hit, ninf, masked)
    wk = jnp.concatenate(w_rows, axis=0)
    w_ref[...] = wk / jnp.sum(wk, axis=0, keepdims=True) * ROUTED_SCALE
    idx_ref[...] = jnp.concatenate(idx_rows, axis=0)

    upper = (lax.broadcasted_iota(I32, (t, t), 0) < lax.broadcasted_iota(I32, (t, t), 1))
    prefix = jnp.dot(chosen.astype(BF16), upper.astype(BF16), preferred_element_type=F32)
    rank_full = prefix + cnt_ref[...]
    rank_rows = [jnp.sum(jnp.where(io_e == ei, rank_full, 0.0), axis=0, keepdims=True) for ei in idx_rows]
    rank_ref[...] = jnp.concatenate(rank_rows, axis=0).astype(I32)
    cnt_ref[...] += jnp.sum(chosen, axis=1, keepdims=True)


def _route(logits_t, router_bias):
    ne, n = logits_t.shape
    t = min(T_ROUTE, n)
    kspec = pl.BlockSpec((TOP_K, t), lambda i: (0, i))
    return pl.pallas_call(
        _route_kernel,
        grid=(n // t,),
        in_specs=[pl.BlockSpec((ne, t), lambda i: (0, i)),
                  pl.BlockSpec((ne, 1), lambda i: (0, 0))],
        out_specs=[kspec, kspec, kspec, pl.BlockSpec((ne, 1), lambda i: (0, 0))],
        out_shape=[jax.ShapeDtypeStruct((TOP_K, n), I32),
                   jax.ShapeDtypeStruct((TOP_K, n), F32),
                   jax.ShapeDtypeStruct((TOP_K, n), I32),
                   jax.ShapeDtypeStruct((ne, 1), F32)],
        compiler_params=_cparams(("arbitrary",)),
        name="route",
    )(logits_t, router_bias.reshape(ne, 1))


def _row(ref, r):
    return ref.at[:, pl.ds(r, 1), :]


def _rows_wait(ref, n_rows, sem):
    span = ref.at[:, pl.ds(0, n_rows), :]
    pltpu.make_async_copy(span, span, sem).wait()


def _block(ref, g):
    return ref.at[:, pl.ds(pl.multiple_of(g * BLK_E, BLK_E), BLK_E), :]


def _dispatch_kernel(idx_ref, rank_ref, pstart_ref, pend_ref, nb_ref, hfp_ref, xs_ref, zero_ref, sem, zsem):
    td = hfp_ref.shape[1]
    nb_max = xs_ref.shape[1] // BLK_E

    @pl.when(pl.program_id(0) == 0)
    def _():
        zero_ref[...] = jnp.zeros_like(zero_ref)

        def fill_expert(e, cnt):
            nonempty = pend_ref[e] > pstart_ref[e]

            @pl.when(nonempty)
            def _():
                pltpu.make_async_copy(zero_ref, _block(xs_ref, pend_ref[e] // BLK_E - 1), zsem).start()

            return cnt + nonempty.astype(I32)

        n_fill = lax.fori_loop(0, N_EXPERTS, fill_expert, 0)

        def fill_tail(g, carry):
            pltpu.make_async_copy(zero_ref, _block(xs_ref, g), zsem).start()
            return carry

        lax.fori_loop(nb_ref[0], nb_max, fill_tail, 0)

        def drain(g, carry):
            pltpu.make_async_copy(zero_ref, _block(xs_ref, 0), zsem).wait()
            return carry

        lax.fori_loop(0, n_fill + nb_max - nb_ref[0], drain, 0)

    def issue(t, carry):
        for k in range(TOP_K):
            d = pstart_ref[idx_ref[k, t]] + rank_ref[k, t]
            pltpu.make_async_copy(_row(hfp_ref, t), _row(xs_ref, d), sem).start()
        return carry

    lax.fori_loop(0, td, issue, 0, unroll=2)
    _rows_wait(xs_ref, TOP_K * td, sem)


def _dispatch(idx, rank, pstarts, pends, n_blocks, hfp, n_slots):
    n = hfp.shape[1]
    td = min(T_DISP, n)
    kspec = pl.BlockSpec((TOP_K, td), lambda i: (0, i), memory_space=pltpu.SMEM)
    smem = pl.BlockSpec(memory_space=pltpu.SMEM)
    return pl.pallas_call(
        _dispatch_kernel,
        grid=(n // td,),
        in_specs=[kspec, kspec, smem, smem, smem,
                  pl.BlockSpec((PACK_W, td, LANES), lambda i: (0, i, 0))],
        out_specs=pl.BlockSpec(memory_space=pl.ANY),
        out_shape=jax.ShapeDtypeStruct((PACK_W, n_slots, LANES), U32),
        scratch_shapes=[pltpu.VMEM((PACK_W, BLK_E, LANES), U32),
                        pltpu.SemaphoreType.DMA(()), pltpu.SemaphoreType.DMA(())],
        compiler_params=_cparams(("arbitrary",)),
        name="dispatch",
    )(idx, rank, pstarts, pends, n_blocks, hfp)


def _experts_kernel(nblk_ref, gstart_ref, nb_ref, xs_ref, wg_ref, wu_ref, wd_ref, ys_ref,
                    xbuf_ref, ybuf_ref, wgb_ref, wub_ref, wdb_ref, xsem, ysem):
    e = pl.program_id(0)
    total = nb_ref[0]
    nb_max = ys_ref.shape[1] // BLK_E

    def x_copy(g):
        slot = g % X_RING
        return pltpu.make_async_copy(_block(xs_ref, g), xbuf_ref.at[slot], xsem.at[slot])

    def y_copy(g):
        slot = g % Y_RING
        return pltpu.make_async_copy(ybuf_ref.at[slot], _block(ys_ref, g), ysem.at[slot])

    @pl.when(e == 0)
    def _():
        for g in range(X_RING - 1):
            @pl.when(g < total)
            def _():
                x_copy(g).start()

    @pl.when(nblk_ref[e] > 0)
    def _():
        wgb_ref[...] = wg_ref[...].astype(BF16)
        wub_ref[...] = wu_ref[...].astype(BF16)
        wdb_ref[...] = wd_ref[...].astype(BF16)

    def block(b, carry):
        g = gstart_ref[e] + b
        x_copy(g).wait()

        @pl.when(g + X_RING - 1 < total)
        def _():
            x_copy(g + X_RING - 1).start()

        xb = _unpack_words([xbuf_ref[g % X_RING, j] for j in range(PACK_W)]).astype(BF16)
        hg = jnp.dot(xb, wgb_ref[...], preferred_element_type=F32)
        hu = jnp.dot(xb, wub_ref[...], preferred_element_type=F32)
        act = (_silu(hg) * hu).astype(BF16)
        y = jnp.dot(act, wdb_ref[...], preferred_element_type=F32)
        words = _pack_rows(y)

        @pl.when(g >= Y_RING)
        def _():
            y_copy(g - Y_RING).wait()

        for j, w in enumerate(words):
            ybuf_ref[g % Y_RING, j] = w
        y_copy(g).start()
        return carry

    lax.fori_loop(0, nblk_ref[e], block, 0)

    @pl.when(e == pl.num_programs(0) - 1)
    def _():
        for back in range(Y_RING):
            @pl.when(total - 1 - back >= 0)
            def _():
                y_copy(total - 1 - back).wait()

        ybuf_ref[0] = jnp.zeros(ybuf_ref.shape[1:], ybuf_ref.dtype)

        def tail_copy(g):
            return pltpu.make_async_copy(ybuf_ref.at[0], _block(ys_ref, g), ysem.at[0])

        def fill(g, carry):
            tail_copy(g).start()
            return carry

        lax.fori_loop(total, nb_max, fill, 0)

        def drain(g, carry):
            tail_copy(g).wait()
            return carry

        lax.fori_loop(total, nb_max, drain, 0)


def _experts(nblk, gstart, n_blocks, xs, w_gate, w_up, w_down):
    p = xs.shape[1]
    ne, d, f = w_gate.shape
    wspec = lambda shape: pl.BlockSpec((None,) + shape, lambda e, *_: (e, 0, 0))
    grid_spec = pltpu.PrefetchScalarGridSpec(
        num_scalar_prefetch=3,
        grid=(ne,),
        in_specs=[pl.BlockSpec(memory_space=pl.ANY), wspec((d, f)), wspec((d, f)), wspec((f, d))],
        out_specs=pl.BlockSpec(memory_space=pl.ANY),
        scratch_shapes=[pltpu.VMEM((X_RING, PACK_W, BLK_E, LANES), U32),
                        pltpu.VMEM((Y_RING, PACK_W, BLK_E, LANES), U32),
                        pltpu.VMEM((d, f), BF16), pltpu.VMEM((d, f), BF16), pltpu.VMEM((f, d), BF16),
                        pltpu.SemaphoreType.DMA((X_RING,)), pltpu.SemaphoreType.DMA((Y_RING,))],
    )
    return pl.pallas_call(
        _experts_kernel,
        grid_spec=grid_spec,
        out_shape=jax.ShapeDtypeStruct((PACK_W, p, LANES), U32),
        compiler_params=_cparams(("arbitrary",)),
        name="experts",
    )(nblk, gstart, n_blocks, xs, w_gate, w_up, w_down)


def _combine_kernel(idx_ref, rank_ref, idxn_ref, rankn_ref, pstart_ref, wt_ref, hfp_ref, x1_ref, gf_ref,
                    wsg_ref, wsu_ref, wsd_ref, g_ref, b_ref, ys_ref, o_ref, buf_ref, sems, *, alpha):
    tc = hfp_ref.shape[1]
    i = pl.program_id(0)
    slot = i % 2

    def gather(iref, rref, s):
        def issue(t, carry):
            for k in range(TOP_K):
                d = pstart_ref[iref[k, t]] + rref[k, t]
                pltpu.make_async_copy(_row(ys_ref, d), buf_ref.at[s, k, :, pl.ds(t, 1), :], sems.at[s]).start()
            return carry

        lax.fori_loop(0, tc, issue, 0, unroll=2)

    @pl.when(i == 0)
    def _():
        gather(idx_ref, rank_ref, 0)

    @pl.when(i + 1 < pl.num_programs(0))
    def _():
        gather(idxn_ref, rankn_ref, 1 - slot)

    hb = _unpack_words([hfp_ref[j] for j in range(PACK_W)]).astype(BF16)
    sg = jnp.dot(hb, wsg_ref[...], preferred_element_type=F32)
    su = jnp.dot(hb, wsu_ref[...], preferred_element_type=F32)
    ffn = jnp.dot((_silu(sg) * su).astype(BF16), wsd_ref[...], preferred_element_type=F32)

    cur = buf_ref.at[slot]
    pltpu.make_async_copy(cur, cur, sems.at[slot]).wait()

    wt = wt_ref[...]
    for k in range(TOP_K):
        yk = _unpack_words([buf_ref[slot, k, j] for j in range(PACK_W)])
        ffn = ffn + wt[:, k:k + 1] * yk
    o_ref[...] = _layer_norm(alpha * x1_ref[...] + gf_ref[...] * ffn, g_ref[...], b_ref[...])


def _combine(idx, rank, pstarts, w_tok, hfp, x1, mod, ws_gate_bf, ws_up_bf, ws_down_bf, ln_g, ln_b, ys, alpha):
    b, s, d = x1.shape
    n = b * s
    tc = min(T_COMB, s)
    nt = s // tc
    ntile = n // tc
    fs = ws_gate_bf.shape[1]
    const = lambda shape: pl.BlockSpec(shape, lambda i: tuple(0 for _ in shape))
    kcur = pl.BlockSpec((TOP_K, tc), lambda i: (0, i), memory_space=pltpu.SMEM)
    knext = pl.BlockSpec((TOP_K, tc), lambda i: (0, jnp.minimum(i + 1, ntile - 1)), memory_space=pltpu.SMEM)
    out = pl.pallas_call(
        functools.partial(_combine_kernel, alpha=alpha),
        grid=(ntile,),
        in_specs=[kcur, kcur, knext, knext, pl.BlockSpec(memory_space=pltpu.SMEM),
                  pl.BlockSpec((tc, TOP_K), lambda i: (i, 0)),
                  pl.BlockSpec((PACK_W, tc, LANES), lambda i: (0, i, 0)),
                  pl.BlockSpec((tc, d), lambda i: (i, 0)),
                  pl.BlockSpec((None, None, 1, d), lambda i: (5, i // nt, 0, 0)),
                  const((d, fs)), const((d, fs)), const((fs, d)), const((1, d)), const((1, d)),
                  pl.BlockSpec(memory_space=pl.ANY)],
        out_specs=pl.BlockSpec((tc, d), lambda i: (i, 0)),
        out_shape=jax.ShapeDtypeStruct((n, d), F32),
        scratch_shapes=[pltpu.VMEM((2, TOP_K, PACK_W, tc, LANES), U32), pltpu.SemaphoreType.DMA((2,))],
        compiler_params=_cparams(("arbitrary",)),
        name="combine",
    )(idx, rank, idx, rank, pstarts, w_tok, hfp, x1.reshape(n, d), mod, ws_gate_bf, ws_up_bf, ws_down_bf,
      ln_g.reshape(1, d), ln_b.reshape(1, d), ys)
    return out.reshape(b, s, d)


def _slot_layout(counts, n_assign):
    cnt = counts[:, 0].astype(I32)
    padded = (cnt + BLK_E - 1) // BLK_E * BLK_E
    pends = jnp.cumsum(padded)
    pstarts = pends - padded
    n_blocks_max = (n_assign + N_EXPERTS * (BLK_E - 1)) // BLK_E
    n_blocks = (pends[-1] // BLK_E).astype(I32).reshape(1)
    return pstarts, pends, padded // BLK_E, pstarts // BLK_E, n_blocks, n_blocks_max * BLK_E


def kernel(x, c, w_ada, b_ada, w_in, w_out, na_rpb, ret_log_decay, ret_gn_g, ln1_g, ln1_b, ln2_g, ln2_b,
           w_router, router_bias, w_gate, w_up, w_down, ws_gate, ws_up, ws_down):
    b, s, d = x.shape
    depth = w_ada.shape[0]
    alpha = (2.0 * depth) ** 0.25
    t = jnp.arange(s, dtype=F32)
    inv_freq = ROPE_BASE ** (-jnp.arange(0, RET_HEAD_DIM, 2, dtype=F32) / RET_HEAD_DIM)
    ang = t[:, None] * inv_freq[None, :]
    cos, sin = jnp.cos(ang), jnp.sin(ang)
    cos2 = jnp.concatenate([cos, cos], axis=-1)
    sin2 = jnp.concatenate([-sin, sin], axis=-1)
    for l in range(depth):
        mod = _mod(c, w_ada[l], b_ada[l])
        proj = _inproj(x, mod, w_in[l].astype(BF16))
        y_na = _natten(proj, _na_bias_table(na_rpb[l]))
        y_r = _retention(proj, ret_log_decay[l], ret_gn_g[l], cos2, sin2)
        wr_t = w_router[l].T
        wr_hi = wr_t.astype(BF16)
        wr_lo = (wr_t - wr_hi.astype(F32)).astype(BF16)
        x1, hfp, logits_t = _outproj(y_na, y_r, x, mod, w_out[l].astype(BF16), ln1_g[l], ln1_b[l],
                                     wr_hi, wr_lo, alpha)
        idx, wts, rank, counts = _route(logits_t, router_bias[l])
        pstarts, pends, nblk, gstart, n_blocks, n_slots = _slot_layout(counts, b * s * TOP_K)
        xs = _dispatch(idx, rank, pstarts, pends, n_blocks, hfp, n_slots)
        ys = _experts(nblk, gstart, n_blocks, xs, w_gate[l], w_up[l], w_down[l])
        x = _combine(idx, rank, pstarts, wts.T, hfp, x1, mod, ws_gate[l].astype(BF16), ws_up[l].astype(BF16),
                     ws_down[l].astype(BF16), ln2_g[l], ln2_b[l], ys, alpha)
    return x
```

```python
import functools

import jax
import jax.numpy as jnp
import numpy as np
from jax import lax
from jax.experimental import pallas as pl
from jax.experimental.pallas import tpu as pltpu

F32 = jnp.float32
BF16 = jnp.bfloat16
U32 = jnp.uint32
I32 = jnp.int32

GRID_W = 64
WIN_R = 8
WIN_C = 16
NA_HEADS = 8
NA_HEAD_DIM = 64
NA_WIDTH = NA_HEADS * NA_HEAD_DIM
RET_HEADS = 4
RET_HEAD_DIM = 128
RET_WIDTH = RET_HEADS * RET_HEAD_DIM
RET_CHUNK = 128
ROPE_BASE = 10000.0
N_EXPERTS = 256
TOP_K = 8
N_GROUPS = 8
TOPK_GROUPS = 4
GROUP_SIZE = N_EXPERTS // N_GROUPS
ROUTED_SCALE = 2.5
LN_EPS = 1e-5
GN_EPS = 1e-6

LANES = 128
VMEM_LIMIT = 56 * 1024 * 1024

TM_PROJ = 512
T_ROUTE = 512
T_DISP = 256
BLK_E = 512
T_COMB = 256
PACK_W = 4
NA_ROWS_PER_ITER = 8
RET_BLOCK = 256
RET_UNROLL = 2
X_RING = 4
Y_RING = 3


def _cparams(sem, vmem=VMEM_LIMIT):
    return pltpu.CompilerParams(dimension_semantics=sem, vmem_limit_bytes=vmem)


def _silu(v):
    return v * jax.nn.sigmoid(v)


def _layer_norm(z, g, b):
    mu = jnp.mean(z, -1, keepdims=True)
    zc = z - mu
    var = jnp.mean(zc * zc, -1, keepdims=True)
    return zc * lax.rsqrt(var + LN_EPS) * g + b


def _pack_rows(v):
    half = v.shape[1] // 2
    vb = v.astype(BF16)
    lo = lax.bitcast_convert_type(vb[:, :half].astype(F32), U32) >> 16
    hi = lax.bitcast_convert_type(vb[:, half:].astype(F32), U32)
    w = hi | lo
    return [w[:, j * LANES:(j + 1) * LANES] for j in range(half // LANES)]


def _unpack_words(words):
    lo = [lax.bitcast_convert_type(w << 16, F32) for w in words]
    hi = [lax.bitcast_convert_type(w & jnp.uint32(0xFFFF0000), F32) for w in words]
    return jnp.concatenate(lo + hi, axis=-1)


def _mod_kernel(c_ref, w_ref, b_ref, o_ref):
    cond = _silu(c_ref[...])
    o_ref[0] = jnp.dot(cond, w_ref[...], precision=lax.Precision.HIGHEST,
                       preferred_element_type=F32) + b_ref[0]


def _mod(c, w_ada, b_ada):
    b, d = c.shape
    n6 = w_ada.shape[1] // d
    out = pl.pallas_call(
        _mod_kernel,
        grid=(n6,),
        in_specs=[pl.BlockSpec((b, d), lambda j: (0, 0)),
                  pl.BlockSpec((d, d), lambda j: (0, j)),
                  pl.BlockSpec((1, 1, d), lambda j: (j, 0, 0))],
        out_specs=pl.BlockSpec((1, b, d), lambda j: (j, 0, 0)),
        out_shape=jax.ShapeDtypeStruct((n6, b, d), F32),
        compiler_params=_cparams(("arbitrary",)),
        name="mod",
    )(c, w_ada, b_ada.reshape(n6, 1, d))
    return out.reshape(n6, b, 1, d)


def _mod_spec(which, d):
    return pl.BlockSpec((None, None, 1, d), lambda b, i, which=which: (which, b, 0, 0))


def _inproj_kernel(x_ref, sc_ref, sh_ref, w_ref, o_ref, *, chunk, q_scale):
    h = (x_ref[...] * (1.0 + sc_ref[...]) + sh_ref[...]).astype(BF16)
    for j in range(o_ref.shape[1] // chunk):
        acc = jnp.dot(h, w_ref[:, j * chunk:(j + 1) * chunk], preferred_element_type=F32)
        if j == 0:
            acc = acc * q_scale
        o_ref[:, j * chunk:(j + 1) * chunk] = acc.astype(o_ref.dtype)


def _inproj(x, mod, w_in_bf):
    b, s, d = x.shape
    e = w_in_bf.shape[1]
    tm = min(TM_PROJ, s)
    return pl.pallas_call(
        functools.partial(_inproj_kernel, chunk=NA_WIDTH, q_scale=NA_HEAD_DIM ** -0.5),
        grid=(b, s // tm),
        in_specs=[pl.BlockSpec((None, tm, d), lambda bi, i: (bi, i, 0)),
                  _mod_spec(1, d), _mod_spec(0, d),
                  pl.BlockSpec((d, e), lambda bi, i: (0, 0))],
        out_specs=pl.BlockSpec((None, tm, e), lambda bi, i: (bi, i, 0)),
        out_shape=jax.ShapeDtypeStruct((b, s, e), BF16),
        compiler_params=_cparams(("parallel", "parallel")),
        name="inproj",
    )(x, mod, mod, w_in_bf)


def _natten_kernel(q_ref, k_ref, v_ref, bias_ref, o_ref, *, rows):
    kspan = WIN_R * GRID_W

    first = lax.broadcasted_iota(I32, (1, LANES), 1) < NA_HEAD_DIM
    zero = jnp.zeros((), BF16)
    one = jnp.ones((), BF16)

    def rows_body(i, carry):
        qrows, krows, scores = [], [], []
        for u in range(NA_ROWS_PER_ITER):
            r = i * NA_ROWS_PER_ITER + u
            rs = jnp.clip(r - WIN_R // 2, 0, rows - WIN_R)
            vi = r - rs
            qrows.append(pl.ds(pl.multiple_of(r * GRID_W, GRID_W), GRID_W))
            krows.append(pl.ds(pl.multiple_of(rs * GRID_W, GRID_W), kspan))
            q = q_ref[qrows[u], :]
            k = k_ref[krows[u], :]
            for j, qh in enumerate((jnp.where(first, q, zero), jnp.where(first, zero, q))):
                s = lax.dot_general(qh, k, (((1,), (1,)), ((), ())), preferred_element_type=F32)
                scores.append(s + bias_ref[j, vi])
        probs = []
        for s in scores:
            m = jnp.max(s, axis=-1, keepdims=True)
            probs.append(jnp.exp(s - m).astype(BF16))
        for u in range(NA_ROWS_PER_ITER):
            v = v_ref[krows[u], :]
            a0 = jnp.dot(probs[2 * u], jnp.where(first, v, one), preferred_element_type=F32)
            a1 = jnp.dot(probs[2 * u + 1], jnp.where(first, one, v), preferred_element_type=F32)
            num = jnp.where(first, a0, a1)
            den = pltpu.roll(jnp.where(first, a1, a0), NA_HEAD_DIM, 1)
            o_ref[qrows[u], :] = (num / den).astype(o_ref.dtype)
        return carry

    lax.fori_loop(0, rows // NA_ROWS_PER_ITER, rows_body, 0)


def _na_bias_table(rpb):
    w = GRID_W
    cq = jnp.arange(w)
    cs = jnp.clip(cq - WIN_C // 2, 0, w - WIN_C)
    ck = jnp.arange(w)
    col_in = (ck[None, :] >= cs[:, None]) & (ck[None, :] < cs[:, None] + WIN_C)
    dc_idx = jnp.clip(ck[None, :] - cq[:, None] + WIN_C - 1, 0, 2 * WIN_C - 2)
    t = rpb[:, :, dc_idx]
    t = jnp.where(col_in[None, None], t, -jnp.inf)
    vi = jnp.arange(WIN_R)
    kr = jnp.arange(WIN_R)
    dr = kr[None, :] - vi[:, None] + WIN_R - 1
    tb = t[:, dr]
    return tb.transpose(0, 1, 3, 2, 4).reshape(rpb.shape[0], WIN_R, w, WIN_R * w).astype(F32)


def _natten(proj, bias_tab):
    b, s, _ = proj.shape
    rows = s // GRID_W
    hp = LANES // NA_HEAD_DIM
    npair = NA_HEADS // hp
    blk = lambda off: pl.BlockSpec((None, s, LANES), lambda bi, p, off=off: (bi, 0, off + p))
    return pl.pallas_call(
        functools.partial(_natten_kernel, rows=rows),
        grid=(b, npair),
        in_specs=[blk(0), blk(npair), blk(2 * npair),
                  pl.BlockSpec((hp, WIN_R, GRID_W, WIN_R * GRID_W), lambda bi, p: (p, 0, 0, 0))],
        out_specs=pl.BlockSpec((None, s, LANES), lambda bi, p: (bi, 0, p)),
        out_shape=jax.ShapeDtypeStruct((b, s, NA_WIDTH), BF16),
        compiler_params=_cparams(("parallel", "parallel")),
        name="natten",
    )(proj, proj, proj, bias_tab)


def _retent_kernel(ld_ref, q_ref, k_ref, v_ref, g_ref, cos_ref, sin_ref, gn_ref, o_ref,
                   qs_ref, ks_ref, sb_ref, *, nchunk):
    c = RET_BLOCK
    dh = RET_HEAD_DIM
    h = pl.program_id(1)
    lgf = ld_ref[0, h]
    lgb = ld_ref[1, h]

    cos2 = cos_ref[...]
    sin2 = sin_ref[...]
    qf = q_ref[...].astype(F32)
    qs_ref[...] = qf * cos2 + pltpu.roll(qf, dh // 2, 1) * sin2
    kf = k_ref[...].astype(F32)
    ks_ref[...] = (kf * cos2 + pltpu.roll(kf, dh // 2, 1) * sin2) * (dh ** -0.5)

    ic = lax.broadcasted_iota(I32, (c, 1), 0).astype(F32)
    ir = lax.broadcasted_iota(I32, (1, c), 1).astype(F32)
    diff = ic - ir
    dmat = jnp.where(diff >= 0, jnp.exp(jnp.maximum(diff, 0.0) * lgf),
                     jnp.exp(jnp.maximum(-diff, 0.0) * lgb))
    kdec_f = jnp.exp((c - 1 - ic) * lgf)
    qdec_f = jnp.exp((ic + 1) * lgf)
    kdec_b = jnp.exp(ic * lgb)
    qdec_b = jnp.exp((c - ic) * lgb)
    one = jnp.ones((1, 1), F32)
    cdec_f = jnp.exp(one * (c * lgf))
    cdec_b = jnp.exp(one * (c * lgb))
    tn = (((0,), (0,)), ((), ()))

    def bwd_body(i, sb):
        n = nchunk - 1 - i
        sb_ref[n] = sb
        rows = pl.ds(pl.multiple_of(n * c, c), c)
        kd = (ks_ref[rows, :] * kdec_b).astype(BF16)
        kv = lax.dot_general(kd, v_ref[rows, :], tn, preferred_element_type=F32)
        return cdec_b * sb + kv

    lax.fori_loop(0, nchunk, bwd_body, jnp.zeros((dh, dh), F32), unroll=RET_UNROLL)

    gn = gn_ref[...]

    def fwd_body(n, sf):
        rows = pl.ds(pl.multiple_of(n * c, c), c)
        qn = qs_ref[rows, :]
        kn = ks_ref[rows, :]
        vn = v_ref[rows, :]
        sc = lax.dot_general(qn.astype(BF16), kn.astype(BF16), (((1,), (1,)), ((), ())),
                             preferred_element_type=F32) * dmat
        y = jnp.dot(sc.astype(BF16), vn, preferred_element_type=F32)
        qd = jnp.concatenate([qn * qdec_f, qn * qdec_b], axis=1).astype(BF16)
        st = jnp.concatenate([sf, sb_ref[n]], axis=0).astype(BF16)
        y = y + jnp.dot(qd, st, preferred_element_type=F32)
        mu = jnp.mean(y, -1, keepdims=True)
        yc = y - mu
        var = jnp.mean(yc * yc, -1, keepdims=True)
        yn = yc * lax.rsqrt(var + GN_EPS) * gn
        o_ref[rows, :] = (_silu(g_ref[rows, :].astype(F32)) * yn).astype(o_ref.dtype)
        kv = lax.dot_general((kn * kdec_f).astype(BF16), vn, tn, preferred_element_type=F32)
        return cdec_f * sf + kv

    lax.fori_loop(0, nchunk, fwd_body, jnp.zeros((dh, dh), F32), unroll=RET_UNROLL)


def _retention(proj, log_decay, gn_g, cos2, sin2):
    b, s, _ = proj.shape
    dh = RET_HEAD_DIM
    nchunk = s // RET_BLOCK
    base = 3 * NA_WIDTH // dh
    blk = lambda off: pl.BlockSpec((None, s, dh), lambda bi, h, off=off: (bi, 0, base + off + h))
    full = pl.BlockSpec((s, dh), lambda bi, h: (0, 0))
    return pl.pallas_call(
        functools.partial(_retent_kernel, nchunk=nchunk),
        grid=(b, RET_HEADS),
        in_specs=[pl.BlockSpec(memory_space=pltpu.SMEM),
                  blk(0), blk(RET_HEADS), blk(2 * RET_HEADS), blk(3 * RET_HEADS),
                  full, full,
                  pl.BlockSpec((1, dh), lambda bi, h: (0, h))],
        out_specs=pl.BlockSpec((None, s, dh), lambda bi, h: (bi, 0, h)),
        out_shape=jax.ShapeDtypeStruct((b, s, RET_WIDTH), BF16),
        scratch_shapes=[pltpu.VMEM((s, dh), F32), pltpu.VMEM((s, dh), F32),
                        pltpu.VMEM((nchunk, dh, dh), F32)],
        compiler_params=_cparams(("parallel", "parallel")),
        name="retent",
    )(log_decay, proj, proj, proj, proj, cos2, sin2, gn_g.reshape(1, RET_WIDTH))


def _outproj_kernel(yna_ref, yr_ref, x_ref, ga_ref, sf_ref, shf_ref, wo1_ref, wo2_ref, g_ref, b_ref,
                    wrh_ref, wrl_ref, x1_ref, hfp_ref, lg_ref, *, alpha):
    mix = jnp.dot(yna_ref[...], wo1_ref[...], preferred_element_type=F32)
    mix = mix + jnp.dot(yr_ref[...], wo2_ref[...], preferred_element_type=F32)
    x1 = _layer_norm(alpha * x_ref[...] + ga_ref[...] * mix, g_ref[...], b_ref[...])
    x1_ref[...] = x1
    hf = x1 * (1.0 + sf_ref[...]) + shf_ref[...]
    for j, w in enumerate(_pack_rows(hf)):
        hfp_ref[j] = w
    hb = hf.astype(BF16)
    hl = (hf - hb.astype(F32)).astype(BF16)
    nt = (((1,), (1,)), ((), ()))
    lg = lax.dot_general(wrh_ref[...], hb, nt, preferred_element_type=F32)
    lg = lg + lax.dot_general(wrh_ref[...], hl, nt, preferred_element_type=F32)
    lg = lg + lax.dot_general(wrl_ref[...], hb, nt, preferred_element_type=F32)
    lg_ref[...] = lg


def _outproj(y_na, y_r, x, mod, w_out_bf, ln_g, ln_b, wr_hi, wr_lo, alpha):
    b, s, d = x.shape
    tm = min(TM_PROJ, s)
    nt = s // tm
    ne = wr_hi.shape[0]
    const = lambda shape: pl.BlockSpec(shape, lambda bi, i: tuple(0 for _ in shape))
    x1, hfp, lg = pl.pallas_call(
        functools.partial(_outproj_kernel, alpha=alpha),
        grid=(b, nt),
        in_specs=[pl.BlockSpec((None, tm, NA_WIDTH), lambda bi, i: (bi, i, 0)),
                  pl.BlockSpec((None, tm, RET_WIDTH), lambda bi, i: (bi, i, 0)),
                  pl.BlockSpec((None, tm, d), lambda bi, i: (bi, i, 0)),
                  _mod_spec(2, d), _mod_spec(4, d), _mod_spec(3, d),
                  pl.BlockSpec((NA_WIDTH, d), lambda bi, i: (0, 0)),
                  pl.BlockSpec((RET_WIDTH, d), lambda bi, i: (1, 0)),
                  const((1, d)), const((1, d)), const((ne, d)), const((ne, d))],
        out_specs=[pl.BlockSpec((None, tm, d), lambda bi, i: (bi, i, 0)),
                   pl.BlockSpec((PACK_W, tm, LANES), lambda bi, i: (0, bi * nt + i, 0)),
                   pl.BlockSpec((ne, tm), lambda bi, i: (0, bi * nt + i))],
        out_shape=[jax.ShapeDtypeStruct((b, s, d), F32),
                   jax.ShapeDtypeStruct((PACK_W, b * s, LANES), U32),
                   jax.ShapeDtypeStruct((ne, b * s), F32)],
        compiler_params=_cparams(("parallel", "parallel")),
        name="outproj",
    )(y_na, y_r, x, mod, mod, mod, w_out_bf, w_out_bf, ln_g.reshape(1, d), ln_b.reshape(1, d), wr_hi, wr_lo)
    return x1, hfp, lg


def _route_kernel(lg_ref, rb_ref, idx_ref, w_ref, rank_ref, cnt_ref):
    t = lg_ref.shape[1]
    ninf = -jnp.inf

    @pl.when(pl.program_id(0) == 0)
    def _():
        cnt_ref[...] = jnp.zeros_like(cnt_ref)

    scores = jax.nn.sigmoid(lg_ref[...])
    sel = scores + rb_ref[...]

    io_g = lax.broadcasted_iota(I32, (GROUP_SIZE, t), 0)
    gs_rows = []
    for g in range(N_GROUPS):
        blk = sel[g * GROUP_SIZE:(g + 1) * GROUP_SIZE, :]
        m1 = jnp.max(blk, axis=0, keepdims=True)
        i1 = jnp.min(jnp.where(blk == m1, io_g, GROUP_SIZE), axis=0, keepdims=True)
        m2 = jnp.max(jnp.where(io_g == i1, ninf, blk), axis=0, keepdims=True)
        gs_rows.append(m1 + m2)
    gs = jnp.concatenate(gs_rows, axis=0)

    io8 = lax.broadcasted_iota(I32, (N_GROUPS, t), 0)
    gsel = jnp.zeros((N_GROUPS, t), F32)
    for _ in range(TOPK_GROUPS):
        m = jnp.max(gs, axis=0, keepdims=True)
        gi = jnp.min(jnp.where(gs == m, io8, N_GROUPS), axis=0, keepdims=True)
        hit = io8 == gi
        gsel = jnp.where(hit, 1.0, gsel)
        gs = jnp.where(hit, ninf, gs)

    masked = jnp.concatenate(
        [jnp.where(gsel[g:g + 1, :] > 0.0, sel[g * GROUP_SIZE:(g + 1) * GROUP_SIZE, :], ninf)
         for g in range(N_GROUPS)], axis=0)

    io_e = lax.broadcasted_iota(I32, (N_EXPERTS, t), 0)
    chosen = jnp.zeros((N_EXPERTS, t), F32)
    idx_rows, w_rows = [], []
    for _ in range(TOP_K):
        m = jnp.max(masked, axis=0, keepdims=True)
        ei = jnp.min(jnp.where(masked == m, io_e, N_EXPERTS), axis=0, keepdims=True)
        hit = io_e == ei
        w_rows.append(jnp.sum(jnp.where(hit, scores, 0.0), axis=0, keepdims=True))
        idx_rows.append(ei)
        chosen = jnp.where(hit, 1.0, chosen)
        masked = jnp.where(hit, ninf, masked)
    wk = jnp.concatenate(w_rows, axis=0)
    w_ref[...] = wk / jnp.sum(wk, axis=0, keepdims=True) * ROUTED_SCALE
    idx_ref[...] = jnp.concatenate(idx_rows, axis=0)

    upper = (lax.broadcasted_iota(I32, (t, t), 0) < lax.broadcasted_iota(I32, (t, t), 1))
    prefix = jnp.dot(chosen.astype(BF16), upper.astype(BF16), preferred_element_type=F32)
    rank_full = prefix + cnt_ref[...]
    rank_rows = [jnp.sum(jnp.where(io_e == ei, rank_full, 0.0), axis=0, keepdims=True) for ei in idx_rows]
    rank_ref[...] = jnp.concatenate(rank_rows, axis=0).astype(I32)
    cnt_ref[...] += jnp.sum(chosen, axis=1, keepdims=True)


def _route(logits_t, router_bias):
    ne, n = logits_t.shape
    t = min(T_ROUTE, n)
    kspec = pl.BlockSpec((TOP_K, t), lambda i: (0, i))
    return pl.pallas_call(
        _route_kernel,
        grid=(n // t,),
        in_specs=[pl.BlockSpec((ne, t), lambda i: (0, i)),
                  pl.BlockSpec((ne, 1), lambda i: (0, 0))],
        out_specs=[kspec, kspec, kspec, pl.BlockSpec((ne, 1), lambda i: (0, 0))],
        out_shape=[jax.ShapeDtypeStruct((TOP_K, n), I32),
                   jax.ShapeDtypeStruct((TOP_K, n), F32),
                   jax.ShapeDtypeStruct((TOP_K, n), I32),
                   jax.ShapeDtypeStruct((ne, 1), F32)],
        compiler_params=_cparams(("arbitrary",)),
        name="route",
    )(logits_t, router_bias.reshape(ne, 1))


def _row(ref, r):
    return ref.at[:, pl.ds(r, 1), :]


def _rows_wait(ref, n_rows, sem):
    span = ref.at[:, pl.ds(0, n_rows), :]
    pltpu.make_async_copy(span, span, sem).wait()


def _block(ref, g):
    return ref.at[:, pl.ds(pl.multiple_of(g * BLK_E, BLK_E), BLK_E), :]


def _dispatch_kernel(idx_ref, rank_ref, pstart_ref, pend_ref, nb_ref, hfp_ref, xs_ref, zero_ref, sem, zsem):
    td = hfp_ref.shape[1]
    nb_max = xs_ref.shape[1] // BLK_E

    @pl.when(pl.program_id(0) == 0)
    def _():
        zero_ref[...] = jnp.zeros_like(zero_ref)

        def fill_expert(e, cnt):
            nonempty = pend_ref[e] > pstart_ref[e]

            @pl.when(nonempty)
            def _():
                pltpu.make_async_copy(zero_ref, _block(xs_ref, pend_ref[e] // BLK_E - 1), zsem).start()

            return cnt + nonempty.astype(I32)

        n_fill = lax.fori_loop(0, N_EXPERTS, fill_expert, 0)

        def fill_tail(g, carry):
            pltpu.make_async_copy(zero_ref, _block(xs_ref, g), zsem).start()
            return carry

        lax.fori_loop(nb_ref[0], nb_max, fill_tail, 0)

        def drain(g, carry):
            pltpu.make_async_copy(zero_ref, _block(xs_ref, 0), zsem).wait()
            return carry

        lax.fori_loop(0, n_fill + nb_max - nb_ref[0], drain, 0)

    def issue(t, carry):
        for k in range(TOP_K):
            d = pstart_ref[idx_ref[k, t]] + rank_ref[k, t]
            pltpu.make_async_copy(_row(hfp_ref, t), _row(xs_ref, d), sem).start()
        return carry

    lax.fori_loop(0, td, issue, 0, unroll=2)
    _rows_wait(xs_ref, TOP_K * td, sem)


def _dispatch(idx, rank, pstarts, pends, n_blocks, hfp, n_slots):
    n = hfp.shape[1]
    td = min(T_DISP, n)
    kspec = pl.BlockSpec((TOP_K, td), lambda i: (0, i), memory_space=pltpu.SMEM)
    smem = pl.BlockSpec(memory_space=pltpu.SMEM)
    return pl.pallas_call(
        _dispatch_kernel,
        grid=(n // td,),
        in_specs=[kspec, kspec, smem, smem, smem,
                  pl.BlockSpec((PACK_W, td, LANES), lambda i: (0, i, 0))],
        out_specs=pl.BlockSpec(memory_space=pl.ANY),
        out_shape=jax.ShapeDtypeStruct((PACK_W, n_slots, LANES), U32),
        scratch_shapes=[pltpu.VMEM((PACK_W, BLK_E, LANES), U32),
                        pltpu.SemaphoreType.DMA(()), pltpu.SemaphoreType.DMA(())],
        compiler_params=_cparams(("arbitrary",)),
        name="dispatch",
    )(idx, rank, pstarts, pends, n_blocks, hfp)


def _experts_kernel(nblk_ref, gstart_ref, nb_ref, xs_ref, wg_ref, wu_ref, wd_ref, ys_ref,
                    xbuf_ref, ybuf_ref, wgb_ref, wub_ref, wdb_ref, xsem, ysem):
    e = pl.program_id(0)
    total = nb_ref[0]
    nb_max = ys_ref.shape[1] // BLK_E

    def x_copy(g):
        slot = g % X_RING
        return pltpu.make_async_copy(_block(xs_ref, g), xbuf_ref.at[slot], xsem.at[slot])

    def y_copy(g):
        slot = g % Y_RING
        return pltpu.make_async_copy(ybuf_ref.at[slot], _block(ys_ref, g), ysem.at[slot])

    @pl.when(e == 0)
    def _():
        for g in range(X_RING - 1):
            @pl.when(g < total)
            def _():
                x_copy(g).start()

    @pl.when(nblk_ref[e] > 0)
    def _():
        wgb_ref[...] = wg_ref[...].astype(BF16)
        wub_ref[...] = wu_ref[...].astype(BF16)
        wdb_ref[...] = wd_ref[...].astype(BF16)

    def block(b, carry):
        g = gstart_ref[e] + b
        x_copy(g).wait()

        @pl.when(g + X_RING - 1 < total)
        def _():
            x_copy(g + X_RING - 1).start()

        xb = _unpack_words([xbuf_ref[g % X_RING, j] for j in range(PACK_W)]).astype(BF16)
        hg = jnp.dot(xb, wgb_ref[...], preferred_element_type=F32)
        hu = jnp.dot(xb, wub_ref[...], preferred_element_type=F32)
        act = (_silu(hg) * hu).astype(BF16)
        y = jnp.dot(act, wdb_ref[...], preferred_element_type=F32)
        words = _pack_rows(y)

        @pl.when(g >= Y_RING)
        def _():
            y_copy(g - Y_RING).wait()

        for j, w in enumerate(words):
            ybuf_ref[g % Y_RING, j] = w
        y_copy(g).start()
        return carry

    lax.fori_loop(0, nblk_ref[e], block, 0)

    @pl.when(e == pl.num_programs(0) - 1)
    def _():
        for back in range(Y_RING):
            @pl.when(total - 1 - back >= 0)
            def _():
                y_copy(total - 1 - back).wait()

        ybuf_ref[0] = jnp.zeros(ybuf_ref.shape[1:], ybuf_ref.dtype)

        def tail_copy(g):
            return pltpu.make_async_copy(ybuf_ref.at[0], _block(ys_ref, g), ysem.at[0])

        def fill(g, carry):
            tail_copy(g).start()
            return carry

        lax.fori_loop(total, nb_max, fill, 0)

        def drain(g, carry):
            tail_copy(g).wait()
            return carry

        lax.fori_loop(total, nb_max, drain, 0)


def _experts(nblk, gstart, n_blocks, xs, w_gate, w_up, w_down):
    p = xs.shape[1]
    ne, d, f = w_gate.shape
    wspec = lambda shape: pl.BlockSpec((None,) + shape, lambda e, *_: (e, 0, 0))
    grid_spec = pltpu.PrefetchScalarGridSpec(
        num_scalar_prefetch=3,
        grid=(ne,),
        in_specs=[pl.BlockSpec(memory_space=pl.ANY), wspec((d, f)), wspec((d, f)), wspec((f, d))],
        out_specs=pl.BlockSpec(memory_space=pl.ANY),
        scratch_shapes=[pltpu.VMEM((X_RING, PACK_W, BLK_E, LANES), U32),
                        pltpu.VMEM((Y_RING, PACK_W, BLK_E, LANES), U32),
                        pltpu.VMEM((d, f), BF16), pltpu.VMEM((d, f), BF16), pltpu.VMEM((f, d), BF16),
                        pltpu.SemaphoreType.DMA((X_RING,)), pltpu.SemaphoreType.DMA((Y_RING,))],
    )
    return pl.pallas_call(
        _experts_kernel,
        grid_spec=grid_spec,
        out_shape=jax.ShapeDtypeStruct((PACK_W, p, LANES), U32),
        compiler_params=_cparams(("arbitrary",)),
        name="experts",
    )(nblk, gstart, n_blocks, xs, w_gate, w_up, w_down)


def _combine_kernel(idx_ref, rank_ref, idxn_ref, rankn_ref, pstart_ref, wt_ref, hfp_ref, x1_ref, gf_ref,
                    wsg_ref, wsu_ref, wsd_ref, g_ref, b_ref, ys_ref, o_ref, buf_ref, sems, *, alpha):
    tc = hfp_ref.shape[1]
    i = pl.program_id(0)
    slot = i % 2

    def gather(iref, rref, s):
        def issue(t, carry):
            for k in range(TOP_K):
                d = pstart_ref[iref[k, t]] + rref[k, t]
                pltpu.make_async_copy(_row(ys_ref, d), buf_ref.at[s, k, :, pl.ds(t, 1), :], sems.at[s]).start()
            return carry

        lax.fori_loop(0, tc, issue, 0, unroll=2)

    @pl.when(i == 0)
    def _():
        gather(idx_ref, rank_ref, 0)

    @pl.when(i + 1 < pl.num_programs(0))
    def _():
        gather(idxn_ref, rankn_ref, 1 - slot)

    hb = _unpack_words([hfp_ref[j] for j in range(PACK_W)]).astype(BF16)
    sg = jnp.dot(hb, wsg_ref[...], preferred_element_type=F32)
    su = jnp.dot(hb, wsu_ref[...], preferred_element_type=F32)
    ffn = jnp.dot((_silu(sg) * su).astype(BF16), wsd_ref[...], preferred_element_type=F32)

    cur = buf_ref.at[slot]
    pltpu.make_async_copy(cur, cur, sems.at[slot]).wait()

    wt = wt_ref[...]
    for k in range(TOP_K):
        yk = _unpack_words([buf_ref[slot, k, j] for j in range(PACK_W)])
        ffn = ffn + wt[:, k:k + 1] * yk
    o_ref[...] = _layer_norm(alpha * x1_ref[...] + gf_ref[...] * ffn, g_ref[...], b_ref[...])


def _combine(idx, rank, pstarts, w_tok, hfp, x1, mod, ws_gate_bf, ws_up_bf, ws_down_bf, ln_g, ln_b, ys, alpha):
    b, s, d = x1.shape
    n = b * s
    tc = min(T_COMB, s)
    nt = s // tc
    ntile = n // tc
    fs = ws_gate_bf.shape[1]
    const = lambda shape: pl.BlockSpec(shape, lambda i: tuple(0 for _ in shape))
    kcur = pl.BlockSpec((TOP_K, tc), lambda i: (0, i), memory_space=pltpu.SMEM)
    knext = pl.BlockSpec((TOP_K, tc), lambda i: (0, jnp.minimum(i + 1, ntile - 1)), memory_space=pltpu.SMEM)
    out = pl.pallas_call(
        functools.partial(_combine_kernel, alpha=alpha),
        grid=(ntile,),
        in_specs=[kcur, kcur, knext, knext, pl.BlockSpec(memory_space=pltpu.SMEM),
                  pl.BlockSpec((tc, TOP_K), lambda i: (i, 0)),
                  pl.BlockSpec((PACK_W, tc, LANES), lambda i: (0, i, 0)),
                  pl.BlockSpec((tc, d), lambda i: (i, 0)),
                  pl.BlockSpec((None, None, 1, d), lambda i: (5, i // nt, 0, 0)),
                  const((d, fs)), const((d, fs)), const((fs, d)), const((1, d)), const((1, d)),
                  pl.BlockSpec(memory_space=pl.ANY)],
        out_specs=pl.BlockSpec((tc, d), lambda i: (i, 0)),
        out_shape=jax.ShapeDtypeStruct((n, d), F32),
        scratch_shapes=[pltpu.VMEM((2, TOP_K, PACK_W, tc, LANES), U32), pltpu.SemaphoreType.DMA((2,))],
        compiler_params=_cparams(("arbitrary",)),
        name="combine",
    )(idx, rank, idx, rank, pstarts, w_tok, hfp, x1.reshape(n, d), mod, ws_gate_bf, ws_up_bf, ws_down_bf,
      ln_g.reshape(1, d), ln_b.reshape(1, d), ys)
    return out.reshape(b, s, d)


def _slot_layout(counts, n_assign):
    cnt = counts[:, 0].astype(I32)
    padded = (cnt + BLK_E - 1) // BLK_E * BLK_E
    pends = jnp.cumsum(padded)
    pstarts = pends - padded
    n_blocks_max = (n_assign + N_EXPERTS * (BLK_E - 1)) // BLK_E
    n_blocks = (pends[-1] // BLK_E).astype(I32).reshape(1)
    return pstarts, pends, padded // BLK_E, pstarts // BLK_E, n_blocks, n_blocks_max * BLK_E


def kernel(x, c, w_ada, b_ada, w_in, w_out, na_rpb, ret_log_decay, ret_gn_g, ln1_g, ln1_b, ln2_g, ln2_b,
           w_router, router_bias, w_gate, w_up, w_down, ws_gate, ws_up, ws_down):
    b, s, d = x.shape
    depth = w_ada.shape[0]
    alpha = (2.0 * depth) ** 0.25
    t = jnp.arange(s, dtype=F32)
    inv_freq = ROPE_BASE ** (-jnp.arange(0, RET_HEAD_DIM, 2, dtype=F32) / RET_HEAD_DIM)
    ang = t[:, None] * inv_freq[None, :]
    cos, sin = jnp.cos(ang), jnp.sin(ang)
    cos2 = jnp.concatenate([cos, cos], axis=-1)
    sin2 = jnp.concatenate([-sin, sin], axis=-1)
    for l in range(depth):
        mod = _mod(c, w_ada[l], b_ada[l])
        proj = _inproj(x, mod, w_in[l].astype(BF16))
        y_na = _natten(proj, _na_bias_table(na_rpb[l]))
        y_r = _retention(proj, ret_log_decay[l], ret_gn_g[l], cos2, sin2)
        wr_t = w_router[l].T
        wr_hi = wr_t.astype(BF16)
        wr_lo = (wr_t - wr_hi.astype(F32)).astype(BF16)
        x1, hfp, logits_t = _outproj(y_na, y_r, x, mod, w_out[l].astype(BF16), ln1_g[l], ln1_b[l],
                                     wr_hi, wr_lo, alpha)
        idx, wts, rank, counts = _route(logits_t, router_bias[l])
        pstarts, pends, nblk, gstart, n_blocks, n_slots = _slot_layout(counts, b * s * TOP_K)
        xs = _dispatch(idx, rank, pstarts, pends, n_blocks, hfp, n_slots)
        ys = _experts(nblk, gstart, n_blocks, xs, w_gate[l], w_up[l], w_down[l])
        x = _combine(idx, rank, pstarts, wts.T, hfp, x1, mod, ws_gate[l].astype(BF16), ws_up[l].astype(BF16),
                     ws_down[l].astype(BF16), ln2_g[l], ln2_b[l], ys, alpha)
    return x
```

```python
import functools

import jax
import jax.numpy as jnp
import numpy as np
from jax import lax
from jax.experimental import pallas as pl
from jax.experimental.pallas import tpu as pltpu
from jax.experimental.pallas import tpu_sc as plsc

F32 = jnp.float32
BF16 = jnp.bfloat16
U32 = jnp.uint32
I32 = jnp.int32

GRID_W = 64
WIN_R = 8
WIN_C = 16
NA_HEADS = 8
NA_HEAD_DIM = 64
NA_WIDTH = NA_HEADS * NA_HEAD_DIM
RET_HEADS = 4
RET_HEAD_DIM = 128
RET_WIDTH = RET_HEADS * RET_HEAD_DIM
RET_CHUNK = 128
ROPE_BASE = 10000.0
N_EXPERTS = 256
TOP_K = 8
N_GROUPS = 8
TOPK_GROUPS = 4
GROUP_SIZE = N_EXPERTS // N_GROUPS
ROUTED_SCALE = 2.5
LN_EPS = 1e-5
GN_EPS = 1e-6

LANES = 128
VMEM_LIMIT = 56 * 1024 * 1024

TM_PROJ = 512
T_ROUTE = 512
T_DISP = 256
BLK_E = 512
T_COMB = 256
PACK_W = 4
NA_ROWS_PER_ITER = 8
RET_BLOCK = 256
RET_UNROLL = 2
X_RING = 4
Y_RING = 3


def _cparams(sem, vmem=VMEM_LIMIT):
    return pltpu.CompilerParams(dimension_semantics=sem, vmem_limit_bytes=vmem)


def _silu(v):
    return v * jax.nn.sigmoid(v)


def _layer_norm(z, g, b):
    mu = jnp.mean(z, -1, keepdims=True)
    zc = z - mu
    var = jnp.mean(zc * zc, -1, keepdims=True)
    return zc * lax.rsqrt(var + LN_EPS) * g + b


def _pack_rows(v):
    half = v.shape[1] // 2
    vb = v.astype(BF16)
    lo = lax.bitcast_convert_type(vb[:, :half].astype(F32), U32) >> 16
    hi = lax.bitcast_convert_type(vb[:, half:].astype(F32), U32)
    w = hi | lo
    return [w[:, j * LANES:(j + 1) * LANES] for j in range(half // LANES)]


def _unpack_words(words):
    lo = [lax.bitcast_convert_type(w << 16, F32) for w in words]
    hi = [lax.bitcast_convert_type(w & jnp.uint32(0xFFFF0000), F32) for w in words]
    return jnp.concatenate(lo + hi, axis=-1)


def _mod_kernel(c_ref, w_ref, b_ref, o_ref):
    cond = _silu(c_ref[...])
    o_ref[0] = jnp.dot(cond, w_ref[...], precision=lax.Precision.HIGHEST,
                       preferred_element_type=F32) + b_ref[0]


def _mod(c, w_ada, b_ada):
    b, d = c.shape
    n6 = w_ada.shape[1] // d
    out = pl.pallas_call(
        _mod_kernel,
        grid=(n6,),
        in_specs=[pl.BlockSpec((b, d), lambda j: (0, 0)),
                  pl.BlockSpec((d, d), lambda j: (0, j)),
                  pl.BlockSpec((1, 1, d), lambda j: (j, 0, 0))],
        out_specs=pl.BlockSpec((1, b, d), lambda j: (j, 0, 0)),
        out_shape=jax.ShapeDtypeStruct((n6, b, d), F32),
        compiler_params=_cparams(("arbitrary",)),
        name="mod",
    )(c, w_ada, b_ada.reshape(n6, 1, d))
    return out.reshape(n6, b, 1, d)


def _mod_spec(which, d):
    return pl.BlockSpec((None, None, 1, d), lambda b, i, which=which: (which, b, 0, 0))


def _inproj_kernel(x_ref, sc_ref, sh_ref, w_ref, o_ref, *, chunk, q_scale):
    h = (x_ref[...] * (1.0 + sc_ref[...]) + sh_ref[...]).astype(BF16)
    for j in range(o_ref.shape[1] // chunk):
        acc = jnp.dot(h, w_ref[:, j * chunk:(j + 1) * chunk], preferred_element_type=F32)
        if j == 0:
            acc = acc * q_scale
        o_ref[:, j * chunk:(j + 1) * chunk] = acc.astype(o_ref.dtype)


def _inproj(x, mod, w_in_bf):
    b, s, d = x.shape
    e = w_in_bf.shape[1]
    tm = min(TM_PROJ, s)
    return pl.pallas_call(
        functools.partial(_inproj_kernel, chunk=NA_WIDTH, q_scale=NA_HEAD_DIM ** -0.5),
        grid=(b, s // tm),
        in_specs=[pl.BlockSpec((None, tm, d), lambda bi, i: (bi, i, 0)),
                  _mod_spec(1, d), _mod_spec(0, d),
                  pl.BlockSpec((d, e), lambda bi, i: (0, 0))],
        out_specs=pl.BlockSpec((None, tm, e), lambda bi, i: (bi, i, 0)),
        out_shape=jax.ShapeDtypeStruct((b, s, e), BF16),
        compiler_params=_cparams(("parallel", "parallel")),
        name="inproj",
    )(x, mod, mod, w_in_bf)


def _natten_kernel(q_ref, k_ref, v_ref, bias_ref, o_ref, *, rows):
    kspan = WIN_R * GRID_W

    first = lax.broadcasted_iota(I32, (1, LANES), 1) < NA_HEAD_DIM
    zero = jnp.zeros((), BF16)
    one = jnp.ones((), BF16)

    def rows_body(i, carry):
        qrows, krows, scores = [], [], []
        for u in range(NA_ROWS_PER_ITER):
            r = i * NA_ROWS_PER_ITER + u
            rs = jnp.clip(r - WIN_R // 2, 0, rows - WIN_R)
            vi = r - rs
            qrows.append(pl.ds(pl.multiple_of(r * GRID_W, GRID_W), GRID_W))
            krows.append(pl.ds(pl.multiple_of(rs * GRID_W, GRID_W), kspan))
            q = q_ref[qrows[u], :]
            k = k_ref[krows[u], :]
            for j, qh in enumerate((jnp.where(first, q, zero), jnp.where(first, zero, q))):
                s = lax.dot_general(qh, k, (((1,), (1,)), ((), ())), preferred_element_type=F32)
                scores.append(s + bias_ref[j, vi])
        probs = []
        for s in scores:
            m = jnp.max(s, axis=-1, keepdims=True)
            probs.append(jnp.exp(s - m).astype(BF16))
        for u in range(NA_ROWS_PER_ITER):
            v = v_ref[krows[u], :]
            a0 = jnp.dot(probs[2 * u], jnp.where(first, v, one), preferred_element_type=F32)
            a1 = jnp.dot(probs[2 * u + 1], jnp.where(first, one, v), preferred_element_type=F32)
            num = jnp.where(first, a0, a1)
            den = pltpu.roll(jnp.where(first, a1, a0), NA_HEAD_DIM, 1)
            o_ref[qrows[u], :] = (num / den).astype(o_ref.dtype)
        return carry

    lax.fori_loop(0, rows // NA_ROWS_PER_ITER, rows_body, 0)


def _na_bias_table(rpb):
    w = GRID_W
    cq = jnp.arange(w)
    cs = jnp.clip(cq - WIN_C // 2, 0, w - WIN_C)
    ck = jnp.arange(w)
    col_in = (ck[None, :] >= cs[:, None]) & (ck[None, :] < cs[:, None] + WIN_C)
    dc_idx = jnp.clip(ck[None, :] - cq[:, None] + WIN_C - 1, 0, 2 * WIN_C - 2)
    t = rpb[:, :, dc_idx]
    t = jnp.where(col_in[None, None], t, -jnp.inf)
    vi = jnp.arange(WIN_R)
    kr = jnp.arange(WIN_R)
    dr = kr[None, :] - vi[:, None] + WIN_R - 1
    tb = t[:, dr]
    return tb.transpose(0, 1, 3, 2, 4).reshape(rpb.shape[0], WIN_R, w, WIN_R * w).astype(F32)


def _natten(proj, bias_tab):
    b, s, _ = proj.shape
    rows = s // GRID_W
    hp = LANES // NA_HEAD_DIM
    npair = NA_HEADS // hp
    blk = lambda off: pl.BlockSpec((None, s, LANES), lambda bi, p, off=off: (bi, 0, off + p))
    return pl.pallas_call(
        functools.partial(_natten_kernel, rows=rows),
        grid=(b, npair),
        in_specs=[blk(0), blk(npair), blk(2 * npair),
                  pl.BlockSpec((hp, WIN_R, GRID_W, WIN_R * GRID_W), lambda bi, p: (p, 0, 0, 0))],
        out_specs=pl.BlockSpec((None, s, LANES), lambda bi, p: (bi, 0, p)),
        out_shape=jax.ShapeDtypeStruct((b, s, NA_WIDTH), BF16),
        compiler_params=_cparams(("parallel", "parallel")),
        name="natten",
    )(proj, proj, proj, bias_tab)


def _retent_kernel(ld_ref, q_ref, k_ref, v_ref, g_ref, cos_ref, sin_ref, gn_ref, o_ref,
                   qs_ref, ks_ref, sb_ref, *, nchunk):
    c = RET_BLOCK
    dh = RET_HEAD_DIM
    h = pl.program_id(1)
    lgf = ld_ref[0, h]
    lgb = ld_ref[1, h]

    cos2 = cos_ref[...]
    sin2 = sin_ref[...]
    qf = q_ref[...].astype(F32)
    qs_ref[...] = qf * cos2 + pltpu.roll(qf, dh // 2, 1) * sin2
    kf = k_ref[...].astype(F32)
    ks_ref[...] = (kf * cos2 + pltpu.roll(kf, dh // 2, 1) * sin2) * (dh ** -0.5)

    ic = lax.broadcasted_iota(I32, (c, 1), 0).astype(F32)
    ir = lax.broadcasted_iota(I32, (1, c), 1).astype(F32)
    diff = ic - ir
    dmat = jnp.where(diff >= 0, jnp.exp(jnp.maximum(diff, 0.0) * lgf),
                     jnp.exp(jnp.maximum(-diff, 0.0) * lgb))
    kdec_f = jnp.exp((c - 1 - ic) * lgf)
    qdec_f = jnp.exp((ic + 1) * lgf)
    kdec_b = jnp.exp(ic * lgb)
    qdec_b = jnp.exp((c - ic) * lgb)
    one = jnp.ones((1, 1), F32)
    cdec_f = jnp.exp(one * (c * lgf))
    cdec_b = jnp.exp(one * (c * lgb))
    tn = (((0,), (0,)), ((), ()))

    def bwd_body(i, sb):
        n = nchunk - 1 - i
        sb_ref[n] = sb
        rows = pl.ds(pl.multiple_of(n * c, c), c)
        kd = (ks_ref[rows, :] * kdec_b).astype(BF16)
        kv = lax.dot_general(kd, v_ref[rows, :], tn, preferred_element_type=F32)
        return cdec_b * sb + kv

    lax.fori_loop(0, nchunk, bwd_body, jnp.zeros((dh, dh), F32), unroll=RET_UNROLL)

    gn = gn_ref[...]

    def fwd_body(n, sf):
        rows = pl.ds(pl.multiple_of(n * c, c), c)
        qn = qs_ref[rows, :]
        kn = ks_ref[rows, :]
        vn = v_ref[rows, :]
        sc = lax.dot_general(qn.astype(BF16), kn.astype(BF16), (((1,), (1,)), ((), ())),
                             preferred_element_type=F32) * dmat
        y = jnp.dot(sc.astype(BF16), vn, preferred_element_type=F32)
        qd = jnp.concatenate([qn * qdec_f, qn * qdec_b], axis=1).astype(BF16)
        st = jnp.concatenate([sf, sb_ref[n]], axis=0).astype(BF16)
        y = y + jnp.dot(qd, st, preferred_element_type=F32)
        mu = jnp.mean(y, -1, keepdims=True)
        yc = y - mu
        var = jnp.mean(yc * yc, -1, keepdims=True)
        yn = yc * lax.rsqrt(var + GN_EPS) * gn
        o_ref[rows, :] = (_silu(g_ref[rows, :].astype(F32)) * yn).astype(o_ref.dtype)
        kv = lax.dot_general((kn * kdec_f).astype(BF16), vn, tn, preferred_element_type=F32)
        return cdec_f * sf + kv

    lax.fori_loop(0, nchunk, fwd_body, jnp.zeros((dh, dh), F32), unroll=RET_UNROLL)


def _retention(proj, log_decay, gn_g, cos2, sin2):
    b, s, _ = proj.shape
    dh = RET_HEAD_DIM
    nchunk = s // RET_BLOCK
    base = 3 * NA_WIDTH // dh
    blk = lambda off: pl.BlockSpec((None, s, dh), lambda bi, h, off=off: (bi, 0, base + off + h))
    full = pl.BlockSpec((s, dh), lambda bi, h: (0, 0))
    return pl.pallas_call(
        functools.partial(_retent_kernel, nchunk=nchunk),
        grid=(b, RET_HEADS),
        in_specs=[pl.BlockSpec(memory_space=pltpu.SMEM),
                  blk(0), blk(RET_HEADS), blk(2 * RET_HEADS), blk(3 * RET_HEADS),
                  full, full,
                  pl.BlockSpec((1, dh), lambda bi, h: (0, h))],
        out_specs=pl.BlockSpec((None, s, dh), lambda bi, h: (bi, 0, h)),
        out_shape=jax.ShapeDtypeStruct((b, s, RET_WIDTH), BF16),
        scratch_shapes=[pltpu.VMEM((s, dh), F32), pltpu.VMEM((s, dh), F32),
                        pltpu.VMEM((nchunk, dh, dh), F32)],
        compiler_params=_cparams(("parallel", "parallel")),
        name="retent",
    )(log_decay, proj, proj, proj, proj, cos2, sin2, gn_g.reshape(1, RET_WIDTH))


def _outproj_kernel(yna_ref, yr_ref, x_ref, ga_ref, sf_ref, shf_ref, wo1_ref, wo2_ref, g_ref, b_ref,
                    wrh_ref, wrl_ref, x1_ref, hfp_ref, lg_ref, *, alpha):
    mix = jnp.dot(yna_ref[...], wo1_ref[...], preferred_element_type=F32)
    mix = mix + jnp.dot(yr_ref[...], wo2_ref[...], preferred_element_type=F32)
    x1 = _layer_norm(alpha * x_ref[...] + ga_ref[...] * mix, g_ref[...], b_ref[...])
    x1_ref[...] = x1
    hf = x1 * (1.0 + sf_ref[...]) + shf_ref[...]
    for j, w in enumerate(_pack_rows(hf)):
        hfp_ref[j] = w
    hb = hf.astype(BF16)
    hl = (hf - hb.astype(F32)).astype(BF16)
    nt = (((1,), (1,)), ((), ()))
    lg = lax.dot_general(wrh_ref[...], hb, nt, preferred_element_type=F32)
    lg = lg + lax.dot_general(wrh_ref[...], hl, nt, preferred_element_type=F32)
    lg = lg + lax.dot_general(wrl_ref[...], hb, nt, preferred_element_type=F32)
    lg_ref[...] = lg


def _outproj(y_na, y_r, x, mod, w_out_bf, ln_g, ln_b, wr_hi, wr_lo, alpha):
    b, s, d = x.shape
    tm = min(TM_PROJ, s)
    nt = s // tm
    ne = wr_hi.shape[0]
    const = lambda shape: pl.BlockSpec(shape, lambda bi, i: tuple(0 for _ in shape))
    x1, hfp, lg = pl.pallas_call(
        functools.partial(_outproj_kernel, alpha=alpha),
        grid=(b, nt),
        in_specs=[pl.BlockSpec((None, tm, NA_WIDTH), lambda bi, i: (bi, i, 0)),
                  pl.BlockSpec((None, tm, RET_WIDTH), lambda bi, i: (bi, i, 0)),
                  pl.BlockSpec((None, tm, d), lambda bi, i: (bi, i, 0)),
                  _mod_spec(2, d), _mod_spec(4, d), _mod_spec(3, d),
                  pl.BlockSpec((NA_WIDTH, d), lambda bi, i: (0, 0)),
                  pl.BlockSpec((RET_WIDTH, d), lambda bi, i: (1, 0)),
                  const((1, d)), const((1, d)), const((ne, d)), const((ne, d))],
        out_specs=[pl.BlockSpec((None, tm, d), lambda bi, i: (bi, i, 0)),
                   pl.BlockSpec((PACK_W, tm, LANES), lambda bi, i: (0, bi * nt + i, 0)),
                   pl.BlockSpec((ne, tm), lambda bi, i: (0, bi * nt + i))],
        out_shape=[jax.ShapeDtypeStruct((b, s, d), F32),
                   jax.ShapeDtypeStruct((PACK_W, b * s, LANES), U32),
                   jax.ShapeDtypeStruct((ne, b * s), F32)],
        compiler_params=_cparams(("parallel", "parallel")),
        name="outproj",
    )(y_na, y_r, x, mod, mod, mod, w_out_bf, w_out_bf, ln_g.reshape(1, d), ln_b.reshape(1, d), wr_hi, wr_lo)
    return x1, hfp, lg


def _route_kernel(lg_ref, rb_ref, idx_ref, w_ref, rank_ref, cnt_ref):
    t = lg_ref.shape[1]
    ninf = -jnp.inf

    @pl.when(pl.program_id(0) == 0)
    def _():
        cnt_ref[...] = jnp.zeros_like(cnt_ref)

    scores = jax.nn.sigmoid(lg_ref[...])
    sel = scores + rb_ref[...]

    io_g = lax.broadcasted_iota(I32, (GROUP_SIZE, t), 0)
    gs_rows = []
    for g in range(N_GROUPS):
        blk = sel[g * GROUP_SIZE:(g + 1) * GROUP_SIZE, :]
        m1 = jnp.max(blk, axis=0, keepdims=True)
        i1 = jnp.min(jnp.where(blk == m1, io_g, GROUP_SIZE), axis=0, keepdims=True)
        m2 = jnp.max(jnp.where(io_g == i1, ninf, blk), axis=0, keepdims=True)
        gs_rows.append(m1 + m2)
    gs = jnp.concatenate(gs_rows, axis=0)

    io8 = lax.broadcasted_iota(I32, (N_GROUPS, t), 0)
    gsel = jnp.zeros((N_GROUPS, t), F32)
    for _ in range(TOPK_GROUPS):
        m = jnp.max(gs, axis=0, keepdims=True)
        gi = jnp.min(jnp.where(gs == m, io8, N_GROUPS), axis=0, keepdims=True)
        hit = io8 == gi
        gsel = jnp.where(hit, 1.0, gsel)
        gs = jnp.where(hit, ninf, gs)

    masked = jnp.concatenate(
        [jnp.where(gsel[g:g + 1, :] > 0.0, sel[g * GROUP_SIZE:(g + 1) * GROUP_SIZE, :], ninf)
         for g in range(N_GROUPS)], axis=0)

    io_e = lax.broadcasted_iota(I32, (N_EXPERTS, t), 0)
    chosen = jnp.zeros((N_EXPERTS, t), F32)
    idx_rows, w_rows = [], []
    for _ in range(TOP_K):
        m = jnp.max(masked, axis=0, keepdims=True)
        ei = jnp.min(jnp.where(masked == m, io_e, N_EXPERTS), axis=0, keepdims=True)
        hit = io_e == ei
        w_rows.append(jnp.sum(jnp.where(hit, scores, 0.0), axis=0, keepdims=True))
        idx_rows.append(ei)
        chosen = jnp.where(hit, 1.0, chosen)
        masked = jnp.where(hit, ninf, masked)
    wk = jnp.concatenate(w_rows, axis=0)
    w_ref[...] = wk / jnp.sum(wk, axis=0, keepdims=True) * ROUTED_SCALE
    idx_ref[...] = jnp.concatenate(idx_rows, axis=0)

    upper = (lax.broadcasted_iota(I32, (t, t), 0) < lax.broadcasted_iota(I32, (t, t), 1))
    prefix = jnp.dot(chosen.astype(BF16), upper.astype(BF16), preferred_element_type=F32)
    rank_full = prefix + cnt_ref[...]
    rank_rows = [jnp.sum(jnp.where(io_e == ei, rank_full, 0.0), axis=0, keepdims=True) for ei in idx_rows]
    rank_ref[...] = jnp.concatenate(rank_rows, axis=0).astype(I32)
    cnt_ref[...] += jnp.sum(chosen, axis=1, keepdims=True)


def _route(logits_t, router_bias):
    ne, n = logits_t.shape
    t = min(T_ROUTE, n)
    kspec = pl.BlockSpec((TOP_K, t), lambda i: (0, i))
    return pl.pallas_call(
        _route_kernel,
        grid=(n // t,),
        in_specs=[pl.BlockSpec((ne, t), lambda i: (0, i)),
                  pl.BlockSpec((ne, 1), lambda i: (0, 0))],
        out_specs=[kspec, kspec, kspec, pl.BlockSpec((ne, 1), lambda i: (0, 0))],
        out_shape=[jax.ShapeDtypeStruct((TOP_K, n), I32),
                   jax.ShapeDtypeStruct((TOP_K, n), F32),
                   jax.ShapeDtypeStruct((TOP_K, n), I32),
                   jax.ShapeDtypeStruct((ne, 1), F32)],
        compiler_params=_cparams(("arbitrary",)),
        name="route",
    )(logits_t, router_bias.reshape(ne, 1))


def _row(ref, r):
    return ref.at[:, pl.ds(r, 1), :]


def _rows_wait(ref, n_rows, sem):
    span = ref.at[:, pl.ds(0, n_rows), :]
    pltpu.make_async_copy(span, span, sem).wait()


def _block(ref, g):
    return ref.at[:, pl.ds(pl.multiple_of(g * BLK_E, BLK_E), BLK_E), :]


def _dispatch_kernel(idx_ref, rank_ref, pstart_ref, pend_ref, nb_ref, hfp_ref, xs_ref, zero_ref, sem, zsem):
    td = hfp_ref.shape[1]
    nb_max = xs_ref.shape[1] // BLK_E

    @pl.when(pl.program_id(0) == 0)
    def _():
        zero_ref[...] = jnp.zeros_like(zero_ref)

        def fill_expert(e, cnt):
            nonempty = pend_ref[e] > pstart_ref[e]

            @pl.when(nonempty)
            def _():
                pltpu.make_async_copy(zero_ref, _block(xs_ref, pend_ref[e] // BLK_E - 1), zsem).start()

            return cnt + nonempty.astype(I32)

        n_fill = lax.fori_loop(0, N_EXPERTS, fill_expert, 0)

        def fill_tail(g, carry):
            pltpu.make_async_copy(zero_ref, _block(xs_ref, g), zsem).start()
            return carry

        lax.fori_loop(nb_ref[0], nb_max, fill_tail, 0)

        def drain(g, carry):
            pltpu.make_async_copy(zero_ref, _block(xs_ref, 0), zsem).wait()
            return carry

        lax.fori_loop(0, n_fill + nb_max - nb_ref[0], drain, 0)

    def issue(t, carry):
        for k in range(TOP_K):
            d = pstart_ref[idx_ref[k, t]] + rank_ref[k, t]
            pltpu.make_async_copy(_row(hfp_ref, t), _row(xs_ref, d), sem).start()
        return carry

    lax.fori_loop(0, td, issue, 0, unroll=2)
    _rows_wait(xs_ref, TOP_K * td, sem)


def _dispatch(idx, rank, pstarts, pends, n_blocks, hfp, n_slots):
    n = hfp.shape[1]
    td = min(T_DISP, n)
    kspec = pl.BlockSpec((TOP_K, td), lambda i: (0, i), memory_space=pltpu.SMEM)
    smem = pl.BlockSpec(memory_space=pltpu.SMEM)
    return pl.pallas_call(
        _dispatch_kernel,
        grid=(n // td,),
        in_specs=[kspec, kspec, smem, smem, smem,
                  pl.BlockSpec((PACK_W, td, LANES), lambda i: (0, i, 0))],
        out_specs=pl.BlockSpec(memory_space=pl.ANY),
        out_shape=jax.ShapeDtypeStruct((PACK_W, n_slots, LANES), U32),
        scratch_shapes=[pltpu.VMEM((PACK_W, BLK_E, LANES), U32),
                        pltpu.SemaphoreType.DMA(()), pltpu.SemaphoreType.DMA(())],
        compiler_params=_cparams(("arbitrary",)),
        name="dispatch",
    )(idx, rank, pstarts, pends, n_blocks, hfp)


SC_WINDOW = 128


def _dispatch_sc(dest_rows, hfp, n_slots):
    n_rows = hfp.shape[0]
    mesh = plsc.VectorSubcoreMesh(core_axis_name="core", subcore_axis_name="subcore")

    @functools.partial(pl.kernel, mesh=mesh, scratch_types=[],
                       out_type=jax.ShapeDtypeStruct((PACK_W * n_slots, LANES), I32))
    def scatter_rows(x_hbm, i_hbm, o_hbm):
        def body(x_vmem, i_vmem):
            for k in range(TOP_K):
                pltpu.sync_copy(x_vmem, o_hbm.at[i_vmem.at[k]])

        pltpu.emit_pipeline(
            body,
            grid=(n_rows // SC_WINDOW,),
            in_specs=[pl.BlockSpec((SC_WINDOW, LANES), lambda i: (i, 0)),
                      pl.BlockSpec((TOP_K, SC_WINDOW), lambda i: (0, i))],
            out_specs=[],
            core_axis_name=("core", "subcore"),
            dimension_semantics=(pltpu.PARALLEL,),
        )(x_hbm, i_hbm)

    return scatter_rows(hfp, dest_rows)


def _experts_kernel(nblk_ref, gstart_ref, cnt_ref, nb_ref, xs_ref, wg_ref, wu_ref, wd_ref, ys_ref,
                    xbuf_ref, ybuf_ref, wgb_ref, wub_ref, wdb_ref, xsem, ysem):
    e = pl.program_id(0)
    total = nb_ref[0]
    nb_max = ys_ref.shape[1] // BLK_E

    def x_copy(g):
        slot = g % X_RING
        return pltpu.make_async_copy(_block(xs_ref, g), xbuf_ref.at[slot], xsem.at[slot])

    def y_copy(g):
        slot = g % Y_RING
        return pltpu.make_async_copy(ybuf_ref.at[slot], _block(ys_ref, g), ysem.at[slot])

    @pl.when(e == 0)
    def _():
        for g in range(X_RING - 1):
            @pl.when(g < total)
            def _():
                x_copy(g).start()

    @pl.when(nblk_ref[e] > 0)
    def _():
        wgb_ref[...] = wg_ref[...].astype(BF16)
        wub_ref[...] = wu_ref[...].astype(BF16)
        wdb_ref[...] = wd_ref[...].astype(BF16)

    def block(b, carry):
        g = gstart_ref[e] + b
        x_copy(g).wait()

        @pl.when(g + X_RING - 1 < total)
        def _():
            x_copy(g + X_RING - 1).start()

        xb = _unpack_words([xbuf_ref[g % X_RING, j] for j in range(PACK_W)]).astype(BF16)
        live = lax.broadcasted_iota(I32, (BLK_E, 1), 0) < cnt_ref[e] - b * BLK_E
        xb = jnp.where(live, xb, jnp.zeros((), BF16))
        hg = jnp.dot(xb, wgb_ref[...], preferred_element_type=F32)
        hu = jnp.dot(xb, wub_ref[...], preferred_element_type=F32)
        act = (_silu(hg) * hu).astype(BF16)
        y = jnp.dot(act, wdb_ref[...], preferred_element_type=F32)
        words = _pack_rows(y)

        @pl.when(g >= Y_RING)
        def _():
            y_copy(g - Y_RING).wait()

        for j, w in enumerate(words):
            ybuf_ref[g % Y_RING, j] = w
        y_copy(g).start()
        return carry

    lax.fori_loop(0, nblk_ref[e], block, 0)

    @pl.when(e == pl.num_programs(0) - 1)
    def _():
        for back in range(Y_RING):
            @pl.when(total - 1 - back >= 0)
            def _():
                y_copy(total - 1 - back).wait()

        ybuf_ref[0] = jnp.zeros(ybuf_ref.shape[1:], ybuf_ref.dtype)

        def tail_copy(g):
            return pltpu.make_async_copy(ybuf_ref.at[0], _block(ys_ref, g), ysem.at[0])

        def fill(g, carry):
            tail_copy(g).start()
            return carry

        lax.fori_loop(total, nb_max, fill, 0)

        def drain(g, carry):
            tail_copy(g).wait()
            return carry

        lax.fori_loop(total, nb_max, drain, 0)


def _experts(nblk, gstart, cnt, n_blocks, xs, w_gate, w_up, w_down):
    p = xs.shape[1]
    ne, d, f = w_gate.shape
    wspec = lambda shape: pl.BlockSpec((None,) + shape, lambda e, *_: (e, 0, 0))
    grid_spec = pltpu.PrefetchScalarGridSpec(
        num_scalar_prefetch=4,
        grid=(ne,),
        in_specs=[pl.BlockSpec(memory_space=pl.ANY), wspec((d, f)), wspec((d, f)), wspec((f, d))],
        out_specs=pl.BlockSpec(memory_space=pl.ANY),
        scratch_shapes=[pltpu.VMEM((X_RING, PACK_W, BLK_E, LANES), U32),
                        pltpu.VMEM((Y_RING, PACK_W, BLK_E, LANES), U32),
                        pltpu.VMEM((d, f), BF16), pltpu.VMEM((d, f), BF16), pltpu.VMEM((f, d), BF16),
                        pltpu.SemaphoreType.DMA((X_RING,)), pltpu.SemaphoreType.DMA((Y_RING,))],
    )
    return pl.pallas_call(
        _experts_kernel,
        grid_spec=grid_spec,
        out_shape=jax.ShapeDtypeStruct((PACK_W, p, LANES), U32),
        compiler_params=_cparams(("arbitrary",)),
        name="experts",
    )(nblk, gstart, cnt, n_blocks, xs, w_gate, w_up, w_down)


def _combine_kernel(idx_ref, rank_ref, idxn_ref, rankn_ref, pstart_ref, wt_ref, hfp_ref, x1_ref, gf_ref,
                    wsg_ref, wsu_ref, wsd_ref, g_ref, b_ref, ys_ref, o_ref, buf_ref, sems, *, alpha):
    tc = hfp_ref.shape[1]
    i = pl.program_id(0)
    slot = i % 2

    def gather(iref, rref, s):
        def issue(t, carry):
            for k in range(TOP_K):
                d = pstart_ref[iref[k, t]] + rref[k, t]
                pltpu.make_async_copy(_row(ys_ref, d), buf_ref.at[s, k, :, pl.ds(t, 1), :], sems.at[s]).start()
            return carry

        lax.fori_loop(0, tc, issue, 0, unroll=2)

    @pl.when(i == 0)
    def _():
        gather(idx_ref, rank_ref, 0)

    @pl.when(i + 1 < pl.num_programs(0))
    def _():
        gather(idxn_ref, rankn_ref, 1 - slot)

    hb = _unpack_words([hfp_ref[j] for j in range(PACK_W)]).astype(BF16)
    sg = jnp.dot(hb, wsg_ref[...], preferred_element_type=F32)
    su = jnp.dot(hb, wsu_ref[...], preferred_element_type=F32)
    ffn = jnp.dot((_silu(sg) * su).astype(BF16), wsd_ref[...], preferred_element_type=F32)

    cur = buf_ref.at[slot]
    pltpu.make_async_copy(cur, cur, sems.at[slot]).wait()

    wt = wt_ref[...]
    for k in range(TOP_K):
        yk = _unpack_words([buf_ref[slot, k, j] for j in range(PACK_W)])
        ffn = ffn + wt[:, k:k + 1] * yk
    o_ref[...] = _layer_norm(alpha * x1_ref[...] + gf_ref[...] * ffn, g_ref[...], b_ref[...])


def _combine(idx, rank, pstarts, w_tok, hfp, x1, mod, ws_gate_bf, ws_up_bf, ws_down_bf, ln_g, ln_b, ys, alpha):
    b, s, d = x1.shape
    n = b * s
    tc = min(T_COMB, s)
    nt = s // tc
    ntile = n // tc
    fs = ws_gate_bf.shape[1]
    const = lambda shape: pl.BlockSpec(shape, lambda i: tuple(0 for _ in shape))
    kcur = pl.BlockSpec((TOP_K, tc), lambda i: (0, i), memory_space=pltpu.SMEM)
    knext = pl.BlockSpec((TOP_K, tc), lambda i: (0, jnp.minimum(i + 1, ntile - 1)), memory_space=pltpu.SMEM)
    out = pl.pallas_call(
        functools.partial(_combine_kernel, alpha=alpha),
        grid=(ntile,),
        in_specs=[kcur, kcur, knext, knext, pl.BlockSpec(memory_space=pltpu.SMEM),
                  pl.BlockSpec((tc, TOP_K), lambda i: (i, 0)),
                  pl.BlockSpec((PACK_W, tc, LANES), lambda i: (0, i, 0)),
                  pl.BlockSpec((tc, d), lambda i: (i, 0)),
                  pl.BlockSpec((None, None, 1, d), lambda i: (5, i // nt, 0, 0)),
                  const((d, fs)), const((d, fs)), const((fs, d)), const((1, d)), const((1, d)),
                  pl.BlockSpec(memory_space=pl.ANY)],
        out_specs=pl.BlockSpec((tc, d), lambda i: (i, 0)),
        out_shape=jax.ShapeDtypeStruct((n, d), F32),
        scratch_shapes=[pltpu.VMEM((2, TOP_K, PACK_W, tc, LANES), U32), pltpu.SemaphoreType.DMA((2,))],
        compiler_params=_cparams(("arbitrary",)),
        name="combine",
    )(idx, rank, idx, rank, pstarts, w_tok, hfp, x1.reshape(n, d), mod, ws_gate_bf, ws_up_bf, ws_down_bf,
      ln_g.reshape(1, d), ln_b.reshape(1, d), ys)
    return out.reshape(b, s, d)


def _slot_layout(counts, n_assign):
    cnt = counts[:, 0].astype(I32)
    padded = (cnt + BLK_E - 1) // BLK_E * BLK_E
    pends = jnp.cumsum(padded)
    pstarts = pends - padded
    n_blocks_max = (n_assign + N_EXPERTS * (BLK_E - 1)) // BLK_E
    n_blocks = (pends[-1] // BLK_E).astype(I32).reshape(1)
    return pstarts, pends, padded // BLK_E, pstarts // BLK_E, n_blocks, n_blocks_max * BLK_E


def kernel(x, c, w_ada, b_ada, w_in, w_out, na_rpb, ret_log_decay, ret_gn_g, ln1_g, ln1_b, ln2_g, ln2_b,
           w_router, router_bias, w_gate, w_up, w_down, ws_gate, ws_up, ws_down):
    b, s, d = x.shape
    depth = w_ada.shape[0]
    alpha = (2.0 * depth) ** 0.25
    t = jnp.arange(s, dtype=F32)
    inv_freq = ROPE_BASE ** (-jnp.arange(0, RET_HEAD_DIM, 2, dtype=F32) / RET_HEAD_DIM)
    ang = t[:, None] * inv_freq[None, :]
    cos, sin = jnp.cos(ang), jnp.sin(ang)
    cos2 = jnp.concatenate([cos, cos], axis=-1)
    sin2 = jnp.concatenate([-sin, sin], axis=-1)
    for l in range(depth):
        mod = _mod(c, w_ada[l], b_ada[l])
        proj = _inproj(x, mod, w_in[l].astype(BF16))
        y_na = _natten(proj, _na_bias_table(na_rpb[l]))
        y_r = _retention(proj, ret_log_decay[l], ret_gn_g[l], cos2, sin2)
        wr_t = w_router[l].T
        wr_hi = wr_t.astype(BF16)
        wr_lo = (wr_t - wr_hi.astype(F32)).astype(BF16)
        x1, hfp, logits_t = _outproj(y_na, y_r, x, mod, w_out[l].astype(BF16), ln1_g[l], ln1_b[l],
                                     wr_hi, wr_lo, alpha)
        idx, wts, rank, counts = _route(logits_t, router_bias[l])
        pstarts, pends, nblk, gstart, n_blocks, n_slots = _slot_layout(counts, b * s * TOP_K)
        cnt = counts[:, 0].astype(I32)
        dest = pstarts[idx] + rank
        plane_off = jnp.arange(PACK_W, dtype=I32) * n_slots
        dest_rows = (dest[:, None, :] + plane_off[None, :, None]).reshape(TOP_K, PACK_W * b * s)
        xs = _dispatch_sc(dest_rows, lax.bitcast_convert_type(hfp, I32).reshape(PACK_W * b * s, LANES), n_slots)
        xs = lax.bitcast_convert_type(xs, U32).reshape(PACK_W, n_slots, LANES)
        ys = _experts(nblk, gstart, cnt, n_blocks, xs, w_gate[l], w_up[l], w_down[l])
        x = _combine(idx, rank, pstarts, wts.T, hfp, x1, mod, ws_gate[l].astype(BF16), ws_up[l].astype(BF16),
                     ws_down[l].astype(BF16), ln2_g[l], ln2_b[l], ys, alpha)
    return x
```

```python
import functools

import jax
import jax.numpy as jnp
import numpy as np
from jax import lax
from jax.experimental import pallas as pl
from jax.experimental.pallas import tpu as pltpu
from jax.experimental.pallas import tpu_sc as plsc

F32 = jnp.float32
BF16 = jnp.bfloat16
U32 = jnp.uint32
I32 = jnp.int32

GRID_W = 64
WIN_R = 8
WIN_C = 16
NA_HEADS = 8
NA_HEAD_DIM = 64
NA_WIDTH = NA_HEADS * NA_HEAD_DIM
RET_HEADS = 4
RET_HEAD_DIM = 128
RET_WIDTH = RET_HEADS * RET_HEAD_DIM
RET_CHUNK = 128
ROPE_BASE = 10000.0
N_EXPERTS = 256
TOP_K = 8
N_GROUPS = 8
TOPK_GROUPS = 4
GROUP_SIZE = N_EXPERTS // N_GROUPS
ROUTED_SCALE = 2.5
LN_EPS = 1e-5
GN_EPS = 1e-6

LANES = 128
VMEM_LIMIT = 56 * 1024 * 1024

TM_PROJ = 512
T_ROUTE = 512
T_SLOT = 1024
SC_WINDOW = 128
BLK_E = 512
T_COMB = 256
PACK_W = 4
NA_ROWS_PER_ITER = 8
RET_BLOCK = 256
RET_UNROLL = 2
X_RING = 4
Y_RING = 3


def _cparams(sem, vmem=VMEM_LIMIT):
    return pltpu.CompilerParams(dimension_semantics=sem, vmem_limit_bytes=vmem)


def _silu(v):
    return v * jax.nn.sigmoid(v)


def _layer_norm(z, g, b):
    mu = jnp.mean(z, -1, keepdims=True)
    zc = z - mu
    var = jnp.mean(zc * zc, -1, keepdims=True)
    return zc * lax.rsqrt(var + LN_EPS) * g + b


def _pack_rows(v):
    half = v.shape[1] // 2
    vb = v.astype(BF16)
    lo = lax.bitcast_convert_type(vb[:, :half].astype(F32), U32) >> 16
    hi = lax.bitcast_convert_type(vb[:, half:].astype(F32), U32)
    w = lax.bitcast_convert_type(hi | lo, I32)
    return [w[:, j * LANES:(j + 1) * LANES] for j in range(half // LANES)]


def _unpack_words(words):
    words = [lax.bitcast_convert_type(w, U32) for w in words]
    lo = [lax.bitcast_convert_type(w << 16, F32) for w in words]
    hi = [lax.bitcast_convert_type(w & jnp.uint32(0xFFFF0000), F32) for w in words]
    return jnp.concatenate(lo + hi, axis=-1)


def _mod_kernel(c_ref, w_ref, b_ref, o_ref):
    cond = _silu(c_ref[...])
    o_ref[0] = jnp.dot(cond, w_ref[...], precision=lax.Precision.HIGHEST,
                       preferred_element_type=F32) + b_ref[0]


def _mod(c, w_ada, b_ada):
    b, d = c.shape
    n6 = w_ada.shape[1] // d
    out = pl.pallas_call(
        _mod_kernel,
        grid=(n6,),
        in_specs=[pl.BlockSpec((b, d), lambda j: (0, 0)),
                  pl.BlockSpec((d, d), lambda j: (0, j)),
                  pl.BlockSpec((1, 1, d), lambda j: (j, 0, 0))],
        out_specs=pl.BlockSpec((1, b, d), lambda j: (j, 0, 0)),
        out_shape=jax.ShapeDtypeStruct((n6, b, d), F32),
        compiler_params=_cparams(("arbitrary",)),
        name="mod",
    )(c, w_ada, b_ada.reshape(n6, 1, d))
    return out.reshape(n6, b, 1, d)


def _mod_spec(which, d):
    return pl.BlockSpec((None, None, 1, d), lambda b, i, which=which: (which, b, 0, 0))


def _inproj_kernel(x_ref, sc_ref, sh_ref, w_ref, o_ref, *, chunk, q_scale):
    h = (x_ref[...] * (1.0 + sc_ref[...]) + sh_ref[...]).astype(BF16)
    for j in range(o_ref.shape[1] // chunk):
        acc = jnp.dot(h, w_ref[:, j * chunk:(j + 1) * chunk], preferred_element_type=F32)
        if j == 0:
            acc = acc * q_scale
        o_ref[:, j * chunk:(j + 1) * chunk] = acc.astype(o_ref.dtype)


def _inproj(x, mod, w_in_bf):
    b, s, d = x.shape
    e = w_in_bf.shape[1]
    tm = min(TM_PROJ, s)
    return pl.pallas_call(
        functools.partial(_inproj_kernel, chunk=NA_WIDTH, q_scale=NA_HEAD_DIM ** -0.5),
        grid=(b, s // tm),
        in_specs=[pl.BlockSpec((None, tm, d), lambda bi, i: (bi, i, 0)),
                  _mod_spec(1, d), _mod_spec(0, d),
                  pl.BlockSpec((d, e), lambda bi, i: (0, 0))],
        out_specs=pl.BlockSpec((None, tm, e), lambda bi, i: (bi, i, 0)),
        out_shape=jax.ShapeDtypeStruct((b, s, e), BF16),
        compiler_params=_cparams(("parallel", "parallel")),
        name="inproj",
    )(x, mod, mod, w_in_bf)


def _natten_kernel(q_ref, k_ref, v_ref, bias_ref, o_ref, *, rows):
    kspan = WIN_R * GRID_W

    first = lax.broadcasted_iota(I32, (1, LANES), 1) < NA_HEAD_DIM
    zero = jnp.zeros((), BF16)
    one = jnp.ones((), BF16)

    def rows_body(i, carry):
        qrows, krows, scores = [], [], []
        for u in range(NA_ROWS_PER_ITER):
            r = i * NA_ROWS_PER_ITER + u
            rs = jnp.clip(r - WIN_R // 2, 0, rows - WIN_R)
            vi = r - rs
            qrows.append(pl.ds(pl.multiple_of(r * GRID_W, GRID_W), GRID_W))
            krows.append(pl.ds(pl.multiple_of(rs * GRID_W, GRID_W), kspan))
            q = q_ref[qrows[u], :]
            k = k_ref[krows[u], :]
            for j, qh in enumerate((jnp.where(first, q, zero), jnp.where(first, zero, q))):
                s = lax.dot_general(qh, k, (((1,), (1,)), ((), ())), preferred_element_type=F32)
                scores.append(s + bias_ref[j, vi])
        probs = []
        for s in scores:
            m = jnp.max(s, axis=-1, keepdims=True)
            probs.append(jnp.exp(s - m).astype(BF16))
        for u in range(NA_ROWS_PER_ITER):
            v = v_ref[krows[u], :]
            a0 = jnp.dot(probs[2 * u], jnp.where(first, v, one), preferred_element_type=F32)
            a1 = jnp.dot(probs[2 * u + 1], jnp.where(first, one, v), preferred_element_type=F32)
            num = jnp.where(first, a0, a1)
            den = pltpu.roll(jnp.where(first, a1, a0), NA_HEAD_DIM, 1)
            o_ref[qrows[u], :] = (num / den).astype(o_ref.dtype)
        return carry

    lax.fori_loop(0, rows // NA_ROWS_PER_ITER, rows_body, 0)


def _na_bias_table(rpb):
    w = GRID_W
    cq = jnp.arange(w)
    cs = jnp.clip(cq - WIN_C // 2, 0, w - WIN_C)
    ck = jnp.arange(w)
    col_in = (ck[None, :] >= cs[:, None]) & (ck[None, :] < cs[:, None] + WIN_C)
    dc_idx = jnp.clip(ck[None, :] - cq[:, None] + WIN_C - 1, 0, 2 * WIN_C - 2)
    t = rpb[:, :, dc_idx]
    t = jnp.where(col_in[None, None], t, -jnp.inf)
    vi = jnp.arange(WIN_R)
    kr = jnp.arange(WIN_R)
    dr = kr[None, :] - vi[:, None] + WIN_R - 1
    tb = t[:, dr]
    return tb.transpose(0, 1, 3, 2, 4).reshape(rpb.shape[0], WIN_R, w, WIN_R * w).astype(F32)


def _natten(proj, bias_tab):
    b, s, _ = proj.shape
    rows = s // GRID_W
    hp = LANES // NA_HEAD_DIM
    npair = NA_HEADS // hp
    blk = lambda off: pl.BlockSpec((None, s, LANES), lambda bi, p, off=off: (bi, 0, off + p))
    return pl.pallas_call(
        functools.partial(_natten_kernel, rows=rows),
        grid=(b, npair),
        in_specs=[blk(0), blk(npair), blk(2 * npair),
                  pl.BlockSpec((hp, WIN_R, GRID_W, WIN_R * GRID_W), lambda bi, p: (p, 0, 0, 0))],
        out_specs=pl.BlockSpec((None, s, LANES), lambda bi, p: (bi, 0, p)),
        out_shape=jax.ShapeDtypeStruct((b, s, NA_WIDTH), BF16),
        compiler_params=_cparams(("parallel", "parallel")),
        name="natten",
    )(proj, proj, proj, bias_tab)


def _retent_kernel(ld_ref, q_ref, k_ref, v_ref, g_ref, cos_ref, sin_ref, gn_ref, o_ref,
                   qs_ref, ks_ref, sb_ref, *, nchunk):
    c = RET_BLOCK
    dh = RET_HEAD_DIM
    h = pl.program_id(1)
    lgf = ld_ref[0, h]
    lgb = ld_ref[1, h]

    cos2 = cos_ref[...]
    sin2 = sin_ref[...]
    qf = q_ref[...].astype(F32)
    qs_ref[...] = qf * cos2 + pltpu.roll(qf, dh // 2, 1) * sin2
    kf = k_ref[...].astype(F32)
    ks_ref[...] = (kf * cos2 + pltpu.roll(kf, dh // 2, 1) * sin2) * (dh ** -0.5)

    ic = lax.broadcasted_iota(I32, (c, 1), 0).astype(F32)
    ir = lax.broadcasted_iota(I32, (1, c), 1).astype(F32)
    diff = ic - ir
    dmat = jnp.where(diff >= 0, jnp.exp(jnp.maximum(diff, 0.0) * lgf),
                     jnp.exp(jnp.maximum(-diff, 0.0) * lgb))
    kdec_f = jnp.exp((c - 1 - ic) * lgf)
    qdec_f = jnp.exp((ic + 1) * lgf)
    kdec_b = jnp.exp(ic * lgb)
    qdec_b = jnp.exp((c - ic) * lgb)
    one = jnp.ones((1, 1), F32)
    cdec_f = jnp.exp(one * (c * lgf))
    cdec_b = jnp.exp(one * (c * lgb))
    tn = (((0,), (0,)), ((), ()))

    def bwd_body(i, sb):
        n = nchunk - 1 - i
        sb_ref[n] = sb
        rows = pl.ds(pl.multiple_of(n * c, c), c)
        kd = (ks_ref[rows, :] * kdec_b).astype(BF16)
        kv = lax.dot_general(kd, v_ref[rows, :], tn, preferred_element_type=F32)
        return cdec_b * sb + kv

    lax.fori_loop(0, nchunk, bwd_body, jnp.zeros((dh, dh), F32), unroll=RET_UNROLL)

    gn = gn_ref[...]

    def fwd_body(n, sf):
        rows = pl.ds(pl.multiple_of(n * c, c), c)
        qn = qs_ref[rows, :]
        kn = ks_ref[rows, :]
        vn = v_ref[rows, :]
        sc = lax.dot_general(qn.astype(BF16), kn.astype(BF16), (((1,), (1,)), ((), ())),
                             preferred_element_type=F32) * dmat
        y = jnp.dot(sc.astype(BF16), vn, preferred_element_type=F32)
        qd = jnp.concatenate([qn * qdec_f, qn * qdec_b], axis=1).astype(BF16)
        st = jnp.concatenate([sf, sb_ref[n]], axis=0).astype(BF16)
        y = y + jnp.dot(qd, st, preferred_element_type=F32)
        mu = jnp.mean(y, -1, keepdims=True)
        yc = y - mu
        var = jnp.mean(yc * yc, -1, keepdims=True)
        yn = yc * lax.rsqrt(var + GN_EPS) * gn
        o_ref[rows, :] = (_silu(g_ref[rows, :].astype(F32)) * yn).astype(o_ref.dtype)
        kv = lax.dot_general((kn * kdec_f).astype(BF16), vn, tn, preferred_element_type=F32)
        return cdec_f * sf + kv

    lax.fori_loop(0, nchunk, fwd_body, jnp.zeros((dh, dh), F32), unroll=RET_UNROLL)


def _retention(proj, log_decay, gn_g, cos2, sin2):
    b, s, _ = proj.shape
    dh = RET_HEAD_DIM
    nchunk = s // RET_BLOCK
    base = 3 * NA_WIDTH // dh
    blk = lambda off: pl.BlockSpec((None, s, dh), lambda bi, h, off=off: (bi, 0, base + off + h))
    full = pl.BlockSpec((s, dh), lambda bi, h: (0, 0))
    return pl.pallas_call(
        functools.partial(_retent_kernel, nchunk=nchunk),
        grid=(b, RET_HEADS),
        in_specs=[pl.BlockSpec(memory_space=pltpu.SMEM),
                  blk(0), blk(RET_HEADS), blk(2 * RET_HEADS), blk(3 * RET_HEADS),
                  full, full,
                  pl.BlockSpec((1, dh), lambda bi, h: (0, h))],
        out_specs=pl.BlockSpec((None, s, dh), lambda bi, h: (bi, 0, h)),
        out_shape=jax.ShapeDtypeStruct((b, s, RET_WIDTH), BF16),
        scratch_shapes=[pltpu.VMEM((s, dh), F32), pltpu.VMEM((s, dh), F32),
                        pltpu.VMEM((nchunk, dh, dh), F32)],
        compiler_params=_cparams(("parallel", "parallel")),
        name="retent",
    )(log_decay, proj, proj, proj, proj, cos2, sin2, gn_g.reshape(1, RET_WIDTH))


def _outproj_kernel(yna_ref, yr_ref, x_ref, ga_ref, sf_ref, shf_ref, wo1_ref, wo2_ref, g_ref, b_ref,
                    wrh_ref, wrl_ref, x1_ref, hfp_ref, lg_ref, *, alpha):
    mix = jnp.dot(yna_ref[...], wo1_ref[...], preferred_element_type=F32)
    mix = mix + jnp.dot(yr_ref[...], wo2_ref[...], preferred_element_type=F32)
    x1 = _layer_norm(alpha * x_ref[...] + ga_ref[...] * mix, g_ref[...], b_ref[...])
    x1_ref[...] = x1
    hf = x1 * (1.0 + sf_ref[...]) + shf_ref[...]
    for j, w in enumerate(_pack_rows(hf)):
        hfp_ref[j] = w
    hb = hf.astype(BF16)
    hl = (hf - hb.astype(F32)).astype(BF16)
    nt = (((1,), (1,)), ((), ()))
    lg = lax.dot_general(wrh_ref[...], hb, nt, preferred_element_type=F32)
    lg = lg + lax.dot_general(wrh_ref[...], hl, nt, preferred_element_type=F32)
    lg = lg + lax.dot_general(wrl_ref[...], hb, nt, preferred_element_type=F32)
    lg_ref[...] = lg


def _outproj(y_na, y_r, x, mod, w_out_bf, ln_g, ln_b, wr_hi, wr_lo, alpha):
    b, s, d = x.shape
    tm = min(TM_PROJ, s)
    nt = s // tm
    ne = wr_hi.shape[0]
    const = lambda shape: pl.BlockSpec(shape, lambda bi, i: tuple(0 for _ in shape))
    x1, hfp, lg = pl.pallas_call(
        functools.partial(_outproj_kernel, alpha=alpha),
        grid=(b, nt),
        in_specs=[pl.BlockSpec((None, tm, NA_WIDTH), lambda bi, i: (bi, i, 0)),
                  pl.BlockSpec((None, tm, RET_WIDTH), lambda bi, i: (bi, i, 0)),
                  pl.BlockSpec((None, tm, d), lambda bi, i: (bi, i, 0)),
                  _mod_spec(2, d), _mod_spec(4, d), _mod_spec(3, d),
                  pl.BlockSpec((NA_WIDTH, d), lambda bi, i: (0, 0)),
                  pl.BlockSpec((RET_WIDTH, d), lambda bi, i: (1, 0)),
                  const((1, d)), const((1, d)), const((ne, d)), const((ne, d))],
        out_specs=[pl.BlockSpec((None, tm, d), lambda bi, i: (bi, i, 0)),
                   pl.BlockSpec((PACK_W, tm, LANES), lambda bi, i: (0, bi * nt + i, 0)),
                   pl.BlockSpec((ne, tm), lambda bi, i: (0, bi * nt + i))],
        out_shape=[jax.ShapeDtypeStruct((b, s, d), F32),
                   jax.ShapeDtypeStruct((PACK_W, b * s, LANES), I32),
                   jax.ShapeDtypeStruct((ne, b * s), F32)],
        compiler_params=_cparams(("parallel", "parallel")),
        name="outproj",
    )(y_na, y_r, x, mod, mod, mod, w_out_bf, w_out_bf, ln_g.reshape(1, d), ln_b.reshape(1, d), wr_hi, wr_lo)
    return x1, hfp, lg


def _route_kernel(lg_ref, rb_ref, idx_ref, w_ref, rank_ref, cnt_ref):
    t = lg_ref.shape[1]
    ninf = -jnp.inf

    @pl.when(pl.program_id(0) == 0)
    def _():
        cnt_ref[...] = jnp.zeros_like(cnt_ref)

    scores = jax.nn.sigmoid(lg_ref[...])
    sel = scores + rb_ref[...]

    io_g = lax.broadcasted_iota(I32, (GROUP_SIZE, t), 0)
    gs_rows = []
    for g in range(N_GROUPS):
        blk = sel[g * GROUP_SIZE:(g + 1) * GROUP_SIZE, :]
        m1 = jnp.max(blk, axis=0, keepdims=True)
        i1 = jnp.min(jnp.where(blk == m1, io_g, GROUP_SIZE), axis=0, keepdims=True)
        m2 = jnp.max(jnp.where(io_g == i1, ninf, blk), axis=0, keepdims=True)
        gs_rows.append(m1 + m2)
    gs = jnp.concatenate(gs_rows, axis=0)

    io8 = lax.broadcasted_iota(I32, (N_GROUPS, t), 0)
    gsel = jnp.zeros((N_GROUPS, t), F32)
    for _ in range(TOPK_GROUPS):
        m = jnp.max(gs, axis=0, keepdims=True)
        gi = jnp.min(jnp.where(gs == m, io8, N_GROUPS), axis=0, keepdims=True)
        hit = io8 == gi
        gsel = jnp.where(hit, 1.0, gsel)
        gs = jnp.where(hit, ninf, gs)

    masked = jnp.concatenate(
        [jnp.where(gsel[g:g + 1, :] > 0.0, sel[g * GROUP_SIZE:(g + 1) * GROUP_SIZE, :], ninf)
         for g in range(N_GROUPS)], axis=0)

    io_e = lax.broadcasted_iota(I32, (N_EXPERTS, t), 0)
    chosen = jnp.zeros((N_EXPERTS, t), F32)
    idx_rows, w_rows = [], []
    for _ in range(TOP_K):
        m = jnp.max(masked, axis=0, keepdims=True)
        ei = jnp.min(jnp.where(masked == m, io_e, N_EXPERTS), axis=0, keepdims=True)
        hit = io_e == ei
        w_rows.append(jnp.sum(jnp.where(hit, scores, 0.0), axis=0, keepdims=True))
        idx_rows.append(ei)
        chosen = jnp.where(hit, 1.0, chosen)
        masked = jnp.where(hit, ninf, masked)
    wk = jnp.concatenate(w_rows, axis=0)
    w_ref[...] = wk / jnp.sum(wk, axis=0, keepdims=True) * ROUTED_SCALE
    idx_ref[...] = jnp.concatenate(idx_rows, axis=0)

    upper = (lax.broadcasted_iota(I32, (t, t), 0) < lax.broadcasted_iota(I32, (t, t), 1))
    prefix = jnp.dot(chosen.astype(BF16), upper.astype(BF16), preferred_element_type=F32)
    rank_full = prefix + cnt_ref[...]
    rank_rows = [jnp.sum(jnp.where(io_e == ei, rank_full, 0.0), axis=0, keepdims=True) for ei in idx_rows]
    rank_ref[...] = jnp.concatenate(rank_rows, axis=0).astype(I32)
    cnt_ref[...] += jnp.sum(chosen, axis=1, keepdims=True)


def _route(logits_t, router_bias):
    ne, n = logits_t.shape
    t = min(T_ROUTE, n)
    kspec = pl.BlockSpec((TOP_K, t), lambda i: (0, i))
    return pl.pallas_call(
        _route_kernel,
        grid=(n // t,),
        in_specs=[pl.BlockSpec((ne, t), lambda i: (0, i)),
                  pl.BlockSpec((ne, 1), lambda i: (0, 0))],
        out_specs=[kspec, kspec, kspec, pl.BlockSpec((ne, 1), lambda i: (0, 0))],
        out_shape=[jax.ShapeDtypeStruct((TOP_K, n), I32),
                   jax.ShapeDtypeStruct((TOP_K, n), F32),
                   jax.ShapeDtypeStruct((TOP_K, n), I32),
                   jax.ShapeDtypeStruct((ne, 1), F32)],
        compiler_params=_cparams(("arbitrary",)),
        name="route",
    )(logits_t, router_bias.reshape(ne, 1))


def _block(ref, g):
    return ref.at[:, pl.ds(pl.multiple_of(g * BLK_E, BLK_E), BLK_E), :]


def _slots_kernel(idx_ref, rank_ref, ps_ref, o_ref, *, n_slots):
    t = idx_ref.shape[1]
    io = lax.broadcasted_iota(I32, (N_EXPERTS, t), 0)
    ps = ps_ref[...]
    for k in range(TOP_K):
        hit = io == idx_ref[k:k + 1, :]
        slot = jnp.sum(jnp.where(hit, ps, 0), axis=0, keepdims=True) + rank_ref[k:k + 1, :]
        for j in range(PACK_W):
            o_ref[k, j:j + 1, :] = slot + j * n_slots


def _slots(idx, rank, pstarts, n_slots):
    n = idx.shape[1]
    t = min(T_SLOT, n)
    kspec = pl.BlockSpec((TOP_K, t), lambda i: (0, i))
    out = pl.pallas_call(
        functools.partial(_slots_kernel, n_slots=n_slots),
        grid=(n // t,),
        in_specs=[kspec, kspec, pl.BlockSpec((N_EXPERTS, 1), lambda i: (0, 0))],
        out_specs=pl.BlockSpec((TOP_K, PACK_W, t), lambda i: (0, 0, i)),
        out_shape=jax.ShapeDtypeStruct((TOP_K, PACK_W, n), I32),
        compiler_params=_cparams(("parallel",)),
        name="slots",
    )(idx, rank, pstarts.reshape(N_EXPERTS, 1))
    return out.reshape(TOP_K, PACK_W * n)


def _sc_mesh():
    return plsc.VectorSubcoreMesh(core_axis_name="core", subcore_axis_name="subcore")


def _scatter_rows_sc(rows, dest_rows, n_out):
    n_rows = rows.shape[0]

    @functools.partial(pl.kernel, mesh=_sc_mesh(), scratch_types=[],
                       out_type=jax.ShapeDtypeStruct((n_out, LANES), I32))
    def scatter_rows(x_hbm, i_hbm, o_hbm):
        def body(x_vmem, i_vmem):
            for k in range(TOP_K):
                pltpu.sync_copy(x_vmem, o_hbm.at[i_vmem.at[k]])

        pltpu.emit_pipeline(
            body,
            grid=(n_rows // SC_WINDOW,),
            in_specs=[pl.BlockSpec((SC_WINDOW, LANES), lambda i: (i, 0)),
                      pl.BlockSpec((TOP_K, SC_WINDOW), lambda i: (0, i))],
            out_specs=[],
            core_axis_name=("core", "subcore"),
            dimension_semantics=(pltpu.PARALLEL,),
        )(x_hbm, i_hbm)

    return scatter_rows(rows, dest_rows)


def _gather_rows_sc(rows, src_rows):
    m = src_rows.shape[1]

    @functools.partial(pl.kernel, mesh=_sc_mesh(), scratch_types=[],
                       out_type=jax.ShapeDtypeStruct((m, LANES), I32))
    def gather_rows(x_hbm, i_hbm, o_hbm):
        def body(i_vmem, o_vmem):
            pltpu.sync_copy(x_hbm.at[i_vmem.at[0]], o_vmem)

        pltpu.emit_pipeline(
            body,
            grid=(m // SC_WINDOW,),
            in_specs=[pl.BlockSpec((1, SC_WINDOW), lambda i: (0, i))],
            out_specs=[pl.BlockSpec((SC_WINDOW, LANES), lambda i: (i, 0))],
            core_axis_name=("core", "subcore"),
            dimension_semantics=(pltpu.PARALLEL,),
        )(i_hbm, o_hbm)

    return gather_rows(rows, src_rows)


def _experts_kernel(nblk_ref, gstart_ref, cnt_ref, nb_ref, xs_ref, wg_ref, wu_ref, wd_ref, ys_ref,
                    xbuf_ref, ybuf_ref, wgb_ref, wub_ref, wdb_ref, xsem, ysem):
    e = pl.program_id(0)
    total = nb_ref[0]
    nb_max = ys_ref.shape[1] // BLK_E

    def x_copy(g):
        slot = g % X_RING
        return pltpu.make_async_copy(_block(xs_ref, g), xbuf_ref.at[slot], xsem.at[slot])

    def y_copy(g):
        slot = g % Y_RING
        return pltpu.make_async_copy(ybuf_ref.at[slot], _block(ys_ref, g), ysem.at[slot])

    @pl.when(e == 0)
    def _():
        for g in range(X_RING - 1):
            @pl.when(g < total)
            def _():
                x_copy(g).start()

    @pl.when(nblk_ref[e] > 0)
    def _():
        wgb_ref[...] = wg_ref[...].astype(BF16)
        wub_ref[...] = wu_ref[...].astype(BF16)
        wdb_ref[...] = wd_ref[...].astype(BF16)

    def block(b, carry):
        g = gstart_ref[e] + b
        x_copy(g).wait()

        @pl.when(g + X_RING - 1 < total)
        def _():
            x_copy(g + X_RING - 1).start()

        xb = _unpack_words([xbuf_ref[g % X_RING, j] for j in range(PACK_W)]).astype(BF16)
        live = lax.broadcasted_iota(I32, (BLK_E, 1), 0) < cnt_ref[e] - b * BLK_E
        xb = jnp.where(live, xb, jnp.zeros((), BF16))
        hg = jnp.dot(xb, wgb_ref[...], preferred_element_type=F32)
        hu = jnp.dot(xb, wub_ref[...], preferred_element_type=F32)
        act = (_silu(hg) * hu).astype(BF16)
        y = jnp.dot(act, wdb_ref[...], preferred_element_type=F32)
        words = _pack_rows(y)

        @pl.when(g >= Y_RING)
        def _():
            y_copy(g - Y_RING).wait()

        for j, w in enumerate(words):
            ybuf_ref[g % Y_RING, j] = w
        y_copy(g).start()
        return carry

    lax.fori_loop(0, nblk_ref[e], block, 0)

    @pl.when(e == pl.num_programs(0) - 1)
    def _():
        for back in range(Y_RING):
            @pl.when(total - 1 - back >= 0)
            def _():
                y_copy(total - 1 - back).wait()

        ybuf_ref[0] = jnp.zeros(ybuf_ref.shape[1:], ybuf_ref.dtype)

        def tail_copy(g):
            return pltpu.make_async_copy(ybuf_ref.at[0], _block(ys_ref, g), ysem.at[0])

        def fill(g, carry):
            tail_copy(g).start()
            return carry

        lax.fori_loop(total, nb_max, fill, 0)

        def drain(g, carry):
            tail_copy(g).wait()
            return carry

        lax.fori_loop(total, nb_max, drain, 0)


def _experts(nblk, gstart, cnt, n_blocks, xs, w_gate, w_up, w_down):
    p = xs.shape[1]
    ne, d, f = w_gate.shape
    wspec = lambda shape: pl.BlockSpec((None,) + shape, lambda e, *_: (e, 0, 0))
    grid_spec = pltpu.PrefetchScalarGridSpec(
        num_scalar_prefetch=4,
        grid=(ne,),
        in_specs=[pl.BlockSpec(memory_space=pl.ANY), wspec((d, f)), wspec((d, f)), wspec((f, d))],
        out_specs=pl.BlockSpec(memory_space=pl.ANY),
        scratch_shapes=[pltpu.VMEM((X_RING, PACK_W, BLK_E, LANES), I32),
                        pltpu.VMEM((Y_RING, PACK_W, BLK_E, LANES), I32),
                        pltpu.VMEM((d, f), BF16), pltpu.VMEM((d, f), BF16), pltpu.VMEM((f, d), BF16),
                        pltpu.SemaphoreType.DMA((X_RING,)), pltpu.SemaphoreType.DMA((Y_RING,))],
    )
    return pl.pallas_call(
        _experts_kernel,
        grid_spec=grid_spec,
        out_shape=jax.ShapeDtypeStruct((PACK_W, p, LANES), I32),
        compiler_params=_cparams(("arbitrary",)),
        name="experts",
    )(nblk, gstart, cnt, n_blocks, xs, w_gate, w_up, w_down)


def _combine_kernel(wt_ref, yk_ref, hfp_ref, x1_ref, gf_ref, wsg_ref, wsu_ref, wsd_ref, g_ref, b_ref, o_ref,
                    *, alpha):
    hb = _unpack_words([hfp_ref[j] for j in range(PACK_W)]).astype(BF16)
    sg = jnp.dot(hb, wsg_ref[...], preferred_element_type=F32)
    su = jnp.dot(hb, wsu_ref[...], preferred_element_type=F32)
    ffn = jnp.dot((_silu(sg) * su).astype(BF16), wsd_ref[...], preferred_element_type=F32)
    wt = wt_ref[...]
    for k in range(TOP_K):
        yk = _unpack_words([yk_ref[k, j] for j in range(PACK_W)])
        ffn = ffn + wt[:, k:k + 1] * yk
    o_ref[...] = _layer_norm(alpha * x1_ref[...] + gf_ref[...] * ffn, g_ref[...], b_ref[...])


def _combine(w_tok, yk, hfp, x1, mod, ws_gate_bf, ws_up_bf, ws_down_bf, ln_g, ln_b, alpha):
    b, s, d = x1.shape
    n = b * s
    tc = min(T_COMB, s)
    nt = s // tc
    fs = ws_gate_bf.shape[1]
    const = lambda shape: pl.BlockSpec(shape, lambda i: tuple(0 for _ in shape))
    out = pl.pallas_call(
        functools.partial(_combine_kernel, alpha=alpha),
        grid=(n // tc,),
        in_specs=[pl.BlockSpec((tc, TOP_K), lambda i: (i, 0)),
                  pl.BlockSpec((TOP_K, PACK_W, tc, LANES), lambda i: (0, 0, i, 0)),
                  pl.BlockSpec((PACK_W, tc, LANES), lambda i: (0, i, 0)),
                  pl.BlockSpec((tc, d), lambda i: (i, 0)),
                  pl.BlockSpec((None, None, 1, d), lambda i: (5, i // nt, 0, 0)),
                  const((d, fs)), const((d, fs)), const((fs, d)), const((1, d)), const((1, d))],
        out_specs=pl.BlockSpec((tc, d), lambda i: (i, 0)),
        out_shape=jax.ShapeDtypeStruct((n, d), F32),
        compiler_params=_cparams(("parallel",)),
        name="combine",
    )(w_tok, yk, hfp, x1.reshape(n, d), mod, ws_gate_bf, ws_up_bf, ws_down_bf,
      ln_g.reshape(1, d), ln_b.reshape(1, d))
    return out.reshape(b, s, d)


def _slot_layout(counts, n_assign):
    cnt = counts[:, 0].astype(I32)
    padded = (cnt + BLK_E - 1) // BLK_E * BLK_E
    pends = jnp.cumsum(padded)
    pstarts = pends - padded
    n_blocks_max = (n_assign + N_EXPERTS * (BLK_E - 1)) // BLK_E
    n_blocks = (pends[-1] // BLK_E).astype(I32).reshape(1)
    return pstarts, pends, padded // BLK_E, pstarts // BLK_E, n_blocks, n_blocks_max * BLK_E


def kernel(x, c, w_ada, b_ada, w_in, w_out, na_rpb, ret_log_decay, ret_gn_g, ln1_g, ln1_b, ln2_g, ln2_b,
           w_router, router_bias, w_gate, w_up, w_down, ws_gate, ws_up, ws_down):
    b, s, d = x.shape
    depth = w_ada.shape[0]
    alpha = (2.0 * depth) ** 0.25
    t = jnp.arange(s, dtype=F32)
    inv_freq = ROPE_BASE ** (-jnp.arange(0, RET_HEAD_DIM, 2, dtype=F32) / RET_HEAD_DIM)
    ang = t[:, None] * inv_freq[None, :]
    cos, sin = jnp.cos(ang), jnp.sin(ang)
    cos2 = jnp.concatenate([cos, cos], axis=-1)
    sin2 = jnp.concatenate([-sin, sin], axis=-1)
    for l in range(depth):
        mod = _mod(c, w_ada[l], b_ada[l])
        proj = _inproj(x, mod, w_in[l].astype(BF16))
        y_na = _natten(proj, _na_bias_table(na_rpb[l]))
        y_r = _retention(proj, ret_log_decay[l], ret_gn_g[l], cos2, sin2)
        wr_t = w_router[l].T
        wr_hi = wr_t.astype(BF16)
        wr_lo = (wr_t - wr_hi.astype(F32)).astype(BF16)
        x1, hfp, logits_t = _outproj(y_na, y_r, x, mod, w_out[l].astype(BF16), ln1_g[l], ln1_b[l],
                                     wr_hi, wr_lo, alpha)
        idx, wts, rank, counts = _route(logits_t, router_bias[l])
        pstarts, pends, nblk, gstart, n_blocks, n_slots = _slot_layout(counts, b * s * TOP_K)
        cnt = counts[:, 0].astype(I32)
        n = b * s
        slot_rows = _slots(idx, rank, pstarts, n_slots)
        xs = _scatter_rows_sc(hfp.reshape(PACK_W * n, LANES), slot_rows, PACK_W * n_slots)
        ys = _experts(nblk, gstart, cnt, n_blocks, xs.reshape(PACK_W, n_slots, LANES),
                      w_gate[l], w_up[l], w_down[l])
        yk = _gather_rows_sc(ys.reshape(PACK_W * n_slots, LANES), slot_rows.reshape(1, TOP_K * PACK_W * n))
        x = _combine(wts.T, yk.reshape(TOP_K, PACK_W, n, LANES), hfp, x1, mod, ws_gate[l].astype(BF16),
                     ws_up[l].astype(BF16), ws_down[l].astype(BF16), ln2_g[l], ln2_b[l], alpha)
    return x
```

```python
import functools

import jax
import jax.numpy as jnp
import numpy as np
from jax import lax
from jax.experimental import pallas as pl
from jax.experimental.pallas import tpu as pltpu
from jax.experimental.pallas import tpu_sc as plsc

F32 = jnp.float32
BF16 = jnp.bfloat16
U32 = jnp.uint32
I32 = jnp.int32

GRID_W = 64
WIN_R = 8
WIN_C = 16
NA_HEADS = 8
NA_HEAD_DIM = 64
NA_WIDTH = NA_HEADS * NA_HEAD_DIM
RET_HEADS = 4
RET_HEAD_DIM = 128
RET_WIDTH = RET_HEADS * RET_HEAD_DIM
RET_CHUNK = 128
ROPE_BASE = 10000.0
N_EXPERTS = 256
TOP_K = 8
N_GROUPS = 8
TOPK_GROUPS = 4
GROUP_SIZE = N_EXPERTS // N_GROUPS
ROUTED_SCALE = 2.5
LN_EPS = 1e-5
GN_EPS = 1e-6

LANES = 128
VMEM_LIMIT = 56 * 1024 * 1024

TM_PROJ = 512
T_ROUTE = 512
T_SLOT = 1024
SC_WINDOW = 128
SC_GATHER_WINDOWS = 2
BLK_E = 512
T_COMB = 256
COMBINE_CHUNKS = 4
PACK_W = 4
NA_ROWS_PER_ITER = 8
RET_BLOCK = 256
RET_UNROLL = 2
X_RING = 4
Y_RING = 3


def _cparams(sem, vmem=VMEM_LIMIT):
    return pltpu.CompilerParams(dimension_semantics=sem, vmem_limit_bytes=vmem)


def _silu(v):
    return v * jax.nn.sigmoid(v)


def _layer_norm(z, g, b):
    mu = jnp.mean(z, -1, keepdims=True)
    zc = z - mu
    var = jnp.mean(zc * zc, -1, keepdims=True)
    return zc * lax.rsqrt(var + LN_EPS) * g + b


def _pack_rows(v):
    half = v.shape[1] // 2
    vb = v.astype(BF16)
    lo = lax.bitcast_convert_type(vb[:, :half].astype(F32), U32) >> 16
    hi = lax.bitcast_convert_type(vb[:, half:].astype(F32), U32)
    w = lax.bitcast_convert_type(hi | lo, I32)
    return [w[:, j * LANES:(j + 1) * LANES] for j in range(half // LANES)]


def _unpack_words(words):
    words = [lax.bitcast_convert_type(w, U32) for w in words]
    lo = [lax.bitcast_convert_type(w << 16, F32) for w in words]
    hi = [lax.bitcast_convert_type(w & jnp.uint32(0xFFFF0000), F32) for w in words]
    return jnp.concatenate(lo + hi, axis=-1)


def _mod_kernel(c_ref, w_ref, b_ref, o_ref):
    cond = _silu(c_ref[...])
    o_ref[0] = jnp.dot(cond, w_ref[...], precision=lax.Precision.HIGHEST,
                       preferred_element_type=F32) + b_ref[0]


def _mod(c, w_ada, b_ada):
    b, d = c.shape
    n6 = w_ada.shape[1] // d
    out = pl.pallas_call(
        _mod_kernel,
        grid=(n6,),
        in_specs=[pl.BlockSpec((b, d), lambda j: (0, 0)),
                  pl.BlockSpec((d, d), lambda j: (0, j)),
                  pl.BlockSpec((1, 1, d), lambda j: (j, 0, 0))],
        out_specs=pl.BlockSpec((1, b, d), lambda j: (j, 0, 0)),
        out_shape=jax.ShapeDtypeStruct((n6, b, d), F32),
        compiler_params=_cparams(("arbitrary",)),
        name="mod",
    )(c, w_ada, b_ada.reshape(n6, 1, d))
    return out.reshape(n6, b, 1, d)


def _mod_spec(which, d):
    return pl.BlockSpec((None, None, 1, d), lambda b, i, which=which: (which, b, 0, 0))


def _inproj_kernel(x_ref, sc_ref, sh_ref, w_ref, o_ref, *, chunk, q_scale):
    h = (x_ref[...] * (1.0 + sc_ref[...]) + sh_ref[...]).astype(BF16)
    for j in range(o_ref.shape[1] // chunk):
        acc = jnp.dot(h, w_ref[:, j * chunk:(j + 1) * chunk], preferred_element_type=F32)
        if j == 0:
            acc = acc * q_scale
        o_ref[:, j * chunk:(j + 1) * chunk] = acc.astype(o_ref.dtype)


def _inproj(x, mod, w_in_bf):
    b, s, d = x.shape
    e = w_in_bf.shape[1]
    tm = min(TM_PROJ, s)
    return pl.pallas_call(
        functools.partial(_inproj_kernel, chunk=NA_WIDTH, q_scale=NA_HEAD_DIM ** -0.5),
        grid=(b, s // tm),
        in_specs=[pl.BlockSpec((None, tm, d), lambda bi, i: (bi, i, 0)),
                  _mod_spec(1, d), _mod_spec(0, d),
                  pl.BlockSpec((d, e), lambda bi, i: (0, 0))],
        out_specs=pl.BlockSpec((None, tm, e), lambda bi, i: (bi, i, 0)),
        out_shape=jax.ShapeDtypeStruct((b, s, e), BF16),
        compiler_params=_cparams(("parallel", "parallel")),
        name="inproj",
    )(x, mod, mod, w_in_bf)


def _natten_kernel(q_ref, k_ref, v_ref, bias_ref, o_ref, *, rows):
    kspan = WIN_R * GRID_W

    first = lax.broadcasted_iota(I32, (1, LANES), 1) < NA_HEAD_DIM
    zero = jnp.zeros((), BF16)
    one = jnp.ones((), BF16)

    def rows_body(i, carry):
        qrows, krows, scores = [], [], []
        for u in range(NA_ROWS_PER_ITER):
            r = i * NA_ROWS_PER_ITER + u
            rs = jnp.clip(r - WIN_R // 2, 0, rows - WIN_R)
            vi = r - rs
            qrows.append(pl.ds(pl.multiple_of(r * GRID_W, GRID_W), GRID_W))
            krows.append(pl.ds(pl.multiple_of(rs * GRID_W, GRID_W), kspan))
            q = q_ref[qrows[u], :]
            k = k_ref[krows[u], :]
            for j, qh in enumerate((jnp.where(first, q, zero), jnp.where(first, zero, q))):
                s = lax.dot_general(qh, k, (((1,), (1,)), ((), ())), preferred_element_type=F32)
                scores.append(s + bias_ref[j, vi])
        probs = []
        for s in scores:
            m = jnp.max(s, axis=-1, keepdims=True)
            probs.append(jnp.exp(s - m).astype(BF16))
        for u in range(NA_ROWS_PER_ITER):
            v = v_ref[krows[u], :]
            a0 = jnp.dot(probs[2 * u], jnp.where(first, v, one), preferred_element_type=F32)
            a1 = jnp.dot(probs[2 * u + 1], jnp.where(first, one, v), preferred_element_type=F32)
            num = jnp.where(first, a0, a1)
            den = pltpu.roll(jnp.where(first, a1, a0), NA_HEAD_DIM, 1)
            o_ref[qrows[u], :] = (num / den).astype(o_ref.dtype)
        return carry

    lax.fori_loop(0, rows // NA_ROWS_PER_ITER, rows_body, 0)


def _na_bias_table(rpb):
    w = GRID_W
    cq = jnp.arange(w)
    cs = jnp.clip(cq - WIN_C // 2, 0, w - WIN_C)
    ck = jnp.arange(w)
    col_in = (ck[None, :] >= cs[:, None]) & (ck[None, :] < cs[:, None] + WIN_C)
    dc_idx = jnp.clip(ck[None, :] - cq[:, None] + WIN_C - 1, 0, 2 * WIN_C - 2)
    t = rpb[:, :, dc_idx]
    t = jnp.where(col_in[None, None], t, -jnp.inf)
    vi = jnp.arange(WIN_R)
    kr = jnp.arange(WIN_R)
    dr = kr[None, :] - vi[:, None] + WIN_R - 1
    tb = t[:, dr]
    return tb.transpose(0, 1, 3, 2, 4).reshape(rpb.shape[0], WIN_R, w, WIN_R * w).astype(F32)


def _natten(proj, bias_tab):
    b, s, _ = proj.shape
    rows = s // GRID_W
    hp = LANES // NA_HEAD_DIM
    npair = NA_HEADS // hp
    blk = lambda off: pl.BlockSpec((None, s, LANES), lambda bi, p, off=off: (bi, 0, off + p))
    return pl.pallas_call(
        functools.partial(_natten_kernel, rows=rows),
        grid=(b, npair),
        in_specs=[blk(0), blk(npair), blk(2 * npair),
                  pl.BlockSpec((hp, WIN_R, GRID_W, WIN_R * GRID_W), lambda bi, p: (p, 0, 0, 0))],
        out_specs=pl.BlockSpec((None, s, LANES), lambda bi, p: (bi, 0, p)),
        out_shape=jax.ShapeDtypeStruct((b, s, NA_WIDTH), BF16),
        compiler_params=_cparams(("parallel", "parallel")),
        name="natten",
    )(proj, proj, proj, bias_tab)


def _retent_kernel(ld_ref, q_ref, k_ref, v_ref, g_ref, cos_ref, sin_ref, gn_ref, o_ref,
                   qs_ref, ks_ref, sb_ref, *, nchunk):
    c = RET_BLOCK
    dh = RET_HEAD_DIM
    h = pl.program_id(1)
    lgf = ld_ref[0, h]
    lgb = ld_ref[1, h]

    cos2 = cos_ref[...]
    sin2 = sin_ref[...]
    qf = q_ref[...].astype(F32)
    qs_ref[...] = qf * cos2 + pltpu.roll(qf, dh // 2, 1) * sin2
    kf = k_ref[...].astype(F32)
    ks_ref[...] = (kf * cos2 + pltpu.roll(kf, dh // 2, 1) * sin2) * (dh ** -0.5)

    ic = lax.broadcasted_iota(I32, (c, 1), 0).astype(F32)
    ir = lax.broadcasted_iota(I32, (1, c), 1).astype(F32)
    diff = ic - ir
    dmat = jnp.where(diff >= 0, jnp.exp(jnp.maximum(diff, 0.0) * lgf),
                     jnp.exp(jnp.maximum(-diff, 0.0) * lgb))
    kdec_f = jnp.exp((c - 1 - ic) * lgf)
    qdec_f = jnp.exp((ic + 1) * lgf)
    kdec_b = jnp.exp(ic * lgb)
    qdec_b = jnp.exp((c - ic) * lgb)
    one = jnp.ones((1, 1), F32)
    cdec_f = jnp.exp(one * (c * lgf))
    cdec_b = jnp.exp(one * (c * lgb))
    tn = (((0,), (0,)), ((), ()))

    def bwd_body(i, sb):
        n = nchunk - 1 - i
        sb_ref[n] = sb
        rows = pl.ds(pl.multiple_of(n * c, c), c)
        kd = (ks_ref[rows, :] * kdec_b).astype(BF16)
        kv = lax.dot_general(kd, v_ref[rows, :], tn, preferred_element_type=F32)
        return cdec_b * sb + kv

    lax.fori_loop(0, nchunk, bwd_body, jnp.zeros((dh, dh), F32), unroll=RET_UNROLL)

    gn = gn_ref[...]

    def fwd_body(n, sf):
        rows = pl.ds(pl.multiple_of(n * c, c), c)
        qn = qs_ref[rows, :]
        kn = ks_ref[rows, :]
        vn = v_ref[rows, :]
        sc = lax.dot_general(qn.astype(BF16), kn.astype(BF16), (((1,), (1,)), ((), ())),
                             preferred_element_type=F32) * dmat
        y = jnp.dot(sc.astype(BF16), vn, preferred_element_type=F32)
        qd = jnp.concatenate([qn * qdec_f, qn * qdec_b], axis=1).astype(BF16)
        st = jnp.concatenate([sf, sb_ref[n]], axis=0).astype(BF16)
        y = y + jnp.dot(qd, st, preferred_element_type=F32)
        mu = jnp.mean(y, -1, keepdims=True)
        yc = y - mu
        var = jnp.mean(yc * yc, -1, keepdims=True)
        yn = yc * lax.rsqrt(var + GN_EPS) * gn
        o_ref[rows, :] = (_silu(g_ref[rows, :].astype(F32)) * yn).astype(o_ref.dtype)
        kv = lax.dot_general((kn * kdec_f).astype(BF16), vn, tn, preferred_element_type=F32)
        return cdec_f * sf + kv

    lax.fori_loop(0, nchunk, fwd_body, jnp.zeros((dh, dh), F32), unroll=RET_UNROLL)


def _retention(proj, log_decay, gn_g, cos2, sin2):
    b, s, _ = proj.shape
    dh = RET_HEAD_DIM
    nchunk = s // RET_BLOCK
    base = 3 * NA_WIDTH // dh
    blk = lambda off: pl.BlockSpec((None, s, dh), lambda bi, h, off=off: (bi, 0, base + off + h))
    full = pl.BlockSpec((s, dh), lambda bi, h: (0, 0))
    return pl.pallas_call(
        functools.partial(_retent_kernel, nchunk=nchunk),
        grid=(b, RET_HEADS),
        in_specs=[pl.BlockSpec(memory_space=pltpu.SMEM),
                  blk(0), blk(RET_HEADS), blk(2 * RET_HEADS), blk(3 * RET_HEADS),
                  full, full,
                  pl.BlockSpec((1, dh), lambda bi, h: (0, h))],
        out_specs=pl.BlockSpec((None, s, dh), lambda bi, h: (bi, 0, h)),
        out_shape=jax.ShapeDtypeStruct((b, s, RET_WIDTH), BF16),
        scratch_shapes=[pltpu.VMEM((s, dh), F32), pltpu.VMEM((s, dh), F32),
                        pltpu.VMEM((nchunk, dh, dh), F32)],
        compiler_params=_cparams(("parallel", "parallel")),
        name="retent",
    )(log_decay, proj, proj, proj, proj, cos2, sin2, gn_g.reshape(1, RET_WIDTH))


def _outproj_kernel(yna_ref, yr_ref, x_ref, ga_ref, sf_ref, shf_ref, wo1_ref, wo2_ref, g_ref, b_ref,
                    wrh_ref, wrl_ref, x1_ref, hfp_ref, lg_ref, *, alpha):
    mix = jnp.dot(yna_ref[...], wo1_ref[...], preferred_element_type=F32)
    mix = mix + jnp.dot(yr_ref[...], wo2_ref[...], preferred_element_type=F32)
    x1 = _layer_norm(alpha * x_ref[...] + ga_ref[...] * mix, g_ref[...], b_ref[...])
    x1_ref[...] = x1
    hf = x1 * (1.0 + sf_ref[...]) + shf_ref[...]
    for j, w in enumerate(_pack_rows(hf)):
        hfp_ref[j] = w
    hb = hf.astype(BF16)
    hl = (hf - hb.astype(F32)).astype(BF16)
    nt = (((1,), (1,)), ((), ()))
    lg = lax.dot_general(wrh_ref[...], hb, nt, preferred_element_type=F32)
    lg = lg + lax.dot_general(wrh_ref[...], hl, nt, preferred_element_type=F32)
    lg = lg + lax.dot_general(wrl_ref[...], hb, nt, preferred_element_type=F32)
    lg_ref[...] = lg


def _outproj(y_na, y_r, x, mod, w_out_bf, ln_g, ln_b, wr_hi, wr_lo, alpha):
    b, s, d = x.shape
    tm = min(TM_PROJ, s)
    nt = s // tm
    ne = wr_hi.shape[0]
    const = lambda shape: pl.BlockSpec(shape, lambda bi, i: tuple(0 for _ in shape))
    x1, hfp, lg = pl.pallas_call(
        functools.partial(_outproj_kernel, alpha=alpha),
        grid=(b, nt),
        in_specs=[pl.BlockSpec((None, tm, NA_WIDTH), lambda bi, i: (bi, i, 0)),
                  pl.BlockSpec((None, tm, RET_WIDTH), lambda bi, i: (bi, i, 0)),
                  pl.BlockSpec((None, tm, d), lambda bi, i: (bi, i, 0)),
                  _mod_spec(2, d), _mod_spec(4, d), _mod_spec(3, d),
                  pl.BlockSpec((NA_WIDTH, d), lambda bi, i: (0, 0)),
                  pl.BlockSpec((RET_WIDTH, d), lambda bi, i: (1, 0)),
                  const((1, d)), const((1, d)), const((ne, d)), const((ne, d))],
        out_specs=[pl.BlockSpec((None, tm, d), lambda bi, i: (bi, i, 0)),
                   pl.BlockSpec((PACK_W, tm, LANES), lambda bi, i: (0, bi * nt + i, 0)),
                   pl.BlockSpec((ne, tm), lambda bi, i: (0, bi * nt + i))],
        out_shape=[jax.ShapeDtypeStruct((b, s, d), F32),
                   jax.ShapeDtypeStruct((PACK_W, b * s, LANES), I32),
                   jax.ShapeDtypeStruct((ne, b * s), F32)],
        compiler_params=_cparams(("parallel", "parallel")),
        name="outproj",
    )(y_na, y_r, x, mod, mod, mod, w_out_bf, w_out_bf, ln_g.reshape(1, d), ln_b.reshape(1, d), wr_hi, wr_lo)
    return x1, hfp, lg


def _route_kernel(lg_ref, rb_ref, idx_ref, w_ref, rank_ref, cnt_ref):
    t = lg_ref.shape[1]
    ninf = -jnp.inf

    @pl.when(pl.program_id(0) == 0)
    def _():
        cnt_ref[...] = jnp.zeros_like(cnt_ref)

    scores = jax.nn.sigmoid(lg_ref[...])
    sel = scores + rb_ref[...]

    io_g = lax.broadcasted_iota(I32, (GROUP_SIZE, t), 0)
    gs_rows = []
    for g in range(N_GROUPS):
        blk = sel[g * GROUP_SIZE:(g + 1) * GROUP_SIZE, :]
        m1 = jnp.max(blk, axis=0, keepdims=True)
        i1 = jnp.min(jnp.where(blk == m1, io_g, GROUP_SIZE), axis=0, keepdims=True)
        m2 = jnp.max(jnp.where(io_g == i1, ninf, blk), axis=0, keepdims=True)
        gs_rows.append(m1 + m2)
    gs = jnp.concatenate(gs_rows, axis=0)

    io8 = lax.broadcasted_iota(I32, (N_GROUPS, t), 0)
    gsel = jnp.zeros((N_GROUPS, t), F32)
    for _ in range(TOPK_GROUPS):
        m = jnp.max(gs, axis=0, keepdims=True)
        gi = jnp.min(jnp.where(gs == m, io8, N_GROUPS), axis=0, keepdims=True)
        hit = io8 == gi
        gsel = jnp.where(hit, 1.0, gsel)
        gs = jnp.where(hit, ninf, gs)

    masked = jnp.concatenate(
        [jnp.where(gsel[g:g + 1, :] > 0.0, sel[g * GROUP_SIZE:(g + 1) * GROUP_SIZE, :], ninf)
         for g in range(N_GROUPS)], axis=0)

    io_e = lax.broadcasted_iota(I32, (N_EXPERTS, t), 0)
    chosen = jnp.zeros((N_EXPERTS, t), F32)
    idx_rows, w_rows = [], []
    for _ in range(TOP_K):
        m = jnp.max(masked, axis=0, keepdims=True)
        ei = jnp.min(jnp.where(masked == m, io_e, N_EXPERTS), axis=0, keepdims=True)
        hit = io_e == ei
        w_rows.append(jnp.sum(jnp.where(hit, scores, 0.0), axis=0, keepdims=True))
        idx_rows.append(ei)
        chosen = jnp.where(hit, 1.0, chosen)
        masked = jnp.where(hit, ninf, masked)
    wk = jnp.concatenate(w_rows, axis=0)
    w_ref[...] = wk / jnp.sum(wk, axis=0, keepdims=True) * ROUTED_SCALE
    idx_ref[...] = jnp.concatenate(idx_rows, axis=0)

    upper = (lax.broadcasted_iota(I32, (t, t), 0) < lax.broadcasted_iota(I32, (t, t), 1))
    prefix = jnp.dot(chosen.astype(BF16), upper.astype(BF16), preferred_element_type=F32)
    rank_full = prefix + cnt_ref[...]
    rank_rows = [jnp.sum(jnp.where(io_e == ei, rank_full, 0.0), axis=0, keepdims=True) for ei in idx_rows]
    rank_ref[...] = jnp.concatenate(rank_rows, axis=0).astype(I32)
    cnt_ref[...] += jnp.sum(chosen, axis=1, keepdims=True)


def _route(logits_t, router_bias):
    ne, n = logits_t.shape
    t = min(T_ROUTE, n)
    kspec = pl.BlockSpec((TOP_K, t), lambda i: (0, i))
    return pl.pallas_call(
        _route_kernel,
        grid=(n // t,),
        in_specs=[pl.BlockSpec((ne, t), lambda i: (0, i)),
                  pl.BlockSpec((ne, 1), lambda i: (0, 0))],
        out_specs=[kspec, kspec, kspec, pl.BlockSpec((ne, 1), lambda i: (0, 0))],
        out_shape=[jax.ShapeDtypeStruct((TOP_K, n), I32),
                   jax.ShapeDtypeStruct((TOP_K, n), F32),
                   jax.ShapeDtypeStruct((TOP_K, n), I32),
                   jax.ShapeDtypeStruct((ne, 1), F32)],
        compiler_params=_cparams(("arbitrary",)),
        name="route",
    )(logits_t, router_bias.reshape(ne, 1))


def _block(ref, g):
    return ref.at[:, pl.ds(pl.multiple_of(g * BLK_E, BLK_E), BLK_E), :]


def _slots_kernel(idx_ref, rank_ref, ps_ref, o_ref, *, n_slots):
    t = idx_ref.shape[1]
    io = lax.broadcasted_iota(I32, (N_EXPERTS, t), 0)
    ps = ps_ref[...]
    for k in range(TOP_K):
        hit = io == idx_ref[k:k + 1, :]
        slot = jnp.sum(jnp.where(hit, ps, 0), axis=0, keepdims=True) + rank_ref[k:k + 1, :]
        for j in range(PACK_W):
            o_ref[k, j:j + 1, :] = slot + j * n_slots


def _slots(idx, rank, pstarts, n_slots):
    n = idx.shape[1]
    t = min(T_SLOT, n)
    kspec = pl.BlockSpec((TOP_K, t), lambda i: (0, i))
    out = pl.pallas_call(
        functools.partial(_slots_kernel, n_slots=n_slots),
        grid=(n // t,),
        in_specs=[kspec, kspec, pl.BlockSpec((N_EXPERTS, 1), lambda i: (0, 0))],
        out_specs=pl.BlockSpec((TOP_K, PACK_W, t), lambda i: (0, 0, i)),
        out_shape=jax.ShapeDtypeStruct((TOP_K, PACK_W, n), I32),
        compiler_params=_cparams(("parallel",)),
        name="slots",
    )(idx, rank, pstarts.reshape(N_EXPERTS, 1))
    return out.reshape(TOP_K, PACK_W * n)


def _sc_mesh():
    return plsc.VectorSubcoreMesh(core_axis_name="core", subcore_axis_name="subcore")


def _scatter_rows_sc(rows, dest_rows, n_out):
    n_rows = rows.shape[0]

    @functools.partial(pl.kernel, mesh=_sc_mesh(), scratch_types=[pltpu.SemaphoreType.DMA],
                       out_type=jax.ShapeDtypeStruct((n_out, LANES), I32))
    def scatter_rows(x_hbm, i_hbm, o_hbm, sem):
        def body(x_vmem, i_vmem):
            copies = [pltpu.async_copy(x_vmem, o_hbm.at[i_vmem.at[k]], sem) for k in range(TOP_K)]
            for c in copies:
                c.wait()

        pltpu.emit_pipeline(
            body,
            grid=(n_rows // SC_WINDOW,),
            in_specs=[pl.BlockSpec((SC_WINDOW, LANES), lambda i: (i, 0)),
                      pl.BlockSpec((TOP_K, SC_WINDOW), lambda i: (0, i))],
            out_specs=[],
            core_axis_name=("core", "subcore"),
            dimension_semantics=(pltpu.PARALLEL,),
        )(x_hbm, i_hbm)

    return scatter_rows(rows, dest_rows)


def _gather_rows_sc(rows, src_rows):
    nwin = src_rows.shape[0]

    @functools.partial(pl.kernel, mesh=_sc_mesh(), scratch_types=[pltpu.SemaphoreType.DMA],
                       out_type=jax.ShapeDtypeStruct((nwin * SC_WINDOW, LANES), I32))
    def gather_rows(x_hbm, i_hbm, o_hbm, sem):
        def body(i_vmem, o_vmem):
            copies = [pltpu.async_copy(x_hbm.at[i_vmem.at[w]], o_vmem.at[pl.ds(w * SC_WINDOW, SC_WINDOW)], sem)
                      for w in range(SC_GATHER_WINDOWS)]
            for c in copies:
                c.wait()

        pltpu.emit_pipeline(
            body,
            grid=(nwin // SC_GATHER_WINDOWS,),
            in_specs=[pl.BlockSpec((SC_GATHER_WINDOWS, SC_WINDOW), lambda i: (i, 0))],
            out_specs=[pl.BlockSpec((SC_GATHER_WINDOWS * SC_WINDOW, LANES), lambda i: (i, 0))],
            core_axis_name=("core", "subcore"),
            dimension_semantics=(pltpu.PARALLEL,),
        )(i_hbm, o_hbm)

    return gather_rows(rows, src_rows)


def _experts_kernel(nblk_ref, gstart_ref, cnt_ref, nb_ref, xs_ref, wg_ref, wu_ref, wd_ref, ys_ref,
                    xbuf_ref, ybuf_ref, wgb_ref, wub_ref, wdb_ref, xsem, ysem):
    e = pl.program_id(0)
    total = nb_ref[0]
    nb_max = ys_ref.shape[1] // BLK_E

    def x_copy(g):
        slot = g % X_RING
        return pltpu.make_async_copy(_block(xs_ref, g), xbuf_ref.at[slot], xsem.at[slot])

    def y_copy(g):
        slot = g % Y_RING
        return pltpu.make_async_copy(ybuf_ref.at[slot], _block(ys_ref, g), ysem.at[slot])

    @pl.when(e == 0)
    def _():
        for g in range(X_RING - 1):
            @pl.when(g < total)
            def _():
                x_copy(g).start()

    @pl.when(nblk_ref[e] > 0)
    def _():
        wgb_ref[...] = wg_ref[...].astype(BF16)
        wub_ref[...] = wu_ref[...].astype(BF16)
        wdb_ref[...] = wd_ref[...].astype(BF16)

    def block(b, carry):
        g = gstart_ref[e] + b
        x_copy(g).wait()

        @pl.when(g + X_RING - 1 < total)
        def _():
            x_copy(g + X_RING - 1).start()

        xb = _unpack_words([xbuf_ref[g % X_RING, j] for j in range(PACK_W)]).astype(BF16)
        live = lax.broadcasted_iota(I32, (BLK_E, 1), 0) < cnt_ref[e] - b * BLK_E
        xb = jnp.where(live, xb, jnp.zeros((), BF16))
        hg = jnp.dot(xb, wgb_ref[...], preferred_element_type=F32)
        hu = jnp.dot(xb, wub_ref[...], preferred_element_type=F32)
        act = (_silu(hg) * hu).astype(BF16)
        y = jnp.dot(act, wdb_ref[...], preferred_element_type=F32)
        words = _pack_rows(y)

        @pl.when(g >= Y_RING)
        def _():
            y_copy(g - Y_RING).wait()

        for j, w in enumerate(words):
            ybuf_ref[g % Y_RING, j] = w
        y_copy(g).start()
        return carry

    lax.fori_loop(0, nblk_ref[e], block, 0)

    @pl.when(e == pl.num_programs(0) - 1)
    def _():
        for back in range(Y_RING):
            @pl.when(total - 1 - back >= 0)
            def _():
                y_copy(total - 1 - back).wait()

        ybuf_ref[0] = jnp.zeros(ybuf_ref.shape[1:], ybuf_ref.dtype)

        def tail_copy(g):
            return pltpu.make_async_copy(ybuf_ref.at[0], _block(ys_ref, g), ysem.at[0])

        def fill(g, carry):
            tail_copy(g).start()
            return carry

        lax.fori_loop(total, nb_max, fill, 0)

        def drain(g, carry):
            tail_copy(g).wait()
            return carry

        lax.fori_loop(total, nb_max, drain, 0)


def _experts(nblk, gstart, cnt, n_blocks, xs, w_gate, w_up, w_down):
    p = xs.shape[1]
    ne, d, f = w_gate.shape
    wspec = lambda shape: pl.BlockSpec((None,) + shape, lambda e, *_: (e, 0, 0))
    grid_spec = pltpu.PrefetchScalarGridSpec(
        num_scalar_prefetch=4,
        grid=(ne,),
        in_specs=[pl.BlockSpec(memory_space=pl.ANY), wspec((d, f)), wspec((d, f)), wspec((f, d))],
        out_specs=pl.BlockSpec(memory_space=pl.ANY),
        scratch_shapes=[pltpu.VMEM((X_RING, PACK_W, BLK_E, LANES), I32),
                        pltpu.VMEM((Y_RING, PACK_W, BLK_E, LANES), I32),
                        pltpu.VMEM((d, f), BF16), pltpu.VMEM((d, f), BF16), pltpu.VMEM((f, d), BF16),
                        pltpu.SemaphoreType.DMA((X_RING,)), pltpu.SemaphoreType.DMA((Y_RING,))],
    )
    return pl.pallas_call(
        _experts_kernel,
        grid_spec=grid_spec,
        out_shape=jax.ShapeDtypeStruct((PACK_W, p, LANES), I32),
        compiler_params=_cparams(("arbitrary",)),
        name="experts",
    )(nblk, gstart, cnt, n_blocks, xs, w_gate, w_up, w_down)


def _combine_kernel(wt_ref, yk_ref, hfp_ref, x1_ref, gf_ref, wsg_ref, wsu_ref, wsd_ref, g_ref, b_ref, *rest,
                    alpha):
    o_ref = rest[-1]
    hb = _unpack_words([hfp_ref[j] for j in range(PACK_W)]).astype(BF16)
    sg = jnp.dot(hb, wsg_ref[...], preferred_element_type=F32)
    su = jnp.dot(hb, wsu_ref[...], preferred_element_type=F32)
    ffn = jnp.dot((_silu(sg) * su).astype(BF16), wsd_ref[...], preferred_element_type=F32)
    wt = wt_ref[...]
    for k in range(TOP_K):
        yk = _unpack_words([yk_ref[k, j] for j in range(PACK_W)])
        ffn = ffn + wt[:, k:k + 1] * yk
    o_ref[...] = _layer_norm(alpha * x1_ref[...] + gf_ref[...] * ffn, g_ref[...], b_ref[...])


def _combine(w_tok, yk, hfp, x1, mod, ws_gate_bf, ws_up_bf, ws_down_bf, ln_g, ln_b, alpha, tok0, earlier):
    n, d = x1.shape
    nc = yk.shape[2]
    s = n // mod.shape[1]
    tc = min(T_COMB, s, nc)
    nt = s // tc
    t0 = tok0 // tc
    fs = ws_gate_bf.shape[1]
    const = lambda shape: pl.BlockSpec(shape, lambda i: tuple(0 for _ in shape))
    in_specs = [pl.BlockSpec((tc, TOP_K), lambda i: (t0 + i, 0)),
                pl.BlockSpec((TOP_K, PACK_W, tc, LANES), lambda i: (0, 0, i, 0)),
                pl.BlockSpec((PACK_W, tc, LANES), lambda i: (0, t0 + i, 0)),
                pl.BlockSpec((tc, d), lambda i: (t0 + i, 0)),
                pl.BlockSpec((None, None, 1, d), lambda i: (5, (t0 + i) // nt, 0, 0)),
                const((d, fs)), const((d, fs)), const((fs, d)), const((1, d)), const((1, d))]
    args = [w_tok, yk, hfp, x1, mod, ws_gate_bf, ws_up_bf, ws_down_bf, ln_g.reshape(1, d), ln_b.reshape(1, d)]
    aliases = {}
    if earlier is not None:
        in_specs.append(pl.BlockSpec(memory_space=pl.ANY))
        args.append(earlier)
        aliases = {len(args) - 1: 0}
    return pl.pallas_call(
        functools.partial(_combine_kernel, alpha=alpha),
        grid=(nc // tc,),
        in_specs=in_specs,
        out_specs=pl.BlockSpec((tc, d), lambda i: (t0 + i, 0)),
        out_shape=jax.ShapeDtypeStruct((n, d), F32),
        input_output_aliases=aliases,
        compiler_params=_cparams(("parallel",)),
        name="combine",
    )(*args)


def _slot_layout(counts, n_assign):
    cnt = counts[:, 0].astype(I32)
    padded = (cnt + BLK_E - 1) // BLK_E * BLK_E
    pends = jnp.cumsum(padded)
    pstarts = pends - padded
    n_blocks_max = (n_assign + N_EXPERTS * (BLK_E - 1)) // BLK_E
    n_blocks = (pends[-1] // BLK_E).astype(I32).reshape(1)
    return pstarts, pends, padded // BLK_E, pstarts // BLK_E, n_blocks, n_blocks_max * BLK_E


def kernel(x, c, w_ada, b_ada, w_in, w_out, na_rpb, ret_log_decay, ret_gn_g, ln1_g, ln1_b, ln2_g, ln2_b,
           w_router, router_bias, w_gate, w_up, w_down, ws_gate, ws_up, ws_down):
    b, s, d = x.shape
    depth = w_ada.shape[0]
    alpha = (2.0 * depth) ** 0.25
    t = jnp.arange(s, dtype=F32)
    inv_freq = ROPE_BASE ** (-jnp.arange(0, RET_HEAD_DIM, 2, dtype=F32) / RET_HEAD_DIM)
    ang = t[:, None] * inv_freq[None, :]
    cos, sin = jnp.cos(ang), jnp.sin(ang)
    cos2 = jnp.concatenate([cos, cos], axis=-1)
    sin2 = jnp.concatenate([-sin, sin], axis=-1)
    for l in range(depth):
        mod = _mod(c, w_ada[l], b_ada[l])
        proj = _inproj(x, mod, w_in[l].astype(BF16))
        y_na = _natten(proj, _na_bias_table(na_rpb[l]))
        y_r = _retention(proj, ret_log_decay[l], ret_gn_g[l], cos2, sin2)
        wr_t = w_router[l].T
        wr_hi = wr_t.astype(BF16)
        wr_lo = (wr_t - wr_hi.astype(F32)).astype(BF16)
        x1, hfp, logits_t = _outproj(y_na, y_r, x, mod, w_out[l].astype(BF16), ln1_g[l], ln1_b[l],
                                     wr_hi, wr_lo, alpha)
        idx, wts, rank, counts = _route(logits_t, router_bias[l])
        pstarts, pends, nblk, gstart, n_blocks, n_slots = _slot_layout(counts, b * s * TOP_K)
        cnt = counts[:, 0].astype(I32)
        n = b * s
        slot_rows = _slots(idx, rank, pstarts, n_slots)
        xs = _scatter_rows_sc(hfp.reshape(PACK_W * n, LANES), slot_rows, PACK_W * n_slots)
        ys = _experts(nblk, gstart, cnt, n_blocks, xs.reshape(PACK_W, n_slots, LANES),
                      w_gate[l], w_up[l], w_down[l])
        ys_rows = ys.reshape(PACK_W * n_slots, LANES)
        slot_tok = slot_rows.reshape(TOP_K, PACK_W, n)
        ws = (ws_gate[l].astype(BF16), ws_up[l].astype(BF16), ws_down[l].astype(BF16))
        w_tok = wts.T
        x1n = x1.reshape(n, d)
        nc = n // COMBINE_CHUNKS
        out = None
        for ci in range(COMBINE_CHUNKS):
            src = slot_tok[:, :, ci * nc:(ci + 1) * nc].reshape(TOP_K * PACK_W * nc // SC_WINDOW, SC_WINDOW)
            yk = _gather_rows_sc(ys_rows, src).reshape(TOP_K, PACK_W, nc, LANES)
            out = _combine(w_tok, yk, hfp, x1n, mod, *ws, ln2_g[l], ln2_b[l], alpha, ci * nc, out)
        x = out.reshape(b, s, d)
    return x
```

```python
import functools

import jax
import jax.numpy as jnp
import numpy as np
from jax import lax
from jax.experimental import pallas as pl
from jax.experimental.pallas import tpu as pltpu
from jax.experimental.pallas import tpu_sc as plsc

F32 = jnp.float32
BF16 = jnp.bfloat16
U32 = jnp.uint32
I32 = jnp.int32

GRID_W = 64
WIN_R = 8
WIN_C = 16
NA_HEADS = 8
NA_HEAD_DIM = 64
NA_WIDTH = NA_HEADS * NA_HEAD_DIM
RET_HEADS = 4
RET_HEAD_DIM = 128
RET_WIDTH = RET_HEADS * RET_HEAD_DIM
RET_CHUNK = 128
ROPE_BASE = 10000.0
N_EXPERTS = 256
TOP_K = 8
N_GROUPS = 8
TOPK_GROUPS = 4
GROUP_SIZE = N_EXPERTS // N_GROUPS
ROUTED_SCALE = 2.5
LOG2_E = 1.4426950408889634
LN_EPS = 1e-5
GN_EPS = 1e-6

LANES = 128
VMEM_LIMIT = 56 * 1024 * 1024

TM_PROJ = 512
T_ROUTE = 512
T_SLOT = 1024
SC_WINDOW = 128
SC_GATHER_WINDOWS = 2
BLK_E = 512
OUTPROJ_PARTS = 2
T_COMB = 256
COMBINE_CHUNKS = 4
PACK_W = 4
NA_ROWS_PER_ITER = 8
RET_BLOCK = 256
RET_UNROLL = 2
X_RING = 4
X_AHEAD = 3
Y_RING = 3


def _cparams(sem, vmem=VMEM_LIMIT):
    return pltpu.CompilerParams(dimension_semantics=sem, vmem_limit_bytes=vmem)


def _silu(v):
    return v * jax.nn.sigmoid(v)


def _layer_norm(z, g, b):
    mu = jnp.mean(z, -1, keepdims=True)
    zc = z - mu
    var = jnp.mean(zc * zc, -1, keepdims=True)
    return zc * lax.rsqrt(var + LN_EPS) * g + b


def _pack_rows(v):
    half = v.shape[1] // 2
    vb = v.astype(BF16)
    lo = lax.bitcast_convert_type(vb[:, :half].astype(F32), U32) >> 16
    hi = lax.bitcast_convert_type(vb[:, half:].astype(F32), U32)
    w = lax.bitcast_convert_type(hi | lo, I32)
    return [w[:, j * LANES:(j + 1) * LANES] for j in range(half // LANES)]


def _unpack_words(words):
    words = [lax.bitcast_convert_type(w, U32) for w in words]
    lo = [lax.bitcast_convert_type(w << 16, F32) for w in words]
    hi = [lax.bitcast_convert_type(w & jnp.uint32(0xFFFF0000), F32) for w in words]
    return jnp.concatenate(lo + hi, axis=-1)


def _mod_kernel(c_ref, w_ref, b_ref, o_ref):
    cond = _silu(c_ref[...])
    o_ref[0] = jnp.dot(cond, w_ref[...], precision=lax.Precision.HIGHEST,
                       preferred_element_type=F32) + b_ref[0]


def _mod(c, w_ada, b_ada):
    b, d = c.shape
    n6 = w_ada.shape[1] // d
    out = pl.pallas_call(
        _mod_kernel,
        grid=(n6,),
        in_specs=[pl.BlockSpec((b, d), lambda j: (0, 0)),
                  pl.BlockSpec((d, d), lambda j: (0, j)),
                  pl.BlockSpec((1, 1, d), lambda j: (j, 0, 0))],
        out_specs=pl.BlockSpec((1, b, d), lambda j: (j, 0, 0)),
        out_shape=jax.ShapeDtypeStruct((n6, b, d), F32),
        compiler_params=_cparams(("arbitrary",)),
        name="mod",
    )(c, w_ada, b_ada.reshape(n6, 1, d))
    return out.reshape(n6, b, 1, d)


def _mod_spec(which, d):
    return pl.BlockSpec((None, None, 1, d), lambda b, i, which=which: (which, b, 0, 0))


def _inproj_kernel(x_ref, sc_ref, sh_ref, w_ref, o_ref, *, chunk, q_scale):
    h = (x_ref[...] * (1.0 + sc_ref[...]) + sh_ref[...]).astype(BF16)
    for j in range(o_ref.shape[1] // chunk):
        acc = jnp.dot(h, w_ref[:, j * chunk:(j + 1) * chunk], preferred_element_type=F32)
        if j == 0:
            acc = acc * q_scale
        o_ref[:, j * chunk:(j + 1) * chunk] = acc.astype(o_ref.dtype)


def _inproj(x, mod, w_in_bf):
    b, s, d = x.shape
    e = w_in_bf.shape[1]
    tm = min(TM_PROJ, s)
    return pl.pallas_call(
        functools.partial(_inproj_kernel, chunk=NA_WIDTH, q_scale=NA_HEAD_DIM ** -0.5 * LOG2_E),
        grid=(b, s // tm),
        in_specs=[pl.BlockSpec((None, tm, d), lambda bi, i: (bi, i, 0)),
                  _mod_spec(1, d), _mod_spec(0, d),
                  pl.BlockSpec((d, e), lambda bi, i: (0, 0))],
        out_specs=pl.BlockSpec((None, tm, e), lambda bi, i: (bi, i, 0)),
        out_shape=jax.ShapeDtypeStruct((b, s, e), BF16),
        compiler_params=_cparams(("parallel", "parallel")),
        name="inproj",
    )(x, mod, mod, w_in_bf)


def _natten_kernel(q_ref, k_ref, v_ref, bias_ref, o_ref, *, rows):
    kspan = WIN_R * GRID_W

    first = lax.broadcasted_iota(I32, (1, LANES), 1) < NA_HEAD_DIM
    zero = jnp.zeros((), BF16)

    def rows_body(i, carry):
        qrows, krows, scores, probs = {}, {}, {}, {}

        def stage_scores(u):
            r = i * NA_ROWS_PER_ITER + u
            rs = jnp.clip(r - WIN_R // 2, 0, rows - WIN_R)
            vi = r - rs
            qrows[u] = pl.ds(pl.multiple_of(r * GRID_W, GRID_W), GRID_W)
            krows[u] = pl.ds(pl.multiple_of(rs * GRID_W, GRID_W), kspan)
            q = q_ref[qrows[u], :]
            qm = jnp.concatenate([jnp.where(first, q, zero), jnp.where(first, zero, q)], axis=0)
            st = lax.dot_general(k_ref[krows[u], :], qm, (((1,), (1,)), ((), ())), preferred_element_type=F32)
            scores[u] = st + bias_ref[vi]

        def stage_softmax(u):
            st = scores.pop(u)
            p = jnp.exp2(st - jnp.max(st, axis=0, keepdims=True))
            probs[u] = (p * (1.0 / jnp.sum(p, axis=0, keepdims=True))).astype(BF16)

        def stage_values(u):
            res = lax.dot_general(probs.pop(u), v_ref[krows[u], :], (((0,), (0,)), ((), ())),
                                  preferred_element_type=F32)
            o_ref[qrows[u], :] = jnp.where(first, res[:GRID_W], res[GRID_W:]).astype(o_ref.dtype)

        for step in range(NA_ROWS_PER_ITER + 2):
            if step < NA_ROWS_PER_ITER:
                stage_scores(step)
            if 0 <= step - 1 < NA_ROWS_PER_ITER:
                stage_softmax(step - 1)
            if 0 <= step - 2 < NA_ROWS_PER_ITER:
                stage_values(step - 2)
        return carry

    lax.fori_loop(0, rows // NA_ROWS_PER_ITER, rows_body, 0)


def _na_bias_table(rpb):
    w = GRID_W
    cq = jnp.arange(w)
    cs = jnp.clip(cq - WIN_C // 2, 0, w - WIN_C)
    ck = jnp.arange(w)
    col_in = (ck[None, :] >= cs[:, None]) & (ck[None, :] < cs[:, None] + WIN_C)
    dc_idx = jnp.clip(ck[None, :] - cq[:, None] + WIN_C - 1, 0, 2 * WIN_C - 2)
    t = rpb[:, :, dc_idx]
    t = jnp.where(col_in[None, None], t, -jnp.inf)
    vi = jnp.arange(WIN_R)
    kr = jnp.arange(WIN_R)
    dr = kr[None, :] - vi[:, None] + WIN_R - 1
    tb = t[:, dr]
    hp = LANES // NA_HEAD_DIM
    tb = tb.reshape(rpb.shape[0] // hp, hp, WIN_R, WIN_R, w, w)
    tb = tb.transpose(0, 2, 3, 5, 1, 4).reshape(rpb.shape[0] // hp, WIN_R, WIN_R * w, hp * w)
    return tb.astype(F32) * LOG2_E


def _natten(proj, bias_tab):
    b, s, _ = proj.shape
    rows = s // GRID_W
    hp = LANES // NA_HEAD_DIM
    npair = NA_HEADS // hp
    blk = lambda off: pl.BlockSpec((None, s, LANES), lambda bi, p, off=off: (bi, 0, off + p))
    return pl.pallas_call(
        functools.partial(_natten_kernel, rows=rows),
        grid=(b, npair),
        in_specs=[blk(0), blk(npair), blk(2 * npair),
                  pl.BlockSpec((None, WIN_R, WIN_R * GRID_W, hp * GRID_W), lambda bi, p: (p, 0, 0, 0))],
        out_specs=pl.BlockSpec((None, s, LANES), lambda bi, p: (bi, 0, p)),
        out_shape=jax.ShapeDtypeStruct((b, s, NA_WIDTH), BF16),
        compiler_params=_cparams(("parallel", "parallel")),
        name="natten",
    )(proj, proj, proj, bias_tab)


def _retent_kernel(ld_ref, q_ref, k_ref, v_ref, g_ref, cos_ref, sin_ref, gn_ref, o_ref,
                   qs_ref, ks_ref, sb_ref, *, nchunk):
    c = RET_BLOCK
    dh = RET_HEAD_DIM
    h = pl.program_id(1)
    lgf = ld_ref[0, h]
    lgb = ld_ref[1, h]

    cos2 = cos_ref[...]
    sin2 = sin_ref[...]
    qf = q_ref[...].astype(F32)
    qs_ref[...] = qf * cos2 + pltpu.roll(qf, dh // 2, 1) * sin2
    kf = k_ref[...].astype(F32)
    ks_ref[...] = (kf * cos2 + pltpu.roll(kf, dh // 2, 1) * sin2) * (dh ** -0.5)

    ic = lax.broadcasted_iota(I32, (c, 1), 0).astype(F32)
    ir = lax.broadcasted_iota(I32, (1, c), 1).astype(F32)
    diff = ic - ir
    dmat = jnp.where(diff >= 0, jnp.exp(jnp.maximum(diff, 0.0) * lgf),
                     jnp.exp(jnp.maximum(-diff, 0.0) * lgb))
    kdec_f = jnp.exp((c - 1 - ic) * lgf)
    qdec_f = jnp.exp((ic + 1) * lgf)
    kdec_b = jnp.exp(ic * lgb)
    qdec_b = jnp.exp((c - ic) * lgb)
    one = jnp.ones((1, 1), F32)
    cdec_f = jnp.exp(one * (c * lgf))
    cdec_b = jnp.exp(one * (c * lgb))
    tn = (((0,), (0,)), ((), ()))

    def bwd_body(i, sb):
        n = nchunk - 1 - i
        sb_ref[n] = sb
        rows = pl.ds(pl.multiple_of(n * c, c), c)
        kd = (ks_ref[rows, :] * kdec_b).astype(BF16)
        kv = lax.dot_general(kd, v_ref[rows, :], tn, preferred_element_type=F32)
        return cdec_b * sb + kv

    lax.fori_loop(0, nchunk, bwd_body, jnp.zeros((dh, dh), F32), unroll=RET_UNROLL)

    gn = gn_ref[...]

    def fwd_body(n, sf):
        rows = pl.ds(pl.multiple_of(n * c, c), c)
        qn = qs_ref[rows, :]
        kn = ks_ref[rows, :]
        vn = v_ref[rows, :]
        sc = lax.dot_general(qn.astype(BF16), kn.astype(BF16), (((1,), (1,)), ((), ())),
                             preferred_element_type=F32) * dmat
        y = jnp.dot(sc.astype(BF16), vn, preferred_element_type=F32)
        qd = jnp.concatenate([qn * qdec_f, qn * qdec_b], axis=1).astype(BF16)
        st = jnp.concatenate([sf, sb_ref[n]], axis=0).astype(BF16)
        y = y + jnp.dot(qd, st, preferred_element_type=F32)
        mu = jnp.mean(y, -1, keepdims=True)
        yc = y - mu
        var = jnp.mean(yc * yc, -1, keepdims=True)
        yn = yc * lax.rsqrt(var + GN_EPS) * gn
        o_ref[rows, :] = (_silu(g_ref[rows, :].astype(F32)) * yn).astype(o_ref.dtype)
        kv = lax.dot_general((kn * kdec_f).astype(BF16), vn, tn, preferred_element_type=F32)
        return cdec_f * sf + kv

    lax.fori_loop(0, nchunk, fwd_body, jnp.zeros((dh, dh), F32), unroll=RET_UNROLL)


def _retention(proj, log_decay, gn_g, cos2, sin2):
    b, s, _ = proj.shape
    dh = RET_HEAD_DIM
    nchunk = s // RET_BLOCK
    base = 3 * NA_WIDTH // dh
    blk = lambda off: pl.BlockSpec((None, s, dh), lambda bi, h, off=off: (bi, 0, base + off + h))
    full = pl.BlockSpec((s, dh), lambda bi, h: (0, 0))
    return pl.pallas_call(
        functools.partial(_retent_kernel, nchunk=nchunk),
        grid=(b, RET_HEADS),
        in_specs=[pl.BlockSpec(memory_space=pltpu.SMEM),
                  blk(0), blk(RET_HEADS), blk(2 * RET_HEADS), blk(3 * RET_HEADS),
                  full, full,
                  pl.BlockSpec((1, dh), lambda bi, h: (0, h))],
        out_specs=pl.BlockSpec((None, s, dh), lambda bi, h: (bi, 0, h)),
        out_shape=jax.ShapeDtypeStruct((b, s, RET_WIDTH), BF16),
        scratch_shapes=[pltpu.VMEM((s, dh), F32), pltpu.VMEM((s, dh), F32),
                        pltpu.VMEM((nchunk, dh, dh), F32)],
        compiler_params=_cparams(("parallel", "parallel")),
        name="retent",
    )(log_decay, proj, proj, proj, proj, cos2, sin2, gn_g.reshape(1, RET_WIDTH))


def _outproj_kernel(yna_ref, yr_ref, x_ref, ga_ref, sf_ref, shf_ref, wo1_ref, wo2_ref, g_ref, b_ref,
                    wrh_ref, wrl_ref, x1_ref, hfp_ref, lg_ref, *, alpha):
    nt = (((1,), (1,)), ((), ()))
    tm = x_ref.shape[0]
    parts = [pl.ds(p * (tm // OUTPROJ_PARTS), tm // OUTPROJ_PARTS) for p in range(OUTPROJ_PARTS)]
    def mix_of(r):
        return (jnp.dot(yna_ref[r, :], wo1_ref[...], preferred_element_type=F32)
                + jnp.dot(yr_ref[r, :], wo2_ref[...], preferred_element_type=F32))

    nxt = mix_of(parts[0])
    for p, r in enumerate(parts):
        mix = nxt
        if p + 1 < len(parts):
            nxt = mix_of(parts[p + 1])
        x1 = _layer_norm(alpha * x_ref[r, :] + ga_ref[...] * mix, g_ref[...], b_ref[...])
        x1_ref[r, :] = x1
        hf = x1 * (1.0 + sf_ref[...]) + shf_ref[...]
        for j, w in enumerate(_pack_rows(hf)):
            hfp_ref[j, r, :] = w
        hb = hf.astype(BF16)
        hl = (hf - hb.astype(F32)).astype(BF16)
        lg = lax.dot_general(wrh_ref[...], hb, nt, preferred_element_type=F32)
        lg = lg + lax.dot_general(wrh_ref[...], hl, nt, preferred_element_type=F32)
        lg = lg + lax.dot_general(wrl_ref[...], hb, nt, preferred_element_type=F32)
        lg_ref[:, r] = lg


def _outproj(y_na, y_r, x, mod, w_out_bf, ln_g, ln_b, wr_hi, wr_lo, alpha):
    b, s, d = x.shape
    tm = min(TM_PROJ, s)
    nt = s // tm
    ne = wr_hi.shape[0]
    const = lambda shape: pl.BlockSpec(shape, lambda bi, i: tuple(0 for _ in shape))
    x1, hfp, lg = pl.pallas_call(
        functools.partial(_outproj_kernel, alpha=alpha),
        grid=(b, nt),
        in_specs=[pl.BlockSpec((None, tm, NA_WIDTH), lambda bi, i: (bi, i, 0)),
                  pl.BlockSpec((None, tm, RET_WIDTH), lambda bi, i: (bi, i, 0)),
                  pl.BlockSpec((None, tm, d), lambda bi, i: (bi, i, 0)),
                  _mod_spec(2, d), _mod_spec(4, d), _mod_spec(3, d),
                  pl.BlockSpec((NA_WIDTH, d), lambda bi, i: (0, 0)),
                  pl.BlockSpec((RET_WIDTH, d), lambda bi, i: (1, 0)),
                  const((1, d)), const((1, d)), const((ne, d)), const((ne, d))],
        out_specs=[pl.BlockSpec((None, tm, d), lambda bi, i: (bi, i, 0)),
                   pl.BlockSpec((PACK_W, tm, LANES), lambda bi, i: (0, bi * nt + i, 0)),
                   pl.BlockSpec((ne, tm), lambda bi, i: (0, bi * nt + i))],
        out_shape=[jax.ShapeDtypeStruct((b, s, d), F32),
                   jax.ShapeDtypeStruct((PACK_W, b * s, LANES), I32),
                   jax.ShapeDtypeStruct((ne, b * s), F32)],
        compiler_params=_cparams(("parallel", "parallel")),
        name="outproj",
    )(y_na, y_r, x, mod, mod, mod, w_out_bf, w_out_bf, ln_g.reshape(1, d), ln_b.reshape(1, d), wr_hi, wr_lo)
    return x1, hfp, lg


def _route_kernel(lg_ref, rb_ref, idx_ref, w_ref, rank_ref, cnt_ref):
    t = lg_ref.shape[1]
    ninf = -jnp.inf

    @pl.when(pl.program_id(0) == 0)
    def _():
        cnt_ref[...] = jnp.zeros_like(cnt_ref)

    scores = jax.nn.sigmoid(lg_ref[...])
    sel = scores + rb_ref[...]

    io_g = lax.broadcasted_iota(I32, (GROUP_SIZE, t), 0)
    gs_rows = []
    for g in range(N_GROUPS):
        blk = sel[g * GROUP_SIZE:(g + 1) * GROUP_SIZE, :]
        m1 = jnp.max(blk, axis=0, keepdims=True)
        i1 = jnp.min(jnp.where(blk == m1, io_g, GROUP_SIZE), axis=0, keepdims=True)
        m2 = jnp.max(jnp.where(io_g == i1, ninf, blk), axis=0, keepdims=True)
        gs_rows.append(m1 + m2)
    gs = jnp.concatenate(gs_rows, axis=0)

    io8 = lax.broadcasted_iota(I32, (N_GROUPS, t), 0)
    gsel = jnp.zeros((N_GROUPS, t), F32)
    for _ in range(TOPK_GROUPS):
        m = jnp.max(gs, axis=0, keepdims=True)
        gi = jnp.min(jnp.where(gs == m, io8, N_GROUPS), axis=0, keepdims=True)
        hit = io8 == gi
        gsel = jnp.where(hit, 1.0, gsel)
        gs = jnp.where(hit, ninf, gs)

    masked = jnp.concatenate(
        [jnp.where(gsel[g:g + 1, :] > 0.0, sel[g * GROUP_SIZE:(g + 1) * GROUP_SIZE, :], ninf)
         for g in range(N_GROUPS)], axis=0)

    io_e = lax.broadcasted_iota(I32, (N_EXPERTS, t), 0)
    chosen = jnp.zeros((N_EXPERTS, t), F32)
    idx_rows, w_rows = [], []
    for _ in range(TOP_K):
        m = jnp.max(masked, axis=0, keepdims=True)
        ei = jnp.min(jnp.where(masked == m, io_e, N_EXPERTS), axis=0, keepdims=True)
        hit = io_e == ei
        w_rows.append(jnp.sum(jnp.where(hit, scores, 0.0), axis=0, keepdims=True))
        idx_rows.append(ei)
        chosen = jnp.where(hit, 1.0, chosen)
        masked = jnp.where(hit, ninf, masked)
    wk = jnp.concatenate(w_rows, axis=0)
    w_ref[...] = wk / jnp.sum(wk, axis=0, keepdims=True) * ROUTED_SCALE
    idx_ref[...] = jnp.concatenate(idx_rows, axis=0)

    upper = (lax.broadcasted_iota(I32, (t, t), 0) < lax.broadcasted_iota(I32, (t, t), 1))
    prefix = jnp.dot(chosen.astype(BF16), upper.astype(BF16), preferred_element_type=F32)
    rank_full = prefix + cnt_ref[...]
    rank_rows = [jnp.sum(jnp.where(io_e == ei, rank_full, 0.0), axis=0, keepdims=True) for ei in idx_rows]
    rank_ref[...] = jnp.concatenate(rank_rows, axis=0).astype(I32)
    cnt_ref[...] += jnp.sum(chosen, axis=1, keepdims=True)


def _route(logits_t, router_bias):
    ne, n = logits_t.shape
    t = min(T_ROUTE, n)
    kspec = pl.BlockSpec((TOP_K, t), lambda i: (0, i))
    return pl.pallas_call(
        _route_kernel,
        grid=(n // t,),
        in_specs=[pl.BlockSpec((ne, t), lambda i: (0, i)),
                  pl.BlockSpec((ne, 1), lambda i: (0, 0))],
        out_specs=[kspec, kspec, kspec, pl.BlockSpec((ne, 1), lambda i: (0, 0))],
        out_shape=[jax.ShapeDtypeStruct((TOP_K, n), I32),
                   jax.ShapeDtypeStruct((TOP_K, n), F32),
                   jax.ShapeDtypeStruct((TOP_K, n), I32),
                   jax.ShapeDtypeStruct((ne, 1), F32)],
        compiler_params=_cparams(("arbitrary",)),
        name="route",
    )(logits_t, router_bias.reshape(ne, 1))


def _block(ref, g):
    return ref.at[:, pl.ds(pl.multiple_of(g * BLK_E, BLK_E), BLK_E), :]


def _slots_kernel(idx_ref, rank_ref, ps_ref, o_ref, *, n_slots):
    t = idx_ref.shape[1]
    io = lax.broadcasted_iota(I32, (N_EXPERTS, t), 0)
    ps = ps_ref[...]
    for k in range(TOP_K):
        hit = io == idx_ref[k:k + 1, :]
        slot = jnp.sum(jnp.where(hit, ps, 0), axis=0, keepdims=True) + rank_ref[k:k + 1, :]
        for j in range(PACK_W):
            o_ref[k, j:j + 1, :] = slot + j * n_slots


def _slots(idx, rank, pstarts, n_slots):
    n = idx.shape[1]
    t = min(T_SLOT, n)
    kspec = pl.BlockSpec((TOP_K, t), lambda i: (0, i))
    out = pl.pallas_call(
        functools.partial(_slots_kernel, n_slots=n_slots),
        grid=(n // t,),
        in_specs=[kspec, kspec, pl.BlockSpec((N_EXPERTS, 1), lambda i: (0, 0))],
        out_specs=pl.BlockSpec((TOP_K, PACK_W, t), lambda i: (0, 0, i)),
        out_shape=jax.ShapeDtypeStruct((TOP_K, PACK_W, n), I32),
        compiler_params=_cparams(("parallel",)),
        name="slots",
    )(idx, rank, pstarts.reshape(N_EXPERTS, 1))
    return out.reshape(TOP_K, PACK_W * n)


def _sc_mesh():
    return plsc.VectorSubcoreMesh(core_axis_name="core", subcore_axis_name="subcore")


def _scatter_rows_sc(rows, dest_rows, n_out):
    n_rows = rows.shape[0]

    @functools.partial(pl.kernel, mesh=_sc_mesh(), scratch_types=[pltpu.SemaphoreType.DMA],
                       out_type=jax.ShapeDtypeStruct((n_out, LANES), I32))
    def scatter_rows(x_hbm, i_hbm, o_hbm, sem):
        def body(x_vmem, i_vmem):
            copies = [pltpu.async_copy(x_vmem, o_hbm.at[i_vmem.at[k]], sem) for k in range(TOP_K)]
            for c in copies:
                c.wait()

        pltpu.emit_pipeline(
            body,
            grid=(n_rows // SC_WINDOW,),
            in_specs=[pl.BlockSpec((SC_WINDOW, LANES), lambda i: (i, 0)),
                      pl.BlockSpec((TOP_K, SC_WINDOW), lambda i: (0, i))],
            out_specs=[],
            core_axis_name=("core", "subcore"),
            dimension_semantics=(pltpu.PARALLEL,),
        )(x_hbm, i_hbm)

    return scatter_rows(rows, dest_rows)


def _gather_rows_sc(rows, src_rows):
    nwin = src_rows.shape[0]

    @functools.partial(pl.kernel, mesh=_sc_mesh(), scratch_types=[pltpu.SemaphoreType.DMA],
                       out_type=jax.ShapeDtypeStruct((nwin * SC_WINDOW, LANES), I32))
    def gather_rows(x_hbm, i_hbm, o_hbm, sem):
        def body(i_vmem, o_vmem):
            copies = [pltpu.async_copy(x_hbm.at[i_vmem.at[w]], o_vmem.at[pl.ds(w * SC_WINDOW, SC_WINDOW)], sem)
                      for w in range(SC_GATHER_WINDOWS)]
            for c in copies:
                c.wait()

        pltpu.emit_pipeline(
            body,
            grid=(nwin // SC_GATHER_WINDOWS,),
            in_specs=[pl.BlockSpec((SC_GATHER_WINDOWS, SC_WINDOW), lambda i: (i, 0))],
            out_specs=[pl.BlockSpec((SC_GATHER_WINDOWS * SC_WINDOW, LANES), lambda i: (i, 0))],
            core_axis_name=("core", "subcore"),
            dimension_semantics=(pltpu.PARALLEL,),
        )(i_hbm, o_hbm)

    return gather_rows(rows, src_rows)


def _experts_kernel(nblk_ref, gstart_ref, cnt_ref, nb_ref, xs_ref, wg_ref, wu_ref, wd_ref, ys_ref,
                    xbuf_ref, ybuf_ref, wgb_ref, wub_ref, wdb_ref, xsem, ysem):
    e = pl.program_id(0)
    total = nb_ref[0]
    nb_max = ys_ref.shape[1] // BLK_E

    def x_copy(g):
        slot = g % X_RING
        return pltpu.make_async_copy(_block(xs_ref, g), xbuf_ref.at[slot], xsem.at[slot])

    def y_copy(g):
        slot = g % Y_RING
        return pltpu.make_async_copy(ybuf_ref.at[slot], _block(ys_ref, g), ysem.at[slot])

    @pl.when(e == 0)
    def _():
        for g in range(X_AHEAD):
            @pl.when(g < total)
            def _():
                x_copy(g).start()

    @pl.when(nblk_ref[e] > 0)
    def _():
        wgb_ref[...] = wg_ref[...].astype(BF16)
        wub_ref[...] = wu_ref[...].astype(BF16)
        wdb_ref[...] = wd_ref[...].astype(BF16)

    def fetch(b):
        g = gstart_ref[e] + b
        x_copy(g).wait()

        @pl.when(g + X_AHEAD < total)
        def _():
            x_copy(g + X_AHEAD).start()

        xb = _unpack_words([xbuf_ref[g % X_RING, j] for j in range(PACK_W)]).astype(BF16)
        live = lax.broadcasted_iota(I32, (BLK_E, 1), 0) < cnt_ref[e] - b * BLK_E
        return jnp.where(live, xb, jnp.zeros((), BF16))

    def hidden(xb):
        hg = jnp.dot(xb, wgb_ref[...], preferred_element_type=F32)
        hu = jnp.dot(xb, wub_ref[...], preferred_element_type=F32)
        return (_silu(hg) * hu).astype(BF16)

    def finish(b, act):
        g = gstart_ref[e] + b
        words = _pack_rows(jnp.dot(act, wdb_ref[...], preferred_element_type=F32))

        @pl.when(g >= Y_RING)
        def _():
            y_copy(g - Y_RING).wait()

        for j, w in enumerate(words):
            ybuf_ref[g % Y_RING, j] = w
        y_copy(g).start()

    def block(b, carry):
        finish(b, hidden(fetch(b)))
        return carry

    lax.fori_loop(0, nblk_ref[e], block, 0)

    @pl.when(e == pl.num_programs(0) - 1)
    def _():
        for back in range(Y_RING):
            @pl.when(total - 1 - back >= 0)
            def _():
                y_copy(total - 1 - back).wait()

        ybuf_ref[0] = jnp.zeros(ybuf_ref.shape[1:], ybuf_ref.dtype)

        def tail_copy(g):
            return pltpu.make_async_copy(ybuf_ref.at[0], _block(ys_ref, g), ysem.at[0])

        def fill(g, carry):
            tail_copy(g).start()
            return carry

        lax.fori_loop(total, nb_max, fill, 0)

        def drain(g, carry):
            tail_copy(g).wait()
            return carry

        lax.fori_loop(total, nb_max, drain, 0)


def _experts(nblk, gstart, cnt, n_blocks, xs, w_gate, w_up, w_down):
    p = xs.shape[1]
    ne, d, f = w_gate.shape
    wspec = lambda shape: pl.BlockSpec((None,) + shape, lambda e, *_: (e, 0, 0))
    grid_spec = pltpu.PrefetchScalarGridSpec(
        num_scalar_prefetch=4,
        grid=(ne,),
        in_specs=[pl.BlockSpec(memory_space=pl.ANY), wspec((d, f)), wspec((d, f)), wspec((f, d))],
        out_specs=pl.BlockSpec(memory_space=pl.ANY),
        scratch_shapes=[pltpu.VMEM((X_RING, PACK_W, BLK_E, LANES), I32),
                        pltpu.VMEM((Y_RING, PACK_W, BLK_E, LANES), I32),
                        pltpu.VMEM((d, f), BF16), pltpu.VMEM((d, f), BF16), pltpu.VMEM((f, d), BF16),
                        pltpu.SemaphoreType.DMA((X_RING,)), pltpu.SemaphoreType.DMA((Y_RING,))],
    )
    return pl.pallas_call(
        _experts_kernel,
        grid_spec=grid_spec,
        out_shape=jax.ShapeDtypeStruct((PACK_W, p, LANES), I32),
        compiler_params=_cparams(("arbitrary",)),
        name="experts",
    )(nblk, gstart, cnt, n_blocks, xs, w_gate, w_up, w_down)


def _combine_kernel(wt_ref, yk_ref, hfp_ref, x1_ref, gf_ref, wsg_ref, wsu_ref, wsd_ref, g_ref, b_ref, *rest,
                    alpha):
    o_ref = rest[-1]
    hb = _unpack_words([hfp_ref[j] for j in range(PACK_W)]).astype(BF16)
    sg = jnp.dot(hb, wsg_ref[...], preferred_element_type=F32)
    su = jnp.dot(hb, wsu_ref[...], preferred_element_type=F32)
    ffn = jnp.dot((_silu(sg) * su).astype(BF16), wsd_ref[...], preferred_element_type=F32)
    wt = wt_ref[...]
    for k in range(TOP_K):
        yk = _unpack_words([yk_ref[k, j] for j in range(PACK_W)])
        ffn = ffn + wt[:, k:k + 1] * yk
    o_ref[...] = _layer_norm(alpha * x1_ref[...] + gf_ref[...] * ffn, g_ref[...], b_ref[...])


def _combine(w_tok, yk, hfp, x1, mod, ws_gate_bf, ws_up_bf, ws_down_bf, ln_g, ln_b, alpha, tok0, earlier):
    n, d = x1.shape
    nc = yk.shape[2]
    s = n // mod.shape[1]
    tc = min(T_COMB, s, nc)
    nt = s // tc
    t0 = tok0 // tc
    fs = ws_gate_bf.shape[1]
    const = lambda shape: pl.BlockSpec(shape, lambda i: tuple(0 for _ in shape))
    in_specs = [pl.BlockSpec((tc, TOP_K), lambda i: (t0 + i, 0)),
                pl.BlockSpec((TOP_K, PACK_W, tc, LANES), lambda i: (0, 0, i, 0)),
                pl.BlockSpec((PACK_W, tc, LANES), lambda i: (0, t0 + i, 0)),
                pl.BlockSpec((tc, d), lambda i: (t0 + i, 0)),
                pl.BlockSpec((None, None, 1, d), lambda i: (5, (t0 + i) // nt, 0, 0)),
                const((d, fs)), const((d, fs)), const((fs, d)), const((1, d)), const((1, d))]
    args = [w_tok, yk, hfp, x1, mod, ws_gate_bf, ws_up_bf, ws_down_bf, ln_g.reshape(1, d), ln_b.reshape(1, d)]
    aliases = {}
    if earlier is not None:
        in_specs.append(pl.BlockSpec(memory_space=pl.ANY))
        args.append(earlier)
        aliases = {len(args) - 1: 0}
    return pl.pallas_call(
        functools.partial(_combine_kernel, alpha=alpha),
        grid=(nc // tc,),
        in_specs=in_specs,
        out_specs=pl.BlockSpec((tc, d), lambda i: (t0 + i, 0)),
        out_shape=jax.ShapeDtypeStruct((n, d), F32),
        input_output_aliases=aliases,
        compiler_params=_cparams(("parallel",)),
        name="combine",
    )(*args)


def _slot_layout(counts, n_assign):
    cnt = counts[:, 0].astype(I32)
    padded = (cnt + BLK_E - 1) // BLK_E * BLK_E
    pends = jnp.cumsum(padded)
    pstarts = pends - padded
    n_blocks_max = (n_assign + N_EXPERTS * (BLK_E - 1)) // BLK_E
    n_blocks = (pends[-1] // BLK_E).astype(I32).reshape(1)
    return pstarts, pends, padded // BLK_E, pstarts // BLK_E, n_blocks, n_blocks_max * BLK_E


def kernel(x, c, w_ada, b_ada, w_in, w_out, na_rpb, ret_log_decay, ret_gn_g, ln1_g, ln1_b, ln2_g, ln2_b,
           w_router, router_bias, w_gate, w_up, w_down, ws_gate, ws_up, ws_down):
    b, s, d = x.shape
    depth = w_ada.shape[0]
    alpha = (2.0 * depth) ** 0.25
    t = jnp.arange(s, dtype=F32)
    inv_freq = ROPE_BASE ** (-jnp.arange(0, RET_HEAD_DIM, 2, dtype=F32) / RET_HEAD_DIM)
    ang = t[:, None] * inv_freq[None, :]
    cos, sin = jnp.cos(ang), jnp.sin(ang)
    cos2 = jnp.concatenate([cos, cos], axis=-1)
    sin2 = jnp.concatenate([-sin, sin], axis=-1)
    for l in range(depth):
        mod = _mod(c, w_ada[l], b_ada[l])
        proj = _inproj(x, mod, w_in[l].astype(BF16))
        y_na = _natten(proj, _na_bias_table(na_rpb[l]))
        y_r = _retention(proj, ret_log_decay[l], ret_gn_g[l], cos2, sin2)
        wr_t = w_router[l].T
        wr_hi = wr_t.astype(BF16)
        wr_lo = (wr_t - wr_hi.astype(F32)).astype(BF16)
        x1, hfp, logits_t = _outproj(y_na, y_r, x, mod, w_out[l].astype(BF16), ln1_g[l], ln1_b[l],
                                     wr_hi, wr_lo, alpha)
        idx, wts, rank, counts = _route(logits_t, router_bias[l])
        pstarts, pends, nblk, gstart, n_blocks, n_slots = _slot_layout(counts, b * s * TOP_K)
        cnt = counts[:, 0].astype(I32)
        n = b * s
        slot_rows = _slots(idx, rank, pstarts, n_slots)
        xs = _scatter_rows_sc(hfp.reshape(PACK_W * n, LANES), slot_rows, PACK_W * n_slots)
        ys = _experts(nblk, gstart, cnt, n_blocks, xs.reshape(PACK_W, n_slots, LANES),
                      w_gate[l], w_up[l], w_down[l])
        ys_rows = ys.reshape(PACK_W * n_slots, LANES)
        slot_tok = slot_rows.reshape(TOP_K, PACK_W, n)
        ws = (ws_gate[l].astype(BF16), ws_up[l].astype(BF16), ws_down[l].astype(BF16))
        w_tok = wts.T
        x1n = x1.reshape(n, d)
        nc = n // COMBINE_CHUNKS
        out = None
        for ci in range(COMBINE_CHUNKS):
            src = slot_tok[:, :, ci * nc:(ci + 1) * nc].reshape(TOP_K * PACK_W * nc // SC_WINDOW, SC_WINDOW)
            yk = _gather_rows_sc(ys_rows, src).reshape(TOP_K, PACK_W, nc, LANES)
            out = _combine(w_tok, yk, hfp, x1n, mod, *ws, ln2_g[l], ln2_b[l], alpha, ci * nc, out)
        x = out.reshape(b, s, d)
    return x
```

```python
import functools
import math

import jax
import jax.numpy as jnp
import numpy as np
from jax import lax
from jax.experimental import pallas as pl
from jax.experimental.pallas import tpu as pltpu
from jax.experimental.pallas import tpu_sc as plsc

F32 = jnp.float32
BF16 = jnp.bfloat16
U32 = jnp.uint32
I32 = jnp.int32

GRID_W = 64
WIN_R = 8
WIN_C = 16
NA_HEADS = 8
NA_HEAD_DIM = 64
NA_WIDTH = NA_HEADS * NA_HEAD_DIM
RET_HEADS = 4
RET_HEAD_DIM = 128
RET_WIDTH = RET_HEADS * RET_HEAD_DIM
RET_CHUNK = 128
ROPE_BASE = 10000.0
N_EXPERTS = 256
TOP_K = 8
N_GROUPS = 8
TOPK_GROUPS = 4
GROUP_SIZE = N_EXPERTS // N_GROUPS
ROUTED_SCALE = 2.5
LOG2_E = 1.4426950408889634
LN_EPS = 1e-5
GN_EPS = 1e-6

LANES = 128
VMEM_LIMIT = 56 * 1024 * 1024

TM_PROJ = 512
T_ROUTE = 512
T_SLOT = 1024
SC_WINDOW = 128
SC_GATHER_WINDOWS = 2
BLK_E = 512
OUTPROJ_PARTS = 2
T_COMB = 256
COMBINE_CHUNKS = 8
PACK_W = 4
NA_ROWS_PER_ITER = 32
RET_BLOCK = 256
RET_UNROLL = 8
X_RING = 4
X_AHEAD = 3
Y_RING = 3


def _cparams(sem, vmem=VMEM_LIMIT):
    return pltpu.CompilerParams(dimension_semantics=sem, vmem_limit_bytes=vmem)


def _silu(v):
    return v * jax.nn.sigmoid(v)


def _layer_norm(z, g, b):
    mu = jnp.mean(z, -1, keepdims=True)
    zc = z - mu
    var = jnp.mean(zc * zc, -1, keepdims=True)
    return zc * lax.rsqrt(var + LN_EPS) * g + b


def _pack_rows(v):
    half = v.shape[1] // 2
    vb = v.astype(BF16)
    lo = lax.bitcast_convert_type(vb[:, :half].astype(F32), U32) >> 16
    hi = lax.bitcast_convert_type(vb[:, half:].astype(F32), U32)
    w = lax.bitcast_convert_type(hi | lo, I32)
    return [w[:, j * LANES:(j + 1) * LANES] for j in range(half // LANES)]


def _unpack_words(words):
    words = [lax.bitcast_convert_type(w, U32) for w in words]
    lo = [lax.bitcast_convert_type(w << 16, F32) for w in words]
    hi = [lax.bitcast_convert_type(w & jnp.uint32(0xFFFF0000), F32) for w in words]
    return jnp.concatenate(lo + hi, axis=-1)


def _mod_kernel(c_ref, w_ref, b_ref, o_ref):
    cond = _silu(c_ref[...])
    o_ref[0] = jnp.dot(cond, w_ref[...], precision=lax.Precision.HIGHEST,
                       preferred_element_type=F32) + b_ref[0]


def _mod(c, w_ada, b_ada):
    b, d = c.shape
    n6 = w_ada.shape[1] // d
    out = pl.pallas_call(
        _mod_kernel,
        grid=(n6,),
        in_specs=[pl.BlockSpec((b, d), lambda j: (0, 0)),
                  pl.BlockSpec((d, d), lambda j: (0, j)),
                  pl.BlockSpec((1, 1, d), lambda j: (j, 0, 0))],
        out_specs=pl.BlockSpec((1, b, d), lambda j: (j, 0, 0)),
        out_shape=jax.ShapeDtypeStruct((n6, b, d), F32),
        compiler_params=_cparams(("arbitrary",)),
        name="mod",
    )(c, w_ada, b_ada.reshape(n6, 1, d))
    return out.reshape(n6, b, 1, d)


def _mod_spec(which, d):
    return pl.BlockSpec((None, None, 1, d), lambda b, i, which=which: (which, b, 0, 0))


def _inproj_kernel(x_ref, sc_ref, sh_ref, w_ref, o_ref, *, chunk, q_scale):
    h = (x_ref[...] * (1.0 + sc_ref[...]) + sh_ref[...]).astype(BF16)
    for j in range(o_ref.shape[1] // chunk):
        acc = jnp.dot(h, w_ref[:, j * chunk:(j + 1) * chunk], preferred_element_type=F32)
        if j == 0:
            acc = acc * q_scale
        o_ref[:, j * chunk:(j + 1) * chunk] = acc.astype(o_ref.dtype)


def _inproj(x, mod, w_in_bf):
    b, s, d = x.shape
    e = w_in_bf.shape[1]
    tm = min(TM_PROJ, s)
    return pl.pallas_call(
        functools.partial(_inproj_kernel, chunk=NA_WIDTH, q_scale=NA_HEAD_DIM ** -0.5 * LOG2_E),
        grid=(b, s // tm),
        in_specs=[pl.BlockSpec((None, tm, d), lambda bi, i: (bi, i, 0)),
                  _mod_spec(1, d), _mod_spec(0, d),
                  pl.BlockSpec((d, e), lambda bi, i: (0, 0))],
        out_specs=pl.BlockSpec((None, tm, e), lambda bi, i: (bi, i, 0)),
        out_shape=jax.ShapeDtypeStruct((b, s, e), BF16),
        compiler_params=_cparams(("parallel", "parallel")),
        name="inproj",
    )(x, mod, mod, w_in_bf)


def _natten_kernel(q_ref, k_ref, v_ref, bias_ref, o_ref, *, rows, rows_per_iter):
    kspan = WIN_R * GRID_W

    first = lax.broadcasted_iota(I32, (1, LANES), 1) < NA_HEAD_DIM
    zero = jnp.zeros((), BF16)

    def rows_body(i, carry):
        qrows, krows, scores, probs = {}, {}, {}, {}

        def stage_scores(u):
            r = i * rows_per_iter + u
            rs = jnp.clip(r - WIN_R // 2, 0, rows - WIN_R)
            vi = r - rs
            qrows[u] = pl.ds(pl.multiple_of(r * GRID_W, GRID_W), GRID_W)
            krows[u] = pl.ds(pl.multiple_of(rs * GRID_W, GRID_W), kspan)
            q = q_ref[qrows[u], :]
            qm = jnp.concatenate([jnp.where(first, q, zero), jnp.where(first, zero, q)], axis=0)
            st = lax.dot_general(k_ref[krows[u], :], qm, (((1,), (1,)), ((), ())), preferred_element_type=F32)
            scores[u] = st + bias_ref[vi]

        def stage_softmax(u):
            st = scores.pop(u)
            p = jnp.exp2(st - jnp.max(st, axis=0, keepdims=True))
            probs[u] = (p * (1.0 / jnp.sum(p, axis=0, keepdims=True))).astype(BF16)

        def stage_values(u):
            res = lax.dot_general(probs.pop(u), v_ref[krows[u], :], (((0,), (0,)), ((), ())),
                                  preferred_element_type=F32)
            o_ref[qrows[u], :] = jnp.where(first, res[:GRID_W], res[GRID_W:]).astype(o_ref.dtype)

        for step in range(rows_per_iter + 2):
            if step < rows_per_iter:
                stage_scores(step)
            if 0 <= step - 1 < rows_per_iter:
                stage_softmax(step - 1)
            if 0 <= step - 2 < rows_per_iter:
                stage_values(step - 2)
        return carry

    lax.fori_loop(0, rows // rows_per_iter, rows_body, 0)


def _na_bias_table(rpb):
    w = GRID_W
    cq = jnp.arange(w)
    cs = jnp.clip(cq - WIN_C // 2, 0, w - WIN_C)
    ck = jnp.arange(w)
    col_in = (ck[None, :] >= cs[:, None]) & (ck[None, :] < cs[:, None] + WIN_C)
    dc_idx = jnp.clip(ck[None, :] - cq[:, None] + WIN_C - 1, 0, 2 * WIN_C - 2)
    t = rpb[:, :, dc_idx]
    t = jnp.where(col_in[None, None], t, -jnp.inf)
    vi = jnp.arange(WIN_R)
    kr = jnp.arange(WIN_R)
    dr = kr[None, :] - vi[:, None] + WIN_R - 1
    tb = t[:, dr]
    hp = LANES // NA_HEAD_DIM
    tb = tb.reshape(rpb.shape[0] // hp, hp, WIN_R, WIN_R, w, w)
    tb = tb.transpose(0, 2, 3, 5, 1, 4).reshape(rpb.shape[0] // hp, WIN_R, WIN_R * w, hp * w)
    return tb.astype(F32) * LOG2_E


def _natten(proj, bias_tab):
    b, s, _ = proj.shape
    rows = s // GRID_W
    hp = LANES // NA_HEAD_DIM
    npair = NA_HEADS // hp
    blk = lambda off: pl.BlockSpec((None, s, LANES), lambda bi, p, off=off: (bi, 0, off + p))
    return pl.pallas_call(
        functools.partial(_natten_kernel, rows=rows, rows_per_iter=math.gcd(rows, NA_ROWS_PER_ITER)),
        grid=(b, npair),
        in_specs=[blk(0), blk(npair), blk(2 * npair),
                  pl.BlockSpec((None, WIN_R, WIN_R * GRID_W, hp * GRID_W), lambda bi, p: (p, 0, 0, 0))],
        out_specs=pl.BlockSpec((None, s, LANES), lambda bi, p: (bi, 0, p)),
        out_shape=jax.ShapeDtypeStruct((b, s, NA_WIDTH), BF16),
        compiler_params=_cparams(("parallel", "parallel")),
        name="natten",
    )(proj, proj, proj, bias_tab)


def _retent_kernel(ld_ref, q_ref, k_ref, v_ref, g_ref, cos_ref, sin_ref, gn_ref, o_ref,
                   qs_ref, ks_ref, sb_ref, *, nchunk):
    c = RET_BLOCK
    dh = RET_HEAD_DIM
    h = pl.program_id(1)
    lgf = ld_ref[0, h]
    lgb = ld_ref[1, h]

    cos2 = cos_ref[...]
    sin2 = sin_ref[...]
    qf = q_ref[...].astype(F32)
    qs_ref[...] = qf * cos2 + pltpu.roll(qf, dh // 2, 1) * sin2
    kf = k_ref[...].astype(F32)
    ks_ref[...] = (kf * cos2 + pltpu.roll(kf, dh // 2, 1) * sin2) * (dh ** -0.5)

    ic = lax.broadcasted_iota(I32, (c, 1), 0).astype(F32)
    ir = lax.broadcasted_iota(I32, (1, c), 1).astype(F32)
    diff = ic - ir
    dmat = jnp.where(diff >= 0, jnp.exp(jnp.maximum(diff, 0.0) * lgf),
                     jnp.exp(jnp.maximum(-diff, 0.0) * lgb))
    kdec_f = jnp.exp((c - 1 - ic) * lgf)
    qdec_f = jnp.exp((ic + 1) * lgf)
    kdec_b = jnp.exp(ic * lgb)
    qdec_b = jnp.exp((c - ic) * lgb)
    one = jnp.ones((1, 1), F32)
    cdec_f = jnp.exp(one * (c * lgf))
    cdec_b = jnp.exp(one * (c * lgb))
    tn = (((0,), (0,)), ((), ()))

    def bwd_body(i, sb):
        n = nchunk - 1 - i
        sb_ref[n] = sb
        rows = pl.ds(pl.multiple_of(n * c, c), c)
        kd = (ks_ref[rows, :] * kdec_b).astype(BF16)
        kv = lax.dot_general(kd, v_ref[rows, :], tn, preferred_element_type=F32)
        return cdec_b * sb + kv

    lax.fori_loop(0, nchunk, bwd_body, jnp.zeros((dh, dh), F32), unroll=min(RET_UNROLL, nchunk))

    gn = gn_ref[...]

    def fwd_body(n, sf):
        rows = pl.ds(pl.multiple_of(n * c, c), c)
        qn = qs_ref[rows, :]
        kn = ks_ref[rows, :]
        vn = v_ref[rows, :]
        sc = lax.dot_general(qn.astype(BF16), kn.astype(BF16), (((1,), (1,)), ((), ())),
                             preferred_element_type=F32) * dmat
        y = jnp.dot(sc.astype(BF16), vn, preferred_element_type=F32)
        qd = jnp.concatenate([qn * qdec_f, qn * qdec_b], axis=1).astype(BF16)
        st = jnp.concatenate([sf, sb_ref[n]], axis=0).astype(BF16)
        y = y + jnp.dot(qd, st, preferred_element_type=F32)
        mu = jnp.mean(y, -1, keepdims=True)
        yc = y - mu
        var = jnp.mean(yc * yc, -1, keepdims=True)
        yn = yc * lax.rsqrt(var + GN_EPS) * gn
        o_ref[rows, :] = (_silu(g_ref[rows, :].astype(F32)) * yn).astype(o_ref.dtype)
        kv = lax.dot_general((kn * kdec_f).astype(BF16), vn, tn, preferred_element_type=F32)
        return cdec_f * sf + kv

    lax.fori_loop(0, nchunk, fwd_body, jnp.zeros((dh, dh), F32), unroll=min(RET_UNROLL, nchunk))


def _retention(proj, log_decay, gn_g, cos2, sin2):
    b, s, _ = proj.shape
    dh = RET_HEAD_DIM
    nchunk = s // RET_BLOCK
    base = 3 * NA_WIDTH // dh
    blk = lambda off: pl.BlockSpec((None, s, dh), lambda bi, h, off=off: (bi, 0, base + off + h))
    full = pl.BlockSpec((s, dh), lambda bi, h: (0, 0))
    return pl.pallas_call(
        functools.partial(_retent_kernel, nchunk=nchunk),
        grid=(b, RET_HEADS),
        in_specs=[pl.BlockSpec(memory_space=pltpu.SMEM),
                  blk(0), blk(RET_HEADS), blk(2 * RET_HEADS), blk(3 * RET_HEADS),
                  full, full,
                  pl.BlockSpec((1, dh), lambda bi, h: (0, h))],
        out_specs=pl.BlockSpec((None, s, dh), lambda bi, h: (bi, 0, h)),
        out_shape=jax.ShapeDtypeStruct((b, s, RET_WIDTH), BF16),
        scratch_shapes=[pltpu.VMEM((s, dh), F32), pltpu.VMEM((s, dh), F32),
                        pltpu.VMEM((nchunk, dh, dh), F32)],
        compiler_params=_cparams(("parallel", "parallel")),
        name="retent",
    )(log_decay, proj, proj, proj, proj, cos2, sin2, gn_g.reshape(1, RET_WIDTH))


def _outproj_kernel(yna_ref, yr_ref, x_ref, ga_ref, sf_ref, shf_ref, wo1_ref, wo2_ref, g_ref, b_ref,
                    wrh_ref, wrl_ref, x1_ref, hfp_ref, lg_ref, *, alpha):
    nt = (((1,), (1,)), ((), ()))
    tm = x_ref.shape[0]
    parts = [pl.ds(p * (tm // OUTPROJ_PARTS), tm // OUTPROJ_PARTS) for p in range(OUTPROJ_PARTS)]
    def mix_of(r):
        return (jnp.dot(yna_ref[r, :], wo1_ref[...], preferred_element_type=F32)
                + jnp.dot(yr_ref[r, :], wo2_ref[...], preferred_element_type=F32))

    nxt = mix_of(parts[0])
    for p, r in enumerate(parts):
        mix = nxt
        if p + 1 < len(parts):
            nxt = mix_of(parts[p + 1])
        x1 = _layer_norm(alpha * x_ref[r, :] + ga_ref[...] * mix, g_ref[...], b_ref[...])
        x1_ref[r, :] = x1
        hf = x1 * (1.0 + sf_ref[...]) + shf_ref[...]
        for j, w in enumerate(_pack_rows(hf)):
            hfp_ref[j, r, :] = w
        hb = hf.astype(BF16)
        hl = (hf - hb.astype(F32)).astype(BF16)
        lg = lax.dot_general(wrh_ref[...], hb, nt, preferred_element_type=F32)
        lg = lg + lax.dot_general(wrh_ref[...], hl, nt, preferred_element_type=F32)
        lg = lg + lax.dot_general(wrl_ref[...], hb, nt, preferred_element_type=F32)
        lg_ref[:, r] = lg


def _outproj(y_na, y_r, x, mod, w_out_bf, ln_g, ln_b, wr_hi, wr_lo, alpha):
    b, s, d = x.shape
    tm = min(TM_PROJ, s)
    nt = s // tm
    ne = wr_hi.shape[0]
    const = lambda shape: pl.BlockSpec(shape, lambda bi, i: tuple(0 for _ in shape))
    x1, hfp, lg = pl.pallas_call(
        functools.partial(_outproj_kernel, alpha=alpha),
        grid=(b, nt),
        in_specs=[pl.BlockSpec((None, tm, NA_WIDTH), lambda bi, i: (bi, i, 0)),
                  pl.BlockSpec((None, tm, RET_WIDTH), lambda bi, i: (bi, i, 0)),
                  pl.BlockSpec((None, tm, d), lambda bi, i: (bi, i, 0)),
                  _mod_spec(2, d), _mod_spec(4, d), _mod_spec(3, d),
                  pl.BlockSpec((NA_WIDTH, d), lambda bi, i: (0, 0)),
                  pl.BlockSpec((RET_WIDTH, d), lambda bi, i: (1, 0)),
                  const((1, d)), const((1, d)), const((ne, d)), const((ne, d))],
        out_specs=[pl.BlockSpec((None, tm, d), lambda bi, i: (bi, i, 0)),
                   pl.BlockSpec((PACK_W, tm, LANES), lambda bi, i: (0, bi * nt + i, 0)),
                   pl.BlockSpec((ne, tm), lambda bi, i: (0, bi * nt + i))],
        out_shape=[jax.ShapeDtypeStruct((b, s, d), F32),
                   jax.ShapeDtypeStruct((PACK_W, b * s, LANES), I32),
                   jax.ShapeDtypeStruct((ne, b * s), F32)],
        compiler_params=_cparams(("parallel", "parallel")),
        name="outproj",
    )(y_na, y_r, x, mod, mod, mod, w_out_bf, w_out_bf, ln_g.reshape(1, d), ln_b.reshape(1, d), wr_hi, wr_lo)
    return x1, hfp, lg


def _route_kernel(lg_ref, rb_ref, idx_ref, w_ref, rank_ref, cnt_ref):
    t = lg_ref.shape[1]
    ninf = -jnp.inf

    @pl.when(pl.program_id(0) == 0)
    def _():
        cnt_ref[...] = jnp.zeros_like(cnt_ref)

    scores = jax.nn.sigmoid(lg_ref[...])
    sel = scores + rb_ref[...]

    io_g = lax.broadcasted_iota(I32, (GROUP_SIZE, t), 0)
    gs_rows = []
    for g in range(N_GROUPS):
        blk = sel[g * GROUP_SIZE:(g + 1) * GROUP_SIZE, :]
        m1 = jnp.max(blk, axis=0, keepdims=True)
        i1 = jnp.min(jnp.where(blk == m1, io_g, GROUP_SIZE), axis=0, keepdims=True)
        m2 = jnp.max(jnp.where(io_g == i1, ninf, blk), axis=0, keepdims=True)
        gs_rows.append(m1 + m2)
    gs = jnp.concatenate(gs_rows, axis=0)

    io8 = lax.broadcasted_iota(I32, (N_GROUPS, t), 0)
    gsel = jnp.zeros((N_GROUPS, t), F32)
    for _ in range(TOPK_GROUPS):
        m = jnp.max(gs, axis=0, keepdims=True)
        gi = jnp.min(jnp.where(gs == m, io8, N_GROUPS), axis=0, keepdims=True)
        hit = io8 == gi
        gsel = jnp.where(hit, 1.0, gsel)
        gs = jnp.where(hit, ninf, gs)

    masked = jnp.concatenate(
        [jnp.where(gsel[g:g + 1, :] > 0.0, sel[g * GROUP_SIZE:(g + 1) * GROUP_SIZE, :], ninf)
         for g in range(N_GROUPS)], axis=0)

    io_e = lax.broadcasted_iota(I32, (N_EXPERTS, t), 0)
    chosen = jnp.zeros((N_EXPERTS, t), F32)
    idx_rows, w_rows = [], []
    for _ in range(TOP_K):
        m = jnp.max(masked, axis=0, keepdims=True)
        ei = jnp.min(jnp.where(masked == m, io_e, N_EXPERTS), axis=0, keepdims=True)
        hit = io_e == ei
        w_rows.append(jnp.sum(jnp.where(hit, scores, 0.0), axis=0, keepdims=True))
        idx_rows.append(ei)
        chosen = jnp.where(hit, 1.0, chosen)
        masked = jnp.where(hit, ninf, masked)
    wk = jnp.concatenate(w_rows, axis=0)
    w_ref[...] = wk / jnp.sum(wk, axis=0, keepdims=True) * ROUTED_SCALE
    idx_ref[...] = jnp.concatenate(idx_rows, axis=0)

    upper = (lax.broadcasted_iota(I32, (t, t), 0) < lax.broadcasted_iota(I32, (t, t), 1))
    prefix = jnp.dot(chosen.astype(BF16), upper.astype(BF16), preferred_element_type=F32)
    rank_full = prefix + cnt_ref[...]
    rank_rows = [jnp.sum(jnp.where(io_e == ei, rank_full, 0.0), axis=0, keepdims=True) for ei in idx_rows]
    rank_ref[...] = jnp.concatenate(rank_rows, axis=0).astype(I32)
    cnt_ref[...] += jnp.sum(chosen, axis=1, keepdims=True)


def _route(logits_t, router_bias):
    ne, n = logits_t.shape
    t = min(T_ROUTE, n)
    kspec = pl.BlockSpec((TOP_K, t), lambda i: (0, i))
    return pl.pallas_call(
        _route_kernel,
        grid=(n // t,),
        in_specs=[pl.BlockSpec((ne, t), lambda i: (0, i)),
                  pl.BlockSpec((ne, 1), lambda i: (0, 0))],
        out_specs=[kspec, kspec, kspec, pl.BlockSpec((ne, 1), lambda i: (0, 0))],
        out_shape=[jax.ShapeDtypeStruct((TOP_K, n), I32),
                   jax.ShapeDtypeStruct((TOP_K, n), F32),
                   jax.ShapeDtypeStruct((TOP_K, n), I32),
                   jax.ShapeDtypeStruct((ne, 1), F32)],
        compiler_params=_cparams(("arbitrary",)),
        name="route",
    )(logits_t, router_bias.reshape(ne, 1))


def _block(ref, g):
    return ref.at[:, pl.ds(pl.multiple_of(g * BLK_E, BLK_E), BLK_E), :]


def _slots_kernel(idx_ref, rank_ref, ps_ref, o_ref, *, n_slots):
    t = idx_ref.shape[1]
    io = lax.broadcasted_iota(I32, (N_EXPERTS, t), 0)
    ps = ps_ref[...]
    for k in range(TOP_K):
        hit = io == idx_ref[k:k + 1, :]
        slot = jnp.sum(jnp.where(hit, ps, 0), axis=0, keepdims=True) + rank_ref[k:k + 1, :]
        for j in range(PACK_W):
            o_ref[k, j:j + 1, :] = slot + j * n_slots


def _slots(idx, rank, pstarts, n_slots):
    n = idx.shape[1]
    t = min(T_SLOT, n)
    kspec = pl.BlockSpec((TOP_K, t), lambda i: (0, i))
    out = pl.pallas_call(
        functools.partial(_slots_kernel, n_slots=n_slots),
        grid=(n // t,),
        in_specs=[kspec, kspec, pl.BlockSpec((N_EXPERTS, 1), lambda i: (0, 0))],
        out_specs=pl.BlockSpec((TOP_K, PACK_W, t), lambda i: (0, 0, i)),
        out_shape=jax.ShapeDtypeStruct((TOP_K, PACK_W, n), I32),
        compiler_params=_cparams(("parallel",)),
        name="slots",
    )(idx, rank, pstarts.reshape(N_EXPERTS, 1))
    return out.reshape(TOP_K, PACK_W * n)


def _sc_mesh():
    return plsc.VectorSubcoreMesh(core_axis_name="core", subcore_axis_name="subcore")


def _scatter_rows_sc(rows, dest_rows, n_out):
    n_rows = rows.shape[0]

    @functools.partial(pl.kernel, mesh=_sc_mesh(), scratch_types=[pltpu.SemaphoreType.DMA],
                       out_type=jax.ShapeDtypeStruct((n_out, LANES), I32))
    def scatter_rows(x_hbm, i_hbm, o_hbm, sem):
        def body(x_vmem, i_vmem):
            copies = [pltpu.async_copy(x_vmem, o_hbm.at[i_vmem.at[k]], sem) for k in range(TOP_K)]
            for c in copies:
                c.wait()

        pltpu.emit_pipeline(
            body,
            grid=(n_rows // SC_WINDOW,),
            in_specs=[pl.BlockSpec((SC_WINDOW, LANES), lambda i: (i, 0)),
                      pl.BlockSpec((TOP_K, SC_WINDOW), lambda i: (0, i))],
            out_specs=[],
            core_axis_name=("core", "subcore"),
            dimension_semantics=(pltpu.PARALLEL,),
        )(x_hbm, i_hbm)

    return scatter_rows(rows, dest_rows)


def _gather_rows_sc(rows, src_rows):
    nwin = src_rows.shape[0]

    @functools.partial(pl.kernel, mesh=_sc_mesh(), scratch_types=[pltpu.SemaphoreType.DMA],
                       out_type=jax.ShapeDtypeStruct((nwin * SC_WINDOW, LANES), I32))
    def gather_rows(x_hbm, i_hbm, o_hbm, sem):
        def body(i_vmem, o_vmem):
            copies = [pltpu.async_copy(x_hbm.at[i_vmem.at[w]], o_vmem.at[pl.ds(w * SC_WINDOW, SC_WINDOW)], sem)
                      for w in range(SC_GATHER_WINDOWS)]
            for c in copies:
                c.wait()

        pltpu.emit_pipeline(
            body,
            grid=(nwin // SC_GATHER_WINDOWS,),
            in_specs=[pl.BlockSpec((SC_GATHER_WINDOWS, SC_WINDOW), lambda i: (i, 0))],
            out_specs=[pl.BlockSpec((SC_GATHER_WINDOWS * SC_WINDOW, LANES), lambda i: (i, 0))],
            core_axis_name=("core", "subcore"),
            dimension_semantics=(pltpu.PARALLEL,),
        )(i_hbm, o_hbm)

    return gather_rows(rows, src_rows)


def _experts_kernel(nblk_ref, gstart_ref, cnt_ref, nb_ref, xs_ref, wg_ref, wu_ref, wd_ref, ys_ref,
                    xbuf_ref, ybuf_ref, wgb_ref, wub_ref, wdb_ref, xsem, ysem):
    e = pl.program_id(0)
    total = nb_ref[0]
    nb_max = ys_ref.shape[1] // BLK_E

    def x_copy(g):
        slot = g % X_RING
        return pltpu.make_async_copy(_block(xs_ref, g), xbuf_ref.at[slot], xsem.at[slot])

    def y_copy(g):
        slot = g % Y_RING
        return pltpu.make_async_copy(ybuf_ref.at[slot], _block(ys_ref, g), ysem.at[slot])

    @pl.when(e == 0)
    def _():
        for g in range(X_AHEAD):
            @pl.when(g < total)
            def _():
                x_copy(g).start()

    @pl.when(nblk_ref[e] > 0)
    def _():
        wgb_ref[...] = wg_ref[...].astype(BF16)
        wub_ref[...] = wu_ref[...].astype(BF16)
        wdb_ref[...] = wd_ref[...].astype(BF16)

    def fetch(b):
        g = gstart_ref[e] + b
        x_copy(g).wait()

        @pl.when(g + X_AHEAD < total)
        def _():
            x_copy(g + X_AHEAD).start()

        xb = _unpack_words([xbuf_ref[g % X_RING, j] for j in range(PACK_W)]).astype(BF16)
        live = lax.broadcasted_iota(I32, (BLK_E, 1), 0) < cnt_ref[e] - b * BLK_E
        return jnp.where(live, xb, jnp.zeros((), BF16))

    def hidden(xb):
        hg = jnp.dot(xb, wgb_ref[...], preferred_element_type=F32)
        hu = jnp.dot(xb, wub_ref[...], preferred_element_type=F32)
        return (_silu(hg) * hu).astype(BF16)

    def finish(b, act):
        g = gstart_ref[e] + b
        words = _pack_rows(jnp.dot(act, wdb_ref[...], preferred_element_type=F32))

        @pl.when(g >= Y_RING)
        def _():
            y_copy(g - Y_RING).wait()

        for j, w in enumerate(words):
            ybuf_ref[g % Y_RING, j] = w
        y_copy(g).start()

    def block(b, carry):
        finish(b, hidden(fetch(b)))
        return carry

    lax.fori_loop(0, nblk_ref[e], block, 0)

    @pl.when(e == pl.num_programs(0) - 1)
    def _():
        for back in range(Y_RING):
            @pl.when(total - 1 - back >= 0)
            def _():
                y_copy(total - 1 - back).wait()

        ybuf_ref[0] = jnp.zeros(ybuf_ref.shape[1:], ybuf_ref.dtype)

        def tail_copy(g):
            return pltpu.make_async_copy(ybuf_ref.at[0], _block(ys_ref, g), ysem.at[0])

        def fill(g, carry):
            tail_copy(g).start()
            return carry

        lax.fori_loop(total, nb_max, fill, 0)

        def drain(g, carry):
            tail_copy(g).wait()
            return carry

        lax.fori_loop(total, nb_max, drain, 0)


def _experts(nblk, gstart, cnt, n_blocks, xs, w_gate, w_up, w_down):
    p = xs.shape[1]
    ne, d, f = w_gate.shape
    wspec = lambda shape: pl.BlockSpec((None,) + shape, lambda e, *_: (e, 0, 0))
    grid_spec = pltpu.PrefetchScalarGridSpec(
        num_scalar_prefetch=4,
        grid=(ne,),
        in_specs=[pl.BlockSpec(memory_space=pl.ANY), wspec((d, f)), wspec((d, f)), wspec((f, d))],
        out_specs=pl.BlockSpec(memory_space=pl.ANY),
        scratch_shapes=[pltpu.VMEM((X_RING, PACK_W, BLK_E, LANES), I32),
                        pltpu.VMEM((Y_RING, PACK_W, BLK_E, LANES), I32),
                        pltpu.VMEM((d, f), BF16), pltpu.VMEM((d, f), BF16), pltpu.VMEM((f, d), BF16),
                        pltpu.SemaphoreType.DMA((X_RING,)), pltpu.SemaphoreType.DMA((Y_RING,))],
    )
    return pl.pallas_call(
        _experts_kernel,
        grid_spec=grid_spec,
        out_shape=jax.ShapeDtypeStruct((PACK_W, p, LANES), I32),
        compiler_params=_cparams(("arbitrary",)),
        name="experts",
    )(nblk, gstart, cnt, n_blocks, xs, w_gate, w_up, w_down)


def _combine_kernel(wt_ref, yk_ref, hfp_ref, x1_ref, gf_ref, wsg_ref, wsu_ref, wsd_ref, g_ref, b_ref, *rest,
                    alpha):
    o_ref = rest[-1]
    hb = _unpack_words([hfp_ref[j] for j in range(PACK_W)]).astype(BF16)
    sg = jnp.dot(hb, wsg_ref[...], preferred_element_type=F32)
    su = jnp.dot(hb, wsu_ref[...], preferred_element_type=F32)
    ffn = jnp.dot((_silu(sg) * su).astype(BF16), wsd_ref[...], preferred_element_type=F32)
    wt = wt_ref[...]
    for k in range(TOP_K):
        yk = _unpack_words([yk_ref[k, j] for j in range(PACK_W)])
        ffn = ffn + wt[:, k:k + 1] * yk
    o_ref[...] = _layer_norm(alpha * x1_ref[...] + gf_ref[...] * ffn, g_ref[...], b_ref[...])


def _combine(w_tok, yk, hfp, x1, mod, ws_gate_bf, ws_up_bf, ws_down_bf, ln_g, ln_b, alpha, tok0, earlier):
    n, d = x1.shape
    nc = yk.shape[2]
    s = n // mod.shape[1]
    tc = min(T_COMB, s, nc)
    nt = s // tc
    t0 = tok0 // tc
    fs = ws_gate_bf.shape[1]
    const = lambda shape: pl.BlockSpec(shape, lambda i: tuple(0 for _ in shape))
    in_specs = [pl.BlockSpec((tc, TOP_K), lambda i: (t0 + i, 0)),
                pl.BlockSpec((TOP_K, PACK_W, tc, LANES), lambda i: (0, 0, i, 0)),
                pl.BlockSpec((PACK_W, tc, LANES), lambda i: (0, t0 + i, 0)),
                pl.BlockSpec((tc, d), lambda i: (t0 + i, 0)),
                pl.BlockSpec((None, None, 1, d), lambda i: (5, (t0 + i) // nt, 0, 0)),
                const((d, fs)), const((d, fs)), const((fs, d)), const((1, d)), const((1, d))]
    args = [w_tok, yk, hfp, x1, mod, ws_gate_bf, ws_up_bf, ws_down_bf, ln_g.reshape(1, d), ln_b.reshape(1, d)]
    aliases = {}
    if earlier is not None:
        in_specs.append(pl.BlockSpec(memory_space=pl.ANY))
        args.append(earlier)
        aliases = {len(args) - 1: 0}
    return pl.pallas_call(
        functools.partial(_combine_kernel, alpha=alpha),
        grid=(nc // tc,),
        in_specs=in_specs,
        out_specs=pl.BlockSpec((tc, d), lambda i: (t0 + i, 0)),
        out_shape=jax.ShapeDtypeStruct((n, d), F32),
        input_output_aliases=aliases,
        compiler_params=_cparams(("parallel",)),
        name="combine",
    )(*args)


def _slot_layout(counts, n_assign):
    cnt = counts[:, 0].astype(I32)
    padded = (cnt + BLK_E - 1) // BLK_E * BLK_E
    pends = jnp.cumsum(padded)
    pstarts = pends - padded
    n_blocks_max = (n_assign + N_EXPERTS * (BLK_E - 1)) // BLK_E
    n_blocks = (pends[-1] // BLK_E).astype(I32).reshape(1)
    return pstarts, pends, padded // BLK_E, pstarts // BLK_E, n_blocks, n_blocks_max * BLK_E


def kernel(x, c, w_ada, b_ada, w_in, w_out, na_rpb, ret_log_decay, ret_gn_g, ln1_g, ln1_b, ln2_g, ln2_b,
           w_router, router_bias, w_gate, w_up, w_down, ws_gate, ws_up, ws_down):
    b, s, d = x.shape
    depth = w_ada.shape[0]
    alpha = (2.0 * depth) ** 0.25
    t = jnp.arange(s, dtype=F32)
    inv_freq = ROPE_BASE ** (-jnp.arange(0, RET_HEAD_DIM, 2, dtype=F32) / RET_HEAD_DIM)
    ang = t[:, None] * inv_freq[None, :]
    cos, sin = jnp.cos(ang), jnp.sin(ang)
    cos2 = jnp.concatenate([cos, cos], axis=-1)
    sin2 = jnp.concatenate([-sin, sin], axis=-1)
    for l in range(depth):
        mod = _mod(c, w_ada[l], b_ada[l])
        proj = _inproj(x, mod, w_in[l].astype(BF16))
        y_na = _natten(proj, _na_bias_table(na_rpb[l]))
        y_r = _retention(proj, ret_log_decay[l], ret_gn_g[l], cos2, sin2)
        wr_t = w_router[l].T
        wr_hi = wr_t.astype(BF16)
        wr_lo = (wr_t - wr_hi.astype(F32)).astype(BF16)
        x1, hfp, logits_t = _outproj(y_na, y_r, x, mod, w_out[l].astype(BF16), ln1_g[l], ln1_b[l],
                                     wr_hi, wr_lo, alpha)
        idx, wts, rank, counts = _route(logits_t, router_bias[l])
        pstarts, pends, nblk, gstart, n_blocks, n_slots = _slot_layout(counts, b * s * TOP_K)
        cnt = counts[:, 0].astype(I32)
        n = b * s
        slot_rows = _slots(idx, rank, pstarts, n_slots)
        xs = _scatter_rows_sc(hfp.reshape(PACK_W * n, LANES), slot_rows, PACK_W * n_slots)
        ys = _experts(nblk, gstart, cnt, n_blocks, xs.reshape(PACK_W, n_slots, LANES),
                      w_gate[l], w_up[l], w_down[l])
        ys_rows = ys.reshape(PACK_W * n_slots, LANES)
        slot_tok = slot_rows.reshape(TOP_K, PACK_W, n)
        ws = (ws_gate[l].astype(BF16), ws_up[l].astype(BF16), ws_down[l].astype(BF16))
        w_tok = wts.T
        x1n = x1.reshape(n, d)
        nc = n // COMBINE_CHUNKS
        out = None
        for ci in range(COMBINE_CHUNKS):
            src = slot_tok[:, :, ci * nc:(ci + 1) * nc].reshape(TOP_K * PACK_W * nc // SC_WINDOW, SC_WINDOW)
            yk = _gather_rows_sc(ys_rows, src).reshape(TOP_K, PACK_W, nc, LANES)
            out = _combine(w_tok, yk, hfp, x1n, mod, *ws, ln2_g[l], ln2_b[l], alpha, ci * nc, out)
        x = out.reshape(b, s, d)
    return x
```

```python
import functools
import math

import jax
import jax.numpy as jnp
import numpy as np
from jax import lax
from jax.experimental import pallas as pl
from jax.experimental.pallas import tpu as pltpu
from jax.experimental.pallas import tpu_sc as plsc

F32 = jnp.float32
BF16 = jnp.bfloat16
U32 = jnp.uint32
I32 = jnp.int32

GRID_W = 64
WIN_R = 8
WIN_C = 16
NA_HEADS = 8
NA_HEAD_DIM = 64
NA_WIDTH = NA_HEADS * NA_HEAD_DIM
RET_HEADS = 4
RET_HEAD_DIM = 128
RET_WIDTH = RET_HEADS * RET_HEAD_DIM
RET_CHUNK = 128
ROPE_BASE = 10000.0
N_EXPERTS = 256
TOP_K = 8
N_GROUPS = 8
TOPK_GROUPS = 4
GROUP_SIZE = N_EXPERTS // N_GROUPS
ROUTED_SCALE = 2.5
LOG2_E = 1.4426950408889634
LN_EPS = 1e-5
GN_EPS = 1e-6

LANES = 128
VMEM_LIMIT = 56 * 1024 * 1024

TM_PROJ = 512
T_ROUTE = 512
T_SLOT = 1024
SC_WINDOW = 128
SC_GATHER_WINDOWS = 2
BLK_E = 512
OUTPROJ_PARTS = 2
T_COMB = 256
MOE_GROUPS = 2
COMBINE_CHUNKS = 4
PACK_W = 4
NA_ROWS_PER_ITER = 32
RET_BLOCK = 256
RET_UNROLL = 8
X_RING = 4
X_AHEAD = 3
Y_RING = 3


def _cparams(sem, vmem=VMEM_LIMIT):
    return pltpu.CompilerParams(dimension_semantics=sem, vmem_limit_bytes=vmem)


def _silu(v):
    return v * jax.nn.sigmoid(v)


def _layer_norm(z, g, b):
    mu = jnp.mean(z, -1, keepdims=True)
    zc = z - mu
    var = jnp.mean(zc * zc, -1, keepdims=True)
    return zc * lax.rsqrt(var + LN_EPS) * g + b


def _pack_rows(v):
    half = v.shape[1] // 2
    vb = v.astype(BF16)
    lo = lax.bitcast_convert_type(vb[:, :half].astype(F32), U32) >> 16
    hi = lax.bitcast_convert_type(vb[:, half:].astype(F32), U32)
    w = lax.bitcast_convert_type(hi | lo, I32)
    return [w[:, j * LANES:(j + 1) * LANES] for j in range(half // LANES)]


def _unpack_words(words):
    words = [lax.bitcast_convert_type(w, U32) for w in words]
    lo = [lax.bitcast_convert_type(w << 16, F32) for w in words]
    hi = [lax.bitcast_convert_type(w & jnp.uint32(0xFFFF0000), F32) for w in words]
    return jnp.concatenate(lo + hi, axis=-1)


def _mod_kernel(c_ref, w_ref, b_ref, o_ref):
    cond = _silu(c_ref[...])
    o_ref[0] = jnp.dot(cond, w_ref[...], precision=lax.Precision.HIGHEST,
                       preferred_element_type=F32) + b_ref[0]


def _mod(c, w_ada, b_ada):
    b, d = c.shape
    n6 = w_ada.shape[1] // d
    out = pl.pallas_call(
        _mod_kernel,
        grid=(n6,),
        in_specs=[pl.BlockSpec((b, d), lambda j: (0, 0)),
                  pl.BlockSpec((d, d), lambda j: (0, j)),
                  pl.BlockSpec((1, 1, d), lambda j: (j, 0, 0))],
        out_specs=pl.BlockSpec((1, b, d), lambda j: (j, 0, 0)),
        out_shape=jax.ShapeDtypeStruct((n6, b, d), F32),
        compiler_params=_cparams(("arbitrary",)),
        name="mod",
    )(c, w_ada, b_ada.reshape(n6, 1, d))
    return out.reshape(n6, b, 1, d)


def _mod_spec(which, d):
    return pl.BlockSpec((None, None, 1, d), lambda b, i, which=which: (which, b, 0, 0))


def _inproj_kernel(x_ref, sc_ref, sh_ref, w_ref, o_ref, *, chunk, q_scale):
    h = (x_ref[...] * (1.0 + sc_ref[...]) + sh_ref[...]).astype(BF16)
    for j in range(o_ref.shape[1] // chunk):
        acc = jnp.dot(h, w_ref[:, j * chunk:(j + 1) * chunk], preferred_element_type=F32)
        if j == 0:
            acc = acc * q_scale
        o_ref[:, j * chunk:(j + 1) * chunk] = acc.astype(o_ref.dtype)


def _inproj(x, mod, w_in_bf):
    b, s, d = x.shape
    e = w_in_bf.shape[1]
    tm = min(TM_PROJ, s)
    return pl.pallas_call(
        functools.partial(_inproj_kernel, chunk=NA_WIDTH, q_scale=NA_HEAD_DIM ** -0.5 * LOG2_E),
        grid=(b, s // tm),
        in_specs=[pl.BlockSpec((None, tm, d), lambda bi, i: (bi, i, 0)),
                  _mod_spec(1, d), _mod_spec(0, d),
                  pl.BlockSpec((d, e), lambda bi, i: (0, 0))],
        out_specs=pl.BlockSpec((None, tm, e), lambda bi, i: (bi, i, 0)),
        out_shape=jax.ShapeDtypeStruct((b, s, e), BF16),
        compiler_params=_cparams(("parallel", "parallel")),
        name="inproj",
    )(x, mod, mod, w_in_bf)


def _natten_kernel(q_ref, k_ref, v_ref, bias_ref, o_ref, *, rows, rows_per_iter):
    kspan = WIN_R * GRID_W

    first = lax.broadcasted_iota(I32, (1, LANES), 1) < NA_HEAD_DIM
    zero = jnp.zeros((), BF16)

    def rows_body(i, carry):
        qrows, krows, scores, probs = {}, {}, {}, {}

        def stage_scores(u):
            r = i * rows_per_iter + u
            rs = jnp.clip(r - WIN_R // 2, 0, rows - WIN_R)
            vi = r - rs
            qrows[u] = pl.ds(pl.multiple_of(r * GRID_W, GRID_W), GRID_W)
            krows[u] = pl.ds(pl.multiple_of(rs * GRID_W, GRID_W), kspan)
            q = q_ref[qrows[u], :]
            qm = jnp.concatenate([jnp.where(first, q, zero), jnp.where(first, zero, q)], axis=0)
            st = lax.dot_general(k_ref[krows[u], :], qm, (((1,), (1,)), ((), ())), preferred_element_type=F32)
            scores[u] = st + bias_ref[vi]

        def stage_softmax(u):
            st = scores.pop(u)
            p = jnp.exp2(st - jnp.max(st, axis=0, keepdims=True))
            probs[u] = (p * (1.0 / jnp.sum(p, axis=0, keepdims=True))).astype(BF16)

        def stage_values(u):
            res = lax.dot_general(probs.pop(u), v_ref[krows[u], :], (((0,), (0,)), ((), ())),
                                  preferred_element_type=F32)
            o_ref[qrows[u], :] = jnp.where(first, res[:GRID_W], res[GRID_W:]).astype(o_ref.dtype)

        for step in range(rows_per_iter + 2):
            if step < rows_per_iter:
                stage_scores(step)
            if 0 <= step - 1 < rows_per_iter:
                stage_softmax(step - 1)
            if 0 <= step - 2 < rows_per_iter:
                stage_values(step - 2)
        return carry

    lax.fori_loop(0, rows // rows_per_iter, rows_body, 0)


def _na_bias_table(rpb):
    w = GRID_W
    cq = jnp.arange(w)
    cs = jnp.clip(cq - WIN_C // 2, 0, w - WIN_C)
    ck = jnp.arange(w)
    col_in = (ck[None, :] >= cs[:, None]) & (ck[None, :] < cs[:, None] + WIN_C)
    dc_idx = jnp.clip(ck[None, :] - cq[:, None] + WIN_C - 1, 0, 2 * WIN_C - 2)
    t = rpb[:, :, dc_idx]
    t = jnp.where(col_in[None, None], t, -jnp.inf)
    vi = jnp.arange(WIN_R)
    kr = jnp.arange(WIN_R)
    dr = kr[None, :] - vi[:, None] + WIN_R - 1
    tb = t[:, dr]
    hp = LANES // NA_HEAD_DIM
    tb = tb.reshape(rpb.shape[0] // hp, hp, WIN_R, WIN_R, w, w)
    tb = tb.transpose(0, 2, 3, 5, 1, 4).reshape(rpb.shape[0] // hp, WIN_R, WIN_R * w, hp * w)
    return tb.astype(F32) * LOG2_E


def _natten(proj, bias_tab):
    b, s, _ = proj.shape
    rows = s // GRID_W
    hp = LANES // NA_HEAD_DIM
    npair = NA_HEADS // hp
    blk = lambda off: pl.BlockSpec((None, s, LANES), lambda bi, p, off=off: (bi, 0, off + p))
    return pl.pallas_call(
        functools.partial(_natten_kernel, rows=rows, rows_per_iter=math.gcd(rows, NA_ROWS_PER_ITER)),
        grid=(b, npair),
        in_specs=[blk(0), blk(npair), blk(2 * npair),
                  pl.BlockSpec((None, WIN_R, WIN_R * GRID_W, hp * GRID_W), lambda bi, p: (p, 0, 0, 0))],
        out_specs=pl.BlockSpec((None, s, LANES), lambda bi, p: (bi, 0, p)),
        out_shape=jax.ShapeDtypeStruct((b, s, NA_WIDTH), BF16),
        compiler_params=_cparams(("parallel", "parallel")),
        name="natten",
    )(proj, proj, proj, bias_tab)


def _retent_kernel(ld_ref, q_ref, k_ref, v_ref, g_ref, cos_ref, sin_ref, gn_ref, o_ref,
                   qs_ref, ks_ref, sb_ref, *, nchunk):
    c = RET_BLOCK
    dh = RET_HEAD_DIM
    h = pl.program_id(1)
    lgf = ld_ref[0, h]
    lgb = ld_ref[1, h]

    cos2 = cos_ref[...]
    sin2 = sin_ref[...]
    qf = q_ref[...].astype(F32)
    qs_ref[...] = qf * cos2 + pltpu.roll(qf, dh // 2, 1) * sin2
    kf = k_ref[...].astype(F32)
    ks_ref[...] = (kf * cos2 + pltpu.roll(kf, dh // 2, 1) * sin2) * (dh ** -0.5)

    ic = lax.broadcasted_iota(I32, (c, 1), 0).astype(F32)
    ir = lax.broadcasted_iota(I32, (1, c), 1).astype(F32)
    diff = ic - ir
    dmat = jnp.where(diff >= 0, jnp.exp(jnp.maximum(diff, 0.0) * lgf),
                     jnp.exp(jnp.maximum(-diff, 0.0) * lgb))
    kdec_f = jnp.exp((c - 1 - ic) * lgf)
    qdec_f = jnp.exp((ic + 1) * lgf)
    kdec_b = jnp.exp(ic * lgb)
    qdec_b = jnp.exp((c - ic) * lgb)
    one = jnp.ones((1, 1), F32)
    cdec_f = jnp.exp(one * (c * lgf))
    cdec_b = jnp.exp(one * (c * lgb))
    tn = (((0,), (0,)), ((), ()))

    def bwd_body(i, sb):
        n = nchunk - 1 - i
        sb_ref[n] = sb
        rows = pl.ds(pl.multiple_of(n * c, c), c)
        kd = (ks_ref[rows, :] * kdec_b).astype(BF16)
        kv = lax.dot_general(kd, v_ref[rows, :], tn, preferred_element_type=F32)
        return cdec_b * sb + kv

    lax.fori_loop(0, nchunk, bwd_body, jnp.zeros((dh, dh), F32), unroll=min(RET_UNROLL, nchunk))

    gn = gn_ref[...]

    def fwd_body(n, sf):
        rows = pl.ds(pl.multiple_of(n * c, c), c)
        qn = qs_ref[rows, :]
        kn = ks_ref[rows, :]
        vn = v_ref[rows, :]
        sc = lax.dot_general(qn.astype(BF16), kn.astype(BF16), (((1,), (1,)), ((), ())),
                             preferred_element_type=F32) * dmat
        y = jnp.dot(sc.astype(BF16), vn, preferred_element_type=F32)
        qd = jnp.concatenate([qn * qdec_f, qn * qdec_b], axis=1).astype(BF16)
        st = jnp.concatenate([sf, sb_ref[n]], axis=0).astype(BF16)
        y = y + jnp.dot(qd, st, preferred_element_type=F32)
        mu = jnp.mean(y, -1, keepdims=True)
        yc = y - mu
        var = jnp.mean(yc * yc, -1, keepdims=True)
        yn = yc * lax.rsqrt(var + GN_EPS) * gn
        o_ref[rows, :] = (_silu(g_ref[rows, :].astype(F32)) * yn).astype(o_ref.dtype)
        kv = lax.dot_general((kn * kdec_f).astype(BF16), vn, tn, preferred_element_type=F32)
        return cdec_f * sf + kv

    lax.fori_loop(0, nchunk, fwd_body, jnp.zeros((dh, dh), F32), unroll=min(RET_UNROLL, nchunk))


def _retention(proj, log_decay, gn_g, cos2, sin2):
    b, s, _ = proj.shape
    dh = RET_HEAD_DIM
    nchunk = s // RET_BLOCK
    base = 3 * NA_WIDTH // dh
    blk = lambda off: pl.BlockSpec((None, s, dh), lambda bi, h, off=off: (bi, 0, base + off + h))
    full = pl.BlockSpec((s, dh), lambda bi, h: (0, 0))
    return pl.pallas_call(
        functools.partial(_retent_kernel, nchunk=nchunk),
        grid=(b, RET_HEADS),
        in_specs=[pl.BlockSpec(memory_space=pltpu.SMEM),
                  blk(0), blk(RET_HEADS), blk(2 * RET_HEADS), blk(3 * RET_HEADS),
                  full, full,
                  pl.BlockSpec((1, dh), lambda bi, h: (0, h))],
        out_specs=pl.BlockSpec((None, s, dh), lambda bi, h: (bi, 0, h)),
        out_shape=jax.ShapeDtypeStruct((b, s, RET_WIDTH), BF16),
        scratch_shapes=[pltpu.VMEM((s, dh), F32), pltpu.VMEM((s, dh), F32),
                        pltpu.VMEM((nchunk, dh, dh), F32)],
        compiler_params=_cparams(("parallel", "parallel")),
        name="retent",
    )(log_decay, proj, proj, proj, proj, cos2, sin2, gn_g.reshape(1, RET_WIDTH))


def _outproj_kernel(yna_ref, yr_ref, x_ref, ga_ref, sf_ref, shf_ref, wo1_ref, wo2_ref, g_ref, b_ref,
                    wrh_ref, wrl_ref, x1_ref, hfp_ref, lg_ref, *, alpha):
    nt = (((1,), (1,)), ((), ()))
    tm = x_ref.shape[0]
    parts = [pl.ds(p * (tm // OUTPROJ_PARTS), tm // OUTPROJ_PARTS) for p in range(OUTPROJ_PARTS)]
    def mix_of(r):
        return (jnp.dot(yna_ref[r, :], wo1_ref[...], preferred_element_type=F32)
                + jnp.dot(yr_ref[r, :], wo2_ref[...], preferred_element_type=F32))

    nxt = mix_of(parts[0])
    for p, r in enumerate(parts):
        mix = nxt
        if p + 1 < len(parts):
            nxt = mix_of(parts[p + 1])
        x1 = _layer_norm(alpha * x_ref[r, :] + ga_ref[...] * mix, g_ref[...], b_ref[...])
        x1_ref[r, :] = x1
        hf = x1 * (1.0 + sf_ref[...]) + shf_ref[...]
        for j, w in enumerate(_pack_rows(hf)):
            hfp_ref[j, r, :] = w
        hb = hf.astype(BF16)
        hl = (hf - hb.astype(F32)).astype(BF16)
        lg = lax.dot_general(wrh_ref[...], hb, nt, preferred_element_type=F32)
        lg = lg + lax.dot_general(wrh_ref[...], hl, nt, preferred_element_type=F32)
        lg = lg + lax.dot_general(wrl_ref[...], hb, nt, preferred_element_type=F32)
        lg_ref[:, r] = lg


def _outproj(y_na, y_r, x, mod, w_out_bf, ln_g, ln_b, wr_hi, wr_lo, alpha):
    b, s, d = x.shape
    tm = min(TM_PROJ, s)
    nt = s // tm
    ne = wr_hi.shape[0]
    const = lambda shape: pl.BlockSpec(shape, lambda bi, i: tuple(0 for _ in shape))
    x1, hfp, lg = pl.pallas_call(
        functools.partial(_outproj_kernel, alpha=alpha),
        grid=(b, nt),
        in_specs=[pl.BlockSpec((None, tm, NA_WIDTH), lambda bi, i: (bi, i, 0)),
                  pl.BlockSpec((None, tm, RET_WIDTH), lambda bi, i: (bi, i, 0)),
                  pl.BlockSpec((None, tm, d), lambda bi, i: (bi, i, 0)),
                  _mod_spec(2, d), _mod_spec(4, d), _mod_spec(3, d),
                  pl.BlockSpec((NA_WIDTH, d), lambda bi, i: (0, 0)),
                  pl.BlockSpec((RET_WIDTH, d), lambda bi, i: (1, 0)),
                  const((1, d)), const((1, d)), const((ne, d)), const((ne, d))],
        out_specs=[pl.BlockSpec((None, tm, d), lambda bi, i: (bi, i, 0)),
                   pl.BlockSpec((PACK_W, tm, LANES), lambda bi, i: (0, bi * nt + i, 0)),
                   pl.BlockSpec((ne, tm), lambda bi, i: (0, bi * nt + i))],
        out_shape=[jax.ShapeDtypeStruct((b, s, d), F32),
                   jax.ShapeDtypeStruct((PACK_W, b * s, LANES), I32),
                   jax.ShapeDtypeStruct((ne, b * s), F32)],
        compiler_params=_cparams(("parallel", "parallel")),
        name="outproj",
    )(y_na, y_r, x, mod, mod, mod, w_out_bf, w_out_bf, ln_g.reshape(1, d), ln_b.reshape(1, d), wr_hi, wr_lo)
    return x1, hfp, lg


def _route_kernel(lg_ref, rb_ref, idx_ref, w_ref, rank_ref, cnt_ref, snap_ref):
    t = lg_ref.shape[1]
    ninf = -jnp.inf
    step = pl.program_id(0)
    steps_per_group = pl.num_programs(0) // MOE_GROUPS

    @pl.when(step == 0)
    def _():
        cnt_ref[...] = jnp.zeros_like(cnt_ref)

    scores = jax.nn.sigmoid(lg_ref[...])
    sel = scores + rb_ref[...]

    io_g = lax.broadcasted_iota(I32, (GROUP_SIZE, t), 0)
    gs_rows = []
    for g in range(N_GROUPS):
        blk = sel[g * GROUP_SIZE:(g + 1) * GROUP_SIZE, :]
        m1 = jnp.max(blk, axis=0, keepdims=True)
        i1 = jnp.min(jnp.where(blk == m1, io_g, GROUP_SIZE), axis=0, keepdims=True)
        m2 = jnp.max(jnp.where(io_g == i1, ninf, blk), axis=0, keepdims=True)
        gs_rows.append(m1 + m2)
    gs = jnp.concatenate(gs_rows, axis=0)

    io8 = lax.broadcasted_iota(I32, (N_GROUPS, t), 0)
    gsel = jnp.zeros((N_GROUPS, t), F32)
    for _ in range(TOPK_GROUPS):
        m = jnp.max(gs, axis=0, keepdims=True)
        gi = jnp.min(jnp.where(gs == m, io8, N_GROUPS), axis=0, keepdims=True)
        hit = io8 == gi
        gsel = jnp.where(hit, 1.0, gsel)
        gs = jnp.where(hit, ninf, gs)

    masked = jnp.concatenate(
        [jnp.where(gsel[g:g + 1, :] > 0.0, sel[g * GROUP_SIZE:(g + 1) * GROUP_SIZE, :], ninf)
         for g in range(N_GROUPS)], axis=0)

    io_e = lax.broadcasted_iota(I32, (N_EXPERTS, t), 0)
    chosen = jnp.zeros((N_EXPERTS, t), F32)
    idx_rows, w_rows = [], []
    for _ in range(TOP_K):
        m = jnp.max(masked, axis=0, keepdims=True)
        ei = jnp.min(jnp.where(masked == m, io_e, N_EXPERTS), axis=0, keepdims=True)
        hit = io_e == ei
        w_rows.append(jnp.sum(jnp.where(hit, scores, 0.0), axis=0, keepdims=True))
        idx_rows.append(ei)
        chosen = jnp.where(hit, 1.0, chosen)
        masked = jnp.where(hit, ninf, masked)
    wk = jnp.concatenate(w_rows, axis=0)
    w_ref[...] = wk / jnp.sum(wk, axis=0, keepdims=True) * ROUTED_SCALE
    idx_ref[...] = jnp.concatenate(idx_rows, axis=0)

    upper = (lax.broadcasted_iota(I32, (t, t), 0) < lax.broadcasted_iota(I32, (t, t), 1))
    prefix = jnp.dot(chosen.astype(BF16), upper.astype(BF16), preferred_element_type=F32)
    rank_full = prefix + cnt_ref[...]
    rank_rows = [jnp.sum(jnp.where(io_e == ei, rank_full, 0.0), axis=0, keepdims=True) for ei in idx_rows]
    rank_ref[...] = jnp.concatenate(rank_rows, axis=0).astype(I32)
    cnt_ref[...] += jnp.sum(chosen, axis=1, keepdims=True)

    for g in range(MOE_GROUPS):
        @pl.when(step == (g + 1) * steps_per_group - 1)
        def _():
            snap_ref[g] = cnt_ref[...]


def _route(logits_t, router_bias):
    ne, n = logits_t.shape
    t = min(T_ROUTE, n)
    kspec = pl.BlockSpec((TOP_K, t), lambda i: (0, i))
    return pl.pallas_call(
        _route_kernel,
        grid=(n // t,),
        in_specs=[pl.BlockSpec((ne, t), lambda i: (0, i)),
                  pl.BlockSpec((ne, 1), lambda i: (0, 0))],
        out_specs=[kspec, kspec, kspec, pl.BlockSpec((ne, 1), lambda i: (0, 0)),
                   pl.BlockSpec((MOE_GROUPS, ne, 1), lambda i: (0, 0, 0))],
        out_shape=[jax.ShapeDtypeStruct((TOP_K, n), I32),
                   jax.ShapeDtypeStruct((TOP_K, n), F32),
                   jax.ShapeDtypeStruct((TOP_K, n), I32),
                   jax.ShapeDtypeStruct((ne, 1), F32),
                   jax.ShapeDtypeStruct((MOE_GROUPS, ne, 1), F32)],
        compiler_params=_cparams(("arbitrary",)),
        name="route",
    )(logits_t, router_bias.reshape(ne, 1))


def _block(ref, g):
    return ref.at[:, pl.ds(pl.multiple_of(g * BLK_E, BLK_E), BLK_E), :]


def _slots_kernel(idx_ref, rank_ref, ps_ref, o_ref, *, n_slots):
    t = idx_ref.shape[1]
    io = lax.broadcasted_iota(I32, (N_EXPERTS, t), 0)
    ps = ps_ref[...]
    for k in range(TOP_K):
        hit = io == idx_ref[k:k + 1, :]
        slot = jnp.sum(jnp.where(hit, ps, 0), axis=0, keepdims=True) + rank_ref[k:k + 1, :]
        for j in range(PACK_W):
            o_ref[k, j:j + 1, :] = slot + j * n_slots


def _slots(idx, rank, slot_base, n_slots, tok0, ng):
    t = min(T_SLOT, ng)
    t0 = tok0 // t
    kspec = pl.BlockSpec((TOP_K, t), lambda i: (0, t0 + i))
    out = pl.pallas_call(
        functools.partial(_slots_kernel, n_slots=n_slots),
        grid=(ng // t,),
        in_specs=[kspec, kspec, pl.BlockSpec((N_EXPERTS, 1), lambda i: (0, 0))],
        out_specs=pl.BlockSpec((TOP_K, PACK_W, t), lambda i: (0, 0, i)),
        out_shape=jax.ShapeDtypeStruct((TOP_K, PACK_W, ng), I32),
        compiler_params=_cparams(("parallel",)),
        name="slots",
    )(idx, rank, slot_base.reshape(N_EXPERTS, 1))
    return out.reshape(TOP_K, PACK_W * ng)


def _sc_mesh():
    return plsc.VectorSubcoreMesh(core_axis_name="core", subcore_axis_name="subcore")


def _scatter_rows_sc(rows, dest_rows, n_out, n_tok, tok0, ng):
    n_rows = PACK_W * ng
    wpg = ng // SC_WINDOW
    wpp = n_tok // SC_WINDOW
    w0 = tok0 // SC_WINDOW

    @functools.partial(pl.kernel, mesh=_sc_mesh(), scratch_types=[pltpu.SemaphoreType.DMA],
                       out_type=jax.ShapeDtypeStruct((n_out, LANES), I32))
    def scatter_rows(x_hbm, i_hbm, o_hbm, sem):
        def body(x_vmem, i_vmem):
            copies = [pltpu.async_copy(x_vmem, o_hbm.at[i_vmem.at[k]], sem) for k in range(TOP_K)]
            for c in copies:
                c.wait()

        pltpu.emit_pipeline(
            body,
            grid=(n_rows // SC_WINDOW,),
            in_specs=[pl.BlockSpec((SC_WINDOW, LANES), lambda i: ((i // wpg) * wpp + w0 + i % wpg, 0)),
                      pl.BlockSpec((TOP_K, SC_WINDOW), lambda i: (0, i))],
            out_specs=[],
            core_axis_name=("core", "subcore"),
            dimension_semantics=(pltpu.PARALLEL,),
        )(x_hbm, i_hbm)

    return scatter_rows(rows, dest_rows)


def _gather_rows_sc(rows, src_rows):
    nwin = src_rows.shape[0]

    @functools.partial(pl.kernel, mesh=_sc_mesh(), scratch_types=[pltpu.SemaphoreType.DMA],
                       out_type=jax.ShapeDtypeStruct((nwin * SC_WINDOW, LANES), I32))
    def gather_rows(x_hbm, i_hbm, o_hbm, sem):
        def body(i_vmem, o_vmem):
            copies = [pltpu.async_copy(x_hbm.at[i_vmem.at[w]], o_vmem.at[pl.ds(w * SC_WINDOW, SC_WINDOW)], sem)
                      for w in range(SC_GATHER_WINDOWS)]
            for c in copies:
                c.wait()

        pltpu.emit_pipeline(
            body,
            grid=(nwin // SC_GATHER_WINDOWS,),
            in_specs=[pl.BlockSpec((SC_GATHER_WINDOWS, SC_WINDOW), lambda i: (i, 0))],
            out_specs=[pl.BlockSpec((SC_GATHER_WINDOWS * SC_WINDOW, LANES), lambda i: (i, 0))],
            core_axis_name=("core", "subcore"),
            dimension_semantics=(pltpu.PARALLEL,),
        )(i_hbm, o_hbm)

    return gather_rows(rows, src_rows)


def _experts_kernel(nblk_ref, gstart_ref, cnt_ref, nb_ref, xs_ref, wg_ref, wu_ref, wd_ref, ys_ref,
                    xbuf_ref, ybuf_ref, wgub_ref, wdb_ref, xsem, ysem):
    e = pl.program_id(0)
    total = nb_ref[0]
    nb_max = ys_ref.shape[1] // BLK_E

    def x_copy(g):
        slot = g % X_RING
        return pltpu.make_async_copy(_block(xs_ref, g), xbuf_ref.at[slot], xsem.at[slot])

    def y_copy(g):
        slot = g % Y_RING
        return pltpu.make_async_copy(ybuf_ref.at[slot], _block(ys_ref, g), ysem.at[slot])

    @pl.when(e == 0)
    def _():
        for g in range(X_AHEAD):
            @pl.when(g < total)
            def _():
                x_copy(g).start()

    @pl.when(nblk_ref[e] > 0)
    def _():
        f = wg_ref.shape[1]
        wgub_ref[:, :f] = wg_ref[...].astype(BF16)
        wgub_ref[:, f:] = wu_ref[...].astype(BF16)
        wdb_ref[...] = wd_ref[...].astype(BF16)

    def fetch(b):
        g = gstart_ref[e] + b
        x_copy(g).wait()

        @pl.when(g + X_AHEAD < total)
        def _():
            x_copy(g + X_AHEAD).start()

        xb = _unpack_words([xbuf_ref[g % X_RING, j] for j in range(PACK_W)]).astype(BF16)
        live = lax.broadcasted_iota(I32, (BLK_E, 1), 0) < cnt_ref[e] - b * BLK_E
        return jnp.where(live, xb, jnp.zeros((), BF16))

    def hidden(xb):
        f = wg_ref.shape[1]
        hgu = jnp.dot(xb, wgub_ref[...], preferred_element_type=F32)
        return (_silu(hgu[:, :f]) * hgu[:, f:]).astype(BF16)

    def finish(b, act):
        g = gstart_ref[e] + b
        words = _pack_rows(jnp.dot(act, wdb_ref[...], preferred_element_type=F32))

        @pl.when(g >= Y_RING)
        def _():
            y_copy(g - Y_RING).wait()

        for j, w in enumerate(words):
            ybuf_ref[g % Y_RING, j] = w
        y_copy(g).start()

    def block(b, carry):
        finish(b, hidden(fetch(b)))
        return carry

    lax.fori_loop(0, nblk_ref[e], block, 0)

    @pl.when(e == pl.num_programs(0) - 1)
    def _():
        for back in range(Y_RING):
            @pl.when(total - 1 - back >= 0)
            def _():
                y_copy(total - 1 - back).wait()

        ybuf_ref[0] = jnp.zeros(ybuf_ref.shape[1:], ybuf_ref.dtype)

        def tail_copy(g):
            return pltpu.make_async_copy(ybuf_ref.at[0], _block(ys_ref, g), ysem.at[0])

        def fill(g, carry):
            tail_copy(g).start()
            return carry

        lax.fori_loop(total, nb_max, fill, 0)

        def drain(g, carry):
            tail_copy(g).wait()
            return carry

        lax.fori_loop(total, nb_max, drain, 0)


def _experts(nblk, gstart, cnt, n_blocks, xs, w_gate, w_up, w_down):
    p = xs.shape[1]
    ne, d, f = w_gate.shape
    wspec = lambda shape: pl.BlockSpec((None,) + shape, lambda e, *_: (e, 0, 0))
    grid_spec = pltpu.PrefetchScalarGridSpec(
        num_scalar_prefetch=4,
        grid=(ne,),
        in_specs=[pl.BlockSpec(memory_space=pl.ANY), wspec((d, f)), wspec((d, f)), wspec((f, d))],
        out_specs=pl.BlockSpec(memory_space=pl.ANY),
        scratch_shapes=[pltpu.VMEM((X_RING, PACK_W, BLK_E, LANES), I32),
                        pltpu.VMEM((Y_RING, PACK_W, BLK_E, LANES), I32),
                        pltpu.VMEM((d, 2 * f), BF16), pltpu.VMEM((f, d), BF16),
                        pltpu.SemaphoreType.DMA((X_RING,)), pltpu.SemaphoreType.DMA((Y_RING,))],
    )
    return pl.pallas_call(
        _experts_kernel,
        grid_spec=grid_spec,
        out_shape=jax.ShapeDtypeStruct((PACK_W, p, LANES), I32),
        compiler_params=_cparams(("arbitrary",)),
        name="experts",
    )(nblk, gstart, cnt, n_blocks, xs, w_gate, w_up, w_down)


def _combine_kernel(wt_ref, yk_ref, hfp_ref, x1_ref, gf_ref, wsg_ref, wsu_ref, wsd_ref, g_ref, b_ref, *rest,
                    alpha):
    o_ref = rest[-1]
    hb = _unpack_words([hfp_ref[j] for j in range(PACK_W)]).astype(BF16)
    sg = jnp.dot(hb, wsg_ref[...], preferred_element_type=F32)
    su = jnp.dot(hb, wsu_ref[...], preferred_element_type=F32)
    ffn = jnp.dot((_silu(sg) * su).astype(BF16), wsd_ref[...], preferred_element_type=F32)
    wt = wt_ref[...]
    for k in range(TOP_K):
        yk = _unpack_words([yk_ref[k, j] for j in range(PACK_W)])
        ffn = ffn + wt[:, k:k + 1] * yk
    o_ref[...] = _layer_norm(alpha * x1_ref[...] + gf_ref[...] * ffn, g_ref[...], b_ref[...])


def _combine(w_tok, yk, hfp, x1, mod, ws_gate_bf, ws_up_bf, ws_down_bf, ln_g, ln_b, alpha, tok0, earlier):
    n, d = x1.shape
    nc = yk.shape[2]
    s = n // mod.shape[1]
    tc = min(T_COMB, s, nc)
    nt = s // tc
    t0 = tok0 // tc
    fs = ws_gate_bf.shape[1]
    const = lambda shape: pl.BlockSpec(shape, lambda i: tuple(0 for _ in shape))
    in_specs = [pl.BlockSpec((tc, TOP_K), lambda i: (t0 + i, 0)),
                pl.BlockSpec((TOP_K, PACK_W, tc, LANES), lambda i: (0, 0, i, 0)),
                pl.BlockSpec((PACK_W, tc, LANES), lambda i: (0, t0 + i, 0)),
                pl.BlockSpec((tc, d), lambda i: (t0 + i, 0)),
                pl.BlockSpec((None, None, 1, d), lambda i: (5, (t0 + i) // nt, 0, 0)),
                const((d, fs)), const((d, fs)), const((fs, d)), const((1, d)), const((1, d))]
    args = [w_tok, yk, hfp, x1, mod, ws_gate_bf, ws_up_bf, ws_down_bf, ln_g.reshape(1, d), ln_b.reshape(1, d)]
    aliases = {}
    if earlier is not None:
        in_specs.append(pl.BlockSpec(memory_space=pl.ANY))
        args.append(earlier)
        aliases = {len(args) - 1: 0}
    return pl.pallas_call(
        functools.partial(_combine_kernel, alpha=alpha),
        grid=(nc // tc,),
        in_specs=in_specs,
        out_specs=pl.BlockSpec((tc, d), lambda i: (t0 + i, 0)),
        out_shape=jax.ShapeDtypeStruct((n, d), F32),
        input_output_aliases=aliases,
        compiler_params=_cparams(("parallel",)),
        name="combine",
    )(*args)


def _slot_layout(counts, n_assign):
    cnt = counts[:, 0].astype(I32)
    padded = (cnt + BLK_E - 1) // BLK_E * BLK_E
    pends = jnp.cumsum(padded)
    pstarts = pends - padded
    n_blocks_max = (n_assign + N_EXPERTS * (BLK_E - 1)) // BLK_E
    n_blocks = (pends[-1] // BLK_E).astype(I32).reshape(1)
    return pstarts, padded // BLK_E, pstarts // BLK_E, n_blocks, n_blocks_max * BLK_E


def kernel(x, c, w_ada, b_ada, w_in, w_out, na_rpb, ret_log_decay, ret_gn_g, ln1_g, ln1_b, ln2_g, ln2_b,
           w_router, router_bias, w_gate, w_up, w_down, ws_gate, ws_up, ws_down):
    b, s, d = x.shape
    depth = w_ada.shape[0]
    alpha = (2.0 * depth) ** 0.25
    t = jnp.arange(s, dtype=F32)
    inv_freq = ROPE_BASE ** (-jnp.arange(0, RET_HEAD_DIM, 2, dtype=F32) / RET_HEAD_DIM)
    ang = t[:, None] * inv_freq[None, :]
    cos, sin = jnp.cos(ang), jnp.sin(ang)
    cos2 = jnp.concatenate([cos, cos], axis=-1)
    sin2 = jnp.concatenate([-sin, sin], axis=-1)
    for l in range(depth):
        mod = _mod(c, w_ada[l], b_ada[l])
        proj = _inproj(x, mod, w_in[l].astype(BF16))
        y_na = _natten(proj, _na_bias_table(na_rpb[l]))
        y_r = _retention(proj, ret_log_decay[l], ret_gn_g[l], cos2, sin2)
        wr_t = w_router[l].T
        wr_hi = wr_t.astype(BF16)
        wr_lo = (wr_t - wr_hi.astype(F32)).astype(BF16)
        x1, hfp, logits_t = _outproj(y_na, y_r, x, mod, w_out[l].astype(BF16), ln1_g[l], ln1_b[l],
                                     wr_hi, wr_lo, alpha)
        idx, wts, rank, _, snaps = _route(logits_t, router_bias[l])
        n = b * s
        ng = n // MOE_GROUPS
        hfp_rows = hfp.reshape(PACK_W * n, LANES)
        groups = []
        before = jnp.zeros((N_EXPERTS, 1), F32)
        for g in range(MOE_GROUPS):
            pstarts, nblk, gstart, n_blocks, n_slots = _slot_layout(snaps[g] - before, ng * TOP_K)
            slot_rows = _slots(idx, rank, pstarts - before[:, 0].astype(I32), n_slots, g * ng, ng)
            xs = _scatter_rows_sc(hfp_rows, slot_rows, PACK_W * n_slots, n, g * ng, ng)
            groups.append((slot_rows, xs, nblk, gstart, (snaps[g] - before)[:, 0].astype(I32), n_blocks, n_slots))
            before = snaps[g]
        ys_groups = [_experts(nblk, gstart, cnt, n_blocks, xs.reshape(PACK_W, n_slots, LANES),
                              w_gate[l], w_up[l], w_down[l]).reshape(PACK_W * n_slots, LANES)
                     for (_, xs, nblk, gstart, cnt, n_blocks, n_slots) in groups]
        ws = (ws_gate[l].astype(BF16), ws_up[l].astype(BF16), ws_down[l].astype(BF16))
        w_tok = wts.T
        x1n = x1.reshape(n, d)
        nc = ng // COMBINE_CHUNKS
        out = None
        for g in range(MOE_GROUPS):
            slot_tok = groups[g][0].reshape(TOP_K, PACK_W, ng)
            for ci in range(COMBINE_CHUNKS):
                src = slot_tok[:, :, ci * nc:(ci + 1) * nc].reshape(TOP_K * PACK_W * nc // SC_WINDOW, SC_WINDOW)
                yk = _gather_rows_sc(ys_groups[g], src).reshape(TOP_K, PACK_W, nc, LANES)
                out = _combine(w_tok, yk, hfp, x1n, mod, *ws, ln2_g[l], ln2_b[l], alpha, g * ng + ci * nc, out)
        x = out.reshape(b, s, d)
    return x
```

```python
import functools
import math

import jax
import jax.numpy as jnp
import numpy as np
from jax import lax
from jax.experimental import pallas as pl
from jax.experimental.pallas import tpu as pltpu
from jax.experimental.pallas import tpu_sc as plsc

F32 = jnp.float32
BF16 = jnp.bfloat16
U32 = jnp.uint32
I32 = jnp.int32

GRID_W = 64
WIN_R = 8
WIN_C = 16
NA_HEADS = 8
NA_HEAD_DIM = 64
NA_WIDTH = NA_HEADS * NA_HEAD_DIM
RET_HEADS = 4
RET_HEAD_DIM = 128
RET_WIDTH = RET_HEADS * RET_HEAD_DIM
RET_CHUNK = 128
ROPE_BASE = 10000.0
N_EXPERTS = 256
TOP_K = 8
N_GROUPS = 8
TOPK_GROUPS = 4
GROUP_SIZE = N_EXPERTS // N_GROUPS
ROUTED_SCALE = 2.5
LOG2_E = 1.4426950408889634
LN_EPS = 1e-5
GN_EPS = 1e-6

LANES = 128
VMEM_LIMIT = 56 * 1024 * 1024

TM_PROJ = 512
T_ROUTE = 512
T_SLOT = 1024
SC_WINDOW = 128
SC_GATHER_WINDOWS = 2
BLK_E = 512
OUTPROJ_PARTS = 2
T_COMB = 512
MOE_GROUPS = 1
COMBINE_CHUNKS = 8
PACK_W = 4
NA_ROWS_PER_ITER = 32
RET_BLOCK = 256
RET_UNROLL = 8
X_RING = 4
X_AHEAD = 3
Y_RING = 3


def _cparams(sem, vmem=VMEM_LIMIT):
    return pltpu.CompilerParams(dimension_semantics=sem, vmem_limit_bytes=vmem)


def _silu(v):
    return v * jax.nn.sigmoid(v)


def _layer_norm(z, g, b):
    mu = jnp.mean(z, -1, keepdims=True)
    zc = z - mu
    var = jnp.mean(zc * zc, -1, keepdims=True)
    return zc * lax.rsqrt(var + LN_EPS) * g + b


def _pack_rows(v):
    half = v.shape[1] // 2
    vb = v.astype(BF16)
    lo = lax.bitcast_convert_type(vb[:, :half].astype(F32), U32) >> 16
    hi = lax.bitcast_convert_type(vb[:, half:].astype(F32), U32)
    w = lax.bitcast_convert_type(hi | lo, I32)
    return [w[:, j * LANES:(j + 1) * LANES] for j in range(half // LANES)]


def _unpack_words(words):
    words = [lax.bitcast_convert_type(w, U32) for w in words]
    lo = [lax.bitcast_convert_type(w << 16, F32) for w in words]
    hi = [lax.bitcast_convert_type(w & jnp.uint32(0xFFFF0000), F32) for w in words]
    return jnp.concatenate(lo + hi, axis=-1)


def _mod_kernel(c_ref, w_ref, b_ref, o_ref):
    cond = _silu(c_ref[...])
    o_ref[0] = jnp.dot(cond, w_ref[...], precision=lax.Precision.HIGHEST,
                       preferred_element_type=F32) + b_ref[0]


def _mod(c, w_ada, b_ada):
    b, d = c.shape
    n6 = w_ada.shape[1] // d
    out = pl.pallas_call(
        _mod_kernel,
        grid=(n6,),
        in_specs=[pl.BlockSpec((b, d), lambda j: (0, 0)),
                  pl.BlockSpec((d, d), lambda j: (0, j)),
                  pl.BlockSpec((1, 1, d), lambda j: (j, 0, 0))],
        out_specs=pl.BlockSpec((1, b, d), lambda j: (j, 0, 0)),
        out_shape=jax.ShapeDtypeStruct((n6, b, d), F32),
        compiler_params=_cparams(("arbitrary",)),
        name="mod",
    )(c, w_ada, b_ada.reshape(n6, 1, d))
    return out.reshape(n6, b, 1, d)


def _mod_spec(which, d):
    return pl.BlockSpec((None, None, 1, d), lambda b, i, which=which: (which, b, 0, 0))


def _inproj_kernel(x_ref, sc_ref, sh_ref, w_ref, o_ref, *, chunk, q_scale):
    h = (x_ref[...] * (1.0 + sc_ref[...]) + sh_ref[...]).astype(BF16)
    for j in range(o_ref.shape[1] // chunk):
        acc = jnp.dot(h, w_ref[:, j * chunk:(j + 1) * chunk], preferred_element_type=F32)
        if j == 0:
            acc = acc * q_scale
        o_ref[:, j * chunk:(j + 1) * chunk] = acc.astype(o_ref.dtype)


def _inproj(x, mod, w_in_bf):
    b, s, d = x.shape
    e = w_in_bf.shape[1]
    tm = min(TM_PROJ, s)
    return pl.pallas_call(
        functools.partial(_inproj_kernel, chunk=NA_WIDTH, q_scale=NA_HEAD_DIM ** -0.5 * LOG2_E),
        grid=(b, s // tm),
        in_specs=[pl.BlockSpec((None, tm, d), lambda bi, i: (bi, i, 0)),
                  _mod_spec(1, d), _mod_spec(0, d),
                  pl.BlockSpec((d, e), lambda bi, i: (0, 0))],
        out_specs=pl.BlockSpec((None, tm, e), lambda bi, i: (bi, i, 0)),
        out_shape=jax.ShapeDtypeStruct((b, s, e), BF16),
        compiler_params=_cparams(("parallel", "parallel")),
        name="inproj",
    )(x, mod, mod, w_in_bf)


def _natten_kernel(q_ref, k_ref, v_ref, bias_ref, o_ref, *, rows, rows_per_iter):
    kspan = WIN_R * GRID_W

    first = lax.broadcasted_iota(I32, (1, LANES), 1) < NA_HEAD_DIM
    zero = jnp.zeros((), BF16)

    def rows_body(i, carry):
        qrows, krows, scores, probs = {}, {}, {}, {}

        def stage_scores(u):
            r = i * rows_per_iter + u
            rs = jnp.clip(r - WIN_R // 2, 0, rows - WIN_R)
            vi = r - rs
            qrows[u] = pl.ds(pl.multiple_of(r * GRID_W, GRID_W), GRID_W)
            krows[u] = pl.ds(pl.multiple_of(rs * GRID_W, GRID_W), kspan)
            q = q_ref[qrows[u], :]
            qm = jnp.concatenate([jnp.where(first, q, zero), jnp.where(first, zero, q)], axis=0)
            st = lax.dot_general(k_ref[krows[u], :], qm, (((1,), (1,)), ((), ())), preferred_element_type=F32)
            scores[u] = st + bias_ref[vi]

        def stage_softmax(u):
            st = scores.pop(u)
            p = jnp.exp2(st - jnp.max(st, axis=0, keepdims=True))
            probs[u] = (p * (1.0 / jnp.sum(p, axis=0, keepdims=True))).astype(BF16)

        def stage_values(u):
            res = lax.dot_general(probs.pop(u), v_ref[krows[u], :], (((0,), (0,)), ((), ())),
                                  preferred_element_type=F32)
            o_ref[qrows[u], :] = jnp.where(first, res[:GRID_W], res[GRID_W:]).astype(o_ref.dtype)

        for step in range(rows_per_iter + 2):
            if step < rows_per_iter:
                stage_scores(step)
            if 0 <= step - 1 < rows_per_iter:
                stage_softmax(step - 1)
            if 0 <= step - 2 < rows_per_iter:
                stage_values(step - 2)
        return carry

    lax.fori_loop(0, rows // rows_per_iter, rows_body, 0)


def _na_bias_table(rpb):
    w = GRID_W
    cq = jnp.arange(w)
    cs = jnp.clip(cq - WIN_C // 2, 0, w - WIN_C)
    ck = jnp.arange(w)
    col_in = (ck[None, :] >= cs[:, None]) & (ck[None, :] < cs[:, None] + WIN_C)
    dc_idx = jnp.clip(ck[None, :] - cq[:, None] + WIN_C - 1, 0, 2 * WIN_C - 2)
    t = rpb[:, :, dc_idx]
    t = jnp.where(col_in[None, None], t, -jnp.inf)
    vi = jnp.arange(WIN_R)
    kr = jnp.arange(WIN_R)
    dr = kr[None, :] - vi[:, None] + WIN_R - 1
    tb = t[:, dr]
    hp = LANES // NA_HEAD_DIM
    tb = tb.reshape(rpb.shape[0] // hp, hp, WIN_R, WIN_R, w, w)
    tb = tb.transpose(0, 2, 3, 5, 1, 4).reshape(rpb.shape[0] // hp, WIN_R, WIN_R * w, hp * w)
    return tb.astype(F32) * LOG2_E


def _natten(proj, bias_tab):
    b, s, _ = proj.shape
    rows = s // GRID_W
    hp = LANES // NA_HEAD_DIM
    npair = NA_HEADS // hp
    blk = lambda off: pl.BlockSpec((None, s, LANES), lambda bi, p, off=off: (bi, 0, off + p))
    return pl.pallas_call(
        functools.partial(_natten_kernel, rows=rows, rows_per_iter=math.gcd(rows, NA_ROWS_PER_ITER)),
        grid=(b, npair),
        in_specs=[blk(0), blk(npair), blk(2 * npair),
                  pl.BlockSpec((None, WIN_R, WIN_R * GRID_W, hp * GRID_W), lambda bi, p: (p, 0, 0, 0))],
        out_specs=pl.BlockSpec((None, s, LANES), lambda bi, p: (bi, 0, p)),
        out_shape=jax.ShapeDtypeStruct((b, s, NA_WIDTH), BF16),
        compiler_params=_cparams(("parallel", "parallel")),
        name="natten",
    )(proj, proj, proj, bias_tab)


def _retent_kernel(ld_ref, q_ref, k_ref, v_ref, g_ref, cos_ref, sin_ref, gn_ref, o_ref,
                   qs_ref, ks_ref, sb_ref, *, nchunk):
    c = RET_BLOCK
    dh = RET_HEAD_DIM
    h = pl.program_id(1)
    lgf = ld_ref[0, h]
    lgb = ld_ref[1, h]

    cos2 = cos_ref[...]
    sin2 = sin_ref[...]
    qf = q_ref[...].astype(F32)
    qs_ref[...] = qf * cos2 + pltpu.roll(qf, dh // 2, 1) * sin2
    kf = k_ref[...].astype(F32)
    ks_ref[...] = (kf * cos2 + pltpu.roll(kf, dh // 2, 1) * sin2) * (dh ** -0.5)

    ic = lax.broadcasted_iota(I32, (c, 1), 0).astype(F32)
    ir = lax.broadcasted_iota(I32, (1, c), 1).astype(F32)
    diff = ic - ir
    dmat = jnp.where(diff >= 0, jnp.exp(jnp.maximum(diff, 0.0) * lgf),
                     jnp.exp(jnp.maximum(-diff, 0.0) * lgb))
    kdec_f = jnp.exp((c - 1 - ic) * lgf)
    qdec_f = jnp.exp((ic + 1) * lgf)
    kdec_b = jnp.exp(ic * lgb)
    qdec_b = jnp.exp((c - ic) * lgb)
    one = jnp.ones((1, 1), F32)
    cdec_f = jnp.exp(one * (c * lgf))
    cdec_b = jnp.exp(one * (c * lgb))
    tn = (((0,), (0,)), ((), ()))

    def bwd_body(i, sb):
        n = nchunk - 1 - i
        sb_ref[n] = sb
        rows = pl.ds(pl.multiple_of(n * c, c), c)
        kd = (ks_ref[rows, :] * kdec_b).astype(BF16)
        kv = lax.dot_general(kd, v_ref[rows, :], tn, preferred_element_type=F32)
        return cdec_b * sb + kv

    lax.fori_loop(0, nchunk, bwd_body, jnp.zeros((dh, dh), F32), unroll=min(RET_UNROLL, nchunk))

    gn = gn_ref[...]

    def fwd_body(n, sf):
        rows = pl.ds(pl.multiple_of(n * c, c), c)
        qn = qs_ref[rows, :]
        kn = ks_ref[rows, :]
        vn = v_ref[rows, :]
        sc = lax.dot_general(qn.astype(BF16), kn.astype(BF16), (((1,), (1,)), ((), ())),
                             preferred_element_type=F32) * dmat
        y = jnp.dot(sc.astype(BF16), vn, preferred_element_type=F32)
        qd = jnp.concatenate([qn * qdec_f, qn * qdec_b], axis=1).astype(BF16)
        st = jnp.concatenate([sf, sb_ref[n]], axis=0).astype(BF16)
        y = y + jnp.dot(qd, st, preferred_element_type=F32)
        mu = jnp.mean(y, -1, keepdims=True)
        yc = y - mu
        var = jnp.mean(yc * yc, -1, keepdims=True)
        yn = yc * lax.rsqrt(var + GN_EPS) * gn
        o_ref[rows, :] = (_silu(g_ref[rows, :].astype(F32)) * yn).astype(o_ref.dtype)
        kv = lax.dot_general((kn * kdec_f).astype(BF16), vn, tn, preferred_element_type=F32)
        return cdec_f * sf + kv

    lax.fori_loop(0, nchunk, fwd_body, jnp.zeros((dh, dh), F32), unroll=min(RET_UNROLL, nchunk))


def _retention(proj, log_decay, gn_g, cos2, sin2):
    b, s, _ = proj.shape
    dh = RET_HEAD_DIM
    nchunk = s // RET_BLOCK
    base = 3 * NA_WIDTH // dh
    blk = lambda off: pl.BlockSpec((None, s, dh), lambda bi, h, off=off: (bi, 0, base + off + h))
    full = pl.BlockSpec((s, dh), lambda bi, h: (0, 0))
    return pl.pallas_call(
        functools.partial(_retent_kernel, nchunk=nchunk),
        grid=(b, RET_HEADS),
        in_specs=[pl.BlockSpec(memory_space=pltpu.SMEM),
                  blk(0), blk(RET_HEADS), blk(2 * RET_HEADS), blk(3 * RET_HEADS),
                  full, full,
                  pl.BlockSpec((1, dh), lambda bi, h: (0, h))],
        out_specs=pl.BlockSpec((None, s, dh), lambda bi, h: (bi, 0, h)),
        out_shape=jax.ShapeDtypeStruct((b, s, RET_WIDTH), BF16),
        scratch_shapes=[pltpu.VMEM((s, dh), F32), pltpu.VMEM((s, dh), F32),
                        pltpu.VMEM((nchunk, dh, dh), F32)],
        compiler_params=_cparams(("parallel", "parallel")),
        name="retent",
    )(log_decay, proj, proj, proj, proj, cos2, sin2, gn_g.reshape(1, RET_WIDTH))


def _outproj_kernel(yna_ref, yr_ref, x_ref, ga_ref, sf_ref, shf_ref, wo1_ref, wo2_ref, g_ref, b_ref,
                    wrh_ref, wrl_ref, x1_ref, hfp_ref, lg_ref, *, alpha):
    nt = (((1,), (1,)), ((), ()))
    tm = x_ref.shape[0]
    parts = [pl.ds(p * (tm // OUTPROJ_PARTS), tm // OUTPROJ_PARTS) for p in range(OUTPROJ_PARTS)]
    def mix_of(r):
        return (jnp.dot(yna_ref[r, :], wo1_ref[...], preferred_element_type=F32)
                + jnp.dot(yr_ref[r, :], wo2_ref[...], preferred_element_type=F32))

    nxt = mix_of(parts[0])
    for p, r in enumerate(parts):
        mix = nxt
        if p + 1 < len(parts):
            nxt = mix_of(parts[p + 1])
        x1 = _layer_norm(alpha * x_ref[r, :] + ga_ref[...] * mix, g_ref[...], b_ref[...])
        x1_ref[r, :] = x1
        hf = x1 * (1.0 + sf_ref[...]) + shf_ref[...]
        for j, w in enumerate(_pack_rows(hf)):
            hfp_ref[j, r, :] = w
        hb = hf.astype(BF16)
        hl = (hf - hb.astype(F32)).astype(BF16)
        lg = lax.dot_general(wrh_ref[...], hb, nt, preferred_element_type=F32)
        lg = lg + lax.dot_general(wrh_ref[...], hl, nt, preferred_element_type=F32)
        lg = lg + lax.dot_general(wrl_ref[...], hb, nt, preferred_element_type=F32)
        lg_ref[:, r] = lg


def _outproj(y_na, y_r, x, mod, w_out_bf, ln_g, ln_b, wr_hi, wr_lo, alpha):
    b, s, d = x.shape
    tm = min(TM_PROJ, s)
    nt = s // tm
    ne = wr_hi.shape[0]
    const = lambda shape: pl.BlockSpec(shape, lambda bi, i: tuple(0 for _ in shape))
    x1, hfp, lg = pl.pallas_call(
        functools.partial(_outproj_kernel, alpha=alpha),
        grid=(b, nt),
        in_specs=[pl.BlockSpec((None, tm, NA_WIDTH), lambda bi, i: (bi, i, 0)),
                  pl.BlockSpec((None, tm, RET_WIDTH), lambda bi, i: (bi, i, 0)),
                  pl.BlockSpec((None, tm, d), lambda bi, i: (bi, i, 0)),
                  _mod_spec(2, d), _mod_spec(4, d), _mod_spec(3, d),
                  pl.BlockSpec((NA_WIDTH, d), lambda bi, i: (0, 0)),
                  pl.BlockSpec((RET_WIDTH, d), lambda bi, i: (1, 0)),
                  const((1, d)), const((1, d)), const((ne, d)), const((ne, d))],
        out_specs=[pl.BlockSpec((None, tm, d), lambda bi, i: (bi, i, 0)),
                   pl.BlockSpec((PACK_W, tm, LANES), lambda bi, i: (0, bi * nt + i, 0)),
                   pl.BlockSpec((ne, tm), lambda bi, i: (0, bi * nt + i))],
        out_shape=[jax.ShapeDtypeStruct((b, s, d), F32),
                   jax.ShapeDtypeStruct((PACK_W, b * s, LANES), I32),
                   jax.ShapeDtypeStruct((ne, b * s), F32)],
        compiler_params=_cparams(("parallel", "parallel")),
        name="outproj",
    )(y_na, y_r, x, mod, mod, mod, w_out_bf, w_out_bf, ln_g.reshape(1, d), ln_b.reshape(1, d), wr_hi, wr_lo)
    return x1, hfp, lg


def _route_kernel(lg_ref, rb_ref, idx_ref, w_ref, rank_ref, cnt_ref, snap_ref):
    t = lg_ref.shape[1]
    ninf = -jnp.inf
    step = pl.program_id(0)
    steps_per_group = pl.num_programs(0) // MOE_GROUPS

    @pl.when(step == 0)
    def _():
        cnt_ref[...] = jnp.zeros_like(cnt_ref)

    scores = jax.nn.sigmoid(lg_ref[...])
    sel = scores + rb_ref[...]

    io_g = lax.broadcasted_iota(I32, (GROUP_SIZE, t), 0)
    gs_rows = []
    for g in range(N_GROUPS):
        blk = sel[g * GROUP_SIZE:(g + 1) * GROUP_SIZE, :]
        m1 = jnp.max(blk, axis=0, keepdims=True)
        i1 = jnp.min(jnp.where(blk == m1, io_g, GROUP_SIZE), axis=0, keepdims=True)
        m2 = jnp.max(jnp.where(io_g == i1, ninf, blk), axis=0, keepdims=True)
        gs_rows.append(m1 + m2)
    gs = jnp.concatenate(gs_rows, axis=0)

    io8 = lax.broadcasted_iota(I32, (N_GROUPS, t), 0)
    gsel = jnp.zeros((N_GROUPS, t), F32)
    for _ in range(TOPK_GROUPS):
        m = jnp.max(gs, axis=0, keepdims=True)
        gi = jnp.min(jnp.where(gs == m, io8, N_GROUPS), axis=0, keepdims=True)
        hit = io8 == gi
        gsel = jnp.where(hit, 1.0, gsel)
        gs = jnp.where(hit, ninf, gs)

    masked = jnp.concatenate(
        [jnp.where(gsel[g:g + 1, :] > 0.0, sel[g * GROUP_SIZE:(g + 1) * GROUP_SIZE, :], ninf)
         for g in range(N_GROUPS)], axis=0)

    io_e = lax.broadcasted_iota(I32, (N_EXPERTS, t), 0)
    chosen = jnp.zeros((N_EXPERTS, t), F32)
    idx_rows, w_rows = [], []
    for _ in range(TOP_K):
        m = jnp.max(masked, axis=0, keepdims=True)
        ei = jnp.min(jnp.where(masked == m, io_e, N_EXPERTS), axis=0, keepdims=True)
        hit = io_e == ei
        w_rows.append(jnp.sum(jnp.where(hit, scores, 0.0), axis=0, keepdims=True))
        idx_rows.append(ei)
        chosen = jnp.where(hit, 1.0, chosen)
        masked = jnp.where(hit, ninf, masked)
    wk = jnp.concatenate(w_rows, axis=0)
    w_ref[...] = wk / jnp.sum(wk, axis=0, keepdims=True) * ROUTED_SCALE
    idx_ref[...] = jnp.concatenate(idx_rows, axis=0)

    upper = (lax.broadcasted_iota(I32, (t, t), 0) < lax.broadcasted_iota(I32, (t, t), 1))
    prefix = jnp.dot(chosen.astype(BF16), upper.astype(BF16), preferred_element_type=F32)
    rank_full = prefix + cnt_ref[...]
    rank_rows = [jnp.sum(jnp.where(io_e == ei, rank_full, 0.0), axis=0, keepdims=True) for ei in idx_rows]
    rank_ref[...] = jnp.concatenate(rank_rows, axis=0).astype(I32)
    cnt_ref[...] += jnp.sum(chosen, axis=1, keepdims=True)

    for g in range(MOE_GROUPS):
        @pl.when(step == (g + 1) * steps_per_group - 1)
        def _():
            snap_ref[g] = cnt_ref[...]


def _route(logits_t, router_bias):
    ne, n = logits_t.shape
    t = min(T_ROUTE, n)
    kspec = pl.BlockSpec((TOP_K, t), lambda i: (0, i))
    return pl.pallas_call(
        _route_kernel,
        grid=(n // t,),
        in_specs=[pl.BlockSpec((ne, t), lambda i: (0, i)),
                  pl.BlockSpec((ne, 1), lambda i: (0, 0))],
        out_specs=[kspec, kspec, kspec, pl.BlockSpec((ne, 1), lambda i: (0, 0)),
                   pl.BlockSpec((MOE_GROUPS, ne, 1), lambda i: (0, 0, 0))],
        out_shape=[jax.ShapeDtypeStruct((TOP_K, n), I32),
                   jax.ShapeDtypeStruct((TOP_K, n), F32),
                   jax.ShapeDtypeStruct((TOP_K, n), I32),
                   jax.ShapeDtypeStruct((ne, 1), F32),
                   jax.ShapeDtypeStruct((MOE_GROUPS, ne, 1), F32)],
        compiler_params=_cparams(("arbitrary",)),
        name="route",
    )(logits_t, router_bias.reshape(ne, 1))


def _block(ref, g):
    return ref.at[:, pl.ds(pl.multiple_of(g * BLK_E, BLK_E), BLK_E), :]


def _slots_kernel(idx_ref, rank_ref, ps_ref, o_ref, *, n_slots):
    t = idx_ref.shape[1]
    io = lax.broadcasted_iota(I32, (N_EXPERTS, t), 0)
    ps = ps_ref[...]
    for k in range(TOP_K):
        hit = io == idx_ref[k:k + 1, :]
        slot = jnp.sum(jnp.where(hit, ps, 0), axis=0, keepdims=True) + rank_ref[k:k + 1, :]
        for j in range(PACK_W):
            o_ref[j * TOP_K + k:j * TOP_K + k + 1, :] = slot + j * n_slots


def _slots(idx, rank, slot_base, n_slots, tok0, ng):
    t = min(T_SLOT, ng)
    t0 = tok0 // t
    kspec = pl.BlockSpec((TOP_K, t), lambda i: (0, t0 + i))
    return pl.pallas_call(
        functools.partial(_slots_kernel, n_slots=n_slots),
        grid=(ng // t,),
        in_specs=[kspec, kspec, pl.BlockSpec((N_EXPERTS, 1), lambda i: (0, 0))],
        out_specs=pl.BlockSpec((PACK_W * TOP_K, t), lambda i: (0, i)),
        out_shape=jax.ShapeDtypeStruct((PACK_W * TOP_K, ng), I32),
        compiler_params=_cparams(("parallel",)),
        name="slots",
    )(idx, rank, slot_base.reshape(N_EXPERTS, 1))


def _sc_mesh():
    return plsc.VectorSubcoreMesh(core_axis_name="core", subcore_axis_name="subcore")


def _scatter_rows_sc(rows, dest_rows, n_out, n_tok, tok0, ng):
    n_rows = PACK_W * ng
    wpg = ng // SC_WINDOW
    wpp = n_tok // SC_WINDOW
    w0 = tok0 // SC_WINDOW

    @functools.partial(pl.kernel, mesh=_sc_mesh(), scratch_types=[pltpu.SemaphoreType.DMA],
                       out_type=jax.ShapeDtypeStruct((n_out, LANES), I32))
    def scatter_rows(x_hbm, i_hbm, o_hbm, sem):
        def body(x_vmem, i_vmem):
            copies = [pltpu.async_copy(x_vmem, o_hbm.at[i_vmem.at[k]], sem) for k in range(TOP_K)]
            for c in copies:
                c.wait()

        pltpu.emit_pipeline(
            body,
            grid=(n_rows // SC_WINDOW,),
            in_specs=[pl.BlockSpec((SC_WINDOW, LANES), lambda i: ((i // wpg) * wpp + w0 + i % wpg, 0)),
                      pl.BlockSpec((TOP_K, SC_WINDOW), lambda i: (i // wpg, i % wpg))],
            out_specs=[],
            core_axis_name=("core", "subcore"),
            dimension_semantics=(pltpu.PARALLEL,),
        )(x_hbm, i_hbm)

    return scatter_rows(rows, dest_rows)


def _gather_rows_sc(rows, slot_rows, tok0, nc):
    nr = slot_rows.shape[0]
    span = SC_GATHER_WINDOWS * SC_WINDOW
    spr = nc // span
    w0 = tok0 // SC_WINDOW

    @functools.partial(pl.kernel, mesh=_sc_mesh(), scratch_types=[pltpu.SemaphoreType.DMA],
                       out_type=jax.ShapeDtypeStruct((nr * nc, LANES), I32))
    def gather_rows(x_hbm, i_hbm, o_hbm, sem):
        def body(*refs):
            o_vmem = refs[-1]
            copies = [pltpu.async_copy(x_hbm.at[i_vmem.at[0]], o_vmem.at[pl.ds(w * SC_WINDOW, SC_WINDOW)], sem)
                      for w, i_vmem in enumerate(refs[:-1])]
            for c in copies:
                c.wait()

        def idx_spec(w):
            return pl.BlockSpec((1, SC_WINDOW),
                                lambda i: (i // spr, w0 + (i % spr) * SC_GATHER_WINDOWS + w))

        pltpu.emit_pipeline(
            body,
            grid=(nr * spr,),
            in_specs=[idx_spec(w) for w in range(SC_GATHER_WINDOWS)],
            out_specs=[pl.BlockSpec((span, LANES), lambda i: (i, 0))],
            core_axis_name=("core", "subcore"),
            dimension_semantics=(pltpu.PARALLEL,),
        )(*([i_hbm] * SC_GATHER_WINDOWS), o_hbm)

    return gather_rows(rows, slot_rows)


def _experts_kernel(nblk_ref, gstart_ref, cnt_ref, nb_ref, xs_ref, wg_ref, wu_ref, wd_ref, ys_ref,
                    xbuf_ref, ybuf_ref, wgub_ref, wdb_ref, xsem, ysem):
    e = pl.program_id(0)
    total = nb_ref[0]
    nb_max = ys_ref.shape[1] // BLK_E

    def x_copy(g):
        slot = g % X_RING
        return pltpu.make_async_copy(_block(xs_ref, g), xbuf_ref.at[slot], xsem.at[slot])

    def y_copy(g):
        slot = g % Y_RING
        return pltpu.make_async_copy(ybuf_ref.at[slot], _block(ys_ref, g), ysem.at[slot])

    @pl.when(e == 0)
    def _():
        for g in range(X_AHEAD):
            @pl.when(g < total)
            def _():
                x_copy(g).start()

    @pl.when(nblk_ref[e] > 0)
    def _():
        f = wg_ref.shape[1]
        wgub_ref[:, :f] = wg_ref[...].astype(BF16)
        wgub_ref[:, f:] = wu_ref[...].astype(BF16)
        wdb_ref[...] = wd_ref[...].astype(BF16)

    def fetch(b):
        g = gstart_ref[e] + b
        x_copy(g).wait()

        @pl.when(g + X_AHEAD < total)
        def _():
            x_copy(g + X_AHEAD).start()

        xb = _unpack_words([xbuf_ref[g % X_RING, j] for j in range(PACK_W)]).astype(BF16)
        live = lax.broadcasted_iota(I32, (BLK_E, 1), 0) < cnt_ref[e] - b * BLK_E
        return jnp.where(live, xb, jnp.zeros((), BF16))

    def hidden(xb):
        f = wg_ref.shape[1]
        hgu = jnp.dot(xb, wgub_ref[...], preferred_element_type=F32)
        return (_silu(hgu[:, :f]) * hgu[:, f:]).astype(BF16)

    def finish(b, act):
        g = gstart_ref[e] + b
        words = _pack_rows(jnp.dot(act, wdb_ref[...], preferred_element_type=F32))

        @pl.when(g >= Y_RING)
        def _():
            y_copy(g - Y_RING).wait()

        for j, w in enumerate(words):
            ybuf_ref[g % Y_RING, j] = w
        y_copy(g).start()

    def block(b, carry):
        finish(b, hidden(fetch(b)))
        return carry

    lax.fori_loop(0, nblk_ref[e], block, 0)

    @pl.when(e == pl.num_programs(0) - 1)
    def _():
        for back in range(Y_RING):
            @pl.when(total - 1 - back >= 0)
            def _():
                y_copy(total - 1 - back).wait()

        ybuf_ref[0] = jnp.zeros(ybuf_ref.shape[1:], ybuf_ref.dtype)

        def tail_copy(g):
            return pltpu.make_async_copy(ybuf_ref.at[0], _block(ys_ref, g), ysem.at[0])

        def fill(g, carry):
            tail_copy(g).start()
            return carry

        lax.fori_loop(total, nb_max, fill, 0)

        def drain(g, carry):
            tail_copy(g).wait()
            return carry

        lax.fori_loop(total, nb_max, drain, 0)


def _experts(nblk, gstart, cnt, n_blocks, xs, w_gate, w_up, w_down):
    p = xs.shape[1]
    ne, d, f = w_gate.shape
    wspec = lambda shape: pl.BlockSpec((None,) + shape, lambda e, *_: (e, 0, 0))
    grid_spec = pltpu.PrefetchScalarGridSpec(
        num_scalar_prefetch=4,
        grid=(ne,),
        in_specs=[pl.BlockSpec(memory_space=pl.ANY), wspec((d, f)), wspec((d, f)), wspec((f, d))],
        out_specs=pl.BlockSpec(memory_space=pl.ANY),
        scratch_shapes=[pltpu.VMEM((X_RING, PACK_W, BLK_E, LANES), I32),
                        pltpu.VMEM((Y_RING, PACK_W, BLK_E, LANES), I32),
                        pltpu.VMEM((d, 2 * f), BF16), pltpu.VMEM((f, d), BF16),
                        pltpu.SemaphoreType.DMA((X_RING,)), pltpu.SemaphoreType.DMA((Y_RING,))],
    )
    return pl.pallas_call(
        _experts_kernel,
        grid_spec=grid_spec,
        out_shape=jax.ShapeDtypeStruct((PACK_W, p, LANES), I32),
        compiler_params=_cparams(("arbitrary",)),
        name="experts",
    )(nblk, gstart, cnt, n_blocks, xs, w_gate, w_up, w_down)


def _combine_kernel(wt_ref, yk_ref, x1_ref, sf_ref, shf_ref, gf_ref, wsg_ref, wsu_ref, wsd_ref, g_ref, b_ref,
                    *rest, alpha):
    o_ref = rest[-1]
    x1 = x1_ref[...]
    hb = (x1 * (1.0 + sf_ref[...]) + shf_ref[...]).astype(BF16)
    sg = jnp.dot(hb, wsg_ref[...], preferred_element_type=F32)
    su = jnp.dot(hb, wsu_ref[...], preferred_element_type=F32)
    ffn = jnp.dot((_silu(sg) * su).astype(BF16), wsd_ref[...], preferred_element_type=F32)
    wt = wt_ref[...]
    for k in range(TOP_K):
        yk = _unpack_words([yk_ref[j, k] for j in range(PACK_W)])
        ffn = ffn + wt[:, k:k + 1] * yk
    o_ref[...] = _layer_norm(alpha * x1 + gf_ref[...] * ffn, g_ref[...], b_ref[...])


def _combine(w_tok, yk, x1, mod, ws_gate_bf, ws_up_bf, ws_down_bf, ln_g, ln_b, alpha, tok0, earlier):
    n, d = x1.shape
    nc = yk.shape[2]
    s = n // mod.shape[1]
    tc = min(T_COMB, s, nc)
    nt = s // tc
    t0 = tok0 // tc
    fs = ws_gate_bf.shape[1]
    const = lambda shape: pl.BlockSpec(shape, lambda i: tuple(0 for _ in shape))
    mod_spec = lambda which: pl.BlockSpec((None, None, 1, d), lambda i: (which, (t0 + i) // nt, 0, 0))
    in_specs = [pl.BlockSpec((tc, TOP_K), lambda i: (t0 + i, 0)),
                pl.BlockSpec((PACK_W, TOP_K, tc, LANES), lambda i: (0, 0, i, 0)),
                pl.BlockSpec((tc, d), lambda i: (t0 + i, 0)),
                mod_spec(4), mod_spec(3), mod_spec(5),
                const((d, fs)), const((d, fs)), const((fs, d)), const((1, d)), const((1, d))]
    args = [w_tok, yk, x1, mod, mod, mod, ws_gate_bf, ws_up_bf, ws_down_bf, ln_g.reshape(1, d), ln_b.reshape(1, d)]
    aliases = {}
    if earlier is not None:
        in_specs.append(pl.BlockSpec(memory_space=pl.ANY))
        args.append(earlier)
        aliases = {len(args) - 1: 0}
    return pl.pallas_call(
        functools.partial(_combine_kernel, alpha=alpha),
        grid=(nc // tc,),
        in_specs=in_specs,
        out_specs=pl.BlockSpec((tc, d), lambda i: (t0 + i, 0)),
        out_shape=jax.ShapeDtypeStruct((n, d), F32),
        input_output_aliases=aliases,
        compiler_params=_cparams(("parallel",)),
        name="combine",
    )(*args)


def _slot_layout(counts, n_assign):
    cnt = counts[:, 0].astype(I32)
    padded = (cnt + BLK_E - 1) // BLK_E * BLK_E
    pends = jnp.cumsum(padded)
    pstarts = pends - padded
    n_blocks_max = (n_assign + N_EXPERTS * (BLK_E - 1)) // BLK_E
    n_blocks = (pends[-1] // BLK_E).astype(I32).reshape(1)
    return pstarts, padded // BLK_E, pstarts // BLK_E, n_blocks, n_blocks_max * BLK_E


def kernel(x, c, w_ada, b_ada, w_in, w_out, na_rpb, ret_log_decay, ret_gn_g, ln1_g, ln1_b, ln2_g, ln2_b,
           w_router, router_bias, w_gate, w_up, w_down, ws_gate, ws_up, ws_down):
    b, s, d = x.shape
    depth = w_ada.shape[0]
    alpha = (2.0 * depth) ** 0.25
    t = jnp.arange(s, dtype=F32)
    inv_freq = ROPE_BASE ** (-jnp.arange(0, RET_HEAD_DIM, 2, dtype=F32) / RET_HEAD_DIM)
    ang = t[:, None] * inv_freq[None, :]
    cos, sin = jnp.cos(ang), jnp.sin(ang)
    cos2 = jnp.concatenate([cos, cos], axis=-1)
    sin2 = jnp.concatenate([-sin, sin], axis=-1)
    for l in range(depth):
        mod = _mod(c, w_ada[l], b_ada[l])
        proj = _inproj(x, mod, w_in[l].astype(BF16))
        y_na = _natten(proj, _na_bias_table(na_rpb[l]))
        y_r = _retention(proj, ret_log_decay[l], ret_gn_g[l], cos2, sin2)
        wr_t = w_router[l].T
        wr_hi = wr_t.astype(BF16)
        wr_lo = (wr_t - wr_hi.astype(F32)).astype(BF16)
        x1, hfp, logits_t = _outproj(y_na, y_r, x, mod, w_out[l].astype(BF16), ln1_g[l], ln1_b[l],
                                     wr_hi, wr_lo, alpha)
        idx, wts, rank, _, snaps = _route(logits_t, router_bias[l])
        n = b * s
        ng = n // MOE_GROUPS
        hfp_rows = hfp.reshape(PACK_W * n, LANES)
        groups = []
        before = jnp.zeros((N_EXPERTS, 1), F32)
        for g in range(MOE_GROUPS):
            pstarts, nblk, gstart, n_blocks, n_slots = _slot_layout(snaps[g] - before, ng * TOP_K)
            slot_rows = _slots(idx, rank, pstarts - before[:, 0].astype(I32), n_slots, g * ng, ng)
            xs = _scatter_rows_sc(hfp_rows, slot_rows, PACK_W * n_slots, n, g * ng, ng)
            groups.append((slot_rows, xs, nblk, gstart, (snaps[g] - before)[:, 0].astype(I32), n_blocks, n_slots))
            before = snaps[g]
        ys_groups = [_experts(nblk, gstart, cnt, n_blocks, xs.reshape(PACK_W, n_slots, LANES),
                              w_gate[l], w_up[l], w_down[l]).reshape(PACK_W * n_slots, LANES)
                     for (_, xs, nblk, gstart, cnt, n_blocks, n_slots) in groups]
        ws = (ws_gate[l].astype(BF16), ws_up[l].astype(BF16), ws_down[l].astype(BF16))
        w_tok = wts.T
        x1n = x1.reshape(n, d)
        nc = ng // COMBINE_CHUNKS
        out = None
        for g in range(MOE_GROUPS):
            for ci in range(COMBINE_CHUNKS):
                yk = _gather_rows_sc(ys_groups[g], groups[g][0], ci * nc, nc).reshape(PACK_W, TOP_K, nc, LANES)
                out = _combine(w_tok, yk, x1n, mod, *ws, ln2_g[l], ln2_b[l], alpha, g * ng + ci * nc, out)
        x = out.reshape(b, s, d)
    return x
```

```python
import functools
import math

import jax
import jax.numpy as jnp
import numpy as np
from jax import lax
from jax.experimental import pallas as pl
from jax.experimental.pallas import tpu as pltpu
from jax.experimental.pallas import tpu_sc as plsc

F32 = jnp.float32
BF16 = jnp.bfloat16
U32 = jnp.uint32
I32 = jnp.int32

GRID_W = 64
WIN_R = 8
WIN_C = 16
NA_HEADS = 8
NA_HEAD_DIM = 64
NA_WIDTH = NA_HEADS * NA_HEAD_DIM
RET_HEADS = 4
RET_HEAD_DIM = 128
RET_WIDTH = RET_HEADS * RET_HEAD_DIM
RET_CHUNK = 128
ROPE_BASE = 10000.0
N_EXPERTS = 256
TOP_K = 8
N_GROUPS = 8
TOPK_GROUPS = 4
GROUP_SIZE = N_EXPERTS // N_GROUPS
ROUTED_SCALE = 2.5
LOG2_E = 1.4426950408889634
LN_EPS = 1e-5
GN_EPS = 1e-6

LANES = 128
VMEM_LIMIT = 56 * 1024 * 1024

TM_PROJ = 512
T_ROUTE = 512
T_SLOT = 1024
SC_WINDOW = 128
SC_GATHER_WINDOWS = 2
BLK_E = 512
OUTPROJ_PARTS = 2
T_COMB = 512
MOE_GROUPS = 1
COMBINE_CHUNKS = 8
PACK_W = 4
NA_ROWS_PER_ITER = 32
RET_BLOCK = 256
RET_UNROLL = 8
X_RING = 4
X_AHEAD = 3
Y_RING = 3


def _cparams(sem, vmem=VMEM_LIMIT):
    return pltpu.CompilerParams(dimension_semantics=sem, vmem_limit_bytes=vmem)


def _silu(v):
    return v * jax.nn.sigmoid(v)


def _layer_norm(z, g, b):
    mu = jnp.mean(z, -1, keepdims=True)
    zc = z - mu
    var = jnp.mean(zc * zc, -1, keepdims=True)
    return zc * lax.rsqrt(var + LN_EPS) * g + b


def _pack_rows(v):
    half = v.shape[1] // 2
    vb = v.astype(BF16)
    lo = lax.bitcast_convert_type(vb[:, :half].astype(F32), U32) >> 16
    hi = lax.bitcast_convert_type(vb[:, half:].astype(F32), U32)
    w = lax.bitcast_convert_type(hi | lo, I32)
    return [w[:, j * LANES:(j + 1) * LANES] for j in range(half // LANES)]


def _unpack_words(words):
    words = [lax.bitcast_convert_type(w, U32) for w in words]
    lo = [lax.bitcast_convert_type(w << 16, F32) for w in words]
    hi = [lax.bitcast_convert_type(w & jnp.uint32(0xFFFF0000), F32) for w in words]
    return jnp.concatenate(lo + hi, axis=-1)


def _mod_kernel(c_ref, w_ref, b_ref, o_ref):
    cond = _silu(c_ref[...])
    o_ref[0] = jnp.dot(cond, w_ref[...], precision=lax.Precision.HIGHEST,
                       preferred_element_type=F32) + b_ref[0]


def _mod(c, w_ada, b_ada):
    b, d = c.shape
    n6 = w_ada.shape[1] // d
    out = pl.pallas_call(
        _mod_kernel,
        grid=(n6,),
        in_specs=[pl.BlockSpec((b, d), lambda j: (0, 0)),
                  pl.BlockSpec((d, d), lambda j: (0, j)),
                  pl.BlockSpec((1, 1, d), lambda j: (j, 0, 0))],
        out_specs=pl.BlockSpec((1, b, d), lambda j: (j, 0, 0)),
        out_shape=jax.ShapeDtypeStruct((n6, b, d), F32),
        compiler_params=_cparams(("arbitrary",)),
        name="mod",
    )(c, w_ada, b_ada.reshape(n6, 1, d))
    return out.reshape(n6, b, 1, d)


def _mod_spec(which, d):
    return pl.BlockSpec((None, None, 1, d), lambda b, i, which=which: (which, b, 0, 0))


def _inproj_kernel(x_ref, sc_ref, sh_ref, w_ref, cos_ref, sin_ref, o_ref, *, chunk, q_scale):
    h = (x_ref[...] * (1.0 + sc_ref[...]) + sh_ref[...]).astype(BF16)
    q_r, k_r = 3 * NA_WIDTH // chunk, 3 * NA_WIDTH // chunk + 1
    dh = RET_HEAD_DIM
    for j in range(o_ref.shape[1] // chunk):
        acc = jnp.dot(h, w_ref[:, j * chunk:(j + 1) * chunk], preferred_element_type=F32)
        if j == 0:
            acc = acc * q_scale
        if j in (q_r, k_r):
            cos2, sin2 = cos_ref[...], sin_ref[...]
            heads = [acc[:, hh * dh:(hh + 1) * dh] for hh in range(chunk // dh)]
            heads = [t * cos2 + pltpu.roll(t, dh // 2, 1) * sin2 for t in heads]
            acc = jnp.concatenate(heads, axis=1)
            if j == k_r:
                acc = acc * dh ** -0.5
        o_ref[:, j * chunk:(j + 1) * chunk] = acc.astype(o_ref.dtype)


def _inproj(x, mod, w_in_bf, cos2, sin2):
    b, s, d = x.shape
    e = w_in_bf.shape[1]
    tm = min(TM_PROJ, s)
    rope = pl.BlockSpec((tm, RET_HEAD_DIM), lambda bi, i: (i, 0))
    return pl.pallas_call(
        functools.partial(_inproj_kernel, chunk=NA_WIDTH, q_scale=NA_HEAD_DIM ** -0.5 * LOG2_E),
        grid=(b, s // tm),
        in_specs=[pl.BlockSpec((None, tm, d), lambda bi, i: (bi, i, 0)),
                  _mod_spec(1, d), _mod_spec(0, d),
                  pl.BlockSpec((d, e), lambda bi, i: (0, 0)), rope, rope],
        out_specs=pl.BlockSpec((None, tm, e), lambda bi, i: (bi, i, 0)),
        out_shape=jax.ShapeDtypeStruct((b, s, e), BF16),
        compiler_params=_cparams(("parallel", "parallel")),
        name="inproj",
    )(x, mod, mod, w_in_bf, cos2, sin2)


def _natten_kernel(q_ref, k_ref, v_ref, bias_ref, o_ref, *, rows, rows_per_iter):
    kspan = WIN_R * GRID_W

    first = lax.broadcasted_iota(I32, (1, LANES), 1) < NA_HEAD_DIM
    zero = jnp.zeros((), BF16)

    def rows_body(i, carry):
        qrows, krows, scores, probs = {}, {}, {}, {}

        def stage_scores(u):
            r = i * rows_per_iter + u
            rs = jnp.clip(r - WIN_R // 2, 0, rows - WIN_R)
            vi = r - rs
            qrows[u] = pl.ds(pl.multiple_of(r * GRID_W, GRID_W), GRID_W)
            krows[u] = pl.ds(pl.multiple_of(rs * GRID_W, GRID_W), kspan)
            q = q_ref[qrows[u], :]
            qm = jnp.concatenate([jnp.where(first, q, zero), jnp.where(first, zero, q)], axis=0)
            st = lax.dot_general(k_ref[krows[u], :], qm, (((1,), (1,)), ((), ())), preferred_element_type=F32)
            scores[u] = st + bias_ref[vi]

        def stage_softmax(u):
            st = scores.pop(u)
            p = jnp.exp2(st - jnp.max(st, axis=0, keepdims=True))
            probs[u] = (p * (1.0 / jnp.sum(p, axis=0, keepdims=True))).astype(BF16)

        def stage_values(u):
            res = lax.dot_general(probs.pop(u), v_ref[krows[u], :], (((0,), (0,)), ((), ())),
                                  preferred_element_type=F32)
            o_ref[qrows[u], :] = jnp.where(first, res[:GRID_W], res[GRID_W:]).astype(o_ref.dtype)

        for step in range(rows_per_iter + 2):
            if step < rows_per_iter:
                stage_scores(step)
            if 0 <= step - 1 < rows_per_iter:
                stage_softmax(step - 1)
            if 0 <= step - 2 < rows_per_iter:
                stage_values(step - 2)
        return carry

    lax.fori_loop(0, rows // rows_per_iter, rows_body, 0)


def _na_bias_table(rpb):
    w = GRID_W
    cq = jnp.arange(w)
    cs = jnp.clip(cq - WIN_C // 2, 0, w - WIN_C)
    ck = jnp.arange(w)
    col_in = (ck[None, :] >= cs[:, None]) & (ck[None, :] < cs[:, None] + WIN_C)
    dc_idx = jnp.clip(ck[None, :] - cq[:, None] + WIN_C - 1, 0, 2 * WIN_C - 2)
    t = rpb[:, :, dc_idx]
    t = jnp.where(col_in[None, None], t, -jnp.inf)
    vi = jnp.arange(WIN_R)
    kr = jnp.arange(WIN_R)
    dr = kr[None, :] - vi[:, None] + WIN_R - 1
    tb = t[:, dr]
    hp = LANES // NA_HEAD_DIM
    tb = tb.reshape(rpb.shape[0] // hp, hp, WIN_R, WIN_R, w, w)
    tb = tb.transpose(0, 2, 3, 5, 1, 4).reshape(rpb.shape[0] // hp, WIN_R, WIN_R * w, hp * w)
    return tb.astype(F32) * LOG2_E


def _natten(proj, bias_tab):
    b, s, _ = proj.shape
    rows = s // GRID_W
    hp = LANES // NA_HEAD_DIM
    npair = NA_HEADS // hp
    blk = lambda off: pl.BlockSpec((None, s, LANES), lambda bi, p, off=off: (bi, 0, off + p))
    return pl.pallas_call(
        functools.partial(_natten_kernel, rows=rows, rows_per_iter=math.gcd(rows, NA_ROWS_PER_ITER)),
        grid=(b, npair),
        in_specs=[blk(0), blk(npair), blk(2 * npair),
                  pl.BlockSpec((None, WIN_R, WIN_R * GRID_W, hp * GRID_W), lambda bi, p: (p, 0, 0, 0))],
        out_specs=pl.BlockSpec((None, s, LANES), lambda bi, p: (bi, 0, p)),
        out_shape=jax.ShapeDtypeStruct((b, s, NA_WIDTH), BF16),
        compiler_params=_cparams(("parallel", "parallel")),
        name="natten",
    )(proj, proj, proj, bias_tab)


def _retent_kernel(ld_ref, q_ref, k_ref, v_ref, g_ref, gn_ref, o_ref, sb_ref, *, nchunk):
    c = RET_BLOCK
    dh = RET_HEAD_DIM
    h = pl.program_id(1)
    lgf = ld_ref[0, h]
    lgb = ld_ref[1, h]

    ic = lax.broadcasted_iota(I32, (c, 1), 0).astype(F32)
    ir = lax.broadcasted_iota(I32, (1, c), 1).astype(F32)
    diff = ic - ir
    dmat = jnp.where(diff >= 0, jnp.exp(jnp.maximum(diff, 0.0) * lgf),
                     jnp.exp(jnp.maximum(-diff, 0.0) * lgb))
    kdec_f = jnp.exp((c - 1 - ic) * lgf)
    qdec_f = jnp.exp((ic + 1) * lgf)
    kdec_b = jnp.exp(ic * lgb)
    qdec_b = jnp.exp((c - ic) * lgb)
    one = jnp.ones((1, 1), F32)
    cdec_f = jnp.exp(one * (c * lgf))
    cdec_b = jnp.exp(one * (c * lgb))
    tn = (((0,), (0,)), ((), ()))

    def bwd_body(i, sb):
        n = nchunk - 1 - i
        sb_ref[n] = sb
        rows = pl.ds(pl.multiple_of(n * c, c), c)
        kd = (k_ref[rows, :].astype(F32) * kdec_b).astype(BF16)
        kv = lax.dot_general(kd, v_ref[rows, :], tn, preferred_element_type=F32)
        return cdec_b * sb + kv

    lax.fori_loop(0, nchunk, bwd_body, jnp.zeros((dh, dh), F32), unroll=min(RET_UNROLL, nchunk))

    gn = gn_ref[...]

    def fwd_body(n, sf):
        rows = pl.ds(pl.multiple_of(n * c, c), c)
        qb = q_ref[rows, :]
        kb = k_ref[rows, :]
        qn = qb.astype(F32)
        kn = kb.astype(F32)
        vn = v_ref[rows, :]
        sc = lax.dot_general(qb, kb, (((1,), (1,)), ((), ())), preferred_element_type=F32) * dmat
        y = jnp.dot(sc.astype(BF16), vn, preferred_element_type=F32)
        qd = jnp.concatenate([qn * qdec_f, qn * qdec_b], axis=1).astype(BF16)
        st = jnp.concatenate([sf, sb_ref[n]], axis=0).astype(BF16)
        y = y + jnp.dot(qd, st, preferred_element_type=F32)
        mu = jnp.mean(y, -1, keepdims=True)
        yc = y - mu
        var = jnp.mean(yc * yc, -1, keepdims=True)
        yn = yc * lax.rsqrt(var + GN_EPS) * gn
        o_ref[rows, :] = (_silu(g_ref[rows, :].astype(F32)) * yn).astype(o_ref.dtype)
        kv = lax.dot_general((kn * kdec_f).astype(BF16), vn, tn, preferred_element_type=F32)
        return cdec_f * sf + kv

    lax.fori_loop(0, nchunk, fwd_body, jnp.zeros((dh, dh), F32), unroll=min(RET_UNROLL, nchunk))


def _retention(proj, log_decay, gn_g):
    b, s, _ = proj.shape
    dh = RET_HEAD_DIM
    nchunk = s // RET_BLOCK
    base = 3 * NA_WIDTH // dh
    blk = lambda off: pl.BlockSpec((None, s, dh), lambda bi, h, off=off: (bi, 0, base + off + h))
    return pl.pallas_call(
        functools.partial(_retent_kernel, nchunk=nchunk),
        grid=(b, RET_HEADS),
        in_specs=[pl.BlockSpec(memory_space=pltpu.SMEM),
                  blk(0), blk(RET_HEADS), blk(2 * RET_HEADS), blk(3 * RET_HEADS),
                  pl.BlockSpec((1, dh), lambda bi, h: (0, h))],
        out_specs=pl.BlockSpec((None, s, dh), lambda bi, h: (bi, 0, h)),
        out_shape=jax.ShapeDtypeStruct((b, s, RET_WIDTH), BF16),
        scratch_shapes=[pltpu.VMEM((nchunk, dh, dh), F32)],
        compiler_params=_cparams(("parallel", "parallel")),
        name="retent",
    )(log_decay, proj, proj, proj, proj, gn_g.reshape(1, RET_WIDTH))


def _outproj_kernel(yna_ref, yr_ref, x_ref, ga_ref, sf_ref, shf_ref, wo1_ref, wo2_ref, g_ref, b_ref,
                    wrh_ref, wrl_ref, x1_ref, hfp_ref, lg_ref, *, alpha):
    nt = (((1,), (1,)), ((), ()))
    tm = x_ref.shape[0]
    parts = [pl.ds(p * (tm // OUTPROJ_PARTS), tm // OUTPROJ_PARTS) for p in range(OUTPROJ_PARTS)]
    def mix_of(r):
        return (jnp.dot(yna_ref[r, :], wo1_ref[...], preferred_element_type=F32)
                + jnp.dot(yr_ref[r, :], wo2_ref[...], preferred_element_type=F32))

    nxt = mix_of(parts[0])
    for p, r in enumerate(parts):
        mix = nxt
        if p + 1 < len(parts):
            nxt = mix_of(parts[p + 1])
        x1 = _layer_norm(alpha * x_ref[r, :] + ga_ref[...] * mix, g_ref[...], b_ref[...])
        x1_ref[r, :] = x1
        hf = x1 * (1.0 + sf_ref[...]) + shf_ref[...]
        for j, w in enumerate(_pack_rows(hf)):
            hfp_ref[j, r, :] = w
        hb = hf.astype(BF16)
        hl = (hf - hb.astype(F32)).astype(BF16)
        lg = lax.dot_general(wrh_ref[...], hb, nt, preferred_element_type=F32)
        lg = lg + lax.dot_general(wrh_ref[...], hl, nt, preferred_element_type=F32)
        lg = lg + lax.dot_general(wrl_ref[...], hb, nt, preferred_element_type=F32)
        lg_ref[:, r] = lg


def _outproj(y_na, y_r, x, mod, w_out_bf, ln_g, ln_b, wr_hi, wr_lo, alpha):
    b, s, d = x.shape
    tm = min(TM_PROJ, s)
    nt = s // tm
    ne = wr_hi.shape[0]
    const = lambda shape: pl.BlockSpec(shape, lambda bi, i: tuple(0 for _ in shape))
    x1, hfp, lg = pl.pallas_call(
        functools.partial(_outproj_kernel, alpha=alpha),
        grid=(b, nt),
        in_specs=[pl.BlockSpec((None, tm, NA_WIDTH), lambda bi, i: (bi, i, 0)),
                  pl.BlockSpec((None, tm, RET_WIDTH), lambda bi, i: (bi, i, 0)),
                  pl.BlockSpec((None, tm, d), lambda bi, i: (bi, i, 0)),
                  _mod_spec(2, d), _mod_spec(4, d), _mod_spec(3, d),
                  pl.BlockSpec((NA_WIDTH, d), lambda bi, i: (0, 0)),
                  pl.BlockSpec((RET_WIDTH, d), lambda bi, i: (1, 0)),
                  const((1, d)), const((1, d)), const((ne, d)), const((ne, d))],
        out_specs=[pl.BlockSpec((None, tm, d), lambda bi, i: (bi, i, 0)),
                   pl.BlockSpec((PACK_W, tm, LANES), lambda bi, i: (0, bi * nt + i, 0)),
                   pl.BlockSpec((ne, tm), lambda bi, i: (0, bi * nt + i))],
        out_shape=[jax.ShapeDtypeStruct((b, s, d), F32),
                   jax.ShapeDtypeStruct((PACK_W, b * s, LANES), I32),
                   jax.ShapeDtypeStruct((ne, b * s), F32)],
        compiler_params=_cparams(("parallel", "parallel")),
        name="outproj",
    )(y_na, y_r, x, mod, mod, mod, w_out_bf, w_out_bf, ln_g.reshape(1, d), ln_b.reshape(1, d), wr_hi, wr_lo)
    return x1, hfp, lg


def _route_kernel(lg_ref, rb_ref, idx_ref, w_ref, rank_ref, cnt_ref, snap_ref):
    t = lg_ref.shape[1]
    ninf = -jnp.inf
    step = pl.program_id(0)
    steps_per_group = pl.num_programs(0) // MOE_GROUPS

    @pl.when(step == 0)
    def _():
        cnt_ref[...] = jnp.zeros_like(cnt_ref)

    scores = jax.nn.sigmoid(lg_ref[...])
    sel = scores + rb_ref[...]

    io_g = lax.broadcasted_iota(I32, (GROUP_SIZE, t), 0)
    gs_rows = []
    for g in range(N_GROUPS):
        blk = sel[g * GROUP_SIZE:(g + 1) * GROUP_SIZE, :]
        m1 = jnp.max(blk, axis=0, keepdims=True)
        i1 = jnp.min(jnp.where(blk == m1, io_g, GROUP_SIZE), axis=0, keepdims=True)
        m2 = jnp.max(jnp.where(io_g == i1, ninf, blk), axis=0, keepdims=True)
        gs_rows.append(m1 + m2)
    gs = jnp.concatenate(gs_rows, axis=0)

    io8 = lax.broadcasted_iota(I32, (N_GROUPS, t), 0)
    gsel = jnp.zeros((N_GROUPS, t), F32)
    for _ in range(TOPK_GROUPS):
        m = jnp.max(gs, axis=0, keepdims=True)
        gi = jnp.min(jnp.where(gs == m, io8, N_GROUPS), axis=0, keepdims=True)
        hit = io8 == gi
        gsel = jnp.where(hit, 1.0, gsel)
        gs = jnp.where(hit, ninf, gs)

    masked = jnp.concatenate(
        [jnp.where(gsel[g:g + 1, :] > 0.0, sel[g * GROUP_SIZE:(g + 1) * GROUP_SIZE, :], ninf)
         for g in range(N_GROUPS)], axis=0)

    io_e = lax.broadcasted_iota(I32, (N_EXPERTS, t), 0)
    chosen = jnp.zeros((N_EXPERTS, t), F32)
    idx_rows, w_rows = [], []
    for _ in range(TOP_K):
        m = jnp.max(masked, axis=0, keepdims=True)
        ei = jnp.min(jnp.where(masked == m, io_e, N_EXPERTS), axis=0, keepdims=True)
        hit = io_e == ei
        w_rows.append(jnp.sum(jnp.where(hit, scores, 0.0), axis=0, keepdims=True))
        idx_rows.append(ei)
        chosen = jnp.where(hit, 1.0, chosen)
        masked = jnp.where(hit, ninf, masked)
    wk = jnp.concatenate(w_rows, axis=0)
    w_ref[...] = wk / jnp.sum(wk, axis=0, keepdims=True) * ROUTED_SCALE
    idx_ref[...] = jnp.concatenate(idx_rows, axis=0)

    upper = (lax.broadcasted_iota(I32, (t, t), 0) < lax.broadcasted_iota(I32, (t, t), 1))
    prefix = jnp.dot(chosen.astype(BF16), upper.astype(BF16), preferred_element_type=F32)
    rank_full = prefix + cnt_ref[...]
    rank_rows = [jnp.sum(jnp.where(io_e == ei, rank_full, 0.0), axis=0, keepdims=True) for ei in idx_rows]
    rank_ref[...] = jnp.concatenate(rank_rows, axis=0).astype(I32)
    cnt_ref[...] += jnp.sum(chosen, axis=1, keepdims=True)

    for g in range(MOE_GROUPS):
        @pl.when(step == (g + 1) * steps_per_group - 1)
        def _():
            snap_ref[g] = cnt_ref[...]


def _route(logits_t, router_bias):
    ne, n = logits_t.shape
    t = min(T_ROUTE, n)
    kspec = pl.BlockSpec((TOP_K, t), lambda i: (0, i))
    return pl.pallas_call(
        _route_kernel,
        grid=(n // t,),
        in_specs=[pl.BlockSpec((ne, t), lambda i: (0, i)),
                  pl.BlockSpec((ne, 1), lambda i: (0, 0))],
        out_specs=[kspec, kspec, kspec, pl.BlockSpec((ne, 1), lambda i: (0, 0)),
                   pl.BlockSpec((MOE_GROUPS, ne, 1), lambda i: (0, 0, 0))],
        out_shape=[jax.ShapeDtypeStruct((TOP_K, n), I32),
                   jax.ShapeDtypeStruct((TOP_K, n), F32),
                   jax.ShapeDtypeStruct((TOP_K, n), I32),
                   jax.ShapeDtypeStruct((ne, 1), F32),
                   jax.ShapeDtypeStruct((MOE_GROUPS, ne, 1), F32)],
        compiler_params=_cparams(("arbitrary",)),
        name="route",
    )(logits_t, router_bias.reshape(ne, 1))


def _block(ref, g):
    return ref.at[:, pl.ds(pl.multiple_of(g * BLK_E, BLK_E), BLK_E), :]


def _slots_kernel(idx_ref, rank_ref, ps_ref, o_ref, *, n_slots):
    t = idx_ref.shape[1]
    io = lax.broadcasted_iota(I32, (N_EXPERTS, t), 0)
    ps = ps_ref[...]
    for k in range(TOP_K):
        hit = io == idx_ref[k:k + 1, :]
        slot = jnp.sum(jnp.where(hit, ps, 0), axis=0, keepdims=True) + rank_ref[k:k + 1, :]
        for j in range(PACK_W):
            o_ref[j * TOP_K + k:j * TOP_K + k + 1, :] = slot + j * n_slots


def _slots(idx, rank, slot_base, n_slots, tok0, ng):
    t = min(T_SLOT, ng)
    t0 = tok0 // t
    kspec = pl.BlockSpec((TOP_K, t), lambda i: (0, t0 + i))
    return pl.pallas_call(
        functools.partial(_slots_kernel, n_slots=n_slots),
        grid=(ng // t,),
        in_specs=[kspec, kspec, pl.BlockSpec((N_EXPERTS, 1), lambda i: (0, 0))],
        out_specs=pl.BlockSpec((PACK_W * TOP_K, t), lambda i: (0, i)),
        out_shape=jax.ShapeDtypeStruct((PACK_W * TOP_K, ng), I32),
        compiler_params=_cparams(("parallel",)),
        name="slots",
    )(idx, rank, slot_base.reshape(N_EXPERTS, 1))


def _sc_mesh():
    return plsc.VectorSubcoreMesh(core_axis_name="core", subcore_axis_name="subcore")


def _scatter_rows_sc(rows, dest_rows, n_out, n_tok, tok0, ng):
    n_rows = PACK_W * ng
    wpg = ng // SC_WINDOW
    wpp = n_tok // SC_WINDOW
    w0 = tok0 // SC_WINDOW

    @functools.partial(pl.kernel, mesh=_sc_mesh(), scratch_types=[pltpu.SemaphoreType.DMA],
                       out_type=jax.ShapeDtypeStruct((n_out, LANES), I32))
    def scatter_rows(x_hbm, i_hbm, o_hbm, sem):
        def body(x_vmem, i_vmem):
            copies = [pltpu.async_copy(x_vmem, o_hbm.at[i_vmem.at[k]], sem) for k in range(TOP_K)]
            for c in copies:
                c.wait()

        pltpu.emit_pipeline(
            body,
            grid=(n_rows // SC_WINDOW,),
            in_specs=[pl.BlockSpec((SC_WINDOW, LANES), lambda i: ((i // wpg) * wpp + w0 + i % wpg, 0)),
                      pl.BlockSpec((TOP_K, SC_WINDOW), lambda i: (i // wpg, i % wpg))],
            out_specs=[],
            core_axis_name=("core", "subcore"),
            dimension_semantics=(pltpu.PARALLEL,),
        )(x_hbm, i_hbm)

    return scatter_rows(rows, dest_rows)


def _gather_rows_sc(rows, slot_rows, tok0, nc):
    nr = slot_rows.shape[0]
    span = SC_GATHER_WINDOWS * SC_WINDOW
    spr = nc // span
    w0 = tok0 // SC_WINDOW

    @functools.partial(pl.kernel, mesh=_sc_mesh(), scratch_types=[pltpu.SemaphoreType.DMA],
                       out_type=jax.ShapeDtypeStruct((nr * nc, LANES), I32))
    def gather_rows(x_hbm, i_hbm, o_hbm, sem):
        def body(*refs):
            o_vmem = refs[-1]
            copies = [pltpu.async_copy(x_hbm.at[i_vmem.at[0]], o_vmem.at[pl.ds(w * SC_WINDOW, SC_WINDOW)], sem)
                      for w, i_vmem in enumerate(refs[:-1])]
            for c in copies:
                c.wait()

        def idx_spec(w):
            return pl.BlockSpec((1, SC_WINDOW),
                                lambda i: (i // spr, w0 + (i % spr) * SC_GATHER_WINDOWS + w))

        pltpu.emit_pipeline(
            body,
            grid=(nr * spr,),
            in_specs=[idx_spec(w) for w in range(SC_GATHER_WINDOWS)],
            out_specs=[pl.BlockSpec((span, LANES), lambda i: (i, 0))],
            core_axis_name=("core", "subcore"),
            dimension_semantics=(pltpu.PARALLEL,),
        )(*([i_hbm] * SC_GATHER_WINDOWS), o_hbm)

    return gather_rows(rows, slot_rows)


def _experts_kernel(nblk_ref, gstart_ref, cnt_ref, nb_ref, xs_ref, wg_ref, wu_ref, wd_ref, ys_ref,
                    xbuf_ref, ybuf_ref, wgub_ref, wdb_ref, xsem, ysem):
    e = pl.program_id(0)
    total = nb_ref[0]
    nb_max = ys_ref.shape[1] // BLK_E

    def x_copy(g):
        slot = g % X_RING
        return pltpu.make_async_copy(_block(xs_ref, g), xbuf_ref.at[slot], xsem.at[slot])

    def y_copy(g):
        slot = g % Y_RING
        return pltpu.make_async_copy(ybuf_ref.at[slot], _block(ys_ref, g), ysem.at[slot])

    @pl.when(e == 0)
    def _():
        for g in range(X_AHEAD):
            @pl.when(g < total)
            def _():
                x_copy(g).start()

    @pl.when(nblk_ref[e] > 0)
    def _():
        f = wg_ref.shape[1]
        wgub_ref[:, :f] = wg_ref[...].astype(BF16)
        wgub_ref[:, f:] = wu_ref[...].astype(BF16)
        wdb_ref[...] = wd_ref[...].astype(BF16)

    def fetch(b):
        g = gstart_ref[e] + b
        x_copy(g).wait()

        @pl.when(g + X_AHEAD < total)
        def _():
            x_copy(g + X_AHEAD).start()

        xb = _unpack_words([xbuf_ref[g % X_RING, j] for j in range(PACK_W)]).astype(BF16)
        live = lax.broadcasted_iota(I32, (BLK_E, 1), 0) < cnt_ref[e] - b * BLK_E
        return jnp.where(live, xb, jnp.zeros((), BF16))

    def hidden(xb):
        f = wg_ref.shape[1]
        hgu = jnp.dot(xb, wgub_ref[...], preferred_element_type=F32)
        return (_silu(hgu[:, :f]) * hgu[:, f:]).astype(BF16)

    def finish(b, act):
        g = gstart_ref[e] + b
        words = _pack_rows(jnp.dot(act, wdb_ref[...], preferred_element_type=F32))

        @pl.when(g >= Y_RING)
        def _():
            y_copy(g - Y_RING).wait()

        for j, w in enumerate(words):
            ybuf_ref[g % Y_RING, j] = w
        y_copy(g).start()

    def block(b, carry):
        finish(b, hidden(fetch(b)))
        return carry

    lax.fori_loop(0, nblk_ref[e], block, 0)

    @pl.when(e == pl.num_programs(0) - 1)
    def _():
        for back in range(Y_RING):
            @pl.when(total - 1 - back >= 0)
            def _():
                y_copy(total - 1 - back).wait()

        ybuf_ref[0] = jnp.zeros(ybuf_ref.shape[1:], ybuf_ref.dtype)

        def tail_copy(g):
            return pltpu.make_async_copy(ybuf_ref.at[0], _block(ys_ref, g), ysem.at[0])

        def fill(g, carry):
            tail_copy(g).start()
            return carry

        lax.fori_loop(total, nb_max, fill, 0)

        def drain(g, carry):
            tail_copy(g).wait()
            return carry

        lax.fori_loop(total, nb_max, drain, 0)


def _experts(nblk, gstart, cnt, n_blocks, xs, w_gate, w_up, w_down):
    p = xs.shape[1]
    ne, d, f = w_gate.shape
    wspec = lambda shape: pl.BlockSpec((None,) + shape, lambda e, *_: (e, 0, 0))
    grid_spec = pltpu.PrefetchScalarGridSpec(
        num_scalar_prefetch=4,
        grid=(ne,),
        in_specs=[pl.BlockSpec(memory_space=pl.ANY), wspec((d, f)), wspec((d, f)), wspec((f, d))],
        out_specs=pl.BlockSpec(memory_space=pl.ANY),
        scratch_shapes=[pltpu.VMEM((X_RING, PACK_W, BLK_E, LANES), I32),
                        pltpu.VMEM((Y_RING, PACK_W, BLK_E, LANES), I32),
                        pltpu.VMEM((d, 2 * f), BF16), pltpu.VMEM((f, d), BF16),
                        pltpu.SemaphoreType.DMA((X_RING,)), pltpu.SemaphoreType.DMA((Y_RING,))],
    )
    return pl.pallas_call(
        _experts_kernel,
        grid_spec=grid_spec,
        out_shape=jax.ShapeDtypeStruct((PACK_W, p, LANES), I32),
        compiler_params=_cparams(("arbitrary",)),
        name="experts",
    )(nblk, gstart, cnt, n_blocks, xs, w_gate, w_up, w_down)


def _combine_kernel(wt_ref, yk_ref, x1_ref, sf_ref, shf_ref, gf_ref, wsg_ref, wsu_ref, wsd_ref, g_ref, b_ref,
                    *rest, alpha):
    o_ref = rest[-1]
    x1 = x1_ref[...]
    hb = (x1 * (1.0 + sf_ref[...]) + shf_ref[...]).astype(BF16)
    sg = jnp.dot(hb, wsg_ref[...], preferred_element_type=F32)
    su = jnp.dot(hb, wsu_ref[...], preferred_element_type=F32)
    ffn = jnp.dot((_silu(sg) * su).astype(BF16), wsd_ref[...], preferred_element_type=F32)
    wt = wt_ref[...]
    for k in range(TOP_K):
        yk = _unpack_words([yk_ref[j, k] for j in range(PACK_W)])
        ffn = ffn + wt[:, k:k + 1] * yk
    o_ref[...] = _layer_norm(alpha * x1 + gf_ref[...] * ffn, g_ref[...], b_ref[...])


def _combine(w_tok, yk, x1, mod, ws_gate_bf, ws_up_bf, ws_down_bf, ln_g, ln_b, alpha, tok0, earlier):
    n, d = x1.shape
    nc = yk.shape[2]
    s = n // mod.shape[1]
    tc = min(T_COMB, s, nc)
    nt = s // tc
    t0 = tok0 // tc
    fs = ws_gate_bf.shape[1]
    const = lambda shape: pl.BlockSpec(shape, lambda i: tuple(0 for _ in shape))
    mod_spec = lambda which: pl.BlockSpec((None, None, 1, d), lambda i: (which, (t0 + i) // nt, 0, 0))
    in_specs = [pl.BlockSpec((tc, TOP_K), lambda i: (t0 + i, 0)),
                pl.BlockSpec((PACK_W, TOP_K, tc, LANES), lambda i: (0, 0, i, 0)),
                pl.BlockSpec((tc, d), lambda i: (t0 + i, 0)),
                mod_spec(4), mod_spec(3), mod_spec(5),
                const((d, fs)), const((d, fs)), const((fs, d)), const((1, d)), const((1, d))]
    args = [w_tok, yk, x1, mod, mod, mod, ws_gate_bf, ws_up_bf, ws_down_bf, ln_g.reshape(1, d), ln_b.reshape(1, d)]
    aliases = {}
    if earlier is not None:
        in_specs.append(pl.BlockSpec(memory_space=pl.ANY))
        args.append(earlier)
        aliases = {len(args) - 1: 0}
    return pl.pallas_call(
        functools.partial(_combine_kernel, alpha=alpha),
        grid=(nc // tc,),
        in_specs=in_specs,
        out_specs=pl.BlockSpec((tc, d), lambda i: (t0 + i, 0)),
        out_shape=jax.ShapeDtypeStruct((n, d), F32),
        input_output_aliases=aliases,
        compiler_params=_cparams(("parallel",)),
        name="combine",
    )(*args)


def _slot_layout(counts, n_assign):
    cnt = counts[:, 0].astype(I32)
    padded = (cnt + BLK_E - 1) // BLK_E * BLK_E
    pends = jnp.cumsum(padded)
    pstarts = pends - padded
    n_blocks_max = (n_assign + N_EXPERTS * (BLK_E - 1)) // BLK_E
    n_blocks = (pends[-1] // BLK_E).astype(I32).reshape(1)
    return pstarts, padded // BLK_E, pstarts // BLK_E, n_blocks, n_blocks_max * BLK_E


def kernel(x, c, w_ada, b_ada, w_in, w_out, na_rpb, ret_log_decay, ret_gn_g, ln1_g, ln1_b, ln2_g, ln2_b,
           w_router, router_bias, w_gate, w_up, w_down, ws_gate, ws_up, ws_down):
    b, s, d = x.shape
    depth = w_ada.shape[0]
    alpha = (2.0 * depth) ** 0.25
    t = jnp.arange(s, dtype=F32)
    inv_freq = ROPE_BASE ** (-jnp.arange(0, RET_HEAD_DIM, 2, dtype=F32) / RET_HEAD_DIM)
    ang = t[:, None] * inv_freq[None, :]
    cos, sin = jnp.cos(ang), jnp.sin(ang)
    cos2 = jnp.concatenate([cos, cos], axis=-1)
    sin2 = jnp.concatenate([-sin, sin], axis=-1)
    for l in range(depth):
        mod = _mod(c, w_ada[l], b_ada[l])
        proj = _inproj(x, mod, w_in[l].astype(BF16), cos2, sin2)
        y_na = _natten(proj, _na_bias_table(na_rpb[l]))
        y_r = _retention(proj, ret_log_decay[l], ret_gn_g[l])
        wr_t = w_router[l].T
        wr_hi = wr_t.astype(BF16)
        wr_lo = (wr_t - wr_hi.astype(F32)).astype(BF16)
        x1, hfp, logits_t = _outproj(y_na, y_r, x, mod, w_out[l].astype(BF16), ln1_g[l], ln1_b[l],
                                     wr_hi, wr_lo, alpha)
        idx, wts, rank, _, snaps = _route(logits_t, router_bias[l])
        n = b * s
        ng = n // MOE_GROUPS
        hfp_rows = hfp.reshape(PACK_W * n, LANES)
        groups = []
        before = jnp.zeros((N_EXPERTS, 1), F32)
        for g in range(MOE_GROUPS):
            pstarts, nblk, gstart, n_blocks, n_slots = _slot_layout(snaps[g] - before, ng * TOP_K)
            slot_rows = _slots(idx, rank, pstarts - before[:, 0].astype(I32), n_slots, g * ng, ng)
            xs = _scatter_rows_sc(hfp_rows, slot_rows, PACK_W * n_slots, n, g * ng, ng)
            groups.append((slot_rows, xs, nblk, gstart, (snaps[g] - before)[:, 0].astype(I32), n_blocks, n_slots))
            before = snaps[g]
        ys_groups = [_experts(nblk, gstart, cnt, n_blocks, xs.reshape(PACK_W, n_slots, LANES),
                              w_gate[l], w_up[l], w_down[l]).reshape(PACK_W * n_slots, LANES)
                     for (_, xs, nblk, gstart, cnt, n_blocks, n_slots) in groups]
        ws = (ws_gate[l].astype(BF16), ws_up[l].astype(BF16), ws_down[l].astype(BF16))
        w_tok = wts.T
        x1n = x1.reshape(n, d)
        nc = ng // COMBINE_CHUNKS
        out = None
        for g in range(MOE_GROUPS):
            for ci in range(COMBINE_CHUNKS):
                yk = _gather_rows_sc(ys_groups[g], groups[g][0], ci * nc, nc).reshape(PACK_W, TOP_K, nc, LANES)
                out = _combine(w_tok, yk, x1n, mod, *ws, ln2_g[l], ln2_b[l], alpha, g * ng + ci * nc, out)
        x = out.reshape(b, s, d)
    return x
```

```python
import functools
import math

import jax
import jax.numpy as jnp
from jax import lax
from jax.experimental import pallas as pl
from jax.experimental.pallas import tpu as pltpu
from jax.experimental.pallas import tpu_sc as plsc

F32 = jnp.float32
BF16 = jnp.bfloat16
U32 = jnp.uint32
I32 = jnp.int32

GRID_W = 64
WIN_R = 8
WIN_C = 16
NA_HEADS = 8
NA_HEAD_DIM = 64
NA_WIDTH = NA_HEADS * NA_HEAD_DIM
RET_HEADS = 4
RET_HEAD_DIM = 128
RET_WIDTH = RET_HEADS * RET_HEAD_DIM
ROPE_BASE = 10000.0
N_EXPERTS = 256
TOP_K = 8
N_GROUPS = 8
TOPK_GROUPS = 4
GROUP_SIZE = N_EXPERTS // N_GROUPS
ROUTED_SCALE = 2.5
LOG2_E = 1.4426950408889634
LN_EPS = 1e-5
GN_EPS = 1e-6

LANES = 128
VMEM_LIMIT = 56 * 1024 * 1024

TM_PROJ = 512
T_ROUTE = 512
T_SLOT = 1024
SC_WINDOW = 128
SC_GATHER_WINDOWS = 2
BLK_E = 512
OUTPROJ_PARTS = 2
T_COMB = 512
COMBINE_CHUNKS = 8
PACK_W = 4
NA_ROWS_PER_ITER = 32
RET_BLOCK = 256
RET_UNROLL = 8
X_RING = 4
X_AHEAD = 3
Y_RING = 3


def _cparams(sem, vmem=VMEM_LIMIT):
    return pltpu.CompilerParams(dimension_semantics=sem, vmem_limit_bytes=vmem)


def _silu(v):
    return v * jax.nn.sigmoid(v)


def _layer_norm(z, g, b):
    mu = jnp.mean(z, -1, keepdims=True)
    zc = z - mu
    var = jnp.mean(zc * zc, -1, keepdims=True)
    return zc * lax.rsqrt(var + LN_EPS) * g + b


def _pack_rows(v):
    half = v.shape[1] // 2
    vb = v.astype(BF16)
    lo = lax.bitcast_convert_type(vb[:, :half].astype(F32), U32) >> 16
    hi = lax.bitcast_convert_type(vb[:, half:].astype(F32), U32)
    w = lax.bitcast_convert_type(hi | lo, I32)
    return [w[:, j * LANES:(j + 1) * LANES] for j in range(half // LANES)]


def _unpack_words(words):
    words = [lax.bitcast_convert_type(w, U32) for w in words]
    lo = [lax.bitcast_convert_type(w << 16, F32) for w in words]
    hi = [lax.bitcast_convert_type(w & jnp.uint32(0xFFFF0000), F32) for w in words]
    return jnp.concatenate(lo + hi, axis=-1)


def _mod_kernel(c_ref, w_ref, b_ref, o_ref):
    cond = _silu(c_ref[...])
    o_ref[0] = jnp.dot(cond, w_ref[...], precision=lax.Precision.HIGHEST,
                       preferred_element_type=F32) + b_ref[0]


def _mod(c, w_ada, b_ada):
    b, d = c.shape
    n6 = w_ada.shape[1] // d
    out = pl.pallas_call(
        _mod_kernel,
        grid=(n6,),
        in_specs=[pl.BlockSpec((b, d), lambda j: (0, 0)),
                  pl.BlockSpec((d, d), lambda j: (0, j)),
                  pl.BlockSpec((1, 1, d), lambda j: (j, 0, 0))],
        out_specs=pl.BlockSpec((1, b, d), lambda j: (j, 0, 0)),
        out_shape=jax.ShapeDtypeStruct((n6, b, d), F32),
        compiler_params=_cparams(("arbitrary",)),
        name="mod",
    )(c, w_ada, b_ada.reshape(n6, 1, d))
    return out.reshape(n6, b, 1, d)


def _mod_spec(which, d):
    return pl.BlockSpec((None, None, 1, d), lambda b, i, which=which: (which, b, 0, 0))


def _inproj_kernel(x_ref, sc_ref, sh_ref, w_ref, cos_ref, sin_ref, o_ref, *, chunk, q_scale):
    h = (x_ref[...] * (1.0 + sc_ref[...]) + sh_ref[...]).astype(BF16)
    q_r, k_r = 3 * NA_WIDTH // chunk, 3 * NA_WIDTH // chunk + 1
    dh = RET_HEAD_DIM
    for j in range(o_ref.shape[1] // chunk):
        acc = jnp.dot(h, w_ref[:, j * chunk:(j + 1) * chunk], preferred_element_type=F32)
        if j == 0:
            acc = acc * q_scale
        if j in (q_r, k_r):
            cos2, sin2 = cos_ref[...], sin_ref[...]
            heads = [acc[:, hh * dh:(hh + 1) * dh] for hh in range(chunk // dh)]
            heads = [t * cos2 + pltpu.roll(t, dh // 2, 1) * sin2 for t in heads]
            acc = jnp.concatenate(heads, axis=1)
            if j == k_r:
                acc = acc * dh ** -0.5
        o_ref[:, j * chunk:(j + 1) * chunk] = acc.astype(o_ref.dtype)


def _inproj(x, mod, w_in_bf, cos2, sin2):
    b, s, d = x.shape
    e = w_in_bf.shape[1]
    tm = min(TM_PROJ, s)
    rope = pl.BlockSpec((tm, RET_HEAD_DIM), lambda bi, i: (i, 0))
    return pl.pallas_call(
        functools.partial(_inproj_kernel, chunk=NA_WIDTH, q_scale=NA_HEAD_DIM ** -0.5 * LOG2_E),
        grid=(b, s // tm),
        in_specs=[pl.BlockSpec((None, tm, d), lambda bi, i: (bi, i, 0)),
                  _mod_spec(1, d), _mod_spec(0, d),
                  pl.BlockSpec((d, e), lambda bi, i: (0, 0)), rope, rope],
        out_specs=pl.BlockSpec((None, tm, e), lambda bi, i: (bi, i, 0)),
        out_shape=jax.ShapeDtypeStruct((b, s, e), BF16),
        compiler_params=_cparams(("parallel", "parallel")),
        name="inproj",
    )(x, mod, mod, w_in_bf, cos2, sin2)


def _natten_kernel(q_ref, k_ref, v_ref, bias_ref, o_ref, *, rows, rows_per_iter):
    kspan = WIN_R * GRID_W

    first = lax.broadcasted_iota(I32, (1, LANES), 1) < NA_HEAD_DIM
    zero = jnp.zeros((), BF16)

    def rows_body(i, carry):
        qrows, krows, scores, probs = {}, {}, {}, {}

        def stage_scores(u):
            r = i * rows_per_iter + u
            rs = jnp.clip(r - WIN_R // 2, 0, rows - WIN_R)
            vi = r - rs
            qrows[u] = pl.ds(pl.multiple_of(r * GRID_W, GRID_W), GRID_W)
            krows[u] = pl.ds(pl.multiple_of(rs * GRID_W, GRID_W), kspan)
            q = q_ref[qrows[u], :]
            qm = jnp.concatenate([jnp.where(first, q, zero), jnp.where(first, zero, q)], axis=0)
            st = lax.dot_general(k_ref[krows[u], :], qm, (((1,), (1,)), ((), ())), preferred_element_type=F32)
            scores[u] = st + bias_ref[vi]

        def stage_softmax(u):
            st = scores.pop(u)
            p = jnp.exp2(st - jnp.max(st, axis=0, keepdims=True))
            probs[u] = (p * (1.0 / jnp.sum(p, axis=0, keepdims=True))).astype(BF16)

        def stage_values(u):
            res = lax.dot_general(probs.pop(u), v_ref[krows[u], :], (((0,), (0,)), ((), ())),
                                  preferred_element_type=F32)
            o_ref[qrows[u], :] = jnp.where(first, res[:GRID_W], res[GRID_W:]).astype(o_ref.dtype)

        for step in range(rows_per_iter + 2):
            if step < rows_per_iter:
                stage_scores(step)
            if 0 <= step - 1 < rows_per_iter:
                stage_softmax(step - 1)
            if 0 <= step - 2 < rows_per_iter:
                stage_values(step - 2)
        return carry

    lax.fori_loop(0, rows // rows_per_iter, rows_body, 0)


def _na_bias_table(rpb):
    w = GRID_W
    cq = jnp.arange(w)
    cs = jnp.clip(cq - WIN_C // 2, 0, w - WIN_C)
    ck = jnp.arange(w)
    col_in = (ck[None, :] >= cs[:, None]) & (ck[None, :] < cs[:, None] + WIN_C)
    dc_idx = jnp.clip(ck[None, :] - cq[:, None] + WIN_C - 1, 0, 2 * WIN_C - 2)
    t = rpb[:, :, dc_idx]
    t = jnp.where(col_in[None, None], t, -jnp.inf)
    vi = jnp.arange(WIN_R)
    kr = jnp.arange(WIN_R)
    dr = kr[None, :] - vi[:, None] + WIN_R - 1
    tb = t[:, dr]
    hp = LANES // NA_HEAD_DIM
    tb = tb.reshape(rpb.shape[0] // hp, hp, WIN_R, WIN_R, w, w)
    tb = tb.transpose(0, 2, 3, 5, 1, 4).reshape(rpb.shape[0] // hp, WIN_R, WIN_R * w, hp * w)
    return tb.astype(F32) * LOG2_E


def _natten(proj, bias_tab):
    b, s, _ = proj.shape
    rows = s // GRID_W
    hp = LANES // NA_HEAD_DIM
    npair = NA_HEADS // hp
    blk = lambda off: pl.BlockSpec((None, s, LANES), lambda bi, p, off=off: (bi, 0, off + p))
    return pl.pallas_call(
        functools.partial(_natten_kernel, rows=rows, rows_per_iter=math.gcd(rows, NA_ROWS_PER_ITER)),
        grid=(b, npair),
        in_specs=[blk(0), blk(npair), blk(2 * npair),
                  pl.BlockSpec((None, WIN_R, WIN_R * GRID_W, hp * GRID_W), lambda bi, p: (p, 0, 0, 0))],
        out_specs=pl.BlockSpec((None, s, LANES), lambda bi, p: (bi, 0, p)),
        out_shape=jax.ShapeDtypeStruct((b, s, NA_WIDTH), BF16),
        compiler_params=_cparams(("parallel", "parallel")),
        name="natten",
    )(proj, proj, proj, bias_tab)


def _retent_kernel(ld_ref, q_ref, k_ref, v_ref, g_ref, gn_ref, o_ref, sb_ref, *, nchunk):
    c = RET_BLOCK
    dh = RET_HEAD_DIM
    h = pl.program_id(1)
    lgf = ld_ref[0, h]
    lgb = ld_ref[1, h]

    ic = lax.broadcasted_iota(I32, (c, 1), 0).astype(F32)
    ir = lax.broadcasted_iota(I32, (1, c), 1).astype(F32)
    diff = ic - ir
    dmat = jnp.where(diff >= 0, jnp.exp(jnp.maximum(diff, 0.0) * lgf),
                     jnp.exp(jnp.maximum(-diff, 0.0) * lgb))
    kdec_f = jnp.exp((c - 1 - ic) * lgf)
    qdec_f = jnp.exp((ic + 1) * lgf)
    kdec_b = jnp.exp(ic * lgb)
    qdec_b = jnp.exp((c - ic) * lgb)
    one = jnp.ones((1, 1), F32)
    cdec_f = jnp.exp(one * (c * lgf))
    cdec_b = jnp.exp(one * (c * lgb))
    tn = (((0,), (0,)), ((), ()))

    def bwd_body(i, sb):
        n = nchunk - 1 - i
        sb_ref[n] = sb
        rows = pl.ds(pl.multiple_of(n * c, c), c)
        kd = (k_ref[rows, :].astype(F32) * kdec_b).astype(BF16)
        kv = lax.dot_general(kd, v_ref[rows, :], tn, preferred_element_type=F32)
        return cdec_b * sb + kv

    lax.fori_loop(0, nchunk, bwd_body, jnp.zeros((dh, dh), F32), unroll=min(RET_UNROLL, nchunk))

    gn = gn_ref[...]

    def fwd_body(n, sf):
        rows = pl.ds(pl.multiple_of(n * c, c), c)
        qb = q_ref[rows, :]
        kb = k_ref[rows, :]
        qn = qb.astype(F32)
        kn = kb.astype(F32)
        vn = v_ref[rows, :]
        sc = lax.dot_general(qb, kb, (((1,), (1,)), ((), ())), preferred_element_type=F32) * dmat
        y = jnp.dot(sc.astype(BF16), vn, preferred_element_type=F32)
        qd = jnp.concatenate([qn * qdec_f, qn * qdec_b], axis=1).astype(BF16)
        st = jnp.concatenate([sf, sb_ref[n]], axis=0).astype(BF16)
        y = y + jnp.dot(qd, st, preferred_element_type=F32)
        mu = jnp.mean(y, -1, keepdims=True)
        yc = y - mu
        var = jnp.mean(yc * yc, -1, keepdims=True)
        yn = yc * lax.rsqrt(var + GN_EPS) * gn
        o_ref[rows, :] = (_silu(g_ref[rows, :].astype(F32)) * yn).astype(o_ref.dtype)
        kv = lax.dot_general((kn * kdec_f).astype(BF16), vn, tn, preferred_element_type=F32)
        return cdec_f * sf + kv

    lax.fori_loop(0, nchunk, fwd_body, jnp.zeros((dh, dh), F32), unroll=min(RET_UNROLL, nchunk))


def _retention(proj, log_decay, gn_g):
    b, s, _ = proj.shape
    dh = RET_HEAD_DIM
    nchunk = s // RET_BLOCK
    base = 3 * NA_WIDTH // dh
    blk = lambda off: pl.BlockSpec((None, s, dh), lambda bi, h, off=off: (bi, 0, base + off + h))
    return pl.pallas_call(
        functools.partial(_retent_kernel, nchunk=nchunk),
        grid=(b, RET_HEADS),
        in_specs=[pl.BlockSpec(memory_space=pltpu.SMEM),
                  blk(0), blk(RET_HEADS), blk(2 * RET_HEADS), blk(3 * RET_HEADS),
                  pl.BlockSpec((1, dh), lambda bi, h: (0, h))],
        out_specs=pl.BlockSpec((None, s, dh), lambda bi, h: (bi, 0, h)),
        out_shape=jax.ShapeDtypeStruct((b, s, RET_WIDTH), BF16),
        scratch_shapes=[pltpu.VMEM((nchunk, dh, dh), F32)],
        compiler_params=_cparams(("parallel", "parallel")),
        name="retent",
    )(log_decay, proj, proj, proj, proj, gn_g.reshape(1, RET_WIDTH))


def _outproj_kernel(yna_ref, yr_ref, x_ref, ga_ref, sf_ref, shf_ref, wo1_ref, wo2_ref, g_ref, b_ref,
                    wrh_ref, wrl_ref, x1_ref, hfp_ref, lg_ref, *, alpha):
    nt = (((1,), (1,)), ((), ()))
    tm = x_ref.shape[0]
    parts = [pl.ds(p * (tm // OUTPROJ_PARTS), tm // OUTPROJ_PARTS) for p in range(OUTPROJ_PARTS)]
    def mix_of(r):
        return (jnp.dot(yna_ref[r, :], wo1_ref[...], preferred_element_type=F32)
                + jnp.dot(yr_ref[r, :], wo2_ref[...], preferred_element_type=F32))

    nxt = mix_of(parts[0])
    for p, r in enumerate(parts):
        mix = nxt
        if p + 1 < len(parts):
            nxt = mix_of(parts[p + 1])
        x1 = _layer_norm(alpha * x_ref[r, :] + ga_ref[...] * mix, g_ref[...], b_ref[...])
        x1_ref[r, :] = x1
        hf = x1 * (1.0 + sf_ref[...]) + shf_ref[...]
        for j, w in enumerate(_pack_rows(hf)):
            hfp_ref[j, r, :] = w
        hb = hf.astype(BF16)
        hl = (hf - hb.astype(F32)).astype(BF16)
        lg = lax.dot_general(wrh_ref[...], hb, nt, preferred_element_type=F32)
        lg = lg + lax.dot_general(wrh_ref[...], hl, nt, preferred_element_type=F32)
        lg = lg + lax.dot_general(wrl_ref[...], hb, nt, preferred_element_type=F32)
        lg_ref[:, r] = lg


def _outproj(y_na, y_r, x, mod, w_out_bf, ln_g, ln_b, wr_hi, wr_lo, alpha):
    b, s, d = x.shape
    tm = min(TM_PROJ, s)
    nt = s // tm
    ne = wr_hi.shape[0]
    const = lambda shape: pl.BlockSpec(shape, lambda bi, i: tuple(0 for _ in shape))
    x1, hfp, lg = pl.pallas_call(
        functools.partial(_outproj_kernel, alpha=alpha),
        grid=(b, nt),
        in_specs=[pl.BlockSpec((None, tm, NA_WIDTH), lambda bi, i: (bi, i, 0)),
                  pl.BlockSpec((None, tm, RET_WIDTH), lambda bi, i: (bi, i, 0)),
                  pl.BlockSpec((None, tm, d), lambda bi, i: (bi, i, 0)),
                  _mod_spec(2, d), _mod_spec(4, d), _mod_spec(3, d),
                  pl.BlockSpec((NA_WIDTH, d), lambda bi, i: (0, 0)),
                  pl.BlockSpec((RET_WIDTH, d), lambda bi, i: (1, 0)),
                  const((1, d)), const((1, d)), const((ne, d)), const((ne, d))],
        out_specs=[pl.BlockSpec((None, tm, d), lambda bi, i: (bi, i, 0)),
                   pl.BlockSpec((PACK_W, tm, LANES), lambda bi, i: (0, bi * nt + i, 0)),
                   pl.BlockSpec((ne, tm), lambda bi, i: (0, bi * nt + i))],
        out_shape=[jax.ShapeDtypeStruct((b, s, d), F32),
                   jax.ShapeDtypeStruct((PACK_W, b * s, LANES), I32),
                   jax.ShapeDtypeStruct((ne, b * s), F32)],
        compiler_params=_cparams(("parallel", "parallel")),
        name="outproj",
    )(y_na, y_r, x, mod, mod, mod, w_out_bf, w_out_bf, ln_g.reshape(1, d), ln_b.reshape(1, d), wr_hi, wr_lo)
    return x1, hfp, lg


def _route_kernel(lg_ref, rb_ref, idx_ref, w_ref, rank_ref, cnt_ref):
    t = lg_ref.shape[1]
    ninf = -jnp.inf

    @pl.when(pl.program_id(0) == 0)
    def _():
        cnt_ref[...] = jnp.zeros_like(cnt_ref)

    scores = jax.nn.sigmoid(lg_ref[...])
    sel = scores + rb_ref[...]

    io_g = lax.broadcasted_iota(I32, (GROUP_SIZE, t), 0)
    gs_rows = []
    for g in range(N_GROUPS):
        blk = sel[g * GROUP_SIZE:(g + 1) * GROUP_SIZE, :]
        m1 = jnp.max(blk, axis=0, keepdims=True)
        i1 = jnp.min(jnp.where(blk == m1, io_g, GROUP_SIZE), axis=0, keepdims=True)
        m2 = jnp.max(jnp.where(io_g == i1, ninf, blk), axis=0, keepdims=True)
        gs_rows.append(m1 + m2)
    gs = jnp.concatenate(gs_rows, axis=0)

    io8 = lax.broadcasted_iota(I32, (N_GROUPS, t), 0)
    gsel = jnp.zeros((N_GROUPS, t), F32)
    for _ in range(TOPK_GROUPS):
        m = jnp.max(gs, axis=0, keepdims=True)
        gi = jnp.min(jnp.where(gs == m, io8, N_GROUPS), axis=0, keepdims=True)
        hit = io8 == gi
        gsel = jnp.where(hit, 1.0, gsel)
        gs = jnp.where(hit, ninf, gs)

    masked = jnp.concatenate(
        [jnp.where(gsel[g:g + 1, :] > 0.0, sel[g * GROUP_SIZE:(g + 1) * GROUP_SIZE, :], ninf)
         for g in range(N_GROUPS)], axis=0)

    io_e = lax.broadcasted_iota(I32, (N_EXPERTS, t), 0)
    chosen = jnp.zeros((N_EXPERTS, t), F32)
    idx_rows, w_rows = [], []
    for _ in range(TOP_K):
        m = jnp.max(masked, axis=0, keepdims=True)
        ei = jnp.min(jnp.where(masked == m, io_e, N_EXPERTS), axis=0, keepdims=True)
        hit = io_e == ei
        w_rows.append(jnp.sum(jnp.where(hit, scores, 0.0), axis=0, keepdims=True))
        idx_rows.append(ei)
        chosen = jnp.where(hit, 1.0, chosen)
        masked = jnp.where(hit, ninf, masked)
    wk = jnp.concatenate(w_rows, axis=0)
    w_ref[...] = wk / jnp.sum(wk, axis=0, keepdims=True) * ROUTED_SCALE
    idx_ref[...] = jnp.concatenate(idx_rows, axis=0)

    upper = (lax.broadcasted_iota(I32, (t, t), 0) < lax.broadcasted_iota(I32, (t, t), 1))
    prefix = jnp.dot(chosen.astype(BF16), upper.astype(BF16), preferred_element_type=F32)
    rank_full = prefix + cnt_ref[...]
    rank_rows = [jnp.sum(jnp.where(io_e == ei, rank_full, 0.0), axis=0, keepdims=True) for ei in idx_rows]
    rank_ref[...] = jnp.concatenate(rank_rows, axis=0).astype(I32)
    cnt_ref[...] += jnp.sum(chosen, axis=1, keepdims=True)


def _route(logits_t, router_bias):
    ne, n = logits_t.shape
    t = min(T_ROUTE, n)
    kspec = pl.BlockSpec((TOP_K, t), lambda i: (0, i))
    return pl.pallas_call(
        _route_kernel,
        grid=(n // t,),
        in_specs=[pl.BlockSpec((ne, t), lambda i: (0, i)),
                  pl.BlockSpec((ne, 1), lambda i: (0, 0))],
        out_specs=[kspec, kspec, kspec, pl.BlockSpec((ne, 1), lambda i: (0, 0))],
        out_shape=[jax.ShapeDtypeStruct((TOP_K, n), I32),
                   jax.ShapeDtypeStruct((TOP_K, n), F32),
                   jax.ShapeDtypeStruct((TOP_K, n), I32),
                   jax.ShapeDtypeStruct((ne, 1), F32)],
        compiler_params=_cparams(("arbitrary",)),
        name="route",
    )(logits_t, router_bias.reshape(ne, 1))


def _block(ref, g):
    return ref.at[:, pl.ds(pl.multiple_of(g * BLK_E, BLK_E), BLK_E), :]


def _slots_kernel(idx_ref, rank_ref, ps_ref, o_ref, *, n_slots):
    t = idx_ref.shape[1]
    io = lax.broadcasted_iota(I32, (N_EXPERTS, t), 0)
    ps = ps_ref[...]
    for k in range(TOP_K):
        hit = io == idx_ref[k:k + 1, :]
        slot = jnp.sum(jnp.where(hit, ps, 0), axis=0, keepdims=True) + rank_ref[k:k + 1, :]
        for j in range(PACK_W):
            o_ref[j * TOP_K + k:j * TOP_K + k + 1, :] = slot + j * n_slots


def _slots(idx, rank, pstarts, n_slots):
    n = idx.shape[1]
    t = min(T_SLOT, n)
    kspec = pl.BlockSpec((TOP_K, t), lambda i: (0, i))
    return pl.pallas_call(
        functools.partial(_slots_kernel, n_slots=n_slots),
        grid=(n // t,),
        in_specs=[kspec, kspec, pl.BlockSpec((N_EXPERTS, 1), lambda i: (0, 0))],
        out_specs=pl.BlockSpec((PACK_W * TOP_K, t), lambda i: (0, i)),
        out_shape=jax.ShapeDtypeStruct((PACK_W * TOP_K, n), I32),
        compiler_params=_cparams(("parallel",)),
        name="slots",
    )(idx, rank, pstarts.reshape(N_EXPERTS, 1))


def _sc_mesh():
    return plsc.VectorSubcoreMesh(core_axis_name="core", subcore_axis_name="subcore")


def _scatter_rows_sc(rows, dest_rows, n_out):
    n_rows = rows.shape[0]
    wpp = n_rows // PACK_W // SC_WINDOW

    @functools.partial(pl.kernel, mesh=_sc_mesh(), scratch_types=[pltpu.SemaphoreType.DMA],
                       out_type=jax.ShapeDtypeStruct((n_out, LANES), I32))
    def scatter_rows(x_hbm, i_hbm, o_hbm, sem):
        def body(x_vmem, i_vmem):
            copies = [pltpu.async_copy(x_vmem, o_hbm.at[i_vmem.at[k]], sem) for k in range(TOP_K)]
            for c in copies:
                c.wait()

        pltpu.emit_pipeline(
            body,
            grid=(n_rows // SC_WINDOW,),
            in_specs=[pl.BlockSpec((SC_WINDOW, LANES), lambda i: (i, 0)),
                      pl.BlockSpec((TOP_K, SC_WINDOW), lambda i: (i // wpp, i % wpp))],
            out_specs=[],
            core_axis_name=("core", "subcore"),
            dimension_semantics=(pltpu.PARALLEL,),
        )(x_hbm, i_hbm)

    return scatter_rows(rows, dest_rows)


def _gather_rows_sc(rows, slot_rows, tok0, nc):
    nr = slot_rows.shape[0]
    span = SC_GATHER_WINDOWS * SC_WINDOW
    spr = nc // span
    w0 = tok0 // SC_WINDOW

    @functools.partial(pl.kernel, mesh=_sc_mesh(), scratch_types=[pltpu.SemaphoreType.DMA],
                       out_type=jax.ShapeDtypeStruct((nr * nc, LANES), I32))
    def gather_rows(x_hbm, i_hbm, o_hbm, sem):
        def body(*refs):
            o_vmem = refs[-1]
            copies = [pltpu.async_copy(x_hbm.at[i_vmem.at[0]], o_vmem.at[pl.ds(w * SC_WINDOW, SC_WINDOW)], sem)
                      for w, i_vmem in enumerate(refs[:-1])]
            for c in copies:
                c.wait()

        def idx_spec(w):
            return pl.BlockSpec((1, SC_WINDOW),
                                lambda i: (i // spr, w0 + (i % spr) * SC_GATHER_WINDOWS + w))

        pltpu.emit_pipeline(
            body,
            grid=(nr * spr,),
            in_specs=[idx_spec(w) for w in range(SC_GATHER_WINDOWS)],
            out_specs=[pl.BlockSpec((span, LANES), lambda i: (i, 0))],
            core_axis_name=("core", "subcore"),
            dimension_semantics=(pltpu.PARALLEL,),
        )(*([i_hbm] * SC_GATHER_WINDOWS), o_hbm)

    return gather_rows(rows, slot_rows)


def _experts_kernel(nblk_ref, gstart_ref, cnt_ref, nb_ref, xs_ref, wg_ref, wu_ref, wd_ref, ys_ref,
                    xbuf_ref, ybuf_ref, wgub_ref, wdb_ref, xsem, ysem):
    e = pl.program_id(0)
    total = nb_ref[0]
    nb_max = ys_ref.shape[1] // BLK_E

    def x_copy(g):
        slot = g % X_RING
        return pltpu.make_async_copy(_block(xs_ref, g), xbuf_ref.at[slot], xsem.at[slot])

    def y_copy(g):
        slot = g % Y_RING
        return pltpu.make_async_copy(ybuf_ref.at[slot], _block(ys_ref, g), ysem.at[slot])

    @pl.when(e == 0)
    def _():
        for g in range(X_AHEAD):
            @pl.when(g < total)
            def _():
                x_copy(g).start()

    @pl.when(nblk_ref[e] > 0)
    def _():
        f = wg_ref.shape[1]
        wgub_ref[:, :f] = wg_ref[...].astype(BF16)
        wgub_ref[:, f:] = wu_ref[...].astype(BF16)
        wdb_ref[...] = wd_ref[...].astype(BF16)

    def fetch(b):
        g = gstart_ref[e] + b
        x_copy(g).wait()

        @pl.when(g + X_AHEAD < total)
        def _():
            x_copy(g + X_AHEAD).start()

        xb = _unpack_words([xbuf_ref[g % X_RING, j] for j in range(PACK_W)]).astype(BF16)
        live = lax.broadcasted_iota(I32, (BLK_E, 1), 0) < cnt_ref[e] - b * BLK_E
        return jnp.where(live, xb, jnp.zeros((), BF16))

    def hidden(xb):
        f = wg_ref.shape[1]
        hgu = jnp.dot(xb, wgub_ref[...], preferred_element_type=F32)
        return (_silu(hgu[:, :f]) * hgu[:, f:]).astype(BF16)

    def finish(b, act):
        g = gstart_ref[e] + b
        words = _pack_rows(jnp.dot(act, wdb_ref[...], preferred_element_type=F32))

        @pl.when(g >= Y_RING)
        def _():
            y_copy(g - Y_RING).wait()

        for j, w in enumerate(words):
            ybuf_ref[g % Y_RING, j] = w
        y_copy(g).start()

    def block(b, carry):
        finish(b, hidden(fetch(b)))
        return carry

    lax.fori_loop(0, nblk_ref[e], block, 0)

    @pl.when(e == pl.num_programs(0) - 1)
    def _():
        for back in range(Y_RING):
            @pl.when(total - 1 - back >= 0)
            def _():
                y_copy(total - 1 - back).wait()

        ybuf_ref[0] = jnp.zeros(ybuf_ref.shape[1:], ybuf_ref.dtype)

        def tail_copy(g):
            return pltpu.make_async_copy(ybuf_ref.at[0], _block(ys_ref, g), ysem.at[0])

        def fill(g, carry):
            tail_copy(g).start()
            return carry

        lax.fori_loop(total, nb_max, fill, 0)

        def drain(g, carry):
            tail_copy(g).wait()
            return carry

        lax.fori_loop(total, nb_max, drain, 0)


def _experts(nblk, gstart, cnt, n_blocks, xs, w_gate, w_up, w_down):
    p = xs.shape[1]
    ne, d, f = w_gate.shape
    wspec = lambda shape: pl.BlockSpec((None,) + shape, lambda e, *_: (e, 0, 0))
    grid_spec = pltpu.PrefetchScalarGridSpec(
        num_scalar_prefetch=4,
        grid=(ne,),
        in_specs=[pl.BlockSpec(memory_space=pl.ANY), wspec((d, f)), wspec((d, f)), wspec((f, d))],
        out_specs=pl.BlockSpec(memory_space=pl.ANY),
        scratch_shapes=[pltpu.VMEM((X_RING, PACK_W, BLK_E, LANES), I32),
                        pltpu.VMEM((Y_RING, PACK_W, BLK_E, LANES), I32),
                        pltpu.VMEM((d, 2 * f), BF16), pltpu.VMEM((f, d), BF16),
                        pltpu.SemaphoreType.DMA((X_RING,)), pltpu.SemaphoreType.DMA((Y_RING,))],
    )
    return pl.pallas_call(
        _experts_kernel,
        grid_spec=grid_spec,
        out_shape=jax.ShapeDtypeStruct((PACK_W, p, LANES), I32),
        compiler_params=_cparams(("arbitrary",)),
        name="experts",
    )(nblk, gstart, cnt, n_blocks, xs, w_gate, w_up, w_down)


def _combine_kernel(wt_ref, yk_ref, x1_ref, sf_ref, shf_ref, gf_ref, wsg_ref, wsu_ref, wsd_ref, g_ref, b_ref,
                    *rest, alpha):
    o_ref = rest[-1]
    x1 = x1_ref[...]
    hb = (x1 * (1.0 + sf_ref[...]) + shf_ref[...]).astype(BF16)
    sg = jnp.dot(hb, wsg_ref[...], preferred_element_type=F32)
    su = jnp.dot(hb, wsu_ref[...], preferred_element_type=F32)
    ffn = jnp.dot((_silu(sg) * su).astype(BF16), wsd_ref[...], preferred_element_type=F32)
    wt = wt_ref[...]
    for k in range(TOP_K):
        yk = _unpack_words([yk_ref[j, k] for j in range(PACK_W)])
        ffn = ffn + wt[:, k:k + 1] * yk
    o_ref[...] = _layer_norm(alpha * x1 + gf_ref[...] * ffn, g_ref[...], b_ref[...])


def _combine(w_tok, yk, x1, mod, ws_gate_bf, ws_up_bf, ws_down_bf, ln_g, ln_b, alpha, tok0, earlier):
    n, d = x1.shape
    nc = yk.shape[2]
    s = n // mod.shape[1]
    tc = min(T_COMB, s, nc)
    nt = s // tc
    t0 = tok0 // tc
    fs = ws_gate_bf.shape[1]
    const = lambda shape: pl.BlockSpec(shape, lambda i: tuple(0 for _ in shape))
    mod_spec = lambda which: pl.BlockSpec((None, None, 1, d), lambda i: (which, (t0 + i) // nt, 0, 0))
    in_specs = [pl.BlockSpec((tc, TOP_K), lambda i: (t0 + i, 0)),
                pl.BlockSpec((PACK_W, TOP_K, tc, LANES), lambda i: (0, 0, i, 0)),
                pl.BlockSpec((tc, d), lambda i: (t0 + i, 0)),
                mod_spec(4), mod_spec(3), mod_spec(5),
                const((d, fs)), const((d, fs)), const((fs, d)), const((1, d)), const((1, d))]
    args = [w_tok, yk, x1, mod, mod, mod, ws_gate_bf, ws_up_bf, ws_down_bf, ln_g.reshape(1, d), ln_b.reshape(1, d)]
    aliases = {}
    if earlier is not None:
        in_specs.append(pl.BlockSpec(memory_space=pl.ANY))
        args.append(earlier)
        aliases = {len(args) - 1: 0}
    return pl.pallas_call(
        functools.partial(_combine_kernel, alpha=alpha),
        grid=(nc // tc,),
        in_specs=in_specs,
        out_specs=pl.BlockSpec((tc, d), lambda i: (t0 + i, 0)),
        out_shape=jax.ShapeDtypeStruct((n, d), F32),
        input_output_aliases=aliases,
        compiler_params=_cparams(("parallel",)),
        name="combine",
    )(*args)


def _slot_layout(counts, n_assign):
    cnt = counts[:, 0].astype(I32)
    padded = (cnt + BLK_E - 1) // BLK_E * BLK_E
    pends = jnp.cumsum(padded)
    pstarts = pends - padded
    n_blocks_max = (n_assign + N_EXPERTS * (BLK_E - 1)) // BLK_E
    n_blocks = (pends[-1] // BLK_E).astype(I32).reshape(1)
    return pstarts, padded // BLK_E, pstarts // BLK_E, n_blocks, n_blocks_max * BLK_E


def kernel(x, c, w_ada, b_ada, w_in, w_out, na_rpb, ret_log_decay, ret_gn_g, ln1_g, ln1_b, ln2_g, ln2_b,
           w_router, router_bias, w_gate, w_up, w_down, ws_gate, ws_up, ws_down):
    b, s, d = x.shape
    depth = w_ada.shape[0]
    alpha = (2.0 * depth) ** 0.25
    t = jnp.arange(s, dtype=F32)
    inv_freq = ROPE_BASE ** (-jnp.arange(0, RET_HEAD_DIM, 2, dtype=F32) / RET_HEAD_DIM)
    ang = t[:, None] * inv_freq[None, :]
    cos, sin = jnp.cos(ang), jnp.sin(ang)
    cos2 = jnp.concatenate([cos, cos], axis=-1)
    sin2 = jnp.concatenate([-sin, sin], axis=-1)
    for l in range(depth):
        mod = _mod(c, w_ada[l], b_ada[l])
        proj = _inproj(x, mod, w_in[l].astype(BF16), cos2, sin2)
        y_na = _natten(proj, _na_bias_table(na_rpb[l]))
        y_r = _retention(proj, ret_log_decay[l], ret_gn_g[l])
        wr_t = w_router[l].T
        wr_hi = wr_t.astype(BF16)
        wr_lo = (wr_t - wr_hi.astype(F32)).astype(BF16)
        x1, hfp, logits_t = _outproj(y_na, y_r, x, mod, w_out[l].astype(BF16), ln1_g[l], ln1_b[l],
                                     wr_hi, wr_lo, alpha)
        idx, wts, rank, counts = _route(logits_t, router_bias[l])
        n = b * s
        pstarts, nblk, gstart, n_blocks, n_slots = _slot_layout(counts, n * TOP_K)
        slot_rows = _slots(idx, rank, pstarts, n_slots)
        xs = _scatter_rows_sc(hfp.reshape(PACK_W * n, LANES), slot_rows, PACK_W * n_slots)
        ys = _experts(nblk, gstart, counts[:, 0].astype(I32), n_blocks, xs.reshape(PACK_W, n_slots, LANES),
                      w_gate[l], w_up[l], w_down[l]).reshape(PACK_W * n_slots, LANES)
        ws = (ws_gate[l].astype(BF16), ws_up[l].astype(BF16), ws_down[l].astype(BF16))
        w_tok = wts.T
        x1n = x1.reshape(n, d)
        nc = n // COMBINE_CHUNKS
        out = None
        for ci in range(COMBINE_CHUNKS):
            yk = _gather_rows_sc(ys, slot_rows, ci * nc, nc).reshape(PACK_W, TOP_K, nc, LANES)
            out = _combine(w_tok, yk, x1n, mod, *ws, ln2_g[l], ln2_b[l], alpha, ci * nc, out)
        x = out.reshape(b, s, d)
    return x
```

```python
import functools
import math

import jax
import jax.numpy as jnp
from jax import lax
from jax.experimental import pallas as pl
from jax.experimental.pallas import tpu as pltpu
from jax.experimental.pallas import tpu_sc as plsc

F32 = jnp.float32
BF16 = jnp.bfloat16
U32 = jnp.uint32
I32 = jnp.int32

GRID_W = 64
WIN_R = 8
WIN_C = 16
NA_HEADS = 8
NA_HEAD_DIM = 64
NA_WIDTH = NA_HEADS * NA_HEAD_DIM
RET_HEADS = 4
RET_HEAD_DIM = 128
RET_WIDTH = RET_HEADS * RET_HEAD_DIM
ROPE_BASE = 10000.0
N_EXPERTS = 256
TOP_K = 8
N_GROUPS = 8
TOPK_GROUPS = 4
GROUP_SIZE = N_EXPERTS // N_GROUPS
ROUTED_SCALE = 2.5
LOG2_E = 1.4426950408889634
LN_EPS = 1e-5
GN_EPS = 1e-6

LANES = 128
VMEM_LIMIT = 56 * 1024 * 1024

TM_INPROJ = 1024
TM_PROJ = 512
T_ROUTE = 512
T_SLOT = 1024
SC_WINDOW = 128
SC_GATHER_WINDOWS = 2
BLK_E = 512
OUTPROJ_PARTS = 2
T_COMB = 512
COMBINE_CHUNKS = 16
PACK_W = 4
NA_ROWS_PER_ITER = 32
RET_BLOCK = 256
RET_UNROLL = 8
X_RING = 4
X_AHEAD = 3
Y_RING = 3


def _cparams(sem, vmem=VMEM_LIMIT):
    return pltpu.CompilerParams(dimension_semantics=sem, vmem_limit_bytes=vmem)


def _silu(v):
    return v * jax.nn.sigmoid(v)


def _layer_norm(z, g, b):
    mu = jnp.mean(z, -1, keepdims=True)
    zc = z - mu
    var = jnp.mean(zc * zc, -1, keepdims=True)
    return zc * lax.rsqrt(var + LN_EPS) * g + b


def _pack_rows(v):
    half = v.shape[1] // 2
    vb = v.astype(BF16)
    lo = lax.bitcast_convert_type(vb[:, :half].astype(F32), U32) >> 16
    hi = lax.bitcast_convert_type(vb[:, half:].astype(F32), U32)
    w = lax.bitcast_convert_type(hi | lo, I32)
    return [w[:, j * LANES:(j + 1) * LANES] for j in range(half // LANES)]


def _unpack_words(words):
    words = [lax.bitcast_convert_type(w, U32) for w in words]
    lo = [lax.bitcast_convert_type(w << 16, F32) for w in words]
    hi = [lax.bitcast_convert_type(w & jnp.uint32(0xFFFF0000), F32) for w in words]
    return jnp.concatenate(lo + hi, axis=-1)


def _mod_kernel(c_ref, w_ref, b_ref, o_ref):
    cond = _silu(c_ref[...])
    o_ref[0] = jnp.dot(cond, w_ref[...], precision=lax.Precision.HIGHEST,
                       preferred_element_type=F32) + b_ref[0]


def _mod(c, w_ada, b_ada):
    b, d = c.shape
    n6 = w_ada.shape[1] // d
    out = pl.pallas_call(
        _mod_kernel,
        grid=(n6,),
        in_specs=[pl.BlockSpec((b, d), lambda j: (0, 0)),
                  pl.BlockSpec((d, d), lambda j: (0, j)),
                  pl.BlockSpec((1, 1, d), lambda j: (j, 0, 0))],
        out_specs=pl.BlockSpec((1, b, d), lambda j: (j, 0, 0)),
        out_shape=jax.ShapeDtypeStruct((n6, b, d), F32),
        compiler_params=_cparams(("arbitrary",)),
        name="mod",
    )(c, w_ada, b_ada.reshape(n6, 1, d))
    return out.reshape(n6, b, 1, d)


def _mod_spec(which, d):
    return pl.BlockSpec((None, None, 1, d), lambda b, i, which=which: (which, b, 0, 0))


def _inproj_kernel(x_ref, sc_ref, sh_ref, w_ref, cos_ref, sin_ref, o_ref, *, chunk, q_scale):
    h = (x_ref[...] * (1.0 + sc_ref[...]) + sh_ref[...]).astype(BF16)
    q_r, k_r = 3 * NA_WIDTH // chunk, 3 * NA_WIDTH // chunk + 1
    dh = RET_HEAD_DIM
    for j in range(o_ref.shape[1] // chunk):
        acc = jnp.dot(h, w_ref[:, j * chunk:(j + 1) * chunk], preferred_element_type=F32)
        if j == 0:
            acc = acc * q_scale
        if j in (q_r, k_r):
            cos2, sin2 = cos_ref[...], sin_ref[...]
            heads = [acc[:, hh * dh:(hh + 1) * dh] for hh in range(chunk // dh)]
            heads = [t * cos2 + pltpu.roll(t, dh // 2, 1) * sin2 for t in heads]
            acc = jnp.concatenate(heads, axis=1)
            if j == k_r:
                acc = acc * dh ** -0.5
        o_ref[:, j * chunk:(j + 1) * chunk] = acc.astype(o_ref.dtype)


def _inproj(x, mod, w_in_bf, cos2, sin2):
    b, s, d = x.shape
    e = w_in_bf.shape[1]
    tm = min(TM_INPROJ, s)
    rope = pl.BlockSpec((tm, RET_HEAD_DIM), lambda bi, i: (i, 0))
    return pl.pallas_call(
        functools.partial(_inproj_kernel, chunk=NA_WIDTH, q_scale=NA_HEAD_DIM ** -0.5 * LOG2_E),
        grid=(b, s // tm),
        in_specs=[pl.BlockSpec((None, tm, d), lambda bi, i: (bi, i, 0)),
                  _mod_spec(1, d), _mod_spec(0, d),
                  pl.BlockSpec((d, e), lambda bi, i: (0, 0)), rope, rope],
        out_specs=pl.BlockSpec((None, tm, e), lambda bi, i: (bi, i, 0)),
        out_shape=jax.ShapeDtypeStruct((b, s, e), BF16),
        compiler_params=_cparams(("parallel", "parallel")),
        name="inproj",
    )(x, mod, mod, w_in_bf, cos2, sin2)


def _natten_kernel(q_ref, k_ref, v_ref, bias_ref, o_ref, *, rows, rows_per_iter):
    kspan = WIN_R * GRID_W

    first = lax.broadcasted_iota(I32, (1, LANES), 1) < NA_HEAD_DIM
    zero = jnp.zeros((), BF16)

    def rows_body(i, carry):
        qrows, krows, scores, probs = {}, {}, {}, {}

        def stage_scores(u):
            r = i * rows_per_iter + u
            rs = jnp.clip(r - WIN_R // 2, 0, rows - WIN_R)
            vi = r - rs
            qrows[u] = pl.ds(pl.multiple_of(r * GRID_W, GRID_W), GRID_W)
            krows[u] = pl.ds(pl.multiple_of(rs * GRID_W, GRID_W), kspan)
            q = q_ref[qrows[u], :]
            qm = jnp.concatenate([jnp.where(first, q, zero), jnp.where(first, zero, q)], axis=0)
            st = lax.dot_general(k_ref[krows[u], :], qm, (((1,), (1,)), ((), ())), preferred_element_type=F32)
            scores[u] = st + bias_ref[vi]

        def stage_softmax(u):
            st = scores.pop(u)
            p = jnp.exp2(st - jnp.max(st, axis=0, keepdims=True))
            probs[u] = (p * (1.0 / jnp.sum(p, axis=0, keepdims=True))).astype(BF16)

        def stage_values(u):
            res = lax.dot_general(probs.pop(u), v_ref[krows[u], :], (((0,), (0,)), ((), ())),
                                  preferred_element_type=F32)
            o_ref[qrows[u], :] = jnp.where(first, res[:GRID_W], res[GRID_W:]).astype(o_ref.dtype)

        for step in range(rows_per_iter + 2):
            if step < rows_per_iter:
                stage_scores(step)
            if 0 <= step - 1 < rows_per_iter:
                stage_softmax(step - 1)
            if 0 <= step - 2 < rows_per_iter:
                stage_values(step - 2)
        return carry

    lax.fori_loop(0, rows // rows_per_iter, rows_body, 0)


def _na_bias_table(rpb):
    w = GRID_W
    cq = jnp.arange(w)
    cs = jnp.clip(cq - WIN_C // 2, 0, w - WIN_C)
    ck = jnp.arange(w)
    col_in = (ck[None, :] >= cs[:, None]) & (ck[None, :] < cs[:, None] + WIN_C)
    dc_idx = jnp.clip(ck[None, :] - cq[:, None] + WIN_C - 1, 0, 2 * WIN_C - 2)
    t = rpb[:, :, dc_idx]
    t = jnp.where(col_in[None, None], t, -jnp.inf)
    vi = jnp.arange(WIN_R)
    kr = jnp.arange(WIN_R)
    dr = kr[None, :] - vi[:, None] + WIN_R - 1
    tb = t[:, dr]
    hp = LANES // NA_HEAD_DIM
    tb = tb.reshape(rpb.shape[0] // hp, hp, WIN_R, WIN_R, w, w)
    tb = tb.transpose(0, 2, 3, 5, 1, 4).reshape(rpb.shape[0] // hp, WIN_R, WIN_R * w, hp * w)
    return tb.astype(F32) * LOG2_E


def _natten(proj, bias_tab):
    b, s, _ = proj.shape
    rows = s // GRID_W
    hp = LANES // NA_HEAD_DIM
    npair = NA_HEADS // hp
    blk = lambda off: pl.BlockSpec((None, s, LANES), lambda bi, p, off=off: (bi, 0, off + p))
    return pl.pallas_call(
        functools.partial(_natten_kernel, rows=rows, rows_per_iter=math.gcd(rows, NA_ROWS_PER_ITER)),
        grid=(b, npair),
        in_specs=[blk(0), blk(npair), blk(2 * npair),
                  pl.BlockSpec((None, WIN_R, WIN_R * GRID_W, hp * GRID_W), lambda bi, p: (p, 0, 0, 0))],
        out_specs=pl.BlockSpec((None, s, LANES), lambda bi, p: (bi, 0, p)),
        out_shape=jax.ShapeDtypeStruct((b, s, NA_WIDTH), BF16),
        compiler_params=_cparams(("parallel", "parallel")),
        name="natten",
    )(proj, proj, proj, bias_tab)


def _retent_kernel(ld_ref, q_ref, k_ref, v_ref, g_ref, gn_ref, o_ref, sb_ref, *, nchunk):
    c = RET_BLOCK
    dh = RET_HEAD_DIM
    h = pl.program_id(1)
    lgf = ld_ref[0, h]
    lgb = ld_ref[1, h]

    ic = lax.broadcasted_iota(I32, (c, 1), 0).astype(F32)
    ir = lax.broadcasted_iota(I32, (1, c), 1).astype(F32)
    diff = ic - ir
    dmat = jnp.where(diff >= 0, jnp.exp(jnp.maximum(diff, 0.0) * lgf),
                     jnp.exp(jnp.maximum(-diff, 0.0) * lgb))
    kdec_f = jnp.exp((c - 1 - ic) * lgf)
    qdec_f = jnp.exp((ic + 1) * lgf)
    kdec_b = jnp.exp(ic * lgb)
    qdec_b = jnp.exp((c - ic) * lgb)
    one = jnp.ones((1, 1), F32)
    cdec_f = jnp.exp(one * (c * lgf))
    cdec_b = jnp.exp(one * (c * lgb))
    tn = (((0,), (0,)), ((), ()))

    def bwd_body(i, sb):
        n = nchunk - 1 - i
        sb_ref[n] = sb
        rows = pl.ds(pl.multiple_of(n * c, c), c)
        kd = (k_ref[rows, :].astype(F32) * kdec_b).astype(BF16)
        kv = lax.dot_general(kd, v_ref[rows, :], tn, preferred_element_type=F32)
        return cdec_b * sb + kv

    lax.fori_loop(0, nchunk, bwd_body, jnp.zeros((dh, dh), F32), unroll=min(RET_UNROLL, nchunk))

    gn = gn_ref[...]

    def fwd_body(n, sf):
        rows = pl.ds(pl.multiple_of(n * c, c), c)
        qb = q_ref[rows, :]
        kb = k_ref[rows, :]
        qn = qb.astype(F32)
        kn = kb.astype(F32)
        vn = v_ref[rows, :]
        sc = lax.dot_general(qb, kb, (((1,), (1,)), ((), ())), preferred_element_type=F32) * dmat
        y = jnp.dot(sc.astype(BF16), vn, preferred_element_type=F32)
        qd = jnp.concatenate([qn * qdec_f, qn * qdec_b], axis=1).astype(BF16)
        st = jnp.concatenate([sf, sb_ref[n]], axis=0).astype(BF16)
        y = y + jnp.dot(qd, st, preferred_element_type=F32)
        mu = jnp.mean(y, -1, keepdims=True)
        yc = y - mu
        var = jnp.mean(yc * yc, -1, keepdims=True)
        yn = yc * lax.rsqrt(var + GN_EPS) * gn
        o_ref[rows, :] = (_silu(g_ref[rows, :].astype(F32)) * yn).astype(o_ref.dtype)
        kv = lax.dot_general((kn * kdec_f).astype(BF16), vn, tn, preferred_element_type=F32)
        return cdec_f * sf + kv

    lax.fori_loop(0, nchunk, fwd_body, jnp.zeros((dh, dh), F32), unroll=min(RET_UNROLL, nchunk))


def _retention(proj, log_decay, gn_g):
    b, s, _ = proj.shape
    dh = RET_HEAD_DIM
    nchunk = s // RET_BLOCK
    base = 3 * NA_WIDTH // dh
    blk = lambda off: pl.BlockSpec((None, s, dh), lambda bi, h, off=off: (bi, 0, base + off + h))
    return pl.pallas_call(
        functools.partial(_retent_kernel, nchunk=nchunk),
        grid=(b, RET_HEADS),
        in_specs=[pl.BlockSpec(memory_space=pltpu.SMEM),
                  blk(0), blk(RET_HEADS), blk(2 * RET_HEADS), blk(3 * RET_HEADS),
                  pl.BlockSpec((1, dh), lambda bi, h: (0, h))],
        out_specs=pl.BlockSpec((None, s, dh), lambda bi, h: (bi, 0, h)),
        out_shape=jax.ShapeDtypeStruct((b, s, RET_WIDTH), BF16),
        scratch_shapes=[pltpu.VMEM((nchunk, dh, dh), F32)],
        compiler_params=_cparams(("parallel", "parallel")),
        name="retent",
    )(log_decay, proj, proj, proj, proj, gn_g.reshape(1, RET_WIDTH))


def _outproj_kernel(yna_ref, yr_ref, x_ref, ga_ref, sf_ref, shf_ref, wo1_ref, wo2_ref, g_ref, b_ref,
                    wrh_ref, wrl_ref, x1_ref, hfp_ref, lg_ref, *, alpha):
    nt = (((1,), (1,)), ((), ()))
    tm = x_ref.shape[0]
    parts = [pl.ds(p * (tm // OUTPROJ_PARTS), tm // OUTPROJ_PARTS) for p in range(OUTPROJ_PARTS)]
    def mix_of(r):
        return (jnp.dot(yna_ref[r, :], wo1_ref[...], preferred_element_type=F32)
                + jnp.dot(yr_ref[r, :], wo2_ref[...], preferred_element_type=F32))

    nxt = mix_of(parts[0])
    for p, r in enumerate(parts):
        mix = nxt
        if p + 1 < len(parts):
            nxt = mix_of(parts[p + 1])
        x1 = _layer_norm(alpha * x_ref[r, :] + ga_ref[...] * mix, g_ref[...], b_ref[...])
        x1_ref[r, :] = x1
        hf = x1 * (1.0 + sf_ref[...]) + shf_ref[...]
        for j, w in enumerate(_pack_rows(hf)):
            hfp_ref[j, r, :] = w
        hb = hf.astype(BF16)
        hl = (hf - hb.astype(F32)).astype(BF16)
        lg = lax.dot_general(wrh_ref[...], hb, nt, preferred_element_type=F32)
        lg = lg + lax.dot_general(wrh_ref[...], hl, nt, preferred_element_type=F32)
        lg = lg + lax.dot_general(wrl_ref[...], hb, nt, preferred_element_type=F32)
        lg_ref[:, r] = lg


def _outproj(y_na, y_r, x, mod, w_out_bf, ln_g, ln_b, wr_hi, wr_lo, alpha):
    b, s, d = x.shape
    tm = min(TM_PROJ, s)
    nt = s // tm
    ne = wr_hi.shape[0]
    const = lambda shape: pl.BlockSpec(shape, lambda bi, i: tuple(0 for _ in shape))
    x1, hfp, lg = pl.pallas_call(
        functools.partial(_outproj_kernel, alpha=alpha),
        grid=(b, nt),
        in_specs=[pl.BlockSpec((None, tm, NA_WIDTH), lambda bi, i: (bi, i, 0)),
                  pl.BlockSpec((None, tm, RET_WIDTH), lambda bi, i: (bi, i, 0)),
                  pl.BlockSpec((None, tm, d), lambda bi, i: (bi, i, 0)),
                  _mod_spec(2, d), _mod_spec(4, d), _mod_spec(3, d),
                  pl.BlockSpec((NA_WIDTH, d), lambda bi, i: (0, 0)),
                  pl.BlockSpec((RET_WIDTH, d), lambda bi, i: (1, 0)),
                  const((1, d)), const((1, d)), const((ne, d)), const((ne, d))],
        out_specs=[pl.BlockSpec((None, tm, d), lambda bi, i: (bi, i, 0)),
                   pl.BlockSpec((PACK_W, tm, LANES), lambda bi, i: (0, bi * nt + i, 0)),
                   pl.BlockSpec((ne, tm), lambda bi, i: (0, bi * nt + i))],
        out_shape=[jax.ShapeDtypeStruct((b, s, d), F32),
                   jax.ShapeDtypeStruct((PACK_W, b * s, LANES), I32),
                   jax.ShapeDtypeStruct((ne, b * s), F32)],
        compiler_params=_cparams(("parallel", "parallel")),
        name="outproj",
    )(y_na, y_r, x, mod, mod, mod, w_out_bf, w_out_bf, ln_g.reshape(1, d), ln_b.reshape(1, d), wr_hi, wr_lo)
    return x1, hfp, lg


def _route_kernel(lg_ref, rb_ref, idx_ref, w_ref, rank_ref, cnt_ref):
    t = lg_ref.shape[1]
    ninf = -jnp.inf

    @pl.when(pl.program_id(0) == 0)
    def _():
        cnt_ref[...] = jnp.zeros_like(cnt_ref)

    scores = jax.nn.sigmoid(lg_ref[...])
    sel = scores + rb_ref[...]

    io_g = lax.broadcasted_iota(I32, (GROUP_SIZE, t), 0)
    gs_rows = []
    for g in range(N_GROUPS):
        blk = sel[g * GROUP_SIZE:(g + 1) * GROUP_SIZE, :]
        m1 = jnp.max(blk, axis=0, keepdims=True)
        i1 = jnp.min(jnp.where(blk == m1, io_g, GROUP_SIZE), axis=0, keepdims=True)
        m2 = jnp.max(jnp.where(io_g == i1, ninf, blk), axis=0, keepdims=True)
        gs_rows.append(m1 + m2)
    gs = jnp.concatenate(gs_rows, axis=0)

    io8 = lax.broadcasted_iota(I32, (N_GROUPS, t), 0)
    gsel = jnp.zeros((N_GROUPS, t), F32)
    for _ in range(TOPK_GROUPS):
        m = jnp.max(gs, axis=0, keepdims=True)
        gi = jnp.min(jnp.where(gs == m, io8, N_GROUPS), axis=0, keepdims=True)
        hit = io8 == gi
        gsel = jnp.where(hit, 1.0, gsel)
        gs = jnp.where(hit, ninf, gs)

    masked = jnp.concatenate(
        [jnp.where(gsel[g:g + 1, :] > 0.0, sel[g * GROUP_SIZE:(g + 1) * GROUP_SIZE, :], ninf)
         for g in range(N_GROUPS)], axis=0)

    io_e = lax.broadcasted_iota(I32, (N_EXPERTS, t), 0)
    chosen = jnp.zeros((N_EXPERTS, t), F32)
    idx_rows, w_rows = [], []
    for _ in range(TOP_K):
        m = jnp.max(masked, axis=0, keepdims=True)
        ei = jnp.min(jnp.where(masked == m, io_e, N_EXPERTS), axis=0, keepdims=True)
        hit = io_e == ei
        w_rows.append(jnp.sum(jnp.where(hit, scores, 0.0), axis=0, keepdims=True))
        idx_rows.append(ei)
        chosen = jnp.where(hit, 1.0, chosen)
        masked = jnp.where(hit, ninf, masked)
    wk = jnp.concatenate(w_rows, axis=0)
    w_ref[...] = wk / jnp.sum(wk, axis=0, keepdims=True) * ROUTED_SCALE
    idx_ref[...] = jnp.concatenate(idx_rows, axis=0)

    upper = (lax.broadcasted_iota(I32, (t, t), 0) < lax.broadcasted_iota(I32, (t, t), 1))
    prefix = jnp.dot(chosen.astype(BF16), upper.astype(BF16), preferred_element_type=F32)
    rank_full = prefix + cnt_ref[...]
    rank_rows = [jnp.sum(jnp.where(io_e == ei, rank_full, 0.0), axis=0, keepdims=True) for ei in idx_rows]
    rank_ref[...] = jnp.concatenate(rank_rows, axis=0).astype(I32)
    cnt_ref[...] += jnp.sum(chosen, axis=1, keepdims=True)


def _route(logits_t, router_bias):
    ne, n = logits_t.shape
    t = min(T_ROUTE, n)
    kspec = pl.BlockSpec((TOP_K, t), lambda i: (0, i))
    return pl.pallas_call(
        _route_kernel,
        grid=(n // t,),
        in_specs=[pl.BlockSpec((ne, t), lambda i: (0, i)),
                  pl.BlockSpec((ne, 1), lambda i: (0, 0))],
        out_specs=[kspec, kspec, kspec, pl.BlockSpec((ne, 1), lambda i: (0, 0))],
        out_shape=[jax.ShapeDtypeStruct((TOP_K, n), I32),
                   jax.ShapeDtypeStruct((TOP_K, n), F32),
                   jax.ShapeDtypeStruct((TOP_K, n), I32),
                   jax.ShapeDtypeStruct((ne, 1), F32)],
        compiler_params=_cparams(("arbitrary",)),
        name="route",
    )(logits_t, router_bias.reshape(ne, 1))


def _block(ref, g):
    return ref.at[:, pl.ds(pl.multiple_of(g * BLK_E, BLK_E), BLK_E), :]


def _slots_kernel(idx_ref, rank_ref, ps_ref, o_ref, *, n_slots):
    t = idx_ref.shape[1]
    io = lax.broadcasted_iota(I32, (N_EXPERTS, t), 0)
    ps = ps_ref[...]
    for k in range(TOP_K):
        hit = io == idx_ref[k:k + 1, :]
        slot = jnp.sum(jnp.where(hit, ps, 0), axis=0, keepdims=True) + rank_ref[k:k + 1, :]
        for j in range(PACK_W):
            o_ref[j * TOP_K + k:j * TOP_K + k + 1, :] = slot + j * n_slots


def _slots(idx, rank, pstarts, n_slots):
    n = idx.shape[1]
    t = min(T_SLOT, n)
    kspec = pl.BlockSpec((TOP_K, t), lambda i: (0, i))
    return pl.pallas_call(
        functools.partial(_slots_kernel, n_slots=n_slots),
        grid=(n // t,),
        in_specs=[kspec, kspec, pl.BlockSpec((N_EXPERTS, 1), lambda i: (0, 0))],
        out_specs=pl.BlockSpec((PACK_W * TOP_K, t), lambda i: (0, i)),
        out_shape=jax.ShapeDtypeStruct((PACK_W * TOP_K, n), I32),
        compiler_params=_cparams(("parallel",)),
        name="slots",
    )(idx, rank, pstarts.reshape(N_EXPERTS, 1))


def _sc_mesh():
    return plsc.VectorSubcoreMesh(core_axis_name="core", subcore_axis_name="subcore")


def _scatter_rows_sc(rows, dest_rows, n_out):
    n_rows = rows.shape[0]
    wpp = n_rows // PACK_W // SC_WINDOW

    @functools.partial(pl.kernel, mesh=_sc_mesh(), scratch_types=[pltpu.SemaphoreType.DMA],
                       out_type=jax.ShapeDtypeStruct((n_out, LANES), I32))
    def scatter_rows(x_hbm, i_hbm, o_hbm, sem):
        def body(x_vmem, i_vmem):
            copies = [pltpu.async_copy(x_vmem, o_hbm.at[i_vmem.at[k]], sem) for k in range(TOP_K)]
            for c in copies:
                c.wait()

        pltpu.emit_pipeline(
            body,
            grid=(n_rows // SC_WINDOW,),
            in_specs=[pl.BlockSpec((SC_WINDOW, LANES), lambda i: (i, 0)),
                      pl.BlockSpec((TOP_K, SC_WINDOW), lambda i: (i // wpp, i % wpp))],
            out_specs=[],
            core_axis_name=("core", "subcore"),
            dimension_semantics=(pltpu.PARALLEL,),
        )(x_hbm, i_hbm)

    return scatter_rows(rows, dest_rows)


def _gather_rows_sc(rows, slot_rows, tok0, nc):
    nr = slot_rows.shape[0]
    span = SC_GATHER_WINDOWS * SC_WINDOW
    spr = nc // span
    w0 = tok0 // SC_WINDOW

    @functools.partial(pl.kernel, mesh=_sc_mesh(), scratch_types=[pltpu.SemaphoreType.DMA],
                       out_type=jax.ShapeDtypeStruct((nr * nc, LANES), I32))
    def gather_rows(x_hbm, i_hbm, o_hbm, sem):
        def body(*refs):
            o_vmem = refs[-1]
            copies = [pltpu.async_copy(x_hbm.at[i_vmem.at[0]], o_vmem.at[pl.ds(w * SC_WINDOW, SC_WINDOW)], sem)
                      for w, i_vmem in enumerate(refs[:-1])]
            for c in copies:
                c.wait()

        def idx_spec(w):
            return pl.BlockSpec((1, SC_WINDOW),
                                lambda i: (i // spr, w0 + (i % spr) * SC_GATHER_WINDOWS + w))

        pltpu.emit_pipeline(
            body,
            grid=(nr * spr,),
            in_specs=[idx_spec(w) for w in range(SC_GATHER_WINDOWS)],
            out_specs=[pl.BlockSpec((span, LANES), lambda i: (i, 0))],
            core_axis_name=("core", "subcore"),
            dimension_semantics=(pltpu.PARALLEL,),
        )(*([i_hbm] * SC_GATHER_WINDOWS), o_hbm)

    return gather_rows(rows, slot_rows)


def _experts_kernel(nblk_ref, gstart_ref, cnt_ref, nb_ref, xs_ref, wg_ref, wu_ref, wd_ref, ys_ref,
                    xbuf_ref, ybuf_ref, wgub_ref, wdb_ref, xsem, ysem):
    e = pl.program_id(0)
    total = nb_ref[0]
    nb_max = ys_ref.shape[1] // BLK_E

    def x_copy(g):
        slot = g % X_RING
        return pltpu.make_async_copy(_block(xs_ref, g), xbuf_ref.at[slot], xsem.at[slot])

    def y_copy(g):
        slot = g % Y_RING
        return pltpu.make_async_copy(ybuf_ref.at[slot], _block(ys_ref, g), ysem.at[slot])

    @pl.when(e == 0)
    def _():
        for g in range(X_AHEAD):
            @pl.when(g < total)
            def _():
                x_copy(g).start()

    @pl.when(nblk_ref[e] > 0)
    def _():
        f = wg_ref.shape[1]
        wgub_ref[:, :f] = wg_ref[...].astype(BF16)
        wgub_ref[:, f:] = wu_ref[...].astype(BF16)
        wdb_ref[...] = wd_ref[...].astype(BF16)

    def fetch(b):
        g = gstart_ref[e] + b
        x_copy(g).wait()

        @pl.when(g + X_AHEAD < total)
        def _():
            x_copy(g + X_AHEAD).start()

        xb = _unpack_words([xbuf_ref[g % X_RING, j] for j in range(PACK_W)]).astype(BF16)
        live = lax.broadcasted_iota(I32, (BLK_E, 1), 0) < cnt_ref[e] - b * BLK_E
        return jnp.where(live, xb, jnp.zeros((), BF16))

    def hidden(xb):
        f = wg_ref.shape[1]
        hgu = jnp.dot(xb, wgub_ref[...], preferred_element_type=F32)
        return (_silu(hgu[:, :f]) * hgu[:, f:]).astype(BF16)

    def finish(b, act):
        g = gstart_ref[e] + b
        words = _pack_rows(jnp.dot(act, wdb_ref[...], preferred_element_type=F32))

        @pl.when(g >= Y_RING)
        def _():
            y_copy(g - Y_RING).wait()

        for j, w in enumerate(words):
            ybuf_ref[g % Y_RING, j] = w
        y_copy(g).start()

    def block(b, carry):
        finish(b, hidden(fetch(b)))
        return carry

    lax.fori_loop(0, nblk_ref[e], block, 0)

    @pl.when(e == pl.num_programs(0) - 1)
    def _():
        for back in range(Y_RING):
            @pl.when(total - 1 - back >= 0)
            def _():
                y_copy(total - 1 - back).wait()

        ybuf_ref[0] = jnp.zeros(ybuf_ref.shape[1:], ybuf_ref.dtype)

        def tail_copy(g):
            return pltpu.make_async_copy(ybuf_ref.at[0], _block(ys_ref, g), ysem.at[0])

        def fill(g, carry):
            tail_copy(g).start()
            return carry

        lax.fori_loop(total, nb_max, fill, 0)

        def drain(g, carry):
            tail_copy(g).wait()
            return carry

        lax.fori_loop(total, nb_max, drain, 0)


def _experts(nblk, gstart, cnt, n_blocks, xs, w_gate, w_up, w_down):
    p = xs.shape[1]
    ne, d, f = w_gate.shape
    wspec = lambda shape: pl.BlockSpec((None,) + shape, lambda e, *_: (e, 0, 0))
    grid_spec = pltpu.PrefetchScalarGridSpec(
        num_scalar_prefetch=4,
        grid=(ne,),
        in_specs=[pl.BlockSpec(memory_space=pl.ANY), wspec((d, f)), wspec((d, f)), wspec((f, d))],
        out_specs=pl.BlockSpec(memory_space=pl.ANY),
        scratch_shapes=[pltpu.VMEM((X_RING, PACK_W, BLK_E, LANES), I32),
                        pltpu.VMEM((Y_RING, PACK_W, BLK_E, LANES), I32),
                        pltpu.VMEM((d, 2 * f), BF16), pltpu.VMEM((f, d), BF16),
                        pltpu.SemaphoreType.DMA((X_RING,)), pltpu.SemaphoreType.DMA((Y_RING,))],
    )
    return pl.pallas_call(
        _experts_kernel,
        grid_spec=grid_spec,
        out_shape=jax.ShapeDtypeStruct((PACK_W, p, LANES), I32),
        compiler_params=_cparams(("arbitrary",)),
        name="experts",
    )(nblk, gstart, cnt, n_blocks, xs, w_gate, w_up, w_down)


def _combine_kernel(wt_ref, yk_ref, x1_ref, sf_ref, shf_ref, gf_ref, wsg_ref, wsu_ref, wsd_ref, g_ref, b_ref,
                    *rest, alpha):
    o_ref = rest[-1]
    x1 = x1_ref[...]
    hb = (x1 * (1.0 + sf_ref[...]) + shf_ref[...]).astype(BF16)
    sg = jnp.dot(hb, wsg_ref[...], preferred_element_type=F32)
    su = jnp.dot(hb, wsu_ref[...], preferred_element_type=F32)
    ffn = jnp.dot((_silu(sg) * su).astype(BF16), wsd_ref[...], preferred_element_type=F32)
    wt = wt_ref[...]
    for k in range(TOP_K):
        yk = _unpack_words([yk_ref[j, k] for j in range(PACK_W)])
        ffn = ffn + wt[:, k:k + 1] * yk
    o_ref[...] = _layer_norm(alpha * x1 + gf_ref[...] * ffn, g_ref[...], b_ref[...])


def _combine(w_tok, yk, x1, mod, ws_gate_bf, ws_up_bf, ws_down_bf, ln_g, ln_b, alpha, tok0, earlier):
    n, d = x1.shape
    nc = yk.shape[2]
    s = n // mod.shape[1]
    tc = min(T_COMB, s, nc)
    nt = s // tc
    t0 = tok0 // tc
    fs = ws_gate_bf.shape[1]
    const = lambda shape: pl.BlockSpec(shape, lambda i: tuple(0 for _ in shape))
    mod_spec = lambda which: pl.BlockSpec((None, None, 1, d), lambda i: (which, (t0 + i) // nt, 0, 0))
    in_specs = [pl.BlockSpec((tc, TOP_K), lambda i: (t0 + i, 0)),
                pl.BlockSpec((PACK_W, TOP_K, tc, LANES), lambda i: (0, 0, i, 0)),
                pl.BlockSpec((tc, d), lambda i: (t0 + i, 0)),
                mod_spec(4), mod_spec(3), mod_spec(5),
                const((d, fs)), const((d, fs)), const((fs, d)), const((1, d)), const((1, d))]
    args = [w_tok, yk, x1, mod, mod, mod, ws_gate_bf, ws_up_bf, ws_down_bf, ln_g.reshape(1, d), ln_b.reshape(1, d)]
    aliases = {}
    if earlier is not None:
        in_specs.append(pl.BlockSpec(memory_space=pl.ANY))
        args.append(earlier)
        aliases = {len(args) - 1: 0}
    return pl.pallas_call(
        functools.partial(_combine_kernel, alpha=alpha),
        grid=(nc // tc,),
        in_specs=in_specs,
        out_specs=pl.BlockSpec((tc, d), lambda i: (t0 + i, 0)),
        out_shape=jax.ShapeDtypeStruct((n, d), F32),
        input_output_aliases=aliases,
        compiler_params=_cparams(("parallel",)),
        name="combine",
    )(*args)


def _slot_layout(counts, n_assign):
    cnt = counts[:, 0].astype(I32)
    padded = (cnt + BLK_E - 1) // BLK_E * BLK_E
    pends = jnp.cumsum(padded)
    pstarts = pends - padded
    n_blocks_max = (n_assign + N_EXPERTS * (BLK_E - 1)) // BLK_E
    n_blocks = (pends[-1] // BLK_E).astype(I32).reshape(1)
    return pstarts, padded // BLK_E, pstarts // BLK_E, n_blocks, n_blocks_max * BLK_E


def kernel(x, c, w_ada, b_ada, w_in, w_out, na_rpb, ret_log_decay, ret_gn_g, ln1_g, ln1_b, ln2_g, ln2_b,
           w_router, router_bias, w_gate, w_up, w_down, ws_gate, ws_up, ws_down):
    b, s, d = x.shape
    depth = w_ada.shape[0]
    alpha = (2.0 * depth) ** 0.25
    t = jnp.arange(s, dtype=F32)
    inv_freq = ROPE_BASE ** (-jnp.arange(0, RET_HEAD_DIM, 2, dtype=F32) / RET_HEAD_DIM)
    ang = t[:, None] * inv_freq[None, :]
    cos, sin = jnp.cos(ang), jnp.sin(ang)
    cos2 = jnp.concatenate([cos, cos], axis=-1)
    sin2 = jnp.concatenate([-sin, sin], axis=-1)
    for l in range(depth):
        mod = _mod(c, w_ada[l], b_ada[l])
        proj = _inproj(x, mod, w_in[l].astype(BF16), cos2, sin2)
        y_na = _natten(proj, _na_bias_table(na_rpb[l]))
        y_r = _retention(proj, ret_log_decay[l], ret_gn_g[l])
        wr_t = w_router[l].T
        wr_hi = wr_t.astype(BF16)
        wr_lo = (wr_t - wr_hi.astype(F32)).astype(BF16)
        x1, hfp, logits_t = _outproj(y_na, y_r, x, mod, w_out[l].astype(BF16), ln1_g[l], ln1_b[l],
                                     wr_hi, wr_lo, alpha)
        idx, wts, rank, counts = _route(logits_t, router_bias[l])
        n = b * s
        pstarts, nblk, gstart, n_blocks, n_slots = _slot_layout(counts, n * TOP_K)
        slot_rows = _slots(idx, rank, pstarts, n_slots)
        xs = _scatter_rows_sc(hfp.reshape(PACK_W * n, LANES), slot_rows, PACK_W * n_slots)
        ys = _experts(nblk, gstart, counts[:, 0].astype(I32), n_blocks, xs.reshape(PACK_W, n_slots, LANES),
                      w_gate[l], w_up[l], w_down[l]).reshape(PACK_W * n_slots, LANES)
        ws = (ws_gate[l].astype(BF16), ws_up[l].astype(BF16), ws_down[l].astype(BF16))
        w_tok = wts.T
        x1n = x1.reshape(n, d)
        nc = n // COMBINE_CHUNKS
        out = None
        for ci in range(COMBINE_CHUNKS):
            yk = _gather_rows_sc(ys, slot_rows, ci * nc, nc).reshape(PACK_W, TOP_K, nc, LANES)
            out = _combine(w_tok, yk, x1n, mod, *ws, ln2_g[l], ln2_b[l], alpha, ci * nc, out)
        x = out.reshape(b, s, d)
    return x
```

```python
import functools
import math

import jax
import jax.numpy as jnp
from jax import lax
from jax.experimental import pallas as pl
from jax.experimental.pallas import tpu as pltpu
from jax.experimental.pallas import tpu_sc as plsc

F32 = jnp.float32
BF16 = jnp.bfloat16
U32 = jnp.uint32
I32 = jnp.int32

GRID_W = 64
WIN_R = 8
WIN_C = 16
NA_HEADS = 8
NA_HEAD_DIM = 64
NA_WIDTH = NA_HEADS * NA_HEAD_DIM
RET_HEADS = 4
RET_HEAD_DIM = 128
RET_WIDTH = RET_HEADS * RET_HEAD_DIM
ROPE_BASE = 10000.0
N_EXPERTS = 256
TOP_K = 8
N_GROUPS = 8
TOPK_GROUPS = 4
GROUP_SIZE = N_EXPERTS // N_GROUPS
ROUTED_SCALE = 2.5
LOG2_E = 1.4426950408889634
LN_EPS = 1e-5
GN_EPS = 1e-6

LANES = 128
VMEM_BYTES = 64 * 1024 * 1024
VMEM_LIMIT = VMEM_BYTES * 7 // 8

TM_INPROJ = 1024
TM_PROJ = 512
T_ROUTE = 512
T_SLOT = 1024
SC_WINDOW = 128
SC_GATHER_WINDOWS = 2
BLK_E = 512
OUTPROJ_PARTS = 2
T_COMB = 512
COMBINE_CHUNKS = 8
PACK_W = 4
NA_ROWS_PER_ITER = 32
RET_BLOCK = 256
RET_UNROLL = 8
X_RING = 4
X_AHEAD = 3
Y_RING = 3


def _cparams(sem, vmem=VMEM_LIMIT):
    return pltpu.CompilerParams(dimension_semantics=sem, vmem_limit_bytes=vmem)


def _silu(v):
    return v * jax.nn.sigmoid(v)


def _layer_norm(z, g, b):
    mu = jnp.mean(z, -1, keepdims=True)
    zc = z - mu
    var = jnp.mean(zc * zc, -1, keepdims=True)
    return zc * lax.rsqrt(var + LN_EPS) * g + b


def _pack_rows(v):
    half = v.shape[1] // 2
    vb = v.astype(BF16)
    lo = lax.bitcast_convert_type(vb[:, :half].astype(F32), U32) >> 16
    hi = lax.bitcast_convert_type(vb[:, half:].astype(F32), U32)
    w = lax.bitcast_convert_type(hi | lo, I32)
    return [w[:, j * LANES:(j + 1) * LANES] for j in range(half // LANES)]


def _unpack_words(words):
    words = [lax.bitcast_convert_type(w, U32) for w in words]
    lo = [lax.bitcast_convert_type(w << 16, F32) for w in words]
    hi = [lax.bitcast_convert_type(w & jnp.uint32(0xFFFF0000), F32) for w in words]
    return jnp.concatenate(lo + hi, axis=-1)


def _mod_kernel(c_ref, w_ref, b_ref, o_ref):
    cond = _silu(c_ref[...])
    o_ref[0] = jnp.dot(cond, w_ref[...], precision=lax.Precision.HIGHEST,
                       preferred_element_type=F32) + b_ref[0]


def _mod(c, w_ada, b_ada):
    b, d = c.shape
    n6 = w_ada.shape[1] // d
    out = pl.pallas_call(
        _mod_kernel,
        grid=(n6,),
        in_specs=[pl.BlockSpec((b, d), lambda j: (0, 0)),
                  pl.BlockSpec((d, d), lambda j: (0, j)),
                  pl.BlockSpec((1, 1, d), lambda j: (j, 0, 0))],
        out_specs=pl.BlockSpec((1, b, d), lambda j: (j, 0, 0)),
        out_shape=jax.ShapeDtypeStruct((n6, b, d), F32),
        compiler_params=_cparams(("arbitrary",)),
        name="mod",
    )(c, w_ada, b_ada.reshape(n6, 1, d))
    return out.reshape(n6, b, 1, d)


def _mod_spec(which, d):
    return pl.BlockSpec((None, None, 1, d), lambda b, i, which=which: (which, b, 0, 0))


def _inproj_kernel(x_ref, sc_ref, sh_ref, w_ref, cos_ref, sin_ref, o_ref, *, chunk, q_scale):
    h = (x_ref[...] * (1.0 + sc_ref[...]) + sh_ref[...]).astype(BF16)
    q_r, k_r = 3 * NA_WIDTH // chunk, 3 * NA_WIDTH // chunk + 1
    dh = RET_HEAD_DIM
    for j in range(o_ref.shape[1] // chunk):
        acc = jnp.dot(h, w_ref[:, j * chunk:(j + 1) * chunk], preferred_element_type=F32)
        if j == 0:
            acc = acc * q_scale
        if j in (q_r, k_r):
            cos2, sin2 = cos_ref[...], sin_ref[...]
            heads = [acc[:, hh * dh:(hh + 1) * dh] for hh in range(chunk // dh)]
            heads = [t * cos2 + pltpu.roll(t, dh // 2, 1) * sin2 for t in heads]
            acc = jnp.concatenate(heads, axis=1)
            if j == k_r:
                acc = acc * dh ** -0.5
        o_ref[:, j * chunk:(j + 1) * chunk] = acc.astype(o_ref.dtype)


def _inproj(x, mod, w_in_bf, cos2, sin2):
    b, s, d = x.shape
    e = w_in_bf.shape[1]
    tm = min(TM_INPROJ, s)
    rope = pl.BlockSpec((tm, RET_HEAD_DIM), lambda bi, i: (i, 0))
    return pl.pallas_call(
        functools.partial(_inproj_kernel, chunk=NA_WIDTH, q_scale=NA_HEAD_DIM ** -0.5 * LOG2_E),
        grid=(b, s // tm),
        in_specs=[pl.BlockSpec((None, tm, d), lambda bi, i: (bi, i, 0)),
                  _mod_spec(1, d), _mod_spec(0, d),
                  pl.BlockSpec((d, e), lambda bi, i: (0, 0)), rope, rope],
        out_specs=pl.BlockSpec((None, tm, e), lambda bi, i: (bi, i, 0)),
        out_shape=jax.ShapeDtypeStruct((b, s, e), BF16),
        compiler_params=_cparams(("parallel", "parallel")),
        name="inproj",
    )(x, mod, mod, w_in_bf, cos2, sin2)


def _natten_kernel(q_ref, k_ref, v_ref, bias_ref, o_ref, *, rows, rows_per_iter):
    kspan = WIN_R * GRID_W

    first = lax.broadcasted_iota(I32, (1, LANES), 1) < NA_HEAD_DIM
    zero = jnp.zeros((), BF16)

    def rows_body(i, carry):
        qrows, krows, scores, probs = {}, {}, {}, {}

        def stage_scores(u):
            r = i * rows_per_iter + u
            rs = jnp.clip(r - WIN_R // 2, 0, rows - WIN_R)
            vi = r - rs
            qrows[u] = pl.ds(pl.multiple_of(r * GRID_W, GRID_W), GRID_W)
            krows[u] = pl.ds(pl.multiple_of(rs * GRID_W, GRID_W), kspan)
            q = q_ref[qrows[u], :]
            qm = jnp.concatenate([jnp.where(first, q, zero), jnp.where(first, zero, q)], axis=0)
            st = lax.dot_general(k_ref[krows[u], :], qm, (((1,), (1,)), ((), ())), preferred_element_type=F32)
            scores[u] = st + bias_ref[vi]

        def stage_softmax(u):
            st = scores.pop(u)
            p = jnp.exp2(st - jnp.max(st, axis=0, keepdims=True))
            probs[u] = (p * (1.0 / jnp.sum(p, axis=0, keepdims=True))).astype(BF16)

        def stage_values(u):
            res = lax.dot_general(probs.pop(u), v_ref[krows[u], :], (((0,), (0,)), ((), ())),
                                  preferred_element_type=F32)
            o_ref[qrows[u], :] = jnp.where(first, res[:GRID_W], res[GRID_W:]).astype(o_ref.dtype)

        for step in range(rows_per_iter + 2):
            if step < rows_per_iter:
                stage_scores(step)
            if 0 <= step - 1 < rows_per_iter:
                stage_softmax(step - 1)
            if 0 <= step - 2 < rows_per_iter:
                stage_values(step - 2)
        return carry

    lax.fori_loop(0, rows // rows_per_iter, rows_body, 0)


def _na_bias_table(rpb):
    w = GRID_W
    cq = jnp.arange(w)
    cs = jnp.clip(cq - WIN_C // 2, 0, w - WIN_C)
    ck = jnp.arange(w)
    col_in = (ck[None, :] >= cs[:, None]) & (ck[None, :] < cs[:, None] + WIN_C)
    dc_idx = jnp.clip(ck[None, :] - cq[:, None] + WIN_C - 1, 0, 2 * WIN_C - 2)
    t = rpb[:, :, dc_idx]
    t = jnp.where(col_in[None, None], t, -jnp.inf)
    vi = jnp.arange(WIN_R)
    kr = jnp.arange(WIN_R)
    dr = kr[None, :] - vi[:, None] + WIN_R - 1
    tb = t[:, dr]
    hp = LANES // NA_HEAD_DIM
    tb = tb.reshape(rpb.shape[0] // hp, hp, WIN_R, WIN_R, w, w)
    tb = tb.transpose(0, 2, 3, 5, 1, 4).reshape(rpb.shape[0] // hp, WIN_R, WIN_R * w, hp * w)
    return tb.astype(F32) * LOG2_E


def _natten(proj, bias_tab):
    b, s, _ = proj.shape
    rows = s // GRID_W
    hp = LANES // NA_HEAD_DIM
    npair = NA_HEADS // hp
    blk = lambda off: pl.BlockSpec((None, s, LANES), lambda bi, p, off=off: (bi, 0, off + p))
    return pl.pallas_call(
        functools.partial(_natten_kernel, rows=rows, rows_per_iter=math.gcd(rows, NA_ROWS_PER_ITER)),
        grid=(b, npair),
        in_specs=[blk(0), blk(npair), blk(2 * npair),
                  pl.BlockSpec((None, WIN_R, WIN_R * GRID_W, hp * GRID_W), lambda bi, p: (p, 0, 0, 0))],
        out_specs=pl.BlockSpec((None, s, LANES), lambda bi, p: (bi, 0, p)),
        out_shape=jax.ShapeDtypeStruct((b, s, NA_WIDTH), BF16),
        compiler_params=_cparams(("parallel", "parallel")),
        name="natten",
    )(proj, proj, proj, bias_tab)


def _retent_kernel(ld_ref, q_ref, k_ref, v_ref, g_ref, gn_ref, o_ref, sb_ref, *, nchunk):
    c = RET_BLOCK
    dh = RET_HEAD_DIM
    h = pl.program_id(1)
    lgf = ld_ref[0, h]
    lgb = ld_ref[1, h]

    ic = lax.broadcasted_iota(I32, (c, 1), 0).astype(F32)
    ir = lax.broadcasted_iota(I32, (1, c), 1).astype(F32)
    diff = ic - ir
    dmat = jnp.where(diff >= 0, jnp.exp(jnp.maximum(diff, 0.0) * lgf),
                     jnp.exp(jnp.maximum(-diff, 0.0) * lgb))
    kdec_f = jnp.exp((c - 1 - ic) * lgf)
    qdec_f = jnp.exp((ic + 1) * lgf)
    kdec_b = jnp.exp(ic * lgb)
    qdec_b = jnp.exp((c - ic) * lgb)
    one = jnp.ones((1, 1), F32)
    cdec_f = jnp.exp(one * (c * lgf))
    cdec_b = jnp.exp(one * (c * lgb))
    tn = (((0,), (0,)), ((), ()))

    def bwd_body(i, sb):
        n = nchunk - 1 - i
        sb_ref[n] = sb
        rows = pl.ds(pl.multiple_of(n * c, c), c)
        kd = (k_ref[rows, :].astype(F32) * kdec_b).astype(BF16)
        kv = lax.dot_general(kd, v_ref[rows, :], tn, preferred_element_type=F32)
        return cdec_b * sb + kv

    lax.fori_loop(0, nchunk, bwd_body, jnp.zeros((dh, dh), F32), unroll=min(RET_UNROLL, nchunk))

    gn = gn_ref[...]

    def fwd_body(n, sf):
        rows = pl.ds(pl.multiple_of(n * c, c), c)
        qb = q_ref[rows, :]
        kb = k_ref[rows, :]
        qn = qb.astype(F32)
        kn = kb.astype(F32)
        vn = v_ref[rows, :]
        sc = lax.dot_general(qb, kb, (((1,), (1,)), ((), ())), preferred_element_type=F32) * dmat
        y = jnp.dot(sc.astype(BF16), vn, preferred_element_type=F32)
        qd = jnp.concatenate([qn * qdec_f, qn * qdec_b], axis=1).astype(BF16)
        st = jnp.concatenate([sf, sb_ref[n]], axis=0).astype(BF16)
        y = y + jnp.dot(qd, st, preferred_element_type=F32)
        mu = jnp.mean(y, -1, keepdims=True)
        yc = y - mu
        var = jnp.mean(yc * yc, -1, keepdims=True)
        yn = yc * lax.rsqrt(var + GN_EPS) * gn
        o_ref[rows, :] = (_silu(g_ref[rows, :].astype(F32)) * yn).astype(o_ref.dtype)
        kv = lax.dot_general((kn * kdec_f).astype(BF16), vn, tn, preferred_element_type=F32)
        return cdec_f * sf + kv

    lax.fori_loop(0, nchunk, fwd_body, jnp.zeros((dh, dh), F32), unroll=min(RET_UNROLL, nchunk))


def _retention(proj, log_decay, gn_g):
    b, s, _ = proj.shape
    dh = RET_HEAD_DIM
    nchunk = s // RET_BLOCK
    base = 3 * NA_WIDTH // dh
    blk = lambda off: pl.BlockSpec((None, s, dh), lambda bi, h, off=off: (bi, 0, base + off + h))
    return pl.pallas_call(
        functools.partial(_retent_kernel, nchunk=nchunk),
        grid=(b, RET_HEADS),
        in_specs=[pl.BlockSpec(memory_space=pltpu.SMEM),
                  blk(0), blk(RET_HEADS), blk(2 * RET_HEADS), blk(3 * RET_HEADS),
                  pl.BlockSpec((1, dh), lambda bi, h: (0, h))],
        out_specs=pl.BlockSpec((None, s, dh), lambda bi, h: (bi, 0, h)),
        out_shape=jax.ShapeDtypeStruct((b, s, RET_WIDTH), BF16),
        scratch_shapes=[pltpu.VMEM((nchunk, dh, dh), F32)],
        compiler_params=_cparams(("parallel", "parallel")),
        name="retent",
    )(log_decay, proj, proj, proj, proj, gn_g.reshape(1, RET_WIDTH))


def _outproj_kernel(yna_ref, yr_ref, x_ref, ga_ref, sf_ref, shf_ref, wo1_ref, wo2_ref, g_ref, b_ref,
                    wrh_ref, wrl_ref, x1_ref, hfp_ref, lg_ref, *, alpha):
    nt = (((1,), (1,)), ((), ()))
    tm = x_ref.shape[0]
    parts = [pl.ds(p * (tm // OUTPROJ_PARTS), tm // OUTPROJ_PARTS) for p in range(OUTPROJ_PARTS)]
    def mix_of(r):
        return (jnp.dot(yna_ref[r, :], wo1_ref[...], preferred_element_type=F32)
                + jnp.dot(yr_ref[r, :], wo2_ref[...], preferred_element_type=F32))

    nxt = mix_of(parts[0])
    for p, r in enumerate(parts):
        mix = nxt
        if p + 1 < len(parts):
            nxt = mix_of(parts[p + 1])
        x1 = _layer_norm(alpha * x_ref[r, :] + ga_ref[...] * mix, g_ref[...], b_ref[...])
        x1_ref[r, :] = x1
        hf = x1 * (1.0 + sf_ref[...]) + shf_ref[...]
        for j, w in enumerate(_pack_rows(hf)):
            hfp_ref[j, r, :] = w
        hb = hf.astype(BF16)
        hl = (hf - hb.astype(F32)).astype(BF16)
        lg = lax.dot_general(wrh_ref[...], hb, nt, preferred_element_type=F32)
        lg = lg + lax.dot_general(wrh_ref[...], hl, nt, preferred_element_type=F32)
        lg = lg + lax.dot_general(wrl_ref[...], hb, nt, preferred_element_type=F32)
        lg_ref[:, r] = lg


def _outproj(y_na, y_r, x, mod, w_out_bf, ln_g, ln_b, wr_hi, wr_lo, alpha):
    b, s, d = x.shape
    tm = min(TM_PROJ, s)
    nt = s // tm
    ne = wr_hi.shape[0]
    const = lambda shape: pl.BlockSpec(shape, lambda bi, i: tuple(0 for _ in shape))
    x1, hfp, lg = pl.pallas_call(
        functools.partial(_outproj_kernel, alpha=alpha),
        grid=(b, nt),
        in_specs=[pl.BlockSpec((None, tm, NA_WIDTH), lambda bi, i: (bi, i, 0)),
                  pl.BlockSpec((None, tm, RET_WIDTH), lambda bi, i: (bi, i, 0)),
                  pl.BlockSpec((None, tm, d), lambda bi, i: (bi, i, 0)),
                  _mod_spec(2, d), _mod_spec(4, d), _mod_spec(3, d),
                  pl.BlockSpec((NA_WIDTH, d), lambda bi, i: (0, 0)),
                  pl.BlockSpec((RET_WIDTH, d), lambda bi, i: (1, 0)),
                  const((1, d)), const((1, d)), const((ne, d)), const((ne, d))],
        out_specs=[pl.BlockSpec((None, tm, d), lambda bi, i: (bi, i, 0)),
                   pl.BlockSpec((PACK_W, tm, LANES), lambda bi, i: (0, bi * nt + i, 0)),
                   pl.BlockSpec((ne, tm), lambda bi, i: (0, bi * nt + i))],
        out_shape=[jax.ShapeDtypeStruct((b, s, d), F32),
                   jax.ShapeDtypeStruct((PACK_W, b * s, LANES), I32),
                   jax.ShapeDtypeStruct((ne, b * s), F32)],
        compiler_params=_cparams(("parallel", "parallel")),
        name="outproj",
    )(y_na, y_r, x, mod, mod, mod, w_out_bf, w_out_bf, ln_g.reshape(1, d), ln_b.reshape(1, d), wr_hi, wr_lo)
    return x1, hfp, lg


def _route_kernel(lg_ref, rb_ref, idx_ref, w_ref, rank_ref, cnt_ref):
    t = lg_ref.shape[1]
    ninf = -jnp.inf

    @pl.when(pl.program_id(0) == 0)
    def _():
        cnt_ref[...] = jnp.zeros_like(cnt_ref)

    scores = jax.nn.sigmoid(lg_ref[...])
    sel = scores + rb_ref[...]

    io_g = lax.broadcasted_iota(I32, (GROUP_SIZE, t), 0)
    gs_rows = []
    for g in range(N_GROUPS):
        blk = sel[g * GROUP_SIZE:(g + 1) * GROUP_SIZE, :]
        m1 = jnp.max(blk, axis=0, keepdims=True)
        i1 = jnp.min(jnp.where(blk == m1, io_g, GROUP_SIZE), axis=0, keepdims=True)
        m2 = jnp.max(jnp.where(io_g == i1, ninf, blk), axis=0, keepdims=True)
        gs_rows.append(m1 + m2)
    gs = jnp.concatenate(gs_rows, axis=0)

    io8 = lax.broadcasted_iota(I32, (N_GROUPS, t), 0)
    gsel = jnp.zeros((N_GROUPS, t), F32)
    for _ in range(TOPK_GROUPS):
        m = jnp.max(gs, axis=0, keepdims=True)
        gi = jnp.min(jnp.where(gs == m, io8, N_GROUPS), axis=0, keepdims=True)
        hit = io8 == gi
        gsel = jnp.where(hit, 1.0, gsel)
        gs = jnp.where(hit, ninf, gs)

    masked = jnp.concatenate(
        [jnp.where(gsel[g:g + 1, :] > 0.0, sel[g * GROUP_SIZE:(g + 1) * GROUP_SIZE, :], ninf)
         for g in range(N_GROUPS)], axis=0)

    io_e = lax.broadcasted_iota(I32, (N_EXPERTS, t), 0)
    chosen = jnp.zeros((N_EXPERTS, t), F32)
    idx_rows, w_rows = [], []
    for _ in range(TOP_K):
        m = jnp.max(masked, axis=0, keepdims=True)
        ei = jnp.min(jnp.where(masked == m, io_e, N_EXPERTS), axis=0, keepdims=True)
        hit = io_e == ei
        w_rows.append(jnp.sum(jnp.where(hit, scores, 0.0), axis=0, keepdims=True))
        idx_rows.append(ei)
        chosen = jnp.where(hit, 1.0, chosen)
        masked = jnp.where(hit, ninf, masked)
    wk = jnp.concatenate(w_rows, axis=0)
    w_ref[...] = wk / jnp.sum(wk, axis=0, keepdims=True) * ROUTED_SCALE
    idx_ref[...] = jnp.concatenate(idx_rows, axis=0)

    upper = (lax.broadcasted_iota(I32, (t, t), 0) < lax.broadcasted_iota(I32, (t, t), 1))
    prefix = jnp.dot(chosen.astype(BF16), upper.astype(BF16), preferred_element_type=F32)
    rank_full = prefix + cnt_ref[...]
    rank_rows = [jnp.sum(jnp.where(io_e == ei, rank_full, 0.0), axis=0, keepdims=True) for ei in idx_rows]
    rank_ref[...] = jnp.concatenate(rank_rows, axis=0).astype(I32)
    cnt_ref[...] += jnp.sum(chosen, axis=1, keepdims=True)


def _route(logits_t, router_bias):
    ne, n = logits_t.shape
    t = min(T_ROUTE, n)
    kspec = pl.BlockSpec((TOP_K, t), lambda i: (0, i))
    return pl.pallas_call(
        _route_kernel,
        grid=(n // t,),
        in_specs=[pl.BlockSpec((ne, t), lambda i: (0, i)),
                  pl.BlockSpec((ne, 1), lambda i: (0, 0))],
        out_specs=[kspec, kspec, kspec, pl.BlockSpec((ne, 1), lambda i: (0, 0))],
        out_shape=[jax.ShapeDtypeStruct((TOP_K, n), I32),
                   jax.ShapeDtypeStruct((TOP_K, n), F32),
                   jax.ShapeDtypeStruct((TOP_K, n), I32),
                   jax.ShapeDtypeStruct((ne, 1), F32)],
        compiler_params=_cparams(("arbitrary",)),
        name="route",
    )(logits_t, router_bias.reshape(ne, 1))


def _block(ref, g):
    return ref.at[:, pl.ds(pl.multiple_of(g * BLK_E, BLK_E), BLK_E), :]


def _slots_kernel(idx_ref, rank_ref, ps_ref, o_ref, *, n_slots):
    t = idx_ref.shape[1]
    io = lax.broadcasted_iota(I32, (N_EXPERTS, t), 0)
    ps = ps_ref[...]
    for k in range(TOP_K):
        hit = io == idx_ref[k:k + 1, :]
        slot = jnp.sum(jnp.where(hit, ps, 0), axis=0, keepdims=True) + rank_ref[k:k + 1, :]
        for j in range(PACK_W):
            o_ref[j * TOP_K + k:j * TOP_K + k + 1, :] = slot + j * n_slots


def _slots(idx, rank, pstarts, n_slots):
    n = idx.shape[1]
    t = min(T_SLOT, n)
    kspec = pl.BlockSpec((TOP_K, t), lambda i: (0, i))
    return pl.pallas_call(
        functools.partial(_slots_kernel, n_slots=n_slots),
        grid=(n // t,),
        in_specs=[kspec, kspec, pl.BlockSpec((N_EXPERTS, 1), lambda i: (0, 0))],
        out_specs=pl.BlockSpec((PACK_W * TOP_K, t), lambda i: (0, i)),
        out_shape=jax.ShapeDtypeStruct((PACK_W * TOP_K, n), I32),
        compiler_params=_cparams(("parallel",)),
        name="slots",
    )(idx, rank, pstarts.reshape(N_EXPERTS, 1))


def _sc_mesh():
    return plsc.VectorSubcoreMesh(core_axis_name="core", subcore_axis_name="subcore")


def _scatter_rows_sc(rows, dest_rows, n_out):
    n_rows = rows.shape[0]
    wpp = n_rows // PACK_W // SC_WINDOW

    @functools.partial(pl.kernel, mesh=_sc_mesh(), scratch_types=[pltpu.SemaphoreType.DMA],
                       out_type=jax.ShapeDtypeStruct((n_out, LANES), I32))
    def scatter_rows(x_hbm, i_hbm, o_hbm, sem):
        def body(x_vmem, i_vmem):
            copies = [pltpu.async_copy(x_vmem, o_hbm.at[i_vmem.at[k]], sem) for k in range(TOP_K)]
            for c in copies:
                c.wait()

        pltpu.emit_pipeline(
            body,
            grid=(n_rows // SC_WINDOW,),
            in_specs=[pl.BlockSpec((SC_WINDOW, LANES), lambda i: (i, 0)),
                      pl.BlockSpec((TOP_K, SC_WINDOW), lambda i: (i // wpp, i % wpp))],
            out_specs=[],
            core_axis_name=("core", "subcore"),
            dimension_semantics=(pltpu.PARALLEL,),
        )(x_hbm, i_hbm)

    return scatter_rows(rows, dest_rows)


def _gather_rows_sc(rows, slot_rows, tok0, nc):
    nr = slot_rows.shape[0]
    span = SC_GATHER_WINDOWS * SC_WINDOW
    spr = nc // span
    w0 = tok0 // SC_WINDOW

    @functools.partial(pl.kernel, mesh=_sc_mesh(), scratch_types=[pltpu.SemaphoreType.DMA],
                       out_type=jax.ShapeDtypeStruct((nr * nc, LANES), I32))
    def gather_rows(x_hbm, i_hbm, o_hbm, sem):
        def body(*refs):
            o_vmem = refs[-1]
            copies = [pltpu.async_copy(x_hbm.at[i_vmem.at[0]], o_vmem.at[pl.ds(w * SC_WINDOW, SC_WINDOW)], sem)
                      for w, i_vmem in enumerate(refs[:-1])]
            for c in copies:
                c.wait()

        def idx_spec(w):
            return pl.BlockSpec((1, SC_WINDOW),
                                lambda i: (i // spr, w0 + (i % spr) * SC_GATHER_WINDOWS + w))

        pltpu.emit_pipeline(
            body,
            grid=(nr * spr,),
            in_specs=[idx_spec(w) for w in range(SC_GATHER_WINDOWS)],
            out_specs=[pl.BlockSpec((span, LANES), lambda i: (i, 0))],
            core_axis_name=("core", "subcore"),
            dimension_semantics=(pltpu.PARALLEL,),
        )(*([i_hbm] * SC_GATHER_WINDOWS), o_hbm)

    return gather_rows(rows, slot_rows)


def _experts_kernel(nblk_ref, gstart_ref, cnt_ref, nb_ref, xs_ref, wg_ref, wu_ref, wd_ref, ys_ref,
                    xbuf_ref, ybuf_ref, wgub_ref, wdb_ref, xsem, ysem):
    e = pl.program_id(0)
    total = nb_ref[0]
    nb_max = ys_ref.shape[1] // BLK_E

    def x_copy(g):
        slot = g % X_RING
        return pltpu.make_async_copy(_block(xs_ref, g), xbuf_ref.at[slot], xsem.at[slot])

    def y_copy(g):
        slot = g % Y_RING
        return pltpu.make_async_copy(ybuf_ref.at[slot], _block(ys_ref, g), ysem.at[slot])

    @pl.when(e == 0)
    def _():
        for g in range(X_AHEAD):
            @pl.when(g < total)
            def _():
                x_copy(g).start()

    @pl.when(nblk_ref[e] > 0)
    def _():
        f = wg_ref.shape[1]
        wgub_ref[:, :f] = wg_ref[...].astype(BF16)
        wgub_ref[:, f:] = wu_ref[...].astype(BF16)
        wdb_ref[...] = wd_ref[...].astype(BF16)

    def fetch(b):
        g = gstart_ref[e] + b
        x_copy(g).wait()

        @pl.when(g + X_AHEAD < total)
        def _():
            x_copy(g + X_AHEAD).start()

        xb = _unpack_words([xbuf_ref[g % X_RING, j] for j in range(PACK_W)]).astype(BF16)
        live = lax.broadcasted_iota(I32, (BLK_E, 1), 0) < cnt_ref[e] - b * BLK_E
        return jnp.where(live, xb, jnp.zeros((), BF16))

    def hidden(xb):
        f = wg_ref.shape[1]
        hgu = jnp.dot(xb, wgub_ref[...], preferred_element_type=F32)
        return (_silu(hgu[:, :f]) * hgu[:, f:]).astype(BF16)

    def finish(b, act):
        g = gstart_ref[e] + b
        words = _pack_rows(jnp.dot(act, wdb_ref[...], preferred_element_type=F32))

        @pl.when(g >= Y_RING)
        def _():
            y_copy(g - Y_RING).wait()

        for j, w in enumerate(words):
            ybuf_ref[g % Y_RING, j] = w
        y_copy(g).start()

    def block(b, carry):
        finish(b, hidden(fetch(b)))
        return carry

    lax.fori_loop(0, nblk_ref[e], block, 0)

    @pl.when(e == pl.num_programs(0) - 1)
    def _():
        for back in range(Y_RING):
            @pl.when(total - 1 - back >= 0)
            def _():
                y_copy(total - 1 - back).wait()

        ybuf_ref[0] = jnp.zeros(ybuf_ref.shape[1:], ybuf_ref.dtype)

        def tail_copy(g):
            return pltpu.make_async_copy(ybuf_ref.at[0], _block(ys_ref, g), ysem.at[0])

        def fill(g, carry):
            tail_copy(g).start()
            return carry

        lax.fori_loop(total, nb_max, fill, 0)

        def drain(g, carry):
            tail_copy(g).wait()
            return carry

        lax.fori_loop(total, nb_max, drain, 0)


def _experts(nblk, gstart, cnt, n_blocks, xs, w_gate, w_up, w_down):
    p = xs.shape[1]
    ne, d, f = w_gate.shape
    wspec = lambda shape: pl.BlockSpec((None,) + shape, lambda e, *_: (e, 0, 0))
    grid_spec = pltpu.PrefetchScalarGridSpec(
        num_scalar_prefetch=4,
        grid=(ne,),
        in_specs=[pl.BlockSpec(memory_space=pl.ANY), wspec((d, f)), wspec((d, f)), wspec((f, d))],
        out_specs=pl.BlockSpec(memory_space=pl.ANY),
        scratch_shapes=[pltpu.VMEM((X_RING, PACK_W, BLK_E, LANES), I32),
                        pltpu.VMEM((Y_RING, PACK_W, BLK_E, LANES), I32),
                        pltpu.VMEM((d, 2 * f), BF16), pltpu.VMEM((f, d), BF16),
                        pltpu.SemaphoreType.DMA((X_RING,)), pltpu.SemaphoreType.DMA((Y_RING,))],
    )
    return pl.pallas_call(
        _experts_kernel,
        grid_spec=grid_spec,
        out_shape=jax.ShapeDtypeStruct((PACK_W, p, LANES), I32),
        compiler_params=_cparams(("arbitrary",)),
        name="experts",
    )(nblk, gstart, cnt, n_blocks, xs, w_gate, w_up, w_down)


def _combine_kernel(wt_ref, yk_ref, x1_ref, sf_ref, shf_ref, gf_ref, wsg_ref, wsu_ref, wsd_ref, g_ref, b_ref,
                    *rest, alpha):
    o_ref = rest[-1]
    x1 = x1_ref[...]
    hb = (x1 * (1.0 + sf_ref[...]) + shf_ref[...]).astype(BF16)
    sg = jnp.dot(hb, wsg_ref[...], preferred_element_type=F32)
    su = jnp.dot(hb, wsu_ref[...], preferred_element_type=F32)
    ffn = jnp.dot((_silu(sg) * su).astype(BF16), wsd_ref[...], preferred_element_type=F32)
    wt = wt_ref[...]
    for k in range(TOP_K):
        yk = _unpack_words([yk_ref[j, k] for j in range(PACK_W)])
        ffn = ffn + wt[:, k:k + 1] * yk
    o_ref[...] = _layer_norm(alpha * x1 + gf_ref[...] * ffn, g_ref[...], b_ref[...])


def _combine(w_tok, yk, x1, mod, ws_gate_bf, ws_up_bf, ws_down_bf, ln_g, ln_b, alpha, tok0, earlier):
    n, d = x1.shape
    nc = yk.shape[2]
    s = n // mod.shape[1]
    tc = min(T_COMB, s, nc)
    nt = s // tc
    t0 = tok0 // tc
    fs = ws_gate_bf.shape[1]
    const = lambda shape: pl.BlockSpec(shape, lambda i: tuple(0 for _ in shape))
    mod_spec = lambda which: pl.BlockSpec((None, None, 1, d), lambda i: (which, (t0 + i) // nt, 0, 0))
    in_specs = [pl.BlockSpec((tc, TOP_K), lambda i: (t0 + i, 0)),
                pl.BlockSpec((PACK_W, TOP_K, tc, LANES), lambda i: (0, 0, i, 0)),
                pl.BlockSpec((tc, d), lambda i: (t0 + i, 0)),
                mod_spec(4), mod_spec(3), mod_spec(5),
                const((d, fs)), const((d, fs)), const((fs, d)), const((1, d)), const((1, d))]
    args = [w_tok, yk, x1, mod, mod, mod, ws_gate_bf, ws_up_bf, ws_down_bf, ln_g.reshape(1, d), ln_b.reshape(1, d)]
    aliases = {}
    if earlier is not None:
        in_specs.append(pl.BlockSpec(memory_space=pl.ANY))
        args.append(earlier)
        aliases = {len(args) - 1: 0}
    return pl.pallas_call(
        functools.partial(_combine_kernel, alpha=alpha),
        grid=(nc // tc,),
        in_specs=in_specs,
        out_specs=pl.BlockSpec((tc, d), lambda i: (t0 + i, 0)),
        out_shape=jax.ShapeDtypeStruct((n, d), F32),
        input_output_aliases=aliases,
        compiler_params=_cparams(("parallel",)),
        name="combine",
    )(*args)


def _slot_layout(counts, n_assign):
    cnt = counts[:, 0].astype(I32)
    padded = (cnt + BLK_E - 1) // BLK_E * BLK_E
    pends = jnp.cumsum(padded)
    pstarts = pends - padded
    n_blocks_max = (n_assign + N_EXPERTS * (BLK_E - 1)) // BLK_E
    n_blocks = (pends[-1] // BLK_E).astype(I32).reshape(1)
    return pstarts, padded // BLK_E, pstarts // BLK_E, n_blocks, n_blocks_max * BLK_E


def kernel(x, c, w_ada, b_ada, w_in, w_out, na_rpb, ret_log_decay, ret_gn_g, ln1_g, ln1_b, ln2_g, ln2_b,
           w_router, router_bias, w_gate, w_up, w_down, ws_gate, ws_up, ws_down):
    b, s, d = x.shape
    depth = w_ada.shape[0]
    alpha = (2.0 * depth) ** 0.25
    t = jnp.arange(s, dtype=F32)
    inv_freq = ROPE_BASE ** (-jnp.arange(0, RET_HEAD_DIM, 2, dtype=F32) / RET_HEAD_DIM)
    ang = t[:, None] * inv_freq[None, :]
    cos, sin = jnp.cos(ang), jnp.sin(ang)
    cos2 = jnp.concatenate([cos, cos], axis=-1)
    sin2 = jnp.concatenate([-sin, sin], axis=-1)
    for l in range(depth):
        mod = _mod(c, w_ada[l], b_ada[l])
        proj = _inproj(x, mod, w_in[l].astype(BF16), cos2, sin2)
        y_na = _natten(proj, _na_bias_table(na_rpb[l]))
        y_r = _retention(proj, ret_log_decay[l], ret_gn_g[l])
        wr_t = w_router[l].T
        wr_hi = wr_t.astype(BF16)
        wr_lo = (wr_t - wr_hi.astype(F32)).astype(BF16)
        x1, hfp, logits_t = _outproj(y_na, y_r, x, mod, w_out[l].astype(BF16), ln1_g[l], ln1_b[l],
                                     wr_hi, wr_lo, alpha)
        idx, wts, rank, counts = _route(logits_t, router_bias[l])
        n = b * s
        pstarts, nblk, gstart, n_blocks, n_slots = _slot_layout(counts, n * TOP_K)
        slot_rows = _slots(idx, rank, pstarts, n_slots)
        xs = _scatter_rows_sc(hfp.reshape(PACK_W * n, LANES), slot_rows, PACK_W * n_slots)
        ys = _experts(nblk, gstart, counts[:, 0].astype(I32), n_blocks, xs.reshape(PACK_W, n_slots, LANES),
                      w_gate[l], w_up[l], w_down[l]).reshape(PACK_W * n_slots, LANES)
        ws = (ws_gate[l].astype(BF16), ws_up[l].astype(BF16), ws_down[l].astype(BF16))
        w_tok = wts.T
        x1n = x1.reshape(n, d)
        nc = n // COMBINE_CHUNKS
        out = None
        for ci in range(COMBINE_CHUNKS):
            yk = _gather_rows_sc(ys, slot_rows, ci * nc, nc).reshape(PACK_W, TOP_K, nc, LANES)
            out = _combine(w_tok, yk, x1n, mod, *ws, ln2_g[l], ln2_b[l], alpha, ci * nc, out)
        x = out.reshape(b, s, d)
    return x
```

```python
import functools
import math

import jax
import jax.numpy as jnp
from jax import lax
from jax.experimental import pallas as pl
from jax.experimental.pallas import tpu as pltpu
from jax.experimental.pallas import tpu_sc as plsc

F32 = jnp.float32
BF16 = jnp.bfloat16
U32 = jnp.uint32
I32 = jnp.int32

GRID_W = 64
WIN_R = 8
WIN_C = 16
NA_HEADS = 8
NA_HEAD_DIM = 64
NA_WIDTH = NA_HEADS * NA_HEAD_DIM
RET_HEADS = 4
RET_HEAD_DIM = 128
RET_WIDTH = RET_HEADS * RET_HEAD_DIM
ROPE_BASE = 10000.0
N_EXPERTS = 256
TOP_K = 8
N_GROUPS = 8
TOPK_GROUPS = 4
GROUP_SIZE = N_EXPERTS // N_GROUPS
ROUTED_SCALE = 2.5
LOG2_E = 1.4426950408889634
LN_EPS = 1e-5
GN_EPS = 1e-6

LANES = 128
VMEM_BYTES = 64 * 1024 * 1024
VMEM_LIMIT = VMEM_BYTES * 7 // 8

TM_INPROJ = 1024
TM_PROJ = 512
T_ROUTE = 512
T_SLOT = 1024
SC_WINDOW = 128
SC_GATHER_WINDOWS = 2
BLK_E = 512
OUTPROJ_PARTS = 2
T_COMB = 512
COMBINE_CHUNKS = 8
PACK_W = 4
NA_ROWS_PER_ITER = 32
RET_BLOCK = 256
RET_UNROLL = 8
X_RING = 4
X_AHEAD = 3
Y_RING = 3


def _cparams(sem, vmem=VMEM_LIMIT):
    return pltpu.CompilerParams(dimension_semantics=sem, vmem_limit_bytes=vmem)


def _silu(v):
    return v * jax.nn.sigmoid(v)


def _layer_norm(z, g, b):
    mu = jnp.mean(z, -1, keepdims=True)
    zc = z - mu
    var = jnp.mean(zc * zc, -1, keepdims=True)
    return zc * lax.rsqrt(var + LN_EPS) * g + b


def _pack_rows(v):
    half = v.shape[1] // 2
    vb = v.astype(BF16)
    lo = lax.bitcast_convert_type(vb[:, :half].astype(F32), U32) >> 16
    hi = lax.bitcast_convert_type(vb[:, half:].astype(F32), U32)
    w = lax.bitcast_convert_type(hi | lo, I32)
    return [w[:, j * LANES:(j + 1) * LANES] for j in range(half // LANES)]


def _unpack_words(words):
    words = [lax.bitcast_convert_type(w, U32) for w in words]
    lo = [lax.bitcast_convert_type(w << 16, F32) for w in words]
    hi = [lax.bitcast_convert_type(w & jnp.uint32(0xFFFF0000), F32) for w in words]
    return jnp.concatenate(lo + hi, axis=-1)


def _mod_kernel(c_ref, w_ref, b_ref, o_ref):
    cond = _silu(c_ref[...])
    o_ref[0] = jnp.dot(cond, w_ref[...], precision=lax.Precision.HIGHEST,
                       preferred_element_type=F32) + b_ref[0]


def _mod(c, w_ada, b_ada):
    b, d = c.shape
    n6 = w_ada.shape[1] // d
    out = pl.pallas_call(
        _mod_kernel,
        grid=(n6,),
        in_specs=[pl.BlockSpec((b, d), lambda j: (0, 0)),
                  pl.BlockSpec((d, d), lambda j: (0, j)),
                  pl.BlockSpec((1, 1, d), lambda j: (j, 0, 0))],
        out_specs=pl.BlockSpec((1, b, d), lambda j: (j, 0, 0)),
        out_shape=jax.ShapeDtypeStruct((n6, b, d), F32),
        compiler_params=_cparams(("arbitrary",)),
        name="mod",
    )(c, w_ada, b_ada.reshape(n6, 1, d))
    return out.reshape(n6, b, 1, d)


def _mod_spec(which, d):
    return pl.BlockSpec((None, None, 1, d), lambda b, i, which=which: (which, b, 0, 0))


def _inproj_kernel(x_ref, sc_ref, sh_ref, w_ref, cos_ref, sin_ref, o_ref, *, chunk, q_scale):
    h = (x_ref[...] * (1.0 + sc_ref[...]) + sh_ref[...]).astype(BF16)
    q_r, k_r = 3 * NA_WIDTH // chunk, 3 * NA_WIDTH // chunk + 1
    dh = RET_HEAD_DIM
    for j in range(o_ref.shape[1] // chunk):
        acc = jnp.dot(h, w_ref[:, j * chunk:(j + 1) * chunk], preferred_element_type=F32)
        if j == 0:
            acc = acc * q_scale
        if j in (q_r, k_r):
            cos2, sin2 = cos_ref[...], sin_ref[...]
            heads = [acc[:, hh * dh:(hh + 1) * dh] for hh in range(chunk // dh)]
            heads = [t * cos2 + pltpu.roll(t, dh // 2, 1) * sin2 for t in heads]
            acc = jnp.concatenate(heads, axis=1)
            if j == k_r:
                acc = acc * dh ** -0.5
        o_ref[:, j * chunk:(j + 1) * chunk] = acc.astype(o_ref.dtype)


def _inproj(x, mod, w_in_bf, cos2, sin2):
    b, s, d = x.shape
    e = w_in_bf.shape[1]
    tm = min(TM_INPROJ, s)
    rope = pl.BlockSpec((tm, RET_HEAD_DIM), lambda bi, i: (i, 0))
    return pl.pallas_call(
        functools.partial(_inproj_kernel, chunk=NA_WIDTH, q_scale=NA_HEAD_DIM ** -0.5 * LOG2_E),
        grid=(b, s // tm),
        in_specs=[pl.BlockSpec((None, tm, d), lambda bi, i: (bi, i, 0)),
                  _mod_spec(1, d), _mod_spec(0, d),
                  pl.BlockSpec((d, e), lambda bi, i: (0, 0)), rope, rope],
        out_specs=pl.BlockSpec((None, tm, e), lambda bi, i: (bi, i, 0)),
        out_shape=jax.ShapeDtypeStruct((b, s, e), BF16),
        compiler_params=_cparams(("parallel", "parallel")),
        name="inproj",
    )(x, mod, mod, w_in_bf, cos2, sin2)


def _natten_kernel(q_ref, k_ref, v_ref, bias_ref, o_ref, *, rows, rows_per_iter):
    kspan = WIN_R * GRID_W

    first = lax.broadcasted_iota(I32, (1, LANES), 1) < NA_HEAD_DIM
    zero = jnp.zeros((), BF16)

    def rows_body(i, carry):
        qrows, krows, scores, probs = {}, {}, {}, {}

        def stage_scores(u):
            r = i * rows_per_iter + u
            rs = jnp.clip(r - WIN_R // 2, 0, rows - WIN_R)
            vi = r - rs
            qrows[u] = pl.ds(pl.multiple_of(r * GRID_W, GRID_W), GRID_W)
            krows[u] = pl.ds(pl.multiple_of(rs * GRID_W, GRID_W), kspan)
            q = q_ref[qrows[u], :]
            qm = jnp.concatenate([jnp.where(first, q, zero), jnp.where(first, zero, q)], axis=0)
            st = lax.dot_general(k_ref[krows[u], :], qm, (((1,), (1,)), ((), ())), preferred_element_type=F32)
            scores[u] = st + bias_ref[vi]

        def stage_softmax(u):
            st = scores.pop(u)
            p = jnp.exp2(st - jnp.max(st, axis=0, keepdims=True))
            probs[u] = (p * (1.0 / jnp.sum(p, axis=0, keepdims=True))).astype(BF16)

        def stage_values(u):
            res = lax.dot_general(probs.pop(u), v_ref[krows[u], :], (((0,), (0,)), ((), ())),
                                  preferred_element_type=F32)
            o_ref[qrows[u], :] = jnp.where(first, res[:GRID_W], res[GRID_W:]).astype(o_ref.dtype)

        for step in range(rows_per_iter + 2):
            if step < rows_per_iter:
                stage_scores(step)
            if 0 <= step - 1 < rows_per_iter:
                stage_softmax(step - 1)
            if 0 <= step - 2 < rows_per_iter:
                stage_values(step - 2)
        return carry

    lax.fori_loop(0, rows // rows_per_iter, rows_body, 0)


def _na_bias_table(rpb):
    w = GRID_W
    cq = jnp.arange(w)
    cs = jnp.clip(cq - WIN_C // 2, 0, w - WIN_C)
    ck = jnp.arange(w)
    col_in = (ck[None, :] >= cs[:, None]) & (ck[None, :] < cs[:, None] + WIN_C)
    dc_idx = jnp.clip(ck[None, :] - cq[:, None] + WIN_C - 1, 0, 2 * WIN_C - 2)
    t = rpb[:, :, dc_idx]
    t = jnp.where(col_in[None, None], t, -jnp.inf)
    vi = jnp.arange(WIN_R)
    kr = jnp.arange(WIN_R)
    dr = kr[None, :] - vi[:, None] + WIN_R - 1
    tb = t[:, dr]
    hp = LANES // NA_HEAD_DIM
    tb = tb.reshape(rpb.shape[0] // hp, hp, WIN_R, WIN_R, w, w)
    tb = tb.transpose(0, 2, 3, 5, 1, 4).reshape(rpb.shape[0] // hp, WIN_R, WIN_R * w, hp * w)
    return tb.astype(F32) * LOG2_E


def _natten(proj, bias_tab):
    b, s, _ = proj.shape
    rows = s // GRID_W
    hp = LANES // NA_HEAD_DIM
    npair = NA_HEADS // hp
    blk = lambda off: pl.BlockSpec((None, s, LANES), lambda bi, p, off=off: (bi, 0, off + p))
    return pl.pallas_call(
        functools.partial(_natten_kernel, rows=rows, rows_per_iter=math.gcd(rows, NA_ROWS_PER_ITER)),
        grid=(b, npair),
        in_specs=[blk(0), blk(npair), blk(2 * npair),
                  pl.BlockSpec((None, WIN_R, WIN_R * GRID_W, hp * GRID_W), lambda bi, p: (p, 0, 0, 0))],
        out_specs=pl.BlockSpec((None, s, LANES), lambda bi, p: (bi, 0, p)),
        out_shape=jax.ShapeDtypeStruct((b, s, NA_WIDTH), BF16),
        compiler_params=_cparams(("parallel", "parallel")),
        name="natten",
    )(proj, proj, proj, bias_tab)


def _retent_kernel(ld_ref, q_ref, k_ref, v_ref, g_ref, gn_ref, o_ref, sb_ref, *, nchunk):
    c = RET_BLOCK
    dh = RET_HEAD_DIM
    h = pl.program_id(1)
    lgf = ld_ref[0, h]
    lgb = ld_ref[1, h]

    ic = lax.broadcasted_iota(I32, (c, 1), 0).astype(F32)
    ir = lax.broadcasted_iota(I32, (1, c), 1).astype(F32)
    diff = ic - ir
    dmat = jnp.where(diff >= 0, jnp.exp(jnp.maximum(diff, 0.0) * lgf),
                     jnp.exp(jnp.maximum(-diff, 0.0) * lgb))
    kdec_f = jnp.exp((c - 1 - ic) * lgf)
    qdec_f = jnp.exp((ic + 1) * lgf)
    kdec_b = jnp.exp(ic * lgb)
    qdec_b = jnp.exp((c - ic) * lgb)
    one = jnp.ones((1, 1), F32)
    cdec_f = jnp.exp(one * (c * lgf))
    cdec_b = jnp.exp(one * (c * lgb))
    tn = (((0,), (0,)), ((), ()))

    def bwd_body(i, sb):
        n = nchunk - 1 - i
        sb_ref[n] = sb
        rows = pl.ds(pl.multiple_of(n * c, c), c)
        kd = (k_ref[rows, :].astype(F32) * kdec_b).astype(BF16)
        kv = lax.dot_general(kd, v_ref[rows, :], tn, preferred_element_type=F32)
        return cdec_b * sb + kv

    lax.fori_loop(0, nchunk, bwd_body, jnp.zeros((dh, dh), F32), unroll=min(RET_UNROLL, nchunk))

    gn = gn_ref[...]

    def fwd_body(n, sf):
        rows = pl.ds(pl.multiple_of(n * c, c), c)
        qb = q_ref[rows, :]
        kb = k_ref[rows, :]
        qn = qb.astype(F32)
        kn = kb.astype(F32)
        vn = v_ref[rows, :]
        sc = lax.dot_general(qb, kb, (((1,), (1,)), ((), ())), preferred_element_type=F32) * dmat
        y = jnp.dot(sc.astype(BF16), vn, preferred_element_type=F32)
        qd = jnp.concatenate([qn * qdec_f, qn * qdec_b], axis=1).astype(BF16)
        st = jnp.concatenate([sf, sb_ref[n]], axis=0).astype(BF16)
        y = y + jnp.dot(qd, st, preferred_element_type=F32)
        mu = jnp.mean(y, -1, keepdims=True)
        yc = y - mu
        var = jnp.mean(yc * yc, -1, keepdims=True)
        yn = yc * lax.rsqrt(var + GN_EPS) * gn
        o_ref[rows, :] = (_silu(g_ref[rows, :].astype(F32)) * yn).astype(o_ref.dtype)
        kv = lax.dot_general((kn * kdec_f).astype(BF16), vn, tn, preferred_element_type=F32)
        return cdec_f * sf + kv

    lax.fori_loop(0, nchunk, fwd_body, jnp.zeros((dh, dh), F32), unroll=min(RET_UNROLL, nchunk))


def _retention(proj, log_decay, gn_g):
    b, s, _ = proj.shape
    dh = RET_HEAD_DIM
    nchunk = s // RET_BLOCK
    base = 3 * NA_WIDTH // dh
    blk = lambda off: pl.BlockSpec((None, s, dh), lambda bi, h, off=off: (bi, 0, base + off + h))
    return pl.pallas_call(
        functools.partial(_retent_kernel, nchunk=nchunk),
        grid=(b, RET_HEADS),
        in_specs=[pl.BlockSpec(memory_space=pltpu.SMEM),
                  blk(0), blk(RET_HEADS), blk(2 * RET_HEADS), blk(3 * RET_HEADS),
                  pl.BlockSpec((1, dh), lambda bi, h: (0, h))],
        out_specs=pl.BlockSpec((None, s, dh), lambda bi, h: (bi, 0, h)),
        out_shape=jax.ShapeDtypeStruct((b, s, RET_WIDTH), BF16),
        scratch_shapes=[pltpu.VMEM((nchunk, dh, dh), F32)],
        compiler_params=_cparams(("parallel", "parallel")),
        name="retent",
    )(log_decay, proj, proj, proj, proj, gn_g.reshape(1, RET_WIDTH))


def _outproj_kernel(yna_ref, yr_ref, x_ref, ga_ref, sf_ref, shf_ref, wo1_ref, wo2_ref, g_ref, b_ref,
                    wrh_ref, wrl_ref, x1_ref, hfp_ref, lg_ref, *, alpha):
    nt = (((1,), (1,)), ((), ()))
    tm = x_ref.shape[0]
    parts = [pl.ds(p * (tm // OUTPROJ_PARTS), tm // OUTPROJ_PARTS) for p in range(OUTPROJ_PARTS)]
    def mix_of(r):
        return (jnp.dot(yna_ref[r, :], wo1_ref[...], preferred_element_type=F32)
                + jnp.dot(yr_ref[r, :], wo2_ref[...], preferred_element_type=F32))

    nxt = mix_of(parts[0])
    for p, r in enumerate(parts):
        mix = nxt
        if p + 1 < len(parts):
            nxt = mix_of(parts[p + 1])
        x1 = _layer_norm(alpha * x_ref[r, :] + ga_ref[...] * mix, g_ref[...], b_ref[...])
        x1_ref[r, :] = x1
        hf = x1 * (1.0 + sf_ref[...]) + shf_ref[...]
        for j, w in enumerate(_pack_rows(hf)):
            hfp_ref[j, r, :] = w
        hb = hf.astype(BF16)
        hl = (hf - hb.astype(F32)).astype(BF16)
        lg = lax.dot_general(wrh_ref[...], hb, nt, preferred_element_type=F32)
        lg = lg + lax.dot_general(wrh_ref[...], hl, nt, preferred_element_type=F32)
        lg = lg + lax.dot_general(wrl_ref[...], hb, nt, preferred_element_type=F32)
        lg_ref[:, r] = lg


def _outproj(y_na, y_r, x, mod, w_out_bf, ln_g, ln_b, wr_hi, wr_lo, alpha):
    b, s, d = x.shape
    tm = min(TM_PROJ, s)
    nt = s // tm
    ne = wr_hi.shape[0]
    const = lambda shape: pl.BlockSpec(shape, lambda bi, i: tuple(0 for _ in shape))
    x1, hfp, lg = pl.pallas_call(
        functools.partial(_outproj_kernel, alpha=alpha),
        grid=(b, nt),
        in_specs=[pl.BlockSpec((None, tm, NA_WIDTH), lambda bi, i: (bi, i, 0)),
                  pl.BlockSpec((None, tm, RET_WIDTH), lambda bi, i: (bi, i, 0)),
                  pl.BlockSpec((None, tm, d), lambda bi, i: (bi, i, 0)),
                  _mod_spec(2, d), _mod_spec(4, d), _mod_spec(3, d),
                  pl.BlockSpec((NA_WIDTH, d), lambda bi, i: (0, 0)),
                  pl.BlockSpec((RET_WIDTH, d), lambda bi, i: (1, 0)),
                  const((1, d)), const((1, d)), const((ne, d)), const((ne, d))],
        out_specs=[pl.BlockSpec((None, tm, d), lambda bi, i: (bi, i, 0)),
                   pl.BlockSpec((PACK_W, tm, LANES), lambda bi, i: (0, bi * nt + i, 0)),
                   pl.BlockSpec((ne, tm), lambda bi, i: (0, bi * nt + i))],
        out_shape=[jax.ShapeDtypeStruct((b, s, d), F32),
                   jax.ShapeDtypeStruct((PACK_W, b * s, LANES), I32),
                   jax.ShapeDtypeStruct((ne, b * s), F32)],
        compiler_params=_cparams(("parallel", "parallel")),
        name="outproj",
    )(y_na, y_r, x, mod, mod, mod, w_out_bf, w_out_bf, ln_g.reshape(1, d), ln_b.reshape(1, d), wr_hi, wr_lo)
    return x1, hfp, lg


def _route_kernel(lg_ref, rb_ref, idx_ref, w_ref, rank_ref, cnt_ref):
    t = lg_ref.shape[1]
    ninf = -jnp.inf

    @pl.when(pl.program_id(0) == 0)
    def _():
        cnt_ref[...] = jnp.zeros_like(cnt_ref)

    scores = jax.nn.sigmoid(lg_ref[...])
    sel = scores + rb_ref[...]

    io_g = lax.broadcasted_iota(I32, (GROUP_SIZE, t), 0)
    gs_rows = []
    for g in range(N_GROUPS):
        blk = sel[g * GROUP_SIZE:(g + 1) * GROUP_SIZE, :]
        m1 = jnp.max(blk, axis=0, keepdims=True)
        i1 = jnp.min(jnp.where(blk == m1, io_g, GROUP_SIZE), axis=0, keepdims=True)
        m2 = jnp.max(jnp.where(io_g == i1, ninf, blk), axis=0, keepdims=True)
        gs_rows.append(m1 + m2)
    gs = jnp.concatenate(gs_rows, axis=0)

    io8 = lax.broadcasted_iota(I32, (N_GROUPS, t), 0)
    gsel = jnp.zeros((N_GROUPS, t), F32)
    for _ in range(TOPK_GROUPS):
        m = jnp.max(gs, axis=0, keepdims=True)
        gi = jnp.min(jnp.where(gs == m, io8, N_GROUPS), axis=0, keepdims=True)
        hit = io8 == gi
        gsel = jnp.where(hit, 1.0, gsel)
        gs = jnp.where(hit, ninf, gs)

    masked = jnp.concatenate(
        [jnp.where(gsel[g:g + 1, :] > 0.0, sel[g * GROUP_SIZE:(g + 1) * GROUP_SIZE, :], ninf)
         for g in range(N_GROUPS)], axis=0)

    io_e = lax.broadcasted_iota(I32, (N_EXPERTS, t), 0)
    chosen = jnp.zeros((N_EXPERTS, t), F32)
    idx_rows, w_rows = [], []
    for _ in range(TOP_K):
        m = jnp.max(masked, axis=0, keepdims=True)
        ei = jnp.min(jnp.where(masked == m, io_e, N_EXPERTS), axis=0, keepdims=True)
        hit = io_e == ei
        w_rows.append(jnp.sum(jnp.where(hit, scores, 0.0), axis=0, keepdims=True))
        idx_rows.append(ei)
        chosen = jnp.where(hit, 1.0, chosen)
        masked = jnp.where(hit, ninf, masked)
    wk = jnp.concatenate(w_rows, axis=0)
    w_ref[...] = wk / jnp.sum(wk, axis=0, keepdims=True) * ROUTED_SCALE
    idx_ref[...] = jnp.concatenate(idx_rows, axis=0)

    upper = (lax.broadcasted_iota(I32, (t, t), 0) < lax.broadcasted_iota(I32, (t, t), 1))
    prefix = jnp.dot(chosen.astype(BF16), upper.astype(BF16), preferred_element_type=F32)
    rank_full = prefix + cnt_ref[...]
    rank_rows = [jnp.sum(jnp.where(io_e == ei, rank_full, 0.0), axis=0, keepdims=True) for ei in idx_rows]
    rank_ref[...] = jnp.concatenate(rank_rows, axis=0).astype(I32)
    cnt_ref[...] += jnp.sum(chosen, axis=1, keepdims=True)


def _route(logits_t, router_bias):
    ne, n = logits_t.shape
    t = min(T_ROUTE, n)
    kspec = pl.BlockSpec((TOP_K, t), lambda i: (0, i))
    return pl.pallas_call(
        _route_kernel,
        grid=(n // t,),
        in_specs=[pl.BlockSpec((ne, t), lambda i: (0, i)),
                  pl.BlockSpec((ne, 1), lambda i: (0, 0))],
        out_specs=[kspec, kspec, kspec, pl.BlockSpec((ne, 1), lambda i: (0, 0))],
        out_shape=[jax.ShapeDtypeStruct((TOP_K, n), I32),
                   jax.ShapeDtypeStruct((TOP_K, n), F32),
                   jax.ShapeDtypeStruct((TOP_K, n), I32),
                   jax.ShapeDtypeStruct((ne, 1), F32)],
        compiler_params=_cparams(("arbitrary",)),
        name="route",
    )(logits_t, router_bias.reshape(ne, 1))


def _block(ref, g):
    return ref.at[:, pl.ds(pl.multiple_of(g * BLK_E, BLK_E), BLK_E), :]


def _slots_kernel(idx_ref, rank_ref, ps_ref, o_ref, *, n_slots):
    t = idx_ref.shape[1]
    io = lax.broadcasted_iota(I32, (N_EXPERTS, t), 0)
    ps = ps_ref[...]
    for k in range(TOP_K):
        hit = io == idx_ref[k:k + 1, :]
        slot = jnp.sum(jnp.where(hit, ps, 0), axis=0, keepdims=True) + rank_ref[k:k + 1, :]
        for j in range(PACK_W):
            o_ref[j * TOP_K + k:j * TOP_K + k + 1, :] = slot + j * n_slots


def _slots(idx, rank, pstarts, n_slots):
    n = idx.shape[1]
    t = min(T_SLOT, n)
    kspec = pl.BlockSpec((TOP_K, t), lambda i: (0, i))
    return pl.pallas_call(
        functools.partial(_slots_kernel, n_slots=n_slots),
        grid=(n // t,),
        in_specs=[kspec, kspec, pl.BlockSpec((N_EXPERTS, 1), lambda i: (0, 0))],
        out_specs=pl.BlockSpec((PACK_W * TOP_K, t), lambda i: (0, i)),
        out_shape=jax.ShapeDtypeStruct((PACK_W * TOP_K, n), I32),
        compiler_params=_cparams(("parallel",)),
        name="slots",
    )(idx, rank, pstarts.reshape(N_EXPERTS, 1))


def _sc_mesh():
    return plsc.VectorSubcoreMesh(core_axis_name="core", subcore_axis_name="subcore")


def _scatter_rows_sc(rows, dest_rows, n_out):
    n_rows = rows.shape[0]
    wpp = n_rows // PACK_W // SC_WINDOW

    @functools.partial(pl.kernel, mesh=_sc_mesh(), scratch_types=[pltpu.SemaphoreType.DMA],
                       out_type=jax.ShapeDtypeStruct((n_out, LANES), I32))
    def scatter_rows(x_hbm, i_hbm, o_hbm, sem):
        def body(x_vmem, i_vmem):
            copies = [pltpu.async_copy(x_vmem, o_hbm.at[i_vmem.at[k]], sem) for k in range(TOP_K)]
            for c in copies:
                c.wait()

        pltpu.emit_pipeline(
            body,
            grid=(n_rows // SC_WINDOW,),
            in_specs=[pl.BlockSpec((SC_WINDOW, LANES), lambda i: (i, 0)),
                      pl.BlockSpec((TOP_K, SC_WINDOW), lambda i: (i // wpp, i % wpp))],
            out_specs=[],
            core_axis_name=("core", "subcore"),
            dimension_semantics=(pltpu.PARALLEL,),
        )(x_hbm, i_hbm)

    return scatter_rows(rows, dest_rows)


def _gather_rows_sc(rows, slot_rows, tok0, nc):
    nr = slot_rows.shape[0]
    span = SC_GATHER_WINDOWS * SC_WINDOW
    spr = nc // span
    w0 = tok0 // SC_WINDOW

    @functools.partial(pl.kernel, mesh=_sc_mesh(), scratch_types=[pltpu.SemaphoreType.DMA],
                       out_type=jax.ShapeDtypeStruct((nr * nc, LANES), I32))
    def gather_rows(x_hbm, i_hbm, o_hbm, sem):
        def body(*refs):
            o_vmem = refs[-1]
            copies = [pltpu.async_copy(x_hbm.at[i_vmem.at[0]], o_vmem.at[pl.ds(w * SC_WINDOW, SC_WINDOW)], sem)
                      for w, i_vmem in enumerate(refs[:-1])]
            for c in copies:
                c.wait()

        def idx_spec(w):
            return pl.BlockSpec((1, SC_WINDOW),
                                lambda i: (i // spr, w0 + (i % spr) * SC_GATHER_WINDOWS + w))

        pltpu.emit_pipeline(
            body,
            grid=(nr * spr,),
            in_specs=[idx_spec(w) for w in range(SC_GATHER_WINDOWS)],
            out_specs=[pl.BlockSpec((span, LANES), lambda i: (i, 0))],
            core_axis_name=("core", "subcore"),
            dimension_semantics=(pltpu.PARALLEL,),
        )(*([i_hbm] * SC_GATHER_WINDOWS), o_hbm)

    return gather_rows(rows, slot_rows)


def _experts_kernel(nblk_ref, gstart_ref, cnt_ref, nb_ref, xs_ref, wg_ref, wu_ref, wd_ref, ys_ref,
                    xbuf_ref, ybuf_ref, wgub_ref, wdb_ref, xsem, ysem):
    e = pl.program_id(0)
    total = nb_ref[0]
    nb_max = ys_ref.shape[1] // BLK_E

    def x_copy(g):
        slot = g % X_RING
        return pltpu.make_async_copy(_block(xs_ref, g), xbuf_ref.at[slot], xsem.at[slot])

    def y_copy(g):
        slot = g % Y_RING
        return pltpu.make_async_copy(ybuf_ref.at[slot], _block(ys_ref, g), ysem.at[slot])

    @pl.when(e == 0)
    def _():
        for g in range(X_AHEAD):
            @pl.when(g < total)
            def _():
                x_copy(g).start()

    @pl.when(nblk_ref[e] > 0)
    def _():
        f = wg_ref.shape[1]
        wgub_ref[:, :f] = wg_ref[...].astype(BF16)
        wgub_ref[:, f:] = wu_ref[...].astype(BF16)
        wdb_ref[...] = wd_ref[...].astype(BF16)

    def fetch(b):
        g = gstart_ref[e] + b
        x_copy(g).wait()

        @pl.when(g + X_AHEAD < total)
        def _():
            x_copy(g + X_AHEAD).start()

        xb = _unpack_words([xbuf_ref[g % X_RING, j] for j in range(PACK_W)]).astype(BF16)
        live = lax.broadcasted_iota(I32, (BLK_E, 1), 0) < cnt_ref[e] - b * BLK_E
        return jnp.where(live, xb, jnp.zeros((), BF16))

    def hidden(xb):
        f = wg_ref.shape[1]
        hgu = jnp.dot(xb, wgub_ref[...], preferred_element_type=F32)
        return (_silu(hgu[:, :f]) * hgu[:, f:]).astype(BF16)

    def finish(b, act):
        g = gstart_ref[e] + b
        words = _pack_rows(jnp.dot(act, wdb_ref[...], preferred_element_type=F32))

        @pl.when(g >= Y_RING)
        def _():
            y_copy(g - Y_RING).wait()

        for j, w in enumerate(words):
            ybuf_ref[g % Y_RING, j] = w
        y_copy(g).start()

    def block(b, carry):
        finish(b, hidden(fetch(b)))
        return carry

    lax.fori_loop(0, nblk_ref[e], block, 0)

    @pl.when(e == pl.num_programs(0) - 1)
    def _():
        for back in range(Y_RING):
            @pl.when(total - 1 - back >= 0)
            def _():
                y_copy(total - 1 - back).wait()

        ybuf_ref[0] = jnp.zeros(ybuf_ref.shape[1:], ybuf_ref.dtype)

        def tail_copy(g):
            return pltpu.make_async_copy(ybuf_ref.at[0], _block(ys_ref, g), ysem.at[0])

        def fill(g, carry):
            tail_copy(g).start()
            return carry

        lax.fori_loop(total, nb_max, fill, 0)

        def drain(g, carry):
            tail_copy(g).wait()
            return carry

        lax.fori_loop(total, nb_max, drain, 0)


def _experts(nblk, gstart, cnt, n_blocks, xs, w_gate, w_up, w_down):
    p = xs.shape[1]
    ne, d, f = w_gate.shape
    wspec = lambda shape: pl.BlockSpec((None,) + shape, lambda e, *_: (e, 0, 0))
    grid_spec = pltpu.PrefetchScalarGridSpec(
        num_scalar_prefetch=4,
        grid=(ne,),
        in_specs=[pl.BlockSpec(memory_space=pl.ANY), wspec((d, f)), wspec((d, f)), wspec((f, d))],
        out_specs=pl.BlockSpec(memory_space=pl.ANY),
        scratch_shapes=[pltpu.VMEM((X_RING, PACK_W, BLK_E, LANES), I32),
                        pltpu.VMEM((Y_RING, PACK_W, BLK_E, LANES), I32),
                        pltpu.VMEM((d, 2 * f), BF16), pltpu.VMEM((f, d), BF16),
                        pltpu.SemaphoreType.DMA((X_RING,)), pltpu.SemaphoreType.DMA((Y_RING,))],
    )
    return pl.pallas_call(
        _experts_kernel,
        grid_spec=grid_spec,
        out_shape=jax.ShapeDtypeStruct((PACK_W, p, LANES), I32),
        compiler_params=_cparams(("arbitrary",)),
        name="experts",
    )(nblk, gstart, cnt, n_blocks, xs, w_gate, w_up, w_down)


def _combine_kernel(wt_ref, yk_ref, x1_ref, sf_ref, shf_ref, gf_ref, wsg_ref, wsu_ref, wsd_ref, g_ref, b_ref,
                    *rest, alpha):
    o_ref = rest[-1]
    x1 = x1_ref[...]
    hb = (x1 * (1.0 + sf_ref[...]) + shf_ref[...]).astype(BF16)
    sg = jnp.dot(hb, wsg_ref[...], preferred_element_type=F32)
    su = jnp.dot(hb, wsu_ref[...], preferred_element_type=F32)
    ffn = jnp.dot((_silu(sg) * su).astype(BF16), wsd_ref[...], preferred_element_type=F32)
    wt = wt_ref[...].T
    for k in range(TOP_K):
        yk = _unpack_words([yk_ref[j, k] for j in range(PACK_W)])
        ffn = ffn + wt[:, k:k + 1] * yk
    o_ref[...] = _layer_norm(alpha * x1 + gf_ref[...] * ffn, g_ref[...], b_ref[...])


def _combine(wts, yk, x1, mod, ws_gate_bf, ws_up_bf, ws_down_bf, ln_g, ln_b, alpha, tok0, earlier):
    n, d = x1.shape
    nc = yk.shape[2]
    s = n // mod.shape[1]
    tc = min(T_COMB, s, nc)
    nt = s // tc
    t0 = tok0 // tc
    fs = ws_gate_bf.shape[1]
    const = lambda shape: pl.BlockSpec(shape, lambda i: tuple(0 for _ in shape))
    mod_spec = lambda which: pl.BlockSpec((None, None, 1, d), lambda i: (which, (t0 + i) // nt, 0, 0))
    in_specs = [pl.BlockSpec((TOP_K, tc), lambda i: (0, t0 + i)),
                pl.BlockSpec((PACK_W, TOP_K, tc, LANES), lambda i: (0, 0, i, 0)),
                pl.BlockSpec((tc, d), lambda i: (t0 + i, 0)),
                mod_spec(4), mod_spec(3), mod_spec(5),
                const((d, fs)), const((d, fs)), const((fs, d)), const((1, d)), const((1, d))]
    args = [wts, yk, x1, mod, mod, mod, ws_gate_bf, ws_up_bf, ws_down_bf, ln_g.reshape(1, d), ln_b.reshape(1, d)]
    aliases = {}
    if earlier is not None:
        in_specs.append(pl.BlockSpec(memory_space=pl.ANY))
        args.append(earlier)
        aliases = {len(args) - 1: 0}
    return pl.pallas_call(
        functools.partial(_combine_kernel, alpha=alpha),
        grid=(nc // tc,),
        in_specs=in_specs,
        out_specs=pl.BlockSpec((tc, d), lambda i: (t0 + i, 0)),
        out_shape=jax.ShapeDtypeStruct((n, d), F32),
        input_output_aliases=aliases,
        compiler_params=_cparams(("parallel",)),
        name="combine",
    )(*args)


def _slot_layout(counts, n_assign):
    cnt = counts[:, 0].astype(I32)
    padded = (cnt + BLK_E - 1) // BLK_E * BLK_E
    pends = jnp.cumsum(padded)
    pstarts = pends - padded
    n_blocks_max = (n_assign + N_EXPERTS * (BLK_E - 1)) // BLK_E
    n_blocks = (pends[-1] // BLK_E).astype(I32).reshape(1)
    return pstarts, padded // BLK_E, pstarts // BLK_E, n_blocks, n_blocks_max * BLK_E


def kernel(x, c, w_ada, b_ada, w_in, w_out, na_rpb, ret_log_decay, ret_gn_g, ln1_g, ln1_b, ln2_g, ln2_b,
           w_router, router_bias, w_gate, w_up, w_down, ws_gate, ws_up, ws_down):
    b, s, d = x.shape
    depth = w_ada.shape[0]
    alpha = (2.0 * depth) ** 0.25
    t = jnp.arange(s, dtype=F32)
    inv_freq = ROPE_BASE ** (-jnp.arange(0, RET_HEAD_DIM, 2, dtype=F32) / RET_HEAD_DIM)
    ang = t[:, None] * inv_freq[None, :]
    cos, sin = jnp.cos(ang), jnp.sin(ang)
    cos2 = jnp.concatenate([cos, cos], axis=-1)
    sin2 = jnp.concatenate([-sin, sin], axis=-1)
    for l in range(depth):
        mod = _mod(c, w_ada[l], b_ada[l])
        proj = _inproj(x, mod, w_in[l].astype(BF16), cos2, sin2)
        y_na = _natten(proj, _na_bias_table(na_rpb[l]))
        y_r = _retention(proj, ret_log_decay[l], ret_gn_g[l])
        wr_t = w_router[l].T
        wr_hi = wr_t.astype(BF16)
        wr_lo = (wr_t - wr_hi.astype(F32)).astype(BF16)
        x1, hfp, logits_t = _outproj(y_na, y_r, x, mod, w_out[l].astype(BF16), ln1_g[l], ln1_b[l],
                                     wr_hi, wr_lo, alpha)
        idx, wts, rank, counts = _route(logits_t, router_bias[l])
        n = b * s
        pstarts, nblk, gstart, n_blocks, n_slots = _slot_layout(counts, n * TOP_K)
        slot_rows = _slots(idx, rank, pstarts, n_slots)
        xs = _scatter_rows_sc(hfp.reshape(PACK_W * n, LANES), slot_rows, PACK_W * n_slots)
        ys = _experts(nblk, gstart, counts[:, 0].astype(I32), n_blocks, xs.reshape(PACK_W, n_slots, LANES),
                      w_gate[l], w_up[l], w_down[l]).reshape(PACK_W * n_slots, LANES)
        ws = (ws_gate[l].astype(BF16), ws_up[l].astype(BF16), ws_down[l].astype(BF16))
        x1n = x1.reshape(n, d)
        nc = n // COMBINE_CHUNKS
        out = None
        for ci in range(COMBINE_CHUNKS):
            yk = _gather_rows_sc(ys, slot_rows, ci * nc, nc).reshape(PACK_W, TOP_K, nc, LANES)
            out = _combine(wts, yk, x1n, mod, *ws, ln2_g[l], ln2_b[l], alpha, ci * nc, out)
        x = out.reshape(b, s, d)
    return x
```

```python
import functools
import math

import jax
import jax.numpy as jnp
from jax import lax
from jax.experimental import pallas as pl
from jax.experimental.pallas import tpu as pltpu
from jax.experimental.pallas import tpu_sc as plsc

F32 = jnp.float32
BF16 = jnp.bfloat16
U32 = jnp.uint32
I32 = jnp.int32

GRID_W = 64
WIN_R = 8
WIN_C = 16
NA_HEADS = 8
NA_HEAD_DIM = 64
NA_WIDTH = NA_HEADS * NA_HEAD_DIM
RET_HEADS = 4
RET_HEAD_DIM = 128
RET_WIDTH = RET_HEADS * RET_HEAD_DIM
ROPE_BASE = 10000.0
N_EXPERTS = 256
TOP_K = 8
N_GROUPS = 8
TOPK_GROUPS = 4
GROUP_SIZE = N_EXPERTS // N_GROUPS
ROUTED_SCALE = 2.5
LOG2_E = 1.4426950408889634
LN_EPS = 1e-5
GN_EPS = 1e-6

LANES = 128
VMEM_BYTES = 64 * 1024 * 1024
VMEM_LIMIT = VMEM_BYTES * 7 // 8

TM_INPROJ = 1024
TM_PROJ = 512
T_ROUTE = 512
T_SLOT = 1024
SC_WINDOW = 128
SC_GATHER_WINDOWS = 2
BLK_E = 512
OUTPROJ_PARTS = 2
T_COMB = 512
COMBINE_CHUNKS = 8
PACK_W = 4
NA_ROWS_PER_ITER = 32
RET_BLOCK = 256
RET_UNROLL = 8
X_RING = 4
X_AHEAD = 3
Y_RING = 3


def _cparams(sem, vmem=VMEM_LIMIT):
    return pltpu.CompilerParams(dimension_semantics=sem, vmem_limit_bytes=vmem)


def _silu(v):
    return v * jax.nn.sigmoid(v)


def _layer_norm(z, g, b):
    mu = jnp.mean(z, -1, keepdims=True)
    zc = z - mu
    var = jnp.mean(zc * zc, -1, keepdims=True)
    return zc * lax.rsqrt(var + LN_EPS) * g + b


def _pack_rows(v):
    half = v.shape[1] // 2
    vb = v.astype(BF16)
    lo = lax.bitcast_convert_type(vb[:, :half].astype(F32), U32) >> 16
    hi = lax.bitcast_convert_type(vb[:, half:].astype(F32), U32)
    w = lax.bitcast_convert_type(hi | lo, I32)
    return [w[:, j * LANES:(j + 1) * LANES] for j in range(half // LANES)]


def _unpack_words(words):
    words = [lax.bitcast_convert_type(w, U32) for w in words]
    lo = [lax.bitcast_convert_type(w << 16, F32) for w in words]
    hi = [lax.bitcast_convert_type(w & jnp.uint32(0xFFFF0000), F32) for w in words]
    return jnp.concatenate(lo + hi, axis=-1)


def _mod_kernel(c_ref, w_ref, b_ref, o_ref):
    cond = _silu(c_ref[...])
    o_ref[0] = jnp.dot(cond, w_ref[...], precision=lax.Precision.HIGHEST,
                       preferred_element_type=F32) + b_ref[0]


def _mod(c, w_ada, b_ada):
    b, d = c.shape
    n6 = w_ada.shape[1] // d
    out = pl.pallas_call(
        _mod_kernel,
        grid=(n6,),
        in_specs=[pl.BlockSpec((b, d), lambda j: (0, 0)),
                  pl.BlockSpec((d, d), lambda j: (0, j)),
                  pl.BlockSpec((1, 1, d), lambda j: (j, 0, 0))],
        out_specs=pl.BlockSpec((1, b, d), lambda j: (j, 0, 0)),
        out_shape=jax.ShapeDtypeStruct((n6, b, d), F32),
        compiler_params=_cparams(("arbitrary",)),
        name="mod",
    )(c, w_ada, b_ada.reshape(n6, 1, d))
    return out.reshape(n6, b, 1, d)


def _mod_spec(which, d):
    return pl.BlockSpec((None, None, 1, d), lambda b, i, which=which: (which, b, 0, 0))


def _inproj_kernel(x_ref, sc_ref, sh_ref, w_ref, cos_ref, sin_ref, o_ref, *, chunk, q_scale):
    h = (x_ref[...] * (1.0 + sc_ref[...]) + sh_ref[...]).astype(BF16)
    q_r, k_r = 3 * NA_WIDTH // chunk, 3 * NA_WIDTH // chunk + 1
    dh = RET_HEAD_DIM
    for j in range(o_ref.shape[1] // chunk):
        acc = jnp.dot(h, w_ref[:, j * chunk:(j + 1) * chunk], preferred_element_type=F32)
        if j == 0:
            acc = acc * q_scale
        if j in (q_r, k_r):
            cos2, sin2 = cos_ref[...], sin_ref[...]
            heads = [acc[:, hh * dh:(hh + 1) * dh] for hh in range(chunk // dh)]
            heads = [t * cos2 + pltpu.roll(t, dh // 2, 1) * sin2 for t in heads]
            acc = jnp.concatenate(heads, axis=1)
            if j == k_r:
                acc = acc * dh ** -0.5
        o_ref[:, j * chunk:(j + 1) * chunk] = acc.astype(o_ref.dtype)


def _inproj(x, mod, w_in_bf, cos2, sin2):
    b, s, d = x.shape
    e = w_in_bf.shape[1]
    tm = min(TM_INPROJ, s)
    rope = pl.BlockSpec((tm, RET_HEAD_DIM), lambda bi, i: (i, 0))
    return pl.pallas_call(
        functools.partial(_inproj_kernel, chunk=NA_WIDTH, q_scale=NA_HEAD_DIM ** -0.5 * LOG2_E),
        grid=(b, s // tm),
        in_specs=[pl.BlockSpec((None, tm, d), lambda bi, i: (bi, i, 0)),
                  _mod_spec(1, d), _mod_spec(0, d),
                  pl.BlockSpec((d, e), lambda bi, i: (0, 0)), rope, rope],
        out_specs=pl.BlockSpec((None, tm, e), lambda bi, i: (bi, i, 0)),
        out_shape=jax.ShapeDtypeStruct((b, s, e), BF16),
        compiler_params=_cparams(("parallel", "parallel")),
        name="inproj",
    )(x, mod, mod, w_in_bf, cos2, sin2)


def _natten_kernel(q_ref, k_ref, v_ref, bias_ref, o_ref, *, rows, rows_per_iter):
    kspan = WIN_R * GRID_W

    first = lax.broadcasted_iota(I32, (1, LANES), 1) < NA_HEAD_DIM
    zero = jnp.zeros((), BF16)

    def rows_body(i, carry):
        qrows, krows, scores, probs = {}, {}, {}, {}

        def stage_scores(u):
            r = i * rows_per_iter + u
            rs = jnp.clip(r - WIN_R // 2, 0, rows - WIN_R)
            vi = r - rs
            qrows[u] = pl.ds(pl.multiple_of(r * GRID_W, GRID_W), GRID_W)
            krows[u] = pl.ds(pl.multiple_of(rs * GRID_W, GRID_W), kspan)
            q = q_ref[qrows[u], :]
            qm = jnp.concatenate([jnp.where(first, q, zero), jnp.where(first, zero, q)], axis=0)
            st = lax.dot_general(k_ref[krows[u], :], qm, (((1,), (1,)), ((), ())), preferred_element_type=F32)
            scores[u] = st + bias_ref[vi]

        def stage_softmax(u):
            st = scores.pop(u)
            p = jnp.exp2(st - jnp.max(st, axis=0, keepdims=True))
            probs[u] = (p * (1.0 / jnp.sum(p, axis=0, keepdims=True))).astype(BF16)

        def stage_values(u):
            res = lax.dot_general(probs.pop(u), v_ref[krows[u], :], (((0,), (0,)), ((), ())),
                                  preferred_element_type=F32)
            o_ref[qrows[u], :] = jnp.where(first, res[:GRID_W], res[GRID_W:]).astype(o_ref.dtype)

        for step in range(rows_per_iter + 2):
            if step < rows_per_iter:
                stage_scores(step)
            if 0 <= step - 1 < rows_per_iter:
                stage_softmax(step - 1)
            if 0 <= step - 2 < rows_per_iter:
                stage_values(step - 2)
        return carry

    lax.fori_loop(0, rows // rows_per_iter, rows_body, 0)


def _na_bias_table(rpb):
    w = GRID_W
    cq = jnp.arange(w)
    cs = jnp.clip(cq - WIN_C // 2, 0, w - WIN_C)
    ck = jnp.arange(w)
    col_in = (ck[None, :] >= cs[:, None]) & (ck[None, :] < cs[:, None] + WIN_C)
    dc_idx = jnp.clip(ck[None, :] - cq[:, None] + WIN_C - 1, 0, 2 * WIN_C - 2)
    t = rpb[:, :, dc_idx]
    t = jnp.where(col_in[None, None], t, -jnp.inf)
    vi = jnp.arange(WIN_R)
    kr = jnp.arange(WIN_R)
    dr = kr[None, :] - vi[:, None] + WIN_R - 1
    tb = t[:, dr]
    hp = LANES // NA_HEAD_DIM
    tb = tb.reshape(rpb.shape[0] // hp, hp, WIN_R, WIN_R, w, w)
    tb = tb.transpose(0, 2, 3, 5, 1, 4).reshape(rpb.shape[0] // hp, WIN_R, WIN_R * w, hp * w)
    return tb.astype(F32) * LOG2_E


def _natten(proj, bias_tab):
    b, s, _ = proj.shape
    rows = s // GRID_W
    hp = LANES // NA_HEAD_DIM
    npair = NA_HEADS // hp
    blk = lambda off: pl.BlockSpec((None, s, LANES), lambda bi, p, off=off: (bi, 0, off + p))
    return pl.pallas_call(
        functools.partial(_natten_kernel, rows=rows, rows_per_iter=math.gcd(rows, NA_ROWS_PER_ITER)),
        grid=(b, npair),
        in_specs=[blk(0), blk(npair), blk(2 * npair),
                  pl.BlockSpec((None, WIN_R, WIN_R * GRID_W, hp * GRID_W), lambda bi, p: (p, 0, 0, 0))],
        out_specs=pl.BlockSpec((None, s, LANES), lambda bi, p: (bi, 0, p)),
        out_shape=jax.ShapeDtypeStruct((b, s, NA_WIDTH), BF16),
        compiler_params=_cparams(("parallel", "parallel")),
        name="natten",
    )(proj, proj, proj, bias_tab)


def _retent_kernel(ld_ref, q_ref, k_ref, v_ref, g_ref, gn_ref, o_ref, sb_ref, *, nchunk):
    c = RET_BLOCK
    dh = RET_HEAD_DIM
    h = pl.program_id(1)
    lgf = ld_ref[0, h]
    lgb = ld_ref[1, h]

    ic = lax.broadcasted_iota(I32, (c, 1), 0).astype(F32)
    ir = lax.broadcasted_iota(I32, (1, c), 1).astype(F32)
    diff = ic - ir
    dmat = jnp.where(diff >= 0, jnp.exp(jnp.maximum(diff, 0.0) * lgf),
                     jnp.exp(jnp.maximum(-diff, 0.0) * lgb))
    kdec_f = jnp.exp((c - 1 - ic) * lgf)
    qdec_f = jnp.exp((ic + 1) * lgf)
    kdec_b = jnp.exp(ic * lgb)
    qdec_b = jnp.exp((c - ic) * lgb)
    one = jnp.ones((1, 1), F32)
    cdec_f = jnp.exp(one * (c * lgf))
    cdec_b = jnp.exp(one * (c * lgb))
    tn = (((0,), (0,)), ((), ()))

    def bwd_body(i, sb):
        n = nchunk - 1 - i
        sb_ref[n] = sb
        rows = pl.ds(pl.multiple_of(n * c, c), c)
        kd = (k_ref[rows, :].astype(F32) * kdec_b).astype(BF16)
        kv = lax.dot_general(kd, v_ref[rows, :], tn, preferred_element_type=F32)
        return cdec_b * sb + kv

    lax.fori_loop(0, nchunk, bwd_body, jnp.zeros((dh, dh), F32), unroll=min(RET_UNROLL, nchunk))

    gn = gn_ref[...]

    def fwd_body(n, sf):
        rows = pl.ds(pl.multiple_of(n * c, c), c)
        qb = q_ref[rows, :]
        kb = k_ref[rows, :]
        qn = qb.astype(F32)
        kn = kb.astype(F32)
        vn = v_ref[rows, :]
        sc = lax.dot_general(qb, kb, (((1,), (1,)), ((), ())), preferred_element_type=F32) * dmat
        y = jnp.dot(sc.astype(BF16), vn, preferred_element_type=F32)
        qd = jnp.concatenate([qn * qdec_f, qn * qdec_b], axis=1).astype(BF16)
        st = jnp.concatenate([sf, sb_ref[n]], axis=0).astype(BF16)
        y = y + jnp.dot(qd, st, preferred_element_type=F32)
        mu = jnp.mean(y, -1, keepdims=True)
        yc = y - mu
        var = jnp.mean(yc * yc, -1, keepdims=True)
        yn = yc * lax.rsqrt(var + GN_EPS) * gn
        o_ref[rows, :] = (_silu(g_ref[rows, :].astype(F32)) * yn).astype(o_ref.dtype)
        kv = lax.dot_general((kn * kdec_f).astype(BF16), vn, tn, preferred_element_type=F32)
        return cdec_f * sf + kv

    lax.fori_loop(0, nchunk, fwd_body, jnp.zeros((dh, dh), F32), unroll=min(RET_UNROLL, nchunk))


def _retention(proj, log_decay, gn_g):
    b, s, _ = proj.shape
    dh = RET_HEAD_DIM
    nchunk = s // RET_BLOCK
    base = 3 * NA_WIDTH // dh
    blk = lambda off: pl.BlockSpec((None, s, dh), lambda bi, h, off=off: (bi, 0, base + off + h))
    return pl.pallas_call(
        functools.partial(_retent_kernel, nchunk=nchunk),
        grid=(b, RET_HEADS),
        in_specs=[pl.BlockSpec(memory_space=pltpu.SMEM),
                  blk(0), blk(RET_HEADS), blk(2 * RET_HEADS), blk(3 * RET_HEADS),
                  pl.BlockSpec((1, dh), lambda bi, h: (0, h))],
        out_specs=pl.BlockSpec((None, s, dh), lambda bi, h: (bi, 0, h)),
        out_shape=jax.ShapeDtypeStruct((b, s, RET_WIDTH), BF16),
        scratch_shapes=[pltpu.VMEM((nchunk, dh, dh), F32)],
        compiler_params=_cparams(("parallel", "parallel")),
        name="retent",
    )(log_decay, proj, proj, proj, proj, gn_g.reshape(1, RET_WIDTH))


def _outproj_kernel(yna_ref, yr_ref, x_ref, ga_ref, sf_ref, shf_ref, wo1_ref, wo2_ref, g_ref, b_ref,
                    wrh_ref, wrl_ref, x1_ref, hfp_ref, lg_ref, *, alpha):
    nt = (((1,), (1,)), ((), ()))
    tm = x_ref.shape[0]
    parts = [pl.ds(p * (tm // OUTPROJ_PARTS), tm // OUTPROJ_PARTS) for p in range(OUTPROJ_PARTS)]
    def mix_of(r):
        return (jnp.dot(yna_ref[r, :], wo1_ref[...], preferred_element_type=F32)
                + jnp.dot(yr_ref[r, :], wo2_ref[...], preferred_element_type=F32))

    nxt = mix_of(parts[0])
    for p, r in enumerate(parts):
        mix = nxt
        if p + 1 < len(parts):
            nxt = mix_of(parts[p + 1])
        x1 = _layer_norm(alpha * x_ref[r, :] + ga_ref[...] * mix, g_ref[...], b_ref[...])
        x1_ref[r, :] = x1
        hf = x1 * (1.0 + sf_ref[...]) + shf_ref[...]
        for j, w in enumerate(_pack_rows(hf)):
            hfp_ref[j, r, :] = w
        hb = hf.astype(BF16)
        hl = (hf - hb.astype(F32)).astype(BF16)
        lg = lax.dot_general(wrh_ref[...], hb, nt, preferred_element_type=F32)
        lg = lg + lax.dot_general(wrh_ref[...], hl, nt, preferred_element_type=F32)
        lg = lg + lax.dot_general(wrl_ref[...], hb, nt, preferred_element_type=F32)
        lg_ref[:, r] = lg


def _outproj(y_na, y_r, x, mod, w_out_bf, ln_g, ln_b, wr_hi, wr_lo, alpha):
    b, s, d = x.shape
    tm = min(TM_PROJ, s)
    nt = s // tm
    ne = wr_hi.shape[0]
    const = lambda shape: pl.BlockSpec(shape, lambda bi, i: tuple(0 for _ in shape))
    x1, hfp, lg = pl.pallas_call(
        functools.partial(_outproj_kernel, alpha=alpha),
        grid=(b, nt),
        in_specs=[pl.BlockSpec((None, tm, NA_WIDTH), lambda bi, i: (bi, i, 0)),
                  pl.BlockSpec((None, tm, RET_WIDTH), lambda bi, i: (bi, i, 0)),
                  pl.BlockSpec((None, tm, d), lambda bi, i: (bi, i, 0)),
                  _mod_spec(2, d), _mod_spec(4, d), _mod_spec(3, d),
                  pl.BlockSpec((NA_WIDTH, d), lambda bi, i: (0, 0)),
                  pl.BlockSpec((RET_WIDTH, d), lambda bi, i: (1, 0)),
                  const((1, d)), const((1, d)), const((ne, d)), const((ne, d))],
        out_specs=[pl.BlockSpec((None, tm, d), lambda bi, i: (bi, i, 0)),
                   pl.BlockSpec((PACK_W, tm, LANES), lambda bi, i: (0, bi * nt + i, 0)),
                   pl.BlockSpec((ne, tm), lambda bi, i: (0, bi * nt + i))],
        out_shape=[jax.ShapeDtypeStruct((b, s, d), F32),
                   jax.ShapeDtypeStruct((PACK_W, b * s, LANES), I32),
                   jax.ShapeDtypeStruct((ne, b * s), F32)],
        compiler_params=_cparams(("parallel", "parallel")),
        name="outproj",
    )(y_na, y_r, x, mod, mod, mod, w_out_bf, w_out_bf, ln_g.reshape(1, d), ln_b.reshape(1, d), wr_hi, wr_lo)
    return x1, hfp, lg


def _route_kernel(lg_ref, rb_ref, idx_ref, w_ref, rank_ref, cnt_ref):
    t = lg_ref.shape[1]
    ninf = -jnp.inf

    @pl.when(pl.program_id(0) == 0)
    def _():
        cnt_ref[...] = jnp.zeros_like(cnt_ref)

    scores = jax.nn.sigmoid(lg_ref[...])
    sel = scores + rb_ref[...]

    io_g = lax.broadcasted_iota(I32, (GROUP_SIZE, t), 0)
    gs_rows = []
    for g in range(N_GROUPS):
        blk = sel[g * GROUP_SIZE:(g + 1) * GROUP_SIZE, :]
        m1 = jnp.max(blk, axis=0, keepdims=True)
        i1 = jnp.min(jnp.where(blk == m1, io_g, GROUP_SIZE), axis=0, keepdims=True)
        m2 = jnp.max(jnp.where(io_g == i1, ninf, blk), axis=0, keepdims=True)
        gs_rows.append(m1 + m2)
    gs = jnp.concatenate(gs_rows, axis=0)

    io8 = lax.broadcasted_iota(I32, (N_GROUPS, t), 0)
    gsel = jnp.zeros((N_GROUPS, t), F32)
    for _ in range(TOPK_GROUPS):
        m = jnp.max(gs, axis=0, keepdims=True)
        gi = jnp.min(jnp.where(gs == m, io8, N_GROUPS), axis=0, keepdims=True)
        hit = io8 == gi
        gsel = jnp.where(hit, 1.0, gsel)
        gs = jnp.where(hit, ninf, gs)

    masked = jnp.concatenate(
        [jnp.where(gsel[g:g + 1, :] > 0.0, sel[g * GROUP_SIZE:(g + 1) * GROUP_SIZE, :], ninf)
         for g in range(N_GROUPS)], axis=0)

    io_e = lax.broadcasted_iota(I32, (N_EXPERTS, t), 0)
    chosen = jnp.zeros((N_EXPERTS, t), F32)
    idx_rows, w_rows = [], []
    for _ in range(TOP_K):
        m = jnp.max(masked, axis=0, keepdims=True)
        ei = jnp.min(jnp.where(masked == m, io_e, N_EXPERTS), axis=0, keepdims=True)
        hit = io_e == ei
        w_rows.append(jnp.sum(jnp.where(hit, scores, 0.0), axis=0, keepdims=True))
        idx_rows.append(ei)
        chosen = jnp.where(hit, 1.0, chosen)
        masked = jnp.where(hit, ninf, masked)
    wk = jnp.concatenate(w_rows, axis=0)
    w_ref[...] = wk / jnp.sum(wk, axis=0, keepdims=True) * ROUTED_SCALE
    idx_ref[...] = jnp.concatenate(idx_rows, axis=0)

    upper = (lax.broadcasted_iota(I32, (t, t), 0) < lax.broadcasted_iota(I32, (t, t), 1))
    prefix = jnp.dot(chosen.astype(BF16), upper.astype(BF16), preferred_element_type=F32)
    rank_full = prefix + cnt_ref[...]
    rank_rows = [jnp.sum(jnp.where(io_e == ei, rank_full, 0.0), axis=0, keepdims=True) for ei in idx_rows]
    rank_ref[...] = jnp.concatenate(rank_rows, axis=0).astype(I32)
    cnt_ref[...] += jnp.sum(chosen, axis=1, keepdims=True)


def _route(logits_t, router_bias):
    ne, n = logits_t.shape
    t = min(T_ROUTE, n)
    kspec = pl.BlockSpec((TOP_K, t), lambda i: (0, i))
    return pl.pallas_call(
        _route_kernel,
        grid=(n // t,),
        in_specs=[pl.BlockSpec((ne, t), lambda i: (0, i)),
                  pl.BlockSpec((ne, 1), lambda i: (0, 0))],
        out_specs=[kspec, kspec, kspec, pl.BlockSpec((ne, 1), lambda i: (0, 0))],
        out_shape=[jax.ShapeDtypeStruct((TOP_K, n), I32),
                   jax.ShapeDtypeStruct((TOP_K, n), F32),
                   jax.ShapeDtypeStruct((TOP_K, n), I32),
                   jax.ShapeDtypeStruct((ne, 1), F32)],
        compiler_params=_cparams(("arbitrary",)),
        name="route",
    )(logits_t, router_bias.reshape(ne, 1))


def _block(ref, g):
    return ref.at[:, pl.ds(pl.multiple_of(g * BLK_E, BLK_E), BLK_E), :]


def _slots_kernel(idx_ref, rank_ref, ps_ref, o_ref, *, n_slots):
    t = idx_ref.shape[1]
    io = lax.broadcasted_iota(I32, (N_EXPERTS, t), 0)
    ps = ps_ref[...]
    for k in range(TOP_K):
        hit = io == idx_ref[k:k + 1, :]
        slot = jnp.sum(jnp.where(hit, ps, 0), axis=0, keepdims=True) + rank_ref[k:k + 1, :]
        for j in range(PACK_W):
            o_ref[j * TOP_K + k:j * TOP_K + k + 1, :] = slot + j * n_slots


def _slots(idx, rank, pstarts, n_slots):
    n = idx.shape[1]
    t = min(T_SLOT, n)
    kspec = pl.BlockSpec((TOP_K, t), lambda i: (0, i))
    return pl.pallas_call(
        functools.partial(_slots_kernel, n_slots=n_slots),
        grid=(n // t,),
        in_specs=[kspec, kspec, pl.BlockSpec((N_EXPERTS, 1), lambda i: (0, 0))],
        out_specs=pl.BlockSpec((PACK_W * TOP_K, t), lambda i: (0, i)),
        out_shape=jax.ShapeDtypeStruct((PACK_W * TOP_K, n), I32),
        compiler_params=_cparams(("parallel",)),
        name="slots",
    )(idx, rank, pstarts.reshape(N_EXPERTS, 1))


def _sc_mesh():
    return plsc.VectorSubcoreMesh(core_axis_name="core", subcore_axis_name="subcore")


def _scatter_rows_sc(rows, dest_rows, n_out):
    n_rows = rows.shape[0]
    wpp = n_rows // PACK_W // SC_WINDOW

    @functools.partial(pl.kernel, mesh=_sc_mesh(), scratch_types=[pltpu.SemaphoreType.DMA],
                       out_type=jax.ShapeDtypeStruct((n_out, LANES), I32))
    def scatter_rows(x_hbm, i_hbm, o_hbm, sem):
        def body(x_vmem, i_vmem):
            copies = [pltpu.async_copy(x_vmem, o_hbm.at[i_vmem.at[k]], sem) for k in range(TOP_K)]
            for c in copies:
                c.wait()

        pltpu.emit_pipeline(
            body,
            grid=(n_rows // SC_WINDOW,),
            in_specs=[pl.BlockSpec((SC_WINDOW, LANES), lambda i: (i, 0)),
                      pl.BlockSpec((TOP_K, SC_WINDOW), lambda i: (i // wpp, i % wpp))],
            out_specs=[],
            core_axis_name=("core", "subcore"),
            dimension_semantics=(pltpu.PARALLEL,),
        )(x_hbm, i_hbm)

    return scatter_rows(rows, dest_rows)


def _gather_rows_sc(rows, slot_rows, tok0, nc):
    nr = slot_rows.shape[0]
    span = SC_GATHER_WINDOWS * SC_WINDOW
    spr = nc // span
    w0 = tok0 // SC_WINDOW

    @functools.partial(pl.kernel, mesh=_sc_mesh(), scratch_types=[pltpu.SemaphoreType.DMA],
                       out_type=jax.ShapeDtypeStruct((nr * nc, LANES), I32))
    def gather_rows(x_hbm, i_hbm, o_hbm, sem):
        def body(*refs):
            o_vmem = refs[-1]
            copies = [pltpu.async_copy(x_hbm.at[i_vmem.at[0]], o_vmem.at[pl.ds(w * SC_WINDOW, SC_WINDOW)], sem)
                      for w, i_vmem in enumerate(refs[:-1])]
            for c in copies:
                c.wait()

        def idx_spec(w):
            return pl.BlockSpec((1, SC_WINDOW),
                                lambda i: (i // spr, w0 + (i % spr) * SC_GATHER_WINDOWS + w))

        pltpu.emit_pipeline(
            body,
            grid=(nr * spr,),
            in_specs=[idx_spec(w) for w in range(SC_GATHER_WINDOWS)],
            out_specs=[pl.BlockSpec((span, LANES), lambda i: (i, 0))],
            core_axis_name=("core", "subcore"),
            dimension_semantics=(pltpu.PARALLEL,),
        )(*([i_hbm] * SC_GATHER_WINDOWS), o_hbm)

    return gather_rows(rows, slot_rows)


def _experts_kernel(nblk_ref, gstart_ref, cnt_ref, nb_ref, xs_ref, wg_ref, wu_ref, wd_ref, ys_ref,
                    xbuf_ref, ybuf_ref, wgub_ref, wdb_ref, xsem, ysem):
    e = pl.program_id(0)
    total = nb_ref[0]
    nb_max = ys_ref.shape[1] // BLK_E

    def x_copy(g):
        slot = g % X_RING
        return pltpu.make_async_copy(_block(xs_ref, g), xbuf_ref.at[slot], xsem.at[slot])

    def y_copy(g):
        slot = g % Y_RING
        return pltpu.make_async_copy(ybuf_ref.at[slot], _block(ys_ref, g), ysem.at[slot])

    @pl.when(e == 0)
    def _():
        for g in range(X_AHEAD):
            @pl.when(g < total)
            def _():
                x_copy(g).start()

    @pl.when(nblk_ref[e] > 0)
    def _():
        f = wg_ref.shape[1]
        wgub_ref[:, :f] = wg_ref[...].astype(BF16)
        wgub_ref[:, f:] = wu_ref[...].astype(BF16)
        wdb_ref[...] = wd_ref[...].astype(BF16)

    def fetch(b):
        g = gstart_ref[e] + b
        x_copy(g).wait()

        @pl.when(g + X_AHEAD < total)
        def _():
            x_copy(g + X_AHEAD).start()

        xb = _unpack_words([xbuf_ref[g % X_RING, j] for j in range(PACK_W)]).astype(BF16)
        live = lax.broadcasted_iota(I32, (BLK_E, 1), 0) < cnt_ref[e] - b * BLK_E
        return jnp.where(live, xb, jnp.zeros((), BF16))

    def hidden(xb):
        f = wg_ref.shape[1]
        hgu = jnp.dot(xb, wgub_ref[...], preferred_element_type=F32)
        return (_silu(hgu[:, :f]) * hgu[:, f:]).astype(BF16)

    def finish(b, act):
        g = gstart_ref[e] + b
        words = _pack_rows(jnp.dot(act, wdb_ref[...], preferred_element_type=F32))

        @pl.when(g >= Y_RING)
        def _():
            y_copy(g - Y_RING).wait()

        for j, w in enumerate(words):
            ybuf_ref[g % Y_RING, j] = w
        y_copy(g).start()

    def block(b, carry):
        finish(b, hidden(fetch(b)))
        return carry

    lax.fori_loop(0, nblk_ref[e], block, 0)

    @pl.when(e == pl.num_programs(0) - 1)
    def _():
        for back in range(Y_RING):
            @pl.when(total - 1 - back >= 0)
            def _():
                y_copy(total - 1 - back).wait()

        ybuf_ref[0] = jnp.zeros(ybuf_ref.shape[1:], ybuf_ref.dtype)

        def tail_copy(g):
            return pltpu.make_async_copy(ybuf_ref.at[0], _block(ys_ref, g), ysem.at[0])

        def fill(g, carry):
            tail_copy(g).start()
            return carry

        lax.fori_loop(total, nb_max, fill, 0)

        def drain(g, carry):
            tail_copy(g).wait()
            return carry

        lax.fori_loop(total, nb_max, drain, 0)


def _experts(nblk, gstart, cnt, n_blocks, xs, w_gate, w_up, w_down):
    p = xs.shape[1]
    ne, d, f = w_gate.shape
    wspec = lambda shape: pl.BlockSpec((None,) + shape, lambda e, *_: (e, 0, 0))
    grid_spec = pltpu.PrefetchScalarGridSpec(
        num_scalar_prefetch=4,
        grid=(ne,),
        in_specs=[pl.BlockSpec(memory_space=pl.ANY), wspec((d, f)), wspec((d, f)), wspec((f, d))],
        out_specs=pl.BlockSpec(memory_space=pl.ANY),
        scratch_shapes=[pltpu.VMEM((X_RING, PACK_W, BLK_E, LANES), I32),
                        pltpu.VMEM((Y_RING, PACK_W, BLK_E, LANES), I32),
                        pltpu.VMEM((d, 2 * f), BF16), pltpu.VMEM((f, d), BF16),
                        pltpu.SemaphoreType.DMA((X_RING,)), pltpu.SemaphoreType.DMA((Y_RING,))],
    )
    return pl.pallas_call(
        _experts_kernel,
        grid_spec=grid_spec,
        out_shape=jax.ShapeDtypeStruct((PACK_W, p, LANES), I32),
        compiler_params=_cparams(("arbitrary",)),
        name="experts",
    )(nblk, gstart, cnt, n_blocks, xs, w_gate, w_up, w_down)


def _combine_kernel(wt_ref, yk_ref, x1_ref, sf_ref, shf_ref, gf_ref, wsg_ref, wsu_ref, wsd_ref, g_ref, b_ref,
                    *rest, alpha):
    o_ref = rest[-1]
    x1 = x1_ref[...]
    hb = (x1 * (1.0 + sf_ref[...]) + shf_ref[...]).astype(BF16)
    sg = jnp.dot(hb, wsg_ref[...], preferred_element_type=F32)
    su = jnp.dot(hb, wsu_ref[...], preferred_element_type=F32)
    ffn = jnp.dot((_silu(sg) * su).astype(BF16), wsd_ref[...], preferred_element_type=F32)
    wt = wt_ref[...].T
    for k in range(TOP_K):
        yk = _unpack_words([yk_ref[j, k] for j in range(PACK_W)])
        ffn = ffn + wt[:, k:k + 1] * yk
    o_ref[...] = _layer_norm(alpha * x1 + gf_ref[...] * ffn, g_ref[...], b_ref[...])


def _combine(wts, yk, x1, mod, ws_gate_bf, ws_up_bf, ws_down_bf, ln_g, ln_b, alpha, tok0, earlier):
    n, d = x1.shape
    nc = yk.shape[2]
    s = n // mod.shape[1]
    tc = min(T_COMB, s, nc)
    nt = s // tc
    t0 = tok0 // tc
    fs = ws_gate_bf.shape[1]
    const = lambda shape: pl.BlockSpec(shape, lambda i: tuple(0 for _ in shape))
    mod_spec = lambda which: pl.BlockSpec((None, None, 1, d), lambda i: (which, (t0 + i) // nt, 0, 0))
    in_specs = [pl.BlockSpec((TOP_K, tc), lambda i: (0, t0 + i)),
                pl.BlockSpec((PACK_W, TOP_K, tc, LANES), lambda i: (0, 0, i, 0)),
                pl.BlockSpec((tc, d), lambda i: (t0 + i, 0)),
                mod_spec(4), mod_spec(3), mod_spec(5),
                const((d, fs)), const((d, fs)), const((fs, d)), const((1, d)), const((1, d))]
    args = [wts, yk, x1, mod, mod, mod, ws_gate_bf, ws_up_bf, ws_down_bf, ln_g.reshape(1, d), ln_b.reshape(1, d)]
    aliases = {}
    if earlier is not None:
        in_specs.append(pl.BlockSpec(memory_space=pl.ANY))
        args.append(earlier)
        aliases = {len(args) - 1: 0}
    return pl.pallas_call(
        functools.partial(_combine_kernel, alpha=alpha),
        grid=(nc // tc,),
        in_specs=in_specs,
        out_specs=pl.BlockSpec((tc, d), lambda i: (t0 + i, 0)),
        out_shape=jax.ShapeDtypeStruct((n, d), F32),
        input_output_aliases=aliases,
        compiler_params=_cparams(("parallel",)),
        name="combine",
    )(*args)


def _slot_layout(counts, n_assign):
    cnt = counts[:, 0].astype(I32)
    padded = (cnt + BLK_E - 1) // BLK_E * BLK_E
    pends = jnp.cumsum(padded)
    pstarts = pends - padded
    n_blocks_max = (n_assign + N_EXPERTS * (BLK_E - 1)) // BLK_E
    n_blocks = (pends[-1] // BLK_E).astype(I32).reshape(1)
    return pstarts, padded // BLK_E, pstarts // BLK_E, n_blocks, n_blocks_max * BLK_E


def _chunk_sizes(n):
    unit = max(n // (2 * COMBINE_CHUNKS), SC_GATHER_WINDOWS * SC_WINDOW)
    k = n // unit
    if k < 4:
        return [n]
    return [unit, unit] + [2 * unit] * ((k - 4) // 2) + [unit] * ((k - 4) % 2) + [unit, unit]


def kernel(x, c, w_ada, b_ada, w_in, w_out, na_rpb, ret_log_decay, ret_gn_g, ln1_g, ln1_b, ln2_g, ln2_b,
           w_router, router_bias, w_gate, w_up, w_down, ws_gate, ws_up, ws_down):
    b, s, d = x.shape
    depth = w_ada.shape[0]
    alpha = (2.0 * depth) ** 0.25
    t = jnp.arange(s, dtype=F32)
    inv_freq = ROPE_BASE ** (-jnp.arange(0, RET_HEAD_DIM, 2, dtype=F32) / RET_HEAD_DIM)
    ang = t[:, None] * inv_freq[None, :]
    cos, sin = jnp.cos(ang), jnp.sin(ang)
    cos2 = jnp.concatenate([cos, cos], axis=-1)
    sin2 = jnp.concatenate([-sin, sin], axis=-1)
    for l in range(depth):
        mod = _mod(c, w_ada[l], b_ada[l])
        proj = _inproj(x, mod, w_in[l].astype(BF16), cos2, sin2)
        y_na = _natten(proj, _na_bias_table(na_rpb[l]))
        y_r = _retention(proj, ret_log_decay[l], ret_gn_g[l])
        wr_t = w_router[l].T
        wr_hi = wr_t.astype(BF16)
        wr_lo = (wr_t - wr_hi.astype(F32)).astype(BF16)
        x1, hfp, logits_t = _outproj(y_na, y_r, x, mod, w_out[l].astype(BF16), ln1_g[l], ln1_b[l],
                                     wr_hi, wr_lo, alpha)
        idx, wts, rank, counts = _route(logits_t, router_bias[l])
        n = b * s
        pstarts, nblk, gstart, n_blocks, n_slots = _slot_layout(counts, n * TOP_K)
        slot_rows = _slots(idx, rank, pstarts, n_slots)
        xs = _scatter_rows_sc(hfp.reshape(PACK_W * n, LANES), slot_rows, PACK_W * n_slots)
        ys = _experts(nblk, gstart, counts[:, 0].astype(I32), n_blocks, xs.reshape(PACK_W, n_slots, LANES),
                      w_gate[l], w_up[l], w_down[l]).reshape(PACK_W * n_slots, LANES)
        ws = (ws_gate[l].astype(BF16), ws_up[l].astype(BF16), ws_down[l].astype(BF16))
        x1n = x1.reshape(n, d)
        out, tok0 = None, 0
        for nc in _chunk_sizes(n):
            yk = _gather_rows_sc(ys, slot_rows, tok0, nc).reshape(PACK_W, TOP_K, nc, LANES)
            out = _combine(wts, yk, x1n, mod, *ws, ln2_g[l], ln2_b[l], alpha, tok0, out)
            tok0 += nc
        x = out.reshape(b, s, d)
    return x
```

```python
import functools
import math

import jax
import jax.numpy as jnp
from jax import lax
from jax.experimental import pallas as pl
from jax.experimental.pallas import tpu as pltpu
from jax.experimental.pallas import tpu_sc as plsc

F32 = jnp.float32
BF16 = jnp.bfloat16
U32 = jnp.uint32
I32 = jnp.int32

GRID_W = 64
WIN_R = 8
WIN_C = 16
NA_HEADS = 8
NA_HEAD_DIM = 64
NA_WIDTH = NA_HEADS * NA_HEAD_DIM
RET_HEADS = 4
RET_HEAD_DIM = 128
RET_WIDTH = RET_HEADS * RET_HEAD_DIM
ROPE_BASE = 10000.0
N_EXPERTS = 256
TOP_K = 8
N_GROUPS = 8
TOPK_GROUPS = 4
GROUP_SIZE = N_EXPERTS // N_GROUPS
ROUTED_SCALE = 2.5
LOG2_E = 1.4426950408889634
LN_EPS = 1e-5
GN_EPS = 1e-6

LANES = 128
VMEM_BYTES = 64 * 1024 * 1024
VMEM_LIMIT = VMEM_BYTES * 7 // 8

TM_INPROJ = 1024
TM_PROJ = 512
T_ROUTE = 512
T_SLOT = 1024
SC_WINDOW = 128
SC_GATHER_WINDOWS = 2
BLK_E = 512
OUTPROJ_PARTS = 2
T_COMB = 512
COMBINE_CHUNKS = 8
PACK_W = 4
NA_ROWS_PER_ITER = 32
RET_BLOCK = 256
RET_UNROLL = 8
X_RING = 4
X_AHEAD = 3
Y_RING = 3


def _cparams(sem, vmem=VMEM_LIMIT):
    return pltpu.CompilerParams(dimension_semantics=sem, vmem_limit_bytes=vmem)


def _silu(v):
    return v * jax.nn.sigmoid(v)


def _layer_norm(z, g, b):
    mu = jnp.mean(z, -1, keepdims=True)
    zc = z - mu
    var = jnp.mean(zc * zc, -1, keepdims=True)
    return zc * lax.rsqrt(var + LN_EPS) * g + b


def _pack_rows(v):
    half = v.shape[1] // 2
    vb = v.astype(BF16)
    lo = lax.bitcast_convert_type(vb[:, :half].astype(F32), U32) >> 16
    hi = lax.bitcast_convert_type(vb[:, half:].astype(F32), U32)
    w = lax.bitcast_convert_type(hi | lo, I32)
    return [w[:, j * LANES:(j + 1) * LANES] for j in range(half // LANES)]


def _unpack_words(words):
    words = [lax.bitcast_convert_type(w, U32) for w in words]
    lo = [lax.bitcast_convert_type(w << 16, F32) for w in words]
    hi = [lax.bitcast_convert_type(w & jnp.uint32(0xFFFF0000), F32) for w in words]
    return jnp.concatenate(lo + hi, axis=-1)


def _mod_kernel(c_ref, w_ref, b_ref, o_ref):
    cond = _silu(c_ref[...])
    o_ref[0] = jnp.dot(cond, w_ref[...], precision=lax.Precision.HIGHEST,
                       preferred_element_type=F32) + b_ref[0]


def _mod(c, w_ada, b_ada):
    b, d = c.shape
    n6 = w_ada.shape[1] // d
    out = pl.pallas_call(
        _mod_kernel,
        grid=(n6,),
        in_specs=[pl.BlockSpec((b, d), lambda j: (0, 0)),
                  pl.BlockSpec((d, d), lambda j: (0, j)),
                  pl.BlockSpec((1, 1, d), lambda j: (j, 0, 0))],
        out_specs=pl.BlockSpec((1, b, d), lambda j: (j, 0, 0)),
        out_shape=jax.ShapeDtypeStruct((n6, b, d), F32),
        compiler_params=_cparams(("arbitrary",)),
        name="mod",
    )(c, w_ada, b_ada.reshape(n6, 1, d))
    return out.reshape(n6, b, 1, d)


def _mod_spec(which, d):
    return pl.BlockSpec((None, None, 1, d), lambda b, i, which=which: (which, b, 0, 0))


def _inproj_kernel(x_ref, sc_ref, sh_ref, w_ref, cos_ref, sin_ref, o_ref, *, chunk, q_scale):
    h = (x_ref[...] * (1.0 + sc_ref[...]) + sh_ref[...]).astype(BF16)
    q_r, k_r = 3 * NA_WIDTH // chunk, 3 * NA_WIDTH // chunk + 1
    dh = RET_HEAD_DIM
    ppc = chunk // LANES
    for j in range(o_ref.shape[0] // ppc):
        acc = jnp.dot(h, w_ref[:, j * chunk:(j + 1) * chunk], preferred_element_type=F32)
        if j == 0:
            acc = acc * q_scale
        if j in (q_r, k_r):
            cos2, sin2 = cos_ref[...], sin_ref[...]
            heads = [acc[:, hh * dh:(hh + 1) * dh] for hh in range(chunk // dh)]
            heads = [t * cos2 + pltpu.roll(t, dh // 2, 1) * sin2 for t in heads]
            acc = jnp.concatenate(heads, axis=1)
            if j == k_r:
                acc = acc * dh ** -0.5
        for p in range(ppc):
            o_ref[j * ppc + p] = acc[:, p * LANES:(p + 1) * LANES].astype(o_ref.dtype)


def _inproj(x, mod, w_in_bf, cos2, sin2):
    b, s, d = x.shape
    e = w_in_bf.shape[1]
    tm = min(TM_INPROJ, s)
    rope = pl.BlockSpec((tm, RET_HEAD_DIM), lambda bi, i: (i, 0))
    return pl.pallas_call(
        functools.partial(_inproj_kernel, chunk=NA_WIDTH, q_scale=NA_HEAD_DIM ** -0.5 * LOG2_E),
        grid=(b, s // tm),
        in_specs=[pl.BlockSpec((None, tm, d), lambda bi, i: (bi, i, 0)),
                  _mod_spec(1, d), _mod_spec(0, d),
                  pl.BlockSpec((d, e), lambda bi, i: (0, 0)), rope, rope],
        out_specs=pl.BlockSpec((None, e // LANES, tm, LANES), lambda bi, i: (bi, 0, i, 0)),
        out_shape=jax.ShapeDtypeStruct((b, e // LANES, s, LANES), BF16),
        compiler_params=_cparams(("parallel", "parallel")),
        name="inproj",
    )(x, mod, mod, w_in_bf, cos2, sin2)


def _natten_kernel(q_ref, k_ref, v_ref, bias_ref, o_ref, *, rows, rows_per_iter):
    kspan = WIN_R * GRID_W

    first = lax.broadcasted_iota(I32, (1, LANES), 1) < NA_HEAD_DIM
    zero = jnp.zeros((), BF16)

    def rows_body(i, carry):
        qrows, krows, scores, probs = {}, {}, {}, {}

        def stage_scores(u):
            r = i * rows_per_iter + u
            rs = jnp.clip(r - WIN_R // 2, 0, rows - WIN_R)
            vi = r - rs
            qrows[u] = pl.ds(pl.multiple_of(r * GRID_W, GRID_W), GRID_W)
            krows[u] = pl.ds(pl.multiple_of(rs * GRID_W, GRID_W), kspan)
            q = q_ref[qrows[u], :]
            qm = jnp.concatenate([jnp.where(first, q, zero), jnp.where(first, zero, q)], axis=0)
            st = lax.dot_general(k_ref[krows[u], :], qm, (((1,), (1,)), ((), ())), preferred_element_type=F32)
            scores[u] = st + bias_ref[vi]

        def stage_softmax(u):
            st = scores.pop(u)
            p = jnp.exp2(st - jnp.max(st, axis=0, keepdims=True))
            probs[u] = (p * (1.0 / jnp.sum(p, axis=0, keepdims=True))).astype(BF16)

        def stage_values(u):
            res = lax.dot_general(probs.pop(u), v_ref[krows[u], :], (((0,), (0,)), ((), ())),
                                  preferred_element_type=F32)
            o_ref[qrows[u], :] = jnp.where(first, res[:GRID_W], res[GRID_W:]).astype(o_ref.dtype)

        for step in range(rows_per_iter + 2):
            if step < rows_per_iter:
                stage_scores(step)
            if 0 <= step - 1 < rows_per_iter:
                stage_softmax(step - 1)
            if 0 <= step - 2 < rows_per_iter:
                stage_values(step - 2)
        return carry

    lax.fori_loop(0, rows // rows_per_iter, rows_body, 0)


def _na_bias_table(rpb):
    w = GRID_W
    cq = jnp.arange(w)
    cs = jnp.clip(cq - WIN_C // 2, 0, w - WIN_C)
    ck = jnp.arange(w)
    col_in = (ck[None, :] >= cs[:, None]) & (ck[None, :] < cs[:, None] + WIN_C)
    dc_idx = jnp.clip(ck[None, :] - cq[:, None] + WIN_C - 1, 0, 2 * WIN_C - 2)
    t = rpb[:, :, dc_idx]
    t = jnp.where(col_in[None, None], t, -jnp.inf)
    vi = jnp.arange(WIN_R)
    kr = jnp.arange(WIN_R)
    dr = kr[None, :] - vi[:, None] + WIN_R - 1
    tb = t[:, dr]
    hp = LANES // NA_HEAD_DIM
    tb = tb.reshape(rpb.shape[0] // hp, hp, WIN_R, WIN_R, w, w)
    tb = tb.transpose(0, 2, 3, 5, 1, 4).reshape(rpb.shape[0] // hp, WIN_R, WIN_R * w, hp * w)
    return tb.astype(F32) * LOG2_E


def _natten(proj, bias_tab):
    b, _, s, _ = proj.shape
    rows = s // GRID_W
    hp = LANES // NA_HEAD_DIM
    npair = NA_HEADS // hp
    blk = lambda off: pl.BlockSpec((None, None, s, LANES), lambda bi, p, off=off: (bi, off + p, 0, 0))
    return pl.pallas_call(
        functools.partial(_natten_kernel, rows=rows, rows_per_iter=math.gcd(rows, NA_ROWS_PER_ITER)),
        grid=(b, npair),
        in_specs=[blk(0), blk(npair), blk(2 * npair),
                  pl.BlockSpec((None, WIN_R, WIN_R * GRID_W, hp * GRID_W), lambda bi, p: (p, 0, 0, 0))],
        out_specs=pl.BlockSpec((None, s, LANES), lambda bi, p: (bi, 0, p)),
        out_shape=jax.ShapeDtypeStruct((b, s, NA_WIDTH), BF16),
        compiler_params=_cparams(("parallel", "parallel")),
        name="natten",
    )(proj, proj, proj, bias_tab)


def _retent_kernel(ld_ref, q_ref, k_ref, v_ref, g_ref, gn_ref, o_ref, sb_ref, *, nchunk):
    c = RET_BLOCK
    dh = RET_HEAD_DIM
    h = pl.program_id(1)
    lgf = ld_ref[0, h]
    lgb = ld_ref[1, h]

    ic = lax.broadcasted_iota(I32, (c, 1), 0).astype(F32)
    ir = lax.broadcasted_iota(I32, (1, c), 1).astype(F32)
    diff = ic - ir
    dmat = jnp.where(diff >= 0, jnp.exp(jnp.maximum(diff, 0.0) * lgf),
                     jnp.exp(jnp.maximum(-diff, 0.0) * lgb))
    kdec_f = jnp.exp((c - 1 - ic) * lgf)
    qdec_f = jnp.exp((ic + 1) * lgf)
    kdec_b = jnp.exp(ic * lgb)
    qdec_b = jnp.exp((c - ic) * lgb)
    one = jnp.ones((1, 1), F32)
    cdec_f = jnp.exp(one * (c * lgf))
    cdec_b = jnp.exp(one * (c * lgb))
    tn = (((0,), (0,)), ((), ()))

    def bwd_body(i, sb):
        n = nchunk - 1 - i
        sb_ref[n] = sb
        rows = pl.ds(pl.multiple_of(n * c, c), c)
        kd = (k_ref[rows, :].astype(F32) * kdec_b).astype(BF16)
        kv = lax.dot_general(kd, v_ref[rows, :], tn, preferred_element_type=F32)
        return cdec_b * sb + kv

    lax.fori_loop(0, nchunk, bwd_body, jnp.zeros((dh, dh), F32), unroll=min(RET_UNROLL, nchunk))

    gn = gn_ref[...]

    def fwd_body(n, sf):
        rows = pl.ds(pl.multiple_of(n * c, c), c)
        qb = q_ref[rows, :]
        kb = k_ref[rows, :]
        qn = qb.astype(F32)
        kn = kb.astype(F32)
        vn = v_ref[rows, :]
        sc = lax.dot_general(qb, kb, (((1,), (1,)), ((), ())), preferred_element_type=F32) * dmat
        y = jnp.dot(sc.astype(BF16), vn, preferred_element_type=F32)
        qd = jnp.concatenate([qn * qdec_f, qn * qdec_b], axis=1).astype(BF16)
        st = jnp.concatenate([sf, sb_ref[n]], axis=0).astype(BF16)
        y = y + jnp.dot(qd, st, preferred_element_type=F32)
        mu = jnp.mean(y, -1, keepdims=True)
        yc = y - mu
        var = jnp.mean(yc * yc, -1, keepdims=True)
        yn = yc * lax.rsqrt(var + GN_EPS) * gn
        o_ref[rows, :] = (_silu(g_ref[rows, :].astype(F32)) * yn).astype(o_ref.dtype)
        kv = lax.dot_general((kn * kdec_f).astype(BF16), vn, tn, preferred_element_type=F32)
        return cdec_f * sf + kv

    lax.fori_loop(0, nchunk, fwd_body, jnp.zeros((dh, dh), F32), unroll=min(RET_UNROLL, nchunk))


def _retention(proj, log_decay, gn_g):
    b, _, s, _ = proj.shape
    dh = RET_HEAD_DIM
    nchunk = s // RET_BLOCK
    base = 3 * NA_WIDTH // dh
    blk = lambda off: pl.BlockSpec((None, None, s, dh), lambda bi, h, off=off: (bi, base + off + h, 0, 0))
    return pl.pallas_call(
        functools.partial(_retent_kernel, nchunk=nchunk),
        grid=(b, RET_HEADS),
        in_specs=[pl.BlockSpec(memory_space=pltpu.SMEM),
                  blk(0), blk(RET_HEADS), blk(2 * RET_HEADS), blk(3 * RET_HEADS),
                  pl.BlockSpec((1, dh), lambda bi, h: (0, h))],
        out_specs=pl.BlockSpec((None, s, dh), lambda bi, h: (bi, 0, h)),
        out_shape=jax.ShapeDtypeStruct((b, s, RET_WIDTH), BF16),
        scratch_shapes=[pltpu.VMEM((nchunk, dh, dh), F32)],
        compiler_params=_cparams(("parallel", "parallel")),
        name="retent",
    )(log_decay, proj, proj, proj, proj, gn_g.reshape(1, RET_WIDTH))


def _outproj_kernel(yna_ref, yr_ref, x_ref, ga_ref, sf_ref, shf_ref, wo1_ref, wo2_ref, g_ref, b_ref,
                    wrh_ref, wrl_ref, x1_ref, hfp_ref, lg_ref, *, alpha):
    nt = (((1,), (1,)), ((), ()))
    tm = x_ref.shape[0]
    parts = [pl.ds(p * (tm // OUTPROJ_PARTS), tm // OUTPROJ_PARTS) for p in range(OUTPROJ_PARTS)]
    def mix_of(r):
        return (jnp.dot(yna_ref[r, :], wo1_ref[...], preferred_element_type=F32)
                + jnp.dot(yr_ref[r, :], wo2_ref[...], preferred_element_type=F32))

    nxt = mix_of(parts[0])
    for p, r in enumerate(parts):
        mix = nxt
        if p + 1 < len(parts):
            nxt = mix_of(parts[p + 1])
        x1 = _layer_norm(alpha * x_ref[r, :] + ga_ref[...] * mix, g_ref[...], b_ref[...])
        x1_ref[r, :] = x1
        hf = x1 * (1.0 + sf_ref[...]) + shf_ref[...]
        for j, w in enumerate(_pack_rows(hf)):
            hfp_ref[j, r, :] = w
        hb = hf.astype(BF16)
        hl = (hf - hb.astype(F32)).astype(BF16)
        lg = lax.dot_general(wrh_ref[...], hb, nt, preferred_element_type=F32)
        lg = lg + lax.dot_general(wrh_ref[...], hl, nt, preferred_element_type=F32)
        lg = lg + lax.dot_general(wrl_ref[...], hb, nt, preferred_element_type=F32)
        lg_ref[:, r] = lg


def _outproj(y_na, y_r, x, mod, w_out_bf, ln_g, ln_b, wr_hi, wr_lo, alpha):
    b, s, d = x.shape
    tm = min(TM_PROJ, s)
    nt = s // tm
    ne = wr_hi.shape[0]
    const = lambda shape: pl.BlockSpec(shape, lambda bi, i: tuple(0 for _ in shape))
    x1, hfp, lg = pl.pallas_call(
        functools.partial(_outproj_kernel, alpha=alpha),
        grid=(b, nt),
        in_specs=[pl.BlockSpec((None, tm, NA_WIDTH), lambda bi, i: (bi, i, 0)),
                  pl.BlockSpec((None, tm, RET_WIDTH), lambda bi, i: (bi, i, 0)),
                  pl.BlockSpec((None, tm, d), lambda bi, i: (bi, i, 0)),
                  _mod_spec(2, d), _mod_spec(4, d), _mod_spec(3, d),
                  pl.BlockSpec((NA_WIDTH, d), lambda bi, i: (0, 0)),
                  pl.BlockSpec((RET_WIDTH, d), lambda bi, i: (1, 0)),
                  const((1, d)), const((1, d)), const((ne, d)), const((ne, d))],
        out_specs=[pl.BlockSpec((None, tm, d), lambda bi, i: (bi, i, 0)),
                   pl.BlockSpec((PACK_W, tm, LANES), lambda bi, i: (0, bi * nt + i, 0)),
                   pl.BlockSpec((ne, tm), lambda bi, i: (0, bi * nt + i))],
        out_shape=[jax.ShapeDtypeStruct((b, s, d), F32),
                   jax.ShapeDtypeStruct((PACK_W, b * s, LANES), I32),
                   jax.ShapeDtypeStruct((ne, b * s), F32)],
        compiler_params=_cparams(("parallel", "parallel")),
        name="outproj",
    )(y_na, y_r, x, mod, mod, mod, w_out_bf, w_out_bf, ln_g.reshape(1, d), ln_b.reshape(1, d), wr_hi, wr_lo)
    return x1, hfp, lg


def _route_kernel(lg_ref, rb_ref, idx_ref, w_ref, rank_ref, cnt_ref):
    t = lg_ref.shape[1]
    ninf = -jnp.inf

    @pl.when(pl.program_id(0) == 0)
    def _():
        cnt_ref[...] = jnp.zeros_like(cnt_ref)

    scores = jax.nn.sigmoid(lg_ref[...])
    sel = scores + rb_ref[...]

    io_g = lax.broadcasted_iota(I32, (GROUP_SIZE, t), 0)
    gs_rows = []
    for g in range(N_GROUPS):
        blk = sel[g * GROUP_SIZE:(g + 1) * GROUP_SIZE, :]
        m1 = jnp.max(blk, axis=0, keepdims=True)
        i1 = jnp.min(jnp.where(blk == m1, io_g, GROUP_SIZE), axis=0, keepdims=True)
        m2 = jnp.max(jnp.where(io_g == i1, ninf, blk), axis=0, keepdims=True)
        gs_rows.append(m1 + m2)
    gs = jnp.concatenate(gs_rows, axis=0)

    io8 = lax.broadcasted_iota(I32, (N_GROUPS, t), 0)
    gsel = jnp.zeros((N_GROUPS, t), F32)
    for _ in range(TOPK_GROUPS):
        m = jnp.max(gs, axis=0, keepdims=True)
        gi = jnp.min(jnp.where(gs == m, io8, N_GROUPS), axis=0, keepdims=True)
        hit = io8 == gi
        gsel = jnp.where(hit, 1.0, gsel)
        gs = jnp.where(hit, ninf, gs)

    masked = jnp.concatenate(
        [jnp.where(gsel[g:g + 1, :] > 0.0, sel[g * GROUP_SIZE:(g + 1) * GROUP_SIZE, :], ninf)
         for g in range(N_GROUPS)], axis=0)

    io_e = lax.broadcasted_iota(I32, (N_EXPERTS, t), 0)
    chosen = jnp.zeros((N_EXPERTS, t), F32)
    idx_rows, w_rows = [], []
    for _ in range(TOP_K):
        m = jnp.max(masked, axis=0, keepdims=True)
        ei = jnp.min(jnp.where(masked == m, io_e, N_EXPERTS), axis=0, keepdims=True)
        hit = io_e == ei
        w_rows.append(jnp.sum(jnp.where(hit, scores, 0.0), axis=0, keepdims=True))
        idx_rows.append(ei)
        chosen = jnp.where(hit, 1.0, chosen)
        masked = jnp.where(hit, ninf, masked)
    wk = jnp.concatenate(w_rows, axis=0)
    w_ref[...] = wk / jnp.sum(wk, axis=0, keepdims=True) * ROUTED_SCALE
    idx_ref[...] = jnp.concatenate(idx_rows, axis=0)

    upper = (lax.broadcasted_iota(I32, (t, t), 0) < lax.broadcasted_iota(I32, (t, t), 1))
    prefix = jnp.dot(chosen.astype(BF16), upper.astype(BF16), preferred_element_type=F32)
    rank_full = prefix + cnt_ref[...]
    rank_rows = [jnp.sum(jnp.where(io_e == ei, rank_full, 0.0), axis=0, keepdims=True) for ei in idx_rows]
    rank_ref[...] = jnp.concatenate(rank_rows, axis=0).astype(I32)
    cnt_ref[...] += jnp.sum(chosen, axis=1, keepdims=True)


def _route(logits_t, router_bias):
    ne, n = logits_t.shape
    t = min(T_ROUTE, n)
    kspec = pl.BlockSpec((TOP_K, t), lambda i: (0, i))
    return pl.pallas_call(
        _route_kernel,
        grid=(n // t,),
        in_specs=[pl.BlockSpec((ne, t), lambda i: (0, i)),
                  pl.BlockSpec((ne, 1), lambda i: (0, 0))],
        out_specs=[kspec, kspec, kspec, pl.BlockSpec((ne, 1), lambda i: (0, 0))],
        out_shape=[jax.ShapeDtypeStruct((TOP_K, n), I32),
                   jax.ShapeDtypeStruct((TOP_K, n), F32),
                   jax.ShapeDtypeStruct((TOP_K, n), I32),
                   jax.ShapeDtypeStruct((ne, 1), F32)],
        compiler_params=_cparams(("arbitrary",)),
        name="route",
    )(logits_t, router_bias.reshape(ne, 1))


def _block(ref, g):
    return ref.at[:, pl.ds(pl.multiple_of(g * BLK_E, BLK_E), BLK_E), :]


def _slots_kernel(idx_ref, rank_ref, ps_ref, o_ref, *, n_slots):
    t = idx_ref.shape[1]
    io = lax.broadcasted_iota(I32, (N_EXPERTS, t), 0)
    ps = ps_ref[...]
    for k in range(TOP_K):
        hit = io == idx_ref[k:k + 1, :]
        slot = jnp.sum(jnp.where(hit, ps, 0), axis=0, keepdims=True) + rank_ref[k:k + 1, :]
        for j in range(PACK_W):
            o_ref[j * TOP_K + k:j * TOP_K + k + 1, :] = slot + j * n_slots


def _slots(idx, rank, pstarts, n_slots):
    n = idx.shape[1]
    t = min(T_SLOT, n)
    kspec = pl.BlockSpec((TOP_K, t), lambda i: (0, i))
    return pl.pallas_call(
        functools.partial(_slots_kernel, n_slots=n_slots),
        grid=(n // t,),
        in_specs=[kspec, kspec, pl.BlockSpec((N_EXPERTS, 1), lambda i: (0, 0))],
        out_specs=pl.BlockSpec((PACK_W * TOP_K, t), lambda i: (0, i)),
        out_shape=jax.ShapeDtypeStruct((PACK_W * TOP_K, n), I32),
        compiler_params=_cparams(("parallel",)),
        name="slots",
    )(idx, rank, pstarts.reshape(N_EXPERTS, 1))


def _sc_mesh():
    return plsc.VectorSubcoreMesh(core_axis_name="core", subcore_axis_name="subcore")


def _scatter_rows_sc(rows, dest_rows, n_out):
    n_rows = rows.shape[0]
    wpp = n_rows // PACK_W // SC_WINDOW

    @functools.partial(pl.kernel, mesh=_sc_mesh(), scratch_types=[pltpu.SemaphoreType.DMA],
                       out_type=jax.ShapeDtypeStruct((n_out, LANES), I32))
    def scatter_rows(x_hbm, i_hbm, o_hbm, sem):
        def body(x_vmem, i_vmem):
            copies = [pltpu.async_copy(x_vmem, o_hbm.at[i_vmem.at[k]], sem) for k in range(TOP_K)]
            for c in copies:
                c.wait()

        pltpu.emit_pipeline(
            body,
            grid=(n_rows // SC_WINDOW,),
            in_specs=[pl.BlockSpec((SC_WINDOW, LANES), lambda i: (i, 0)),
                      pl.BlockSpec((TOP_K, SC_WINDOW), lambda i: (i // wpp, i % wpp))],
            out_specs=[],
            core_axis_name=("core", "subcore"),
            dimension_semantics=(pltpu.PARALLEL,),
        )(x_hbm, i_hbm)

    return scatter_rows(rows, dest_rows)


def _gather_rows_sc(rows, slot_rows, tok0, nc):
    nr = slot_rows.shape[0]
    span = SC_GATHER_WINDOWS * SC_WINDOW
    spr = nc // span
    w0 = tok0 // SC_WINDOW

    @functools.partial(pl.kernel, mesh=_sc_mesh(), scratch_types=[pltpu.SemaphoreType.DMA],
                       out_type=jax.ShapeDtypeStruct((nr * nc, LANES), I32))
    def gather_rows(x_hbm, i_hbm, o_hbm, sem):
        def body(*refs):
            o_vmem = refs[-1]
            copies = [pltpu.async_copy(x_hbm.at[i_vmem.at[0]], o_vmem.at[pl.ds(w * SC_WINDOW, SC_WINDOW)], sem)
                      for w, i_vmem in enumerate(refs[:-1])]
            for c in copies:
                c.wait()

        def idx_spec(w):
            return pl.BlockSpec((1, SC_WINDOW),
                                lambda i: (i // spr, w0 + (i % spr) * SC_GATHER_WINDOWS + w))

        pltpu.emit_pipeline(
            body,
            grid=(nr * spr,),
            in_specs=[idx_spec(w) for w in range(SC_GATHER_WINDOWS)],
            out_specs=[pl.BlockSpec((span, LANES), lambda i: (i, 0))],
            core_axis_name=("core", "subcore"),
            dimension_semantics=(pltpu.PARALLEL,),
        )(*([i_hbm] * SC_GATHER_WINDOWS), o_hbm)

    return gather_rows(rows, slot_rows)


def _experts_kernel(nblk_ref, gstart_ref, cnt_ref, nb_ref, xs_ref, wg_ref, wu_ref, wd_ref, ys_ref,
                    xbuf_ref, ybuf_ref, wgub_ref, wdb_ref, xsem, ysem):
    e = pl.program_id(0)
    total = nb_ref[0]
    nb_max = ys_ref.shape[1] // BLK_E

    def x_copy(g):
        slot = g % X_RING
        return pltpu.make_async_copy(_block(xs_ref, g), xbuf_ref.at[slot], xsem.at[slot])

    def y_copy(g):
        slot = g % Y_RING
        return pltpu.make_async_copy(ybuf_ref.at[slot], _block(ys_ref, g), ysem.at[slot])

    @pl.when(e == 0)
    def _():
        for g in range(X_AHEAD):
            @pl.when(g < total)
            def _():
                x_copy(g).start()

    @pl.when(nblk_ref[e] > 0)
    def _():
        f = wg_ref.shape[1]
        wgub_ref[:, :f] = wg_ref[...].astype(BF16)
        wgub_ref[:, f:] = wu_ref[...].astype(BF16)
        wdb_ref[...] = wd_ref[...].astype(BF16)

    def fetch(b):
        g = gstart_ref[e] + b
        x_copy(g).wait()

        @pl.when(g + X_AHEAD < total)
        def _():
            x_copy(g + X_AHEAD).start()

        xb = _unpack_words([xbuf_ref[g % X_RING, j] for j in range(PACK_W)]).astype(BF16)
        live = lax.broadcasted_iota(I32, (BLK_E, 1), 0) < cnt_ref[e] - b * BLK_E
        return jnp.where(live, xb, jnp.zeros((), BF16))

    def hidden(xb):
        f = wg_ref.shape[1]
        hgu = jnp.dot(xb, wgub_ref[...], preferred_element_type=F32)
        return (_silu(hgu[:, :f]) * hgu[:, f:]).astype(BF16)

    def finish(b, act):
        g = gstart_ref[e] + b
        words = _pack_rows(jnp.dot(act, wdb_ref[...], preferred_element_type=F32))

        @pl.when(g >= Y_RING)
        def _():
            y_copy(g - Y_RING).wait()

        for j, w in enumerate(words):
            ybuf_ref[g % Y_RING, j] = w
        y_copy(g).start()

    def block(b, carry):
        finish(b, hidden(fetch(b)))
        return carry

    lax.fori_loop(0, nblk_ref[e], block, 0)

    @pl.when(e == pl.num_programs(0) - 1)
    def _():
        for back in range(Y_RING):
            @pl.when(total - 1 - back >= 0)
            def _():
                y_copy(total - 1 - back).wait()

        ybuf_ref[0] = jnp.zeros(ybuf_ref.shape[1:], ybuf_ref.dtype)

        def tail_copy(g):
            return pltpu.make_async_copy(ybuf_ref.at[0], _block(ys_ref, g), ysem.at[0])

        def fill(g, carry):
            tail_copy(g).start()
            return carry

        lax.fori_loop(total, nb_max, fill, 0)

        def drain(g, carry):
            tail_copy(g).wait()
            return carry

        lax.fori_loop(total, nb_max, drain, 0)


def _experts(nblk, gstart, cnt, n_blocks, xs, w_gate, w_up, w_down):
    p = xs.shape[1]
    ne, d, f = w_gate.shape
    wspec = lambda shape: pl.BlockSpec((None,) + shape, lambda e, *_: (e, 0, 0))
    grid_spec = pltpu.PrefetchScalarGridSpec(
        num_scalar_prefetch=4,
        grid=(ne,),
        in_specs=[pl.BlockSpec(memory_space=pl.ANY), wspec((d, f)), wspec((d, f)), wspec((f, d))],
        out_specs=pl.BlockSpec(memory_space=pl.ANY),
        scratch_shapes=[pltpu.VMEM((X_RING, PACK_W, BLK_E, LANES), I32),
                        pltpu.VMEM((Y_RING, PACK_W, BLK_E, LANES), I32),
                        pltpu.VMEM((d, 2 * f), BF16), pltpu.VMEM((f, d), BF16),
                        pltpu.SemaphoreType.DMA((X_RING,)), pltpu.SemaphoreType.DMA((Y_RING,))],
    )
    return pl.pallas_call(
        _experts_kernel,
        grid_spec=grid_spec,
        out_shape=jax.ShapeDtypeStruct((PACK_W, p, LANES), I32),
        compiler_params=_cparams(("arbitrary",)),
        name="experts",
    )(nblk, gstart, cnt, n_blocks, xs, w_gate, w_up, w_down)


def _combine_kernel(wt_ref, yk_ref, x1_ref, sf_ref, shf_ref, gf_ref, wsg_ref, wsu_ref, wsd_ref, g_ref, b_ref,
                    *rest, alpha):
    o_ref = rest[-1]
    x1 = x1_ref[...]
    hb = (x1 * (1.0 + sf_ref[...]) + shf_ref[...]).astype(BF16)
    sg = jnp.dot(hb, wsg_ref[...], preferred_element_type=F32)
    su = jnp.dot(hb, wsu_ref[...], preferred_element_type=F32)
    ffn = jnp.dot((_silu(sg) * su).astype(BF16), wsd_ref[...], preferred_element_type=F32)
    wt = wt_ref[...].T
    for k in range(TOP_K):
        yk = _unpack_words([yk_ref[j, k] for j in range(PACK_W)])
        ffn = ffn + wt[:, k:k + 1] * yk
    o_ref[...] = _layer_norm(alpha * x1 + gf_ref[...] * ffn, g_ref[...], b_ref[...])


def _combine(wts, yk, x1, mod, ws_gate_bf, ws_up_bf, ws_down_bf, ln_g, ln_b, alpha, tok0, earlier):
    n, d = x1.shape
    nc = yk.shape[2]
    s = n // mod.shape[1]
    tc = min(T_COMB, s, nc)
    nt = s // tc
    t0 = tok0 // tc
    fs = ws_gate_bf.shape[1]
    const = lambda shape: pl.BlockSpec(shape, lambda i: tuple(0 for _ in shape))
    mod_spec = lambda which: pl.BlockSpec((None, None, 1, d), lambda i: (which, (t0 + i) // nt, 0, 0))
    in_specs = [pl.BlockSpec((TOP_K, tc), lambda i: (0, t0 + i)),
                pl.BlockSpec((PACK_W, TOP_K, tc, LANES), lambda i: (0, 0, i, 0)),
                pl.BlockSpec((tc, d), lambda i: (t0 + i, 0)),
                mod_spec(4), mod_spec(3), mod_spec(5),
                const((d, fs)), const((d, fs)), const((fs, d)), const((1, d)), const((1, d))]
    args = [wts, yk, x1, mod, mod, mod, ws_gate_bf, ws_up_bf, ws_down_bf, ln_g.reshape(1, d), ln_b.reshape(1, d)]
    aliases = {}
    if earlier is not None:
        in_specs.append(pl.BlockSpec(memory_space=pl.ANY))
        args.append(earlier)
        aliases = {len(args) - 1: 0}
    return pl.pallas_call(
        functools.partial(_combine_kernel, alpha=alpha),
        grid=(nc // tc,),
        in_specs=in_specs,
        out_specs=pl.BlockSpec((tc, d), lambda i: (t0 + i, 0)),
        out_shape=jax.ShapeDtypeStruct((n, d), F32),
        input_output_aliases=aliases,
        compiler_params=_cparams(("parallel",)),
        name="combine",
    )(*args)


def _slot_layout(counts, n_assign):
    cnt = counts[:, 0].astype(I32)
    padded = (cnt + BLK_E - 1) // BLK_E * BLK_E
    pends = jnp.cumsum(padded)
    pstarts = pends - padded
    n_blocks_max = (n_assign + N_EXPERTS * (BLK_E - 1)) // BLK_E
    n_blocks = (pends[-1] // BLK_E).astype(I32).reshape(1)
    return pstarts, padded // BLK_E, pstarts // BLK_E, n_blocks, n_blocks_max * BLK_E


def kernel(x, c, w_ada, b_ada, w_in, w_out, na_rpb, ret_log_decay, ret_gn_g, ln1_g, ln1_b, ln2_g, ln2_b,
           w_router, router_bias, w_gate, w_up, w_down, ws_gate, ws_up, ws_down):
    b, s, d = x.shape
    depth = w_ada.shape[0]
    alpha = (2.0 * depth) ** 0.25
    t = jnp.arange(s, dtype=F32)
    inv_freq = ROPE_BASE ** (-jnp.arange(0, RET_HEAD_DIM, 2, dtype=F32) / RET_HEAD_DIM)
    ang = t[:, None] * inv_freq[None, :]
    cos, sin = jnp.cos(ang), jnp.sin(ang)
    cos2 = jnp.concatenate([cos, cos], axis=-1)
    sin2 = jnp.concatenate([-sin, sin], axis=-1)
    for l in range(depth):
        mod = _mod(c, w_ada[l], b_ada[l])
        proj = _inproj(x, mod, w_in[l].astype(BF16), cos2, sin2)
        y_na = _natten(proj, _na_bias_table(na_rpb[l]))
        y_r = _retention(proj, ret_log_decay[l], ret_gn_g[l])
        wr_t = w_router[l].T
        wr_hi = wr_t.astype(BF16)
        wr_lo = (wr_t - wr_hi.astype(F32)).astype(BF16)
        x1, hfp, logits_t = _outproj(y_na, y_r, x, mod, w_out[l].astype(BF16), ln1_g[l], ln1_b[l],
                                     wr_hi, wr_lo, alpha)
        idx, wts, rank, counts = _route(logits_t, router_bias[l])
        n = b * s
        pstarts, nblk, gstart, n_blocks, n_slots = _slot_layout(counts, n * TOP_K)
        slot_rows = _slots(idx, rank, pstarts, n_slots)
        xs = _scatter_rows_sc(hfp.reshape(PACK_W * n, LANES), slot_rows, PACK_W * n_slots)
        ys = _experts(nblk, gstart, counts[:, 0].astype(I32), n_blocks, xs.reshape(PACK_W, n_slots, LANES),
                      w_gate[l], w_up[l], w_down[l]).reshape(PACK_W * n_slots, LANES)
        ws = (ws_gate[l].astype(BF16), ws_up[l].astype(BF16), ws_down[l].astype(BF16))
        x1n = x1.reshape(n, d)
        nc = n // COMBINE_CHUNKS
        out = None
        for ci in range(COMBINE_CHUNKS):
            yk = _gather_rows_sc(ys, slot_rows, ci * nc, nc).reshape(PACK_W, TOP_K, nc, LANES)
            out = _combine(wts, yk, x1n, mod, *ws, ln2_g[l], ln2_b[l], alpha, ci * nc, out)
        x = out.reshape(b, s, d)
    return x
```

```python
import functools
import math

import jax
import jax.numpy as jnp
from jax import lax
from jax.experimental import pallas as pl
from jax.experimental.pallas import tpu as pltpu
from jax.experimental.pallas import tpu_sc as plsc

F32 = jnp.float32
BF16 = jnp.bfloat16
U32 = jnp.uint32
I32 = jnp.int32

GRID_W = 64
WIN_R = 8
WIN_C = 16
NA_HEADS = 8
NA_HEAD_DIM = 64
NA_WIDTH = NA_HEADS * NA_HEAD_DIM
RET_HEADS = 4
RET_HEAD_DIM = 128
RET_WIDTH = RET_HEADS * RET_HEAD_DIM
ROPE_BASE = 10000.0
N_EXPERTS = 256
TOP_K = 8
N_GROUPS = 8
TOPK_GROUPS = 4
GROUP_SIZE = N_EXPERTS // N_GROUPS
ROUTED_SCALE = 2.5
LOG2_E = 1.4426950408889634
LN_EPS = 1e-5
GN_EPS = 1e-6

LANES = 128
VMEM_BYTES = 64 * 1024 * 1024
VMEM_LIMIT = VMEM_BYTES * 7 // 8

TM_INPROJ = 1024
TM_PROJ = 512
T_ROUTE = 512
T_SLOT = 1024
SC_WINDOW = 128
SC_GATHER_WINDOWS = 2
BLK_E = 512
OUTPROJ_PARTS = 2
T_COMB = 512
COMBINE_CHUNKS = 8
PACK_W = 4
NA_ROWS_PER_ITER = 32
RET_BLOCK = 256
X_RING = 4
X_AHEAD = 3
Y_RING = 3


def _cparams(sem, vmem=VMEM_LIMIT):
    return pltpu.CompilerParams(dimension_semantics=sem, vmem_limit_bytes=vmem)


def _silu(v):
    return v * jax.nn.sigmoid(v)


def _layer_norm(z, g, b):
    mu = jnp.mean(z, -1, keepdims=True)
    zc = z - mu
    var = jnp.mean(zc * zc, -1, keepdims=True)
    return zc * lax.rsqrt(var + LN_EPS) * g + b


def _pack_rows(v):
    half = v.shape[1] // 2
    vb = v.astype(BF16)
    lo = lax.bitcast_convert_type(vb[:, :half].astype(F32), U32) >> 16
    hi = lax.bitcast_convert_type(vb[:, half:].astype(F32), U32)
    w = lax.bitcast_convert_type(hi | lo, I32)
    return [w[:, j * LANES:(j + 1) * LANES] for j in range(half // LANES)]


def _unpack_words(words):
    words = [lax.bitcast_convert_type(w, U32) for w in words]
    lo = [lax.bitcast_convert_type(w << 16, F32) for w in words]
    hi = [lax.bitcast_convert_type(w & jnp.uint32(0xFFFF0000), F32) for w in words]
    return jnp.concatenate(lo + hi, axis=-1)


def _mod_kernel(c_ref, w_ref, b_ref, o_ref):
    cond = _silu(c_ref[...])
    o_ref[0] = jnp.dot(cond, w_ref[...], precision=lax.Precision.HIGHEST,
                       preferred_element_type=F32) + b_ref[0]


def _mod(c, w_ada, b_ada):
    b, d = c.shape
    n6 = w_ada.shape[1] // d
    out = pl.pallas_call(
        _mod_kernel,
        grid=(n6,),
        in_specs=[pl.BlockSpec((b, d), lambda j: (0, 0)),
                  pl.BlockSpec((d, d), lambda j: (0, j)),
                  pl.BlockSpec((1, 1, d), lambda j: (j, 0, 0))],
        out_specs=pl.BlockSpec((1, b, d), lambda j: (j, 0, 0)),
        out_shape=jax.ShapeDtypeStruct((n6, b, d), F32),
        compiler_params=_cparams(("arbitrary",)),
        name="mod",
    )(c, w_ada, b_ada.reshape(n6, 1, d))
    return out.reshape(n6, b, 1, d)


def _mod_spec(which, d):
    return pl.BlockSpec((None, None, 1, d), lambda b, i, which=which: (which, b, 0, 0))


def _inproj_kernel(x_ref, sc_ref, sh_ref, w_ref, cos_ref, sin_ref, o_ref, *, chunk, q_scale):
    h = (x_ref[...] * (1.0 + sc_ref[...]) + sh_ref[...]).astype(BF16)
    q_r, k_r = 3 * NA_WIDTH // chunk, 3 * NA_WIDTH // chunk + 1
    dh = RET_HEAD_DIM
    for j in range(o_ref.shape[1] // chunk):
        acc = jnp.dot(h, w_ref[:, j * chunk:(j + 1) * chunk], preferred_element_type=F32)
        if j == 0:
            acc = acc * q_scale
        if j in (q_r, k_r):
            cos2, sin2 = cos_ref[...], sin_ref[...]
            heads = [acc[:, hh * dh:(hh + 1) * dh] for hh in range(chunk // dh)]
            heads = [t * cos2 + pltpu.roll(t, dh // 2, 1) * sin2 for t in heads]
            acc = jnp.concatenate(heads, axis=1)
            if j == k_r:
                acc = acc * dh ** -0.5
        o_ref[:, j * chunk:(j + 1) * chunk] = acc.astype(o_ref.dtype)


def _inproj(x, mod, w_in_bf, cos2, sin2):
    b, s, d = x.shape
    e = w_in_bf.shape[1]
    tm = min(TM_INPROJ, s)
    rope = pl.BlockSpec((tm, RET_HEAD_DIM), lambda bi, i: (i, 0))
    return pl.pallas_call(
        functools.partial(_inproj_kernel, chunk=NA_WIDTH, q_scale=NA_HEAD_DIM ** -0.5 * LOG2_E),
        grid=(b, s // tm),
        in_specs=[pl.BlockSpec((None, tm, d), lambda bi, i: (bi, i, 0)),
                  _mod_spec(1, d), _mod_spec(0, d),
                  pl.BlockSpec((d, e), lambda bi, i: (0, 0)), rope, rope],
        out_specs=pl.BlockSpec((None, tm, e), lambda bi, i: (bi, i, 0)),
        out_shape=jax.ShapeDtypeStruct((b, s, e), BF16),
        compiler_params=_cparams(("parallel", "parallel")),
        name="inproj",
    )(x, mod, mod, w_in_bf, cos2, sin2)


def _natten_kernel(q_ref, k_ref, v_ref, bias_ref, o_ref, *, rows, rows_per_iter):
    kspan = WIN_R * GRID_W

    first = lax.broadcasted_iota(I32, (1, LANES), 1) < NA_HEAD_DIM
    zero = jnp.zeros((), BF16)

    def rows_body(i, carry):
        qrows, krows, scores, probs = {}, {}, {}, {}

        def stage_scores(u):
            r = i * rows_per_iter + u
            rs = jnp.clip(r - WIN_R // 2, 0, rows - WIN_R)
            vi = r - rs
            qrows[u] = pl.ds(pl.multiple_of(r * GRID_W, GRID_W), GRID_W)
            krows[u] = pl.ds(pl.multiple_of(rs * GRID_W, GRID_W), kspan)
            q = q_ref[qrows[u], :]
            qm = jnp.concatenate([jnp.where(first, q, zero), jnp.where(first, zero, q)], axis=0)
            st = lax.dot_general(k_ref[krows[u], :], qm, (((1,), (1,)), ((), ())), preferred_element_type=F32)
            scores[u] = st + bias_ref[vi]

        def stage_softmax(u):
            st = scores.pop(u)
            p = jnp.exp2(st - jnp.max(st, axis=0, keepdims=True))
            probs[u] = (p * (1.0 / jnp.sum(p, axis=0, keepdims=True))).astype(BF16)

        def stage_values(u):
            res = lax.dot_general(probs.pop(u), v_ref[krows[u], :], (((0,), (0,)), ((), ())),
                                  preferred_element_type=F32)
            o_ref[qrows[u], :] = jnp.where(first, res[:GRID_W], res[GRID_W:]).astype(o_ref.dtype)

        for step in range(rows_per_iter + 2):
            if step < rows_per_iter:
                stage_scores(step)
            if 0 <= step - 1 < rows_per_iter:
                stage_softmax(step - 1)
            if 0 <= step - 2 < rows_per_iter:
                stage_values(step - 2)
        return carry

    lax.fori_loop(0, rows // rows_per_iter, rows_body, 0)


def _na_bias_table(rpb):
    w = GRID_W
    cq = jnp.arange(w)
    cs = jnp.clip(cq - WIN_C // 2, 0, w - WIN_C)
    ck = jnp.arange(w)
    col_in = (ck[None, :] >= cs[:, None]) & (ck[None, :] < cs[:, None] + WIN_C)
    dc_idx = jnp.clip(ck[None, :] - cq[:, None] + WIN_C - 1, 0, 2 * WIN_C - 2)
    t = rpb[:, :, dc_idx]
    t = jnp.where(col_in[None, None], t, -jnp.inf)
    vi = jnp.arange(WIN_R)
    kr = jnp.arange(WIN_R)
    dr = kr[None, :] - vi[:, None] + WIN_R - 1
    tb = t[:, dr]
    hp = LANES // NA_HEAD_DIM
    tb = tb.reshape(rpb.shape[0] // hp, hp, WIN_R, WIN_R, w, w)
    tb = tb.transpose(0, 2, 3, 5, 1, 4).reshape(rpb.shape[0] // hp, WIN_R, WIN_R * w, hp * w)
    return tb.astype(F32) * LOG2_E


def _natten(proj, bias_tab):
    b, s, _ = proj.shape
    rows = s // GRID_W
    hp = LANES // NA_HEAD_DIM
    npair = NA_HEADS // hp
    blk = lambda off: pl.BlockSpec((None, s, LANES), lambda bi, p, off=off: (bi, 0, off + p))
    return pl.pallas_call(
        functools.partial(_natten_kernel, rows=rows, rows_per_iter=math.gcd(rows, NA_ROWS_PER_ITER)),
        grid=(b, npair),
        in_specs=[blk(0), blk(npair), blk(2 * npair),
                  pl.BlockSpec((None, WIN_R, WIN_R * GRID_W, hp * GRID_W), lambda bi, p: (p, 0, 0, 0))],
        out_specs=pl.BlockSpec((None, s, LANES), lambda bi, p: (bi, 0, p)),
        out_shape=jax.ShapeDtypeStruct((b, s, NA_WIDTH), BF16),
        compiler_params=_cparams(("parallel", "parallel")),
        name="natten",
    )(proj, proj, proj, bias_tab)


def _retent_kernel(ld_ref, q_ref, k_ref, v_ref, g_ref, gn_ref, o_ref, st_ref, *, nchunk):
    c = RET_BLOCK
    dh = RET_HEAD_DIM
    h = pl.program_id(1)
    lgf = ld_ref[0, h]
    lgb = ld_ref[1, h]

    ic = lax.broadcasted_iota(I32, (c, 1), 0).astype(F32)
    ir = lax.broadcasted_iota(I32, (1, c), 1).astype(F32)
    diff = ic - ir
    dmat = jnp.where(diff >= 0, jnp.exp(jnp.maximum(diff, 0.0) * lgf),
                     jnp.exp(jnp.maximum(-diff, 0.0) * lgb))
    kdec_f = jnp.exp((c - 1 - ic) * lgf)
    qdec_f = jnp.exp((ic + 1) * lgf)
    kdec_b = jnp.exp(ic * lgb)
    qdec_b = jnp.exp((c - ic) * lgb)
    one = jnp.ones((1, 1), F32)
    cdec_f = jnp.exp(one * (c * lgf))
    cdec_b = jnp.exp(one * (c * lgb))
    tn = (((0,), (0,)), ((), ()))

    chunks = [pl.ds(n * c, c) for n in range(nchunk)]
    kv = []
    for rows in chunks:
        kn = k_ref[rows, :].astype(F32)
        kd = jnp.concatenate([kn * kdec_f, kn * kdec_b], axis=1).astype(BF16)
        kv.append(lax.dot_general(kd, v_ref[rows, :], tn, preferred_element_type=F32))

    sf = [jnp.zeros((dh, dh), F32)]
    for n in range(1, nchunk):
        sf.append(cdec_f * sf[n - 1] + kv[n - 1][:dh])
    sb = [jnp.zeros((dh, dh), F32)]
    for n in range(nchunk - 2, -1, -1):
        sb.insert(0, cdec_b * sb[0] + kv[n + 1][dh:])
    for n in range(nchunk):
        st_ref[n] = jnp.concatenate([sf[n], sb[n]], axis=0).astype(BF16)

    gn = gn_ref[...]
    for n, rows in enumerate(chunks):
        qb = q_ref[rows, :]
        qn = qb.astype(F32)
        sc = lax.dot_general(qb, k_ref[rows, :], (((1,), (1,)), ((), ())), preferred_element_type=F32) * dmat
        y = jnp.dot(sc.astype(BF16), v_ref[rows, :], preferred_element_type=F32)
        qd = jnp.concatenate([qn * qdec_f, qn * qdec_b], axis=1).astype(BF16)
        y = y + jnp.dot(qd, st_ref[n], preferred_element_type=F32)
        mu = jnp.mean(y, -1, keepdims=True)
        yc = y - mu
        var = jnp.mean(yc * yc, -1, keepdims=True)
        yn = yc * lax.rsqrt(var + GN_EPS) * gn
        o_ref[rows, :] = (_silu(g_ref[rows, :].astype(F32)) * yn).astype(o_ref.dtype)


def _retention(proj, log_decay, gn_g):
    b, s, _ = proj.shape
    dh = RET_HEAD_DIM
    nchunk = s // RET_BLOCK
    base = 3 * NA_WIDTH // dh
    blk = lambda off: pl.BlockSpec((None, s, dh), lambda bi, h, off=off: (bi, 0, base + off + h))
    return pl.pallas_call(
        functools.partial(_retent_kernel, nchunk=nchunk),
        grid=(b, RET_HEADS),
        in_specs=[pl.BlockSpec(memory_space=pltpu.SMEM),
                  blk(0), blk(RET_HEADS), blk(2 * RET_HEADS), blk(3 * RET_HEADS),
                  pl.BlockSpec((1, dh), lambda bi, h: (0, h))],
        out_specs=pl.BlockSpec((None, s, dh), lambda bi, h: (bi, 0, h)),
        out_shape=jax.ShapeDtypeStruct((b, s, RET_WIDTH), BF16),
        scratch_shapes=[pltpu.VMEM((nchunk, 2 * dh, dh), BF16)],
        compiler_params=_cparams(("parallel", "parallel")),
        name="retent",
    )(log_decay, proj, proj, proj, proj, gn_g.reshape(1, RET_WIDTH))


def _outproj_kernel(yna_ref, yr_ref, x_ref, ga_ref, sf_ref, shf_ref, wo1_ref, wo2_ref, g_ref, b_ref,
                    wrh_ref, wrl_ref, x1_ref, hfp_ref, lg_ref, *, alpha):
    nt = (((1,), (1,)), ((), ()))
    tm = x_ref.shape[0]
    parts = [pl.ds(p * (tm // OUTPROJ_PARTS), tm // OUTPROJ_PARTS) for p in range(OUTPROJ_PARTS)]
    def mix_of(r):
        return (jnp.dot(yna_ref[r, :], wo1_ref[...], preferred_element_type=F32)
                + jnp.dot(yr_ref[r, :], wo2_ref[...], preferred_element_type=F32))

    nxt = mix_of(parts[0])
    for p, r in enumerate(parts):
        mix = nxt
        if p + 1 < len(parts):
            nxt = mix_of(parts[p + 1])
        x1 = _layer_norm(alpha * x_ref[r, :] + ga_ref[...] * mix, g_ref[...], b_ref[...])
        x1_ref[r, :] = x1
        hf = x1 * (1.0 + sf_ref[...]) + shf_ref[...]
        for j, w in enumerate(_pack_rows(hf)):
            hfp_ref[j, r, :] = w
        hb = hf.astype(BF16)
        hl = (hf - hb.astype(F32)).astype(BF16)
        lg = lax.dot_general(wrh_ref[...], hb, nt, preferred_element_type=F32)
        lg = lg + lax.dot_general(wrh_ref[...], hl, nt, preferred_element_type=F32)
        lg = lg + lax.dot_general(wrl_ref[...], hb, nt, preferred_element_type=F32)
        lg_ref[:, r] = lg


def _outproj(y_na, y_r, x, mod, w_out_bf, ln_g, ln_b, wr_hi, wr_lo, alpha):
    b, s, d = x.shape
    tm = min(TM_PROJ, s)
    nt = s // tm
    ne = wr_hi.shape[0]
    const = lambda shape: pl.BlockSpec(shape, lambda bi, i: tuple(0 for _ in shape))
    x1, hfp, lg = pl.pallas_call(
        functools.partial(_outproj_kernel, alpha=alpha),
        grid=(b, nt),
        in_specs=[pl.BlockSpec((None, tm, NA_WIDTH), lambda bi, i: (bi, i, 0)),
                  pl.BlockSpec((None, tm, RET_WIDTH), lambda bi, i: (bi, i, 0)),
                  pl.BlockSpec((None, tm, d), lambda bi, i: (bi, i, 0)),
                  _mod_spec(2, d), _mod_spec(4, d), _mod_spec(3, d),
                  pl.BlockSpec((NA_WIDTH, d), lambda bi, i: (0, 0)),
                  pl.BlockSpec((RET_WIDTH, d), lambda bi, i: (1, 0)),
                  const((1, d)), const((1, d)), const((ne, d)), const((ne, d))],
        out_specs=[pl.BlockSpec((None, tm, d), lambda bi, i: (bi, i, 0)),
                   pl.BlockSpec((PACK_W, tm, LANES), lambda bi, i: (0, bi * nt + i, 0)),
                   pl.BlockSpec((ne, tm), lambda bi, i: (0, bi * nt + i))],
        out_shape=[jax.ShapeDtypeStruct((b, s, d), F32),
                   jax.ShapeDtypeStruct((PACK_W, b * s, LANES), I32),
                   jax.ShapeDtypeStruct((ne, b * s), F32)],
        compiler_params=_cparams(("parallel", "parallel")),
        name="outproj",
    )(y_na, y_r, x, mod, mod, mod, w_out_bf, w_out_bf, ln_g.reshape(1, d), ln_b.reshape(1, d), wr_hi, wr_lo)
    return x1, hfp, lg


def _route_kernel(lg_ref, rb_ref, idx_ref, w_ref, rank_ref, cnt_ref):
    t = lg_ref.shape[1]
    ninf = -jnp.inf

    @pl.when(pl.program_id(0) == 0)
    def _():
        cnt_ref[...] = jnp.zeros_like(cnt_ref)

    scores = jax.nn.sigmoid(lg_ref[...])
    sel = scores + rb_ref[...]

    io_g = lax.broadcasted_iota(I32, (GROUP_SIZE, t), 0)
    gs_rows = []
    for g in range(N_GROUPS):
        blk = sel[g * GROUP_SIZE:(g + 1) * GROUP_SIZE, :]
        m1 = jnp.max(blk, axis=0, keepdims=True)
        i1 = jnp.min(jnp.where(blk == m1, io_g, GROUP_SIZE), axis=0, keepdims=True)
        m2 = jnp.max(jnp.where(io_g == i1, ninf, blk), axis=0, keepdims=True)
        gs_rows.append(m1 + m2)
    gs = jnp.concatenate(gs_rows, axis=0)

    io8 = lax.broadcasted_iota(I32, (N_GROUPS, t), 0)
    gsel = jnp.zeros((N_GROUPS, t), F32)
    for _ in range(TOPK_GROUPS):
        m = jnp.max(gs, axis=0, keepdims=True)
        gi = jnp.min(jnp.where(gs == m, io8, N_GROUPS), axis=0, keepdims=True)
        hit = io8 == gi
        gsel = jnp.where(hit, 1.0, gsel)
        gs = jnp.where(hit, ninf, gs)

    masked = jnp.concatenate(
        [jnp.where(gsel[g:g + 1, :] > 0.0, sel[g * GROUP_SIZE:(g + 1) * GROUP_SIZE, :], ninf)
         for g in range(N_GROUPS)], axis=0)

    io_e = lax.broadcasted_iota(I32, (N_EXPERTS, t), 0)
    chosen = jnp.zeros((N_EXPERTS, t), F32)
    idx_rows, w_rows = [], []
    for _ in range(TOP_K):
        m = jnp.max(masked, axis=0, keepdims=True)
        ei = jnp.min(jnp.where(masked == m, io_e, N_EXPERTS), axis=0, keepdims=True)
        hit = io_e == ei
        w_rows.append(jnp.sum(jnp.where(hit, scores, 0.0), axis=0, keepdims=True))
        idx_rows.append(ei)
        chosen = jnp.where(hit, 1.0, chosen)
        masked = jnp.where(hit, ninf, masked)
    wk = jnp.concatenate(w_rows, axis=0)
    w_ref[...] = wk / jnp.sum(wk, axis=0, keepdims=True) * ROUTED_SCALE
    idx_ref[...] = jnp.concatenate(idx_rows, axis=0)

    upper = (lax.broadcasted_iota(I32, (t, t), 0) < lax.broadcasted_iota(I32, (t, t), 1))
    prefix = jnp.dot(chosen.astype(BF16), upper.astype(BF16), preferred_element_type=F32)
    rank_full = prefix + cnt_ref[...]
    rank_rows = [jnp.sum(jnp.where(io_e == ei, rank_full, 0.0), axis=0, keepdims=True) for ei in idx_rows]
    rank_ref[...] = jnp.concatenate(rank_rows, axis=0).astype(I32)
    cnt_ref[...] += jnp.sum(chosen, axis=1, keepdims=True)


def _route(logits_t, router_bias):
    ne, n = logits_t.shape
    t = min(T_ROUTE, n)
    kspec = pl.BlockSpec((TOP_K, t), lambda i: (0, i))
    return pl.pallas_call(
        _route_kernel,
        grid=(n // t,),
        in_specs=[pl.BlockSpec((ne, t), lambda i: (0, i)),
                  pl.BlockSpec((ne, 1), lambda i: (0, 0))],
        out_specs=[kspec, kspec, kspec, pl.BlockSpec((ne, 1), lambda i: (0, 0))],
        out_shape=[jax.ShapeDtypeStruct((TOP_K, n), I32),
                   jax.ShapeDtypeStruct((TOP_K, n), F32),
                   jax.ShapeDtypeStruct((TOP_K, n), I32),
                   jax.ShapeDtypeStruct((ne, 1), F32)],
        compiler_params=_cparams(("arbitrary",)),
        name="route",
    )(logits_t, router_bias.reshape(ne, 1))


def _block(ref, g):
    return ref.at[:, pl.ds(pl.multiple_of(g * BLK_E, BLK_E), BLK_E), :]


def _slots_kernel(idx_ref, rank_ref, ps_ref, o_ref, *, n_slots):
    t = idx_ref.shape[1]
    io = lax.broadcasted_iota(I32, (N_EXPERTS, t), 0)
    ps = ps_ref[...]
    for k in range(TOP_K):
        hit = io == idx_ref[k:k + 1, :]
        slot = jnp.sum(jnp.where(hit, ps, 0), axis=0, keepdims=True) + rank_ref[k:k + 1, :]
        for j in range(PACK_W):
            o_ref[j * TOP_K + k:j * TOP_K + k + 1, :] = slot + j * n_slots


def _slots(idx, rank, pstarts, n_slots):
    n = idx.shape[1]
    t = min(T_SLOT, n)
    kspec = pl.BlockSpec((TOP_K, t), lambda i: (0, i))
    return pl.pallas_call(
        functools.partial(_slots_kernel, n_slots=n_slots),
        grid=(n // t,),
        in_specs=[kspec, kspec, pl.BlockSpec((N_EXPERTS, 1), lambda i: (0, 0))],
        out_specs=pl.BlockSpec((PACK_W * TOP_K, t), lambda i: (0, i)),
        out_shape=jax.ShapeDtypeStruct((PACK_W * TOP_K, n), I32),
        compiler_params=_cparams(("parallel",)),
        name="slots",
    )(idx, rank, pstarts.reshape(N_EXPERTS, 1))


def _sc_mesh():
    return plsc.VectorSubcoreMesh(core_axis_name="core", subcore_axis_name="subcore")


def _scatter_rows_sc(rows, dest_rows, n_out):
    n_rows = rows.shape[0]
    wpp = n_rows // PACK_W // SC_WINDOW

    @functools.partial(pl.kernel, mesh=_sc_mesh(), scratch_types=[pltpu.SemaphoreType.DMA],
                       out_type=jax.ShapeDtypeStruct((n_out, LANES), I32))
    def scatter_rows(x_hbm, i_hbm, o_hbm, sem):
        def body(x_vmem, i_vmem):
            copies = [pltpu.async_copy(x_vmem, o_hbm.at[i_vmem.at[k]], sem) for k in range(TOP_K)]
            for c in copies:
                c.wait()

        pltpu.emit_pipeline(
            body,
            grid=(n_rows // SC_WINDOW,),
            in_specs=[pl.BlockSpec((SC_WINDOW, LANES), lambda i: (i, 0)),
                      pl.BlockSpec((TOP_K, SC_WINDOW), lambda i: (i // wpp, i % wpp))],
            out_specs=[],
            core_axis_name=("core", "subcore"),
            dimension_semantics=(pltpu.PARALLEL,),
        )(x_hbm, i_hbm)

    return scatter_rows(rows, dest_rows)


def _gather_rows_sc(rows, slot_rows, tok0, nc):
    nr = slot_rows.shape[0]
    span = SC_GATHER_WINDOWS * SC_WINDOW
    spr = nc // span
    w0 = tok0 // SC_WINDOW

    @functools.partial(pl.kernel, mesh=_sc_mesh(), scratch_types=[pltpu.SemaphoreType.DMA],
                       out_type=jax.ShapeDtypeStruct((nr * nc, LANES), I32))
    def gather_rows(x_hbm, i_hbm, o_hbm, sem):
        def body(*refs):
            o_vmem = refs[-1]
            copies = [pltpu.async_copy(x_hbm.at[i_vmem.at[0]], o_vmem.at[pl.ds(w * SC_WINDOW, SC_WINDOW)], sem)
                      for w, i_vmem in enumerate(refs[:-1])]
            for c in copies:
                c.wait()

        def idx_spec(w):
            return pl.BlockSpec((1, SC_WINDOW),
                                lambda i: (i // spr, w0 + (i % spr) * SC_GATHER_WINDOWS + w))

        pltpu.emit_pipeline(
            body,
            grid=(nr * spr,),
            in_specs=[idx_spec(w) for w in range(SC_GATHER_WINDOWS)],
            out_specs=[pl.BlockSpec((span, LANES), lambda i: (i, 0))],
            core_axis_name=("core", "subcore"),
            dimension_semantics=(pltpu.PARALLEL,),
        )(*([i_hbm] * SC_GATHER_WINDOWS), o_hbm)

    return gather_rows(rows, slot_rows)


def _experts_kernel(nblk_ref, gstart_ref, cnt_ref, nb_ref, xs_ref, wg_ref, wu_ref, wd_ref, ys_ref,
                    xbuf_ref, ybuf_ref, wgub_ref, wdb_ref, xsem, ysem):
    e = pl.program_id(0)
    total = nb_ref[0]
    nb_max = ys_ref.shape[1] // BLK_E

    def x_copy(g):
        slot = g % X_RING
        return pltpu.make_async_copy(_block(xs_ref, g), xbuf_ref.at[slot], xsem.at[slot])

    def y_copy(g):
        slot = g % Y_RING
        return pltpu.make_async_copy(ybuf_ref.at[slot], _block(ys_ref, g), ysem.at[slot])

    @pl.when(e == 0)
    def _():
        for g in range(X_AHEAD):
            @pl.when(g < total)
            def _():
                x_copy(g).start()

    @pl.when(nblk_ref[e] > 0)
    def _():
        f = wg_ref.shape[1]
        wgub_ref[:, :f] = wg_ref[...].astype(BF16)
        wgub_ref[:, f:] = wu_ref[...].astype(BF16)
        wdb_ref[...] = wd_ref[...].astype(BF16)

    def fetch(b):
        g = gstart_ref[e] + b
        x_copy(g).wait()

        @pl.when(g + X_AHEAD < total)
        def _():
            x_copy(g + X_AHEAD).start()

        xb = _unpack_words([xbuf_ref[g % X_RING, j] for j in range(PACK_W)]).astype(BF16)
        live = lax.broadcasted_iota(I32, (BLK_E, 1), 0) < cnt_ref[e] - b * BLK_E
        return jnp.where(live, xb, jnp.zeros((), BF16))

    def hidden(xb):
        f = wg_ref.shape[1]
        hgu = jnp.dot(xb, wgub_ref[...], preferred_element_type=F32)
        return (_silu(hgu[:, :f]) * hgu[:, f:]).astype(BF16)

    def finish(b, act):
        g = gstart_ref[e] + b
        words = _pack_rows(jnp.dot(act, wdb_ref[...], preferred_element_type=F32))

        @pl.when(g >= Y_RING)
        def _():
            y_copy(g - Y_RING).wait()

        for j, w in enumerate(words):
            ybuf_ref[g % Y_RING, j] = w
        y_copy(g).start()

    def block(b, carry):
        finish(b, hidden(fetch(b)))
        return carry

    lax.fori_loop(0, nblk_ref[e], block, 0)

    @pl.when(e == pl.num_programs(0) - 1)
    def _():
        for back in range(Y_RING):
            @pl.when(total - 1 - back >= 0)
            def _():
                y_copy(total - 1 - back).wait()

        ybuf_ref[0] = jnp.zeros(ybuf_ref.shape[1:], ybuf_ref.dtype)

        def tail_copy(g):
            return pltpu.make_async_copy(ybuf_ref.at[0], _block(ys_ref, g), ysem.at[0])

        def fill(g, carry):
            tail_copy(g).start()
            return carry

        lax.fori_loop(total, nb_max, fill, 0)

        def drain(g, carry):
            tail_copy(g).wait()
            return carry

        lax.fori_loop(total, nb_max, drain, 0)


def _experts(nblk, gstart, cnt, n_blocks, xs, w_gate, w_up, w_down):
    p = xs.shape[1]
    ne, d, f = w_gate.shape
    wspec = lambda shape: pl.BlockSpec((None,) + shape, lambda e, *_: (e, 0, 0))
    grid_spec = pltpu.PrefetchScalarGridSpec(
        num_scalar_prefetch=4,
        grid=(ne,),
        in_specs=[pl.BlockSpec(memory_space=pl.ANY), wspec((d, f)), wspec((d, f)), wspec((f, d))],
        out_specs=pl.BlockSpec(memory_space=pl.ANY),
        scratch_shapes=[pltpu.VMEM((X_RING, PACK_W, BLK_E, LANES), I32),
                        pltpu.VMEM((Y_RING, PACK_W, BLK_E, LANES), I32),
                        pltpu.VMEM((d, 2 * f), BF16), pltpu.VMEM((f, d), BF16),
                        pltpu.SemaphoreType.DMA((X_RING,)), pltpu.SemaphoreType.DMA((Y_RING,))],
    )
    return pl.pallas_call(
        _experts_kernel,
        grid_spec=grid_spec,
        out_shape=jax.ShapeDtypeStruct((PACK_W, p, LANES), I32),
        compiler_params=_cparams(("arbitrary",)),
        name="experts",
    )(nblk, gstart, cnt, n_blocks, xs, w_gate, w_up, w_down)


def _combine_kernel(wt_ref, yk_ref, x1_ref, sf_ref, shf_ref, gf_ref, wsg_ref, wsu_ref, wsd_ref, g_ref, b_ref,
                    *rest, alpha):
    o_ref = rest[-1]
    x1 = x1_ref[...]
    hb = (x1 * (1.0 + sf_ref[...]) + shf_ref[...]).astype(BF16)
    sg = jnp.dot(hb, wsg_ref[...], preferred_element_type=F32)
    su = jnp.dot(hb, wsu_ref[...], preferred_element_type=F32)
    ffn = jnp.dot((_silu(sg) * su).astype(BF16), wsd_ref[...], preferred_element_type=F32)
    wt = wt_ref[...].T
    for k in range(TOP_K):
        yk = _unpack_words([yk_ref[j, k] for j in range(PACK_W)])
        ffn = ffn + wt[:, k:k + 1] * yk
    o_ref[...] = _layer_norm(alpha * x1 + gf_ref[...] * ffn, g_ref[...], b_ref[...])


def _combine(wts, yk, x1, mod, ws_gate_bf, ws_up_bf, ws_down_bf, ln_g, ln_b, alpha, tok0, earlier):
    n, d = x1.shape
    nc = yk.shape[2]
    s = n // mod.shape[1]
    tc = min(T_COMB, s, nc)
    nt = s // tc
    t0 = tok0 // tc
    fs = ws_gate_bf.shape[1]
    const = lambda shape: pl.BlockSpec(shape, lambda i: tuple(0 for _ in shape))
    mod_spec = lambda which: pl.BlockSpec((None, None, 1, d), lambda i: (which, (t0 + i) // nt, 0, 0))
    in_specs = [pl.BlockSpec((TOP_K, tc), lambda i: (0, t0 + i)),
                pl.BlockSpec((PACK_W, TOP_K, tc, LANES), lambda i: (0, 0, i, 0)),
                pl.BlockSpec((tc, d), lambda i: (t0 + i, 0)),
                mod_spec(4), mod_spec(3), mod_spec(5),
                const((d, fs)), const((d, fs)), const((fs, d)), const((1, d)), const((1, d))]
    args = [wts, yk, x1, mod, mod, mod, ws_gate_bf, ws_up_bf, ws_down_bf, ln_g.reshape(1, d), ln_b.reshape(1, d)]
    aliases = {}
    if earlier is not None:
        in_specs.append(pl.BlockSpec(memory_space=pl.ANY))
        args.append(earlier)
        aliases = {len(args) - 1: 0}
    return pl.pallas_call(
        functools.partial(_combine_kernel, alpha=alpha),
        grid=(nc // tc,),
        in_specs=in_specs,
        out_specs=pl.BlockSpec((tc, d), lambda i: (t0 + i, 0)),
        out_shape=jax.ShapeDtypeStruct((n, d), F32),
        input_output_aliases=aliases,
        compiler_params=_cparams(("parallel",)),
        name="combine",
    )(*args)


def _slot_layout(counts, n_assign):
    cnt = counts[:, 0].astype(I32)
    padded = (cnt + BLK_E - 1) // BLK_E * BLK_E
    pends = jnp.cumsum(padded)
    pstarts = pends - padded
    n_blocks_max = (n_assign + N_EXPERTS * (BLK_E - 1)) // BLK_E
    n_blocks = (pends[-1] // BLK_E).astype(I32).reshape(1)
    return pstarts, padded // BLK_E, pstarts // BLK_E, n_blocks, n_blocks_max * BLK_E


def kernel(x, c, w_ada, b_ada, w_in, w_out, na_rpb, ret_log_decay, ret_gn_g, ln1_g, ln1_b, ln2_g, ln2_b,
           w_router, router_bias, w_gate, w_up, w_down, ws_gate, ws_up, ws_down):
    b, s, d = x.shape
    depth = w_ada.shape[0]
    alpha = (2.0 * depth) ** 0.25
    t = jnp.arange(s, dtype=F32)
    inv_freq = ROPE_BASE ** (-jnp.arange(0, RET_HEAD_DIM, 2, dtype=F32) / RET_HEAD_DIM)
    ang = t[:, None] * inv_freq[None, :]
    cos, sin = jnp.cos(ang), jnp.sin(ang)
    cos2 = jnp.concatenate([cos, cos], axis=-1)
    sin2 = jnp.concatenate([-sin, sin], axis=-1)
    for l in range(depth):
        mod = _mod(c, w_ada[l], b_ada[l])
        proj = _inproj(x, mod, w_in[l].astype(BF16), cos2, sin2)
        y_na = _natten(proj, _na_bias_table(na_rpb[l]))
        y_r = _retention(proj, ret_log_decay[l], ret_gn_g[l])
        wr_t = w_router[l].T
        wr_hi = wr_t.astype(BF16)
        wr_lo = (wr_t - wr_hi.astype(F32)).astype(BF16)
        x1, hfp, logits_t = _outproj(y_na, y_r, x, mod, w_out[l].astype(BF16), ln1_g[l], ln1_b[l],
                                     wr_hi, wr_lo, alpha)
        idx, wts, rank, counts = _route(logits_t, router_bias[l])
        n = b * s
        pstarts, nblk, gstart, n_blocks, n_slots = _slot_layout(counts, n * TOP_K)
        slot_rows = _slots(idx, rank, pstarts, n_slots)
        xs = _scatter_rows_sc(hfp.reshape(PACK_W * n, LANES), slot_rows, PACK_W * n_slots)
        ys = _experts(nblk, gstart, counts[:, 0].astype(I32), n_blocks, xs.reshape(PACK_W, n_slots, LANES),
                      w_gate[l], w_up[l], w_down[l]).reshape(PACK_W * n_slots, LANES)
        ws = (ws_gate[l].astype(BF16), ws_up[l].astype(BF16), ws_down[l].astype(BF16))
        x1n = x1.reshape(n, d)
        nc = n // COMBINE_CHUNKS
        out = None
        for ci in range(COMBINE_CHUNKS):
            yk = _gather_rows_sc(ys, slot_rows, ci * nc, nc).reshape(PACK_W, TOP_K, nc, LANES)
            out = _combine(wts, yk, x1n, mod, *ws, ln2_g[l], ln2_b[l], alpha, ci * nc, out)
        x = out.reshape(b, s, d)
    return x
```

```python
import functools
import math

import jax
import jax.numpy as jnp
from jax import lax
from jax.experimental import pallas as pl
from jax.experimental.pallas import tpu as pltpu
from jax.experimental.pallas import tpu_sc as plsc

F32 = jnp.float32
BF16 = jnp.bfloat16
U32 = jnp.uint32
I32 = jnp.int32

GRID_W = 64
WIN_R = 8
WIN_C = 16
NA_HEADS = 8
NA_HEAD_DIM = 64
NA_WIDTH = NA_HEADS * NA_HEAD_DIM
RET_HEADS = 4
RET_HEAD_DIM = 128
RET_WIDTH = RET_HEADS * RET_HEAD_DIM
ROPE_BASE = 10000.0
N_EXPERTS = 256
TOP_K = 8
N_GROUPS = 8
TOPK_GROUPS = 4
GROUP_SIZE = N_EXPERTS // N_GROUPS
ROUTED_SCALE = 2.5
LOG2_E = 1.4426950408889634
LN_EPS = 1e-5
GN_EPS = 1e-6

LANES = 128
VMEM_BYTES = 64 * 1024 * 1024
VMEM_LIMIT = VMEM_BYTES * 7 // 8

TM_INPROJ = 1024
TM_PROJ = 512
T_ROUTE = 512
T_SLOT = 1024
SC_WINDOW = 128
SC_GATHER_WINDOWS = 2
BLK_E = 512
OUTPROJ_PARTS = 2
T_COMB = 512
COMBINE_CHUNKS = 8
PACK_W = 4
NA_ROWS_PER_ITER = 32
RET_BLOCK = 256
X_RING = 4
X_AHEAD = 3
Y_RING = 3


def _cparams(sem, vmem=VMEM_LIMIT):
    return pltpu.CompilerParams(dimension_semantics=sem, vmem_limit_bytes=vmem)


def _silu(v):
    return v * jax.nn.sigmoid(v)


def _layer_norm(z, g, b):
    mu = jnp.mean(z, -1, keepdims=True)
    zc = z - mu
    var = jnp.mean(zc * zc, -1, keepdims=True)
    return zc * lax.rsqrt(var + LN_EPS) * g + b


def _pack_rows(v):
    half = v.shape[1] // 2
    vb = v.astype(BF16)
    lo = lax.bitcast_convert_type(vb[:, :half].astype(F32), U32) >> 16
    hi = lax.bitcast_convert_type(vb[:, half:].astype(F32), U32)
    w = lax.bitcast_convert_type(hi | lo, I32)
    return [w[:, j * LANES:(j + 1) * LANES] for j in range(half // LANES)]


def _unpack_words(words):
    words = [lax.bitcast_convert_type(w, U32) for w in words]
    lo = [lax.bitcast_convert_type(w << 16, F32) for w in words]
    hi = [lax.bitcast_convert_type(w & jnp.uint32(0xFFFF0000), F32) for w in words]
    return jnp.concatenate(lo + hi, axis=-1)


def _mod_kernel(c_ref, w_ref, b_ref, o_ref):
    cond = _silu(c_ref[...])
    o_ref[0] = jnp.dot(cond, w_ref[...], precision=lax.Precision.HIGHEST,
                       preferred_element_type=F32) + b_ref[0]


def _mod(c, w_ada, b_ada):
    b, d = c.shape
    n6 = w_ada.shape[1] // d
    out = pl.pallas_call(
        _mod_kernel,
        grid=(n6,),
        in_specs=[pl.BlockSpec((b, d), lambda j: (0, 0)),
                  pl.BlockSpec((d, d), lambda j: (0, j)),
                  pl.BlockSpec((1, 1, d), lambda j: (j, 0, 0))],
        out_specs=pl.BlockSpec((1, b, d), lambda j: (j, 0, 0)),
        out_shape=jax.ShapeDtypeStruct((n6, b, d), F32),
        compiler_params=_cparams(("arbitrary",)),
        name="mod",
    )(c, w_ada, b_ada.reshape(n6, 1, d))
    return out.reshape(n6, b, 1, d)


def _mod_spec(which, d):
    return pl.BlockSpec((None, None, 1, d), lambda b, i, which=which: (which, b, 0, 0))


def _inproj_kernel(x_ref, sc_ref, sh_ref, w_ref, cos_ref, sin_ref, o_ref, *, chunk, q_scale):
    h = (x_ref[...] * (1.0 + sc_ref[...]) + sh_ref[...]).astype(BF16)
    q_r, k_r = 3 * NA_WIDTH // chunk, 3 * NA_WIDTH // chunk + 1
    dh = RET_HEAD_DIM
    for j in range(o_ref.shape[1] // chunk):
        acc = jnp.dot(h, w_ref[:, j * chunk:(j + 1) * chunk], preferred_element_type=F32)
        if j == 0:
            acc = acc * q_scale
        if j in (q_r, k_r):
            cos2, sin2 = cos_ref[...], sin_ref[...]
            heads = [acc[:, hh * dh:(hh + 1) * dh] for hh in range(chunk // dh)]
            heads = [t * cos2 + pltpu.roll(t, dh // 2, 1) * sin2 for t in heads]
            acc = jnp.concatenate(heads, axis=1)
            if j == k_r:
                acc = acc * dh ** -0.5
        o_ref[:, j * chunk:(j + 1) * chunk] = acc.astype(o_ref.dtype)


def _inproj(x, mod, w_in_bf, cos2, sin2):
    b, s, d = x.shape
    e = w_in_bf.shape[1]
    tm = min(TM_INPROJ, s)
    rope = pl.BlockSpec((tm, RET_HEAD_DIM), lambda bi, i: (i, 0))
    return pl.pallas_call(
        functools.partial(_inproj_kernel, chunk=NA_WIDTH, q_scale=NA_HEAD_DIM ** -0.5 * LOG2_E),
        grid=(b, s // tm),
        in_specs=[pl.BlockSpec((None, tm, d), lambda bi, i: (bi, i, 0)),
                  _mod_spec(1, d), _mod_spec(0, d),
                  pl.BlockSpec((d, e), lambda bi, i: (0, 0)), rope, rope],
        out_specs=pl.BlockSpec((None, tm, e), lambda bi, i: (bi, i, 0)),
        out_shape=jax.ShapeDtypeStruct((b, s, e), BF16),
        compiler_params=_cparams(("parallel", "parallel")),
        name="inproj",
    )(x, mod, mod, w_in_bf, cos2, sin2)


def _natten_kernel(q_ref, k_ref, v_ref, bias_ref, o_ref, *, rows, rows_per_iter):
    kspan = WIN_R * GRID_W

    first = lax.broadcasted_iota(I32, (1, LANES), 1) < NA_HEAD_DIM
    zero = jnp.zeros((), BF16)

    def rows_body(i, carry):
        qrows, krows, scores, probs = {}, {}, {}, {}

        def stage_scores(u):
            r = i * rows_per_iter + u
            rs = jnp.clip(r - WIN_R // 2, 0, rows - WIN_R)
            vi = r - rs
            qrows[u] = pl.ds(pl.multiple_of(r * GRID_W, GRID_W), GRID_W)
            krows[u] = pl.ds(pl.multiple_of(rs * GRID_W, GRID_W), kspan)
            q = q_ref[qrows[u], :]
            qm = jnp.concatenate([jnp.where(first, q, zero), jnp.where(first, zero, q)], axis=0)
            st = lax.dot_general(k_ref[krows[u], :], qm, (((1,), (1,)), ((), ())), preferred_element_type=F32)
            scores[u] = st + bias_ref[vi]

        def stage_softmax(u):
            st = scores.pop(u)
            p = jnp.exp2(st - jnp.max(st, axis=0, keepdims=True))
            probs[u] = (p * (1.0 / jnp.sum(p, axis=0, keepdims=True))).astype(BF16)

        def stage_values(u):
            res = lax.dot_general(probs.pop(u), v_ref[krows[u], :], (((0,), (0,)), ((), ())),
                                  preferred_element_type=F32)
            o_ref[qrows[u], :] = jnp.where(first, res[:GRID_W], res[GRID_W:]).astype(o_ref.dtype)

        for step in range(rows_per_iter + 2):
            if step < rows_per_iter:
                stage_scores(step)
            if 0 <= step - 1 < rows_per_iter:
                stage_softmax(step - 1)
            if 0 <= step - 2 < rows_per_iter:
                stage_values(step - 2)
        return carry

    lax.fori_loop(0, rows // rows_per_iter, rows_body, 0)


def _na_bias_table(rpb):
    w = GRID_W
    cq = jnp.arange(w)
    cs = jnp.clip(cq - WIN_C // 2, 0, w - WIN_C)
    ck = jnp.arange(w)
    col_in = (ck[None, :] >= cs[:, None]) & (ck[None, :] < cs[:, None] + WIN_C)
    dc_idx = jnp.clip(ck[None, :] - cq[:, None] + WIN_C - 1, 0, 2 * WIN_C - 2)
    t = rpb[:, :, dc_idx]
    t = jnp.where(col_in[None, None], t, -jnp.inf)
    vi = jnp.arange(WIN_R)
    kr = jnp.arange(WIN_R)
    dr = kr[None, :] - vi[:, None] + WIN_R - 1
    tb = t[:, dr]
    hp = LANES // NA_HEAD_DIM
    tb = tb.reshape(rpb.shape[0] // hp, hp, WIN_R, WIN_R, w, w)
    tb = tb.transpose(0, 2, 3, 5, 1, 4).reshape(rpb.shape[0] // hp, WIN_R, WIN_R * w, hp * w)
    return tb.astype(F32) * LOG2_E


def _natten(proj, bias_tab):
    b, s, _ = proj.shape
    rows = s // GRID_W
    hp = LANES // NA_HEAD_DIM
    npair = NA_HEADS // hp
    blk = lambda off: pl.BlockSpec((None, s, LANES), lambda p, bi, off=off: (bi, 0, off + p))
    return pl.pallas_call(
        functools.partial(_natten_kernel, rows=rows, rows_per_iter=math.gcd(rows, NA_ROWS_PER_ITER)),
        grid=(npair, b),
        in_specs=[blk(0), blk(npair), blk(2 * npair),
                  pl.BlockSpec((None, WIN_R, WIN_R * GRID_W, hp * GRID_W), lambda p, bi: (p, 0, 0, 0))],
        out_specs=pl.BlockSpec((None, s, LANES), lambda p, bi: (bi, 0, p)),
        out_shape=jax.ShapeDtypeStruct((b, s, NA_WIDTH), BF16),
        compiler_params=_cparams(("parallel", "parallel")),
        name="natten",
    )(proj, proj, proj, bias_tab)


def _retent_kernel(ld_ref, q_ref, k_ref, v_ref, g_ref, gn_ref, o_ref, st_ref, *, nchunk):
    c = RET_BLOCK
    dh = RET_HEAD_DIM
    h = pl.program_id(1)
    lgf = ld_ref[0, h]
    lgb = ld_ref[1, h]

    ic = lax.broadcasted_iota(I32, (c, 1), 0).astype(F32)
    ir = lax.broadcasted_iota(I32, (1, c), 1).astype(F32)
    diff = ic - ir
    dmat = jnp.where(diff >= 0, jnp.exp(jnp.maximum(diff, 0.0) * lgf),
                     jnp.exp(jnp.maximum(-diff, 0.0) * lgb))
    kdec_f = jnp.exp((c - 1 - ic) * lgf)
    qdec_f = jnp.exp((ic + 1) * lgf)
    kdec_b = jnp.exp(ic * lgb)
    qdec_b = jnp.exp((c - ic) * lgb)
    one = jnp.ones((1, 1), F32)
    cdec_f = jnp.exp(one * (c * lgf))
    cdec_b = jnp.exp(one * (c * lgb))
    tn = (((0,), (0,)), ((), ()))

    chunks = [pl.ds(n * c, c) for n in range(nchunk)]
    kv = []
    for rows in chunks:
        kn = k_ref[rows, :].astype(F32)
        kd = jnp.concatenate([kn * kdec_f, kn * kdec_b], axis=1).astype(BF16)
        kv.append(lax.dot_general(kd, v_ref[rows, :], tn, preferred_element_type=F32))

    sf = [jnp.zeros((dh, dh), F32)]
    for n in range(1, nchunk):
        sf.append(cdec_f * sf[n - 1] + kv[n - 1][:dh])
    sb = [jnp.zeros((dh, dh), F32)]
    for n in range(nchunk - 2, -1, -1):
        sb.insert(0, cdec_b * sb[0] + kv[n + 1][dh:])
    for n in range(nchunk):
        st_ref[n] = jnp.concatenate([sf[n], sb[n]], axis=0).astype(BF16)

    gn = gn_ref[...]
    for n, rows in enumerate(chunks):
        qb = q_ref[rows, :]
        qn = qb.astype(F32)
        sc = lax.dot_general(qb, k_ref[rows, :], (((1,), (1,)), ((), ())), preferred_element_type=F32) * dmat
        y = jnp.dot(sc.astype(BF16), v_ref[rows, :], preferred_element_type=F32)
        qd = jnp.concatenate([qn * qdec_f, qn * qdec_b], axis=1).astype(BF16)
        y = y + jnp.dot(qd, st_ref[n], preferred_element_type=F32)
        mu = jnp.mean(y, -1, keepdims=True)
        yc = y - mu
        var = jnp.mean(yc * yc, -1, keepdims=True)
        yn = yc * lax.rsqrt(var + GN_EPS) * gn
        o_ref[rows, :] = (_silu(g_ref[rows, :].astype(F32)) * yn).astype(o_ref.dtype)


def _retention(proj, log_decay, gn_g):
    b, s, _ = proj.shape
    dh = RET_HEAD_DIM
    nchunk = s // RET_BLOCK
    base = 3 * NA_WIDTH // dh
    blk = lambda off: pl.BlockSpec((None, s, dh), lambda bi, h, off=off: (bi, 0, base + off + h))
    return pl.pallas_call(
        functools.partial(_retent_kernel, nchunk=nchunk),
        grid=(b, RET_HEADS),
        in_specs=[pl.BlockSpec(memory_space=pltpu.SMEM),
                  blk(0), blk(RET_HEADS), blk(2 * RET_HEADS), blk(3 * RET_HEADS),
                  pl.BlockSpec((1, dh), lambda bi, h: (0, h))],
        out_specs=pl.BlockSpec((None, s, dh), lambda bi, h: (bi, 0, h)),
        out_shape=jax.ShapeDtypeStruct((b, s, RET_WIDTH), BF16),
        scratch_shapes=[pltpu.VMEM((nchunk, 2 * dh, dh), BF16)],
        compiler_params=_cparams(("parallel", "parallel")),
        name="retent",
    )(log_decay, proj, proj, proj, proj, gn_g.reshape(1, RET_WIDTH))


def _outproj_kernel(yna_ref, yr_ref, x_ref, ga_ref, sf_ref, shf_ref, wo1_ref, wo2_ref, g_ref, b_ref,
                    wrh_ref, wrl_ref, x1_ref, hfp_ref, lg_ref, *, alpha):
    nt = (((1,), (1,)), ((), ()))
    tm = x_ref.shape[0]
    parts = [pl.ds(p * (tm // OUTPROJ_PARTS), tm // OUTPROJ_PARTS) for p in range(OUTPROJ_PARTS)]
    def mix_of(r):
        return (jnp.dot(yna_ref[r, :], wo1_ref[...], preferred_element_type=F32)
                + jnp.dot(yr_ref[r, :], wo2_ref[...], preferred_element_type=F32))

    nxt = mix_of(parts[0])
    for p, r in enumerate(parts):
        mix = nxt
        if p + 1 < len(parts):
            nxt = mix_of(parts[p + 1])
        x1 = _layer_norm(alpha * x_ref[r, :] + ga_ref[...] * mix, g_ref[...], b_ref[...])
        x1_ref[r, :] = x1
        hf = x1 * (1.0 + sf_ref[...]) + shf_ref[...]
        for j, w in enumerate(_pack_rows(hf)):
            hfp_ref[j, r, :] = w
        hb = hf.astype(BF16)
        hl = (hf - hb.astype(F32)).astype(BF16)
        lg = lax.dot_general(wrh_ref[...], hb, nt, preferred_element_type=F32)
        lg = lg + lax.dot_general(wrh_ref[...], hl, nt, preferred_element_type=F32)
        lg = lg + lax.dot_general(wrl_ref[...], hb, nt, preferred_element_type=F32)
        lg_ref[:, r] = lg


def _outproj(y_na, y_r, x, mod, w_out_bf, ln_g, ln_b, wr_hi, wr_lo, alpha):
    b, s, d = x.shape
    tm = min(TM_PROJ, s)
    nt = s // tm
    ne = wr_hi.shape[0]
    const = lambda shape: pl.BlockSpec(shape, lambda bi, i: tuple(0 for _ in shape))
    x1, hfp, lg = pl.pallas_call(
        functools.partial(_outproj_kernel, alpha=alpha),
        grid=(b, nt),
        in_specs=[pl.BlockSpec((None, tm, NA_WIDTH), lambda bi, i: (bi, i, 0)),
                  pl.BlockSpec((None, tm, RET_WIDTH), lambda bi, i: (bi, i, 0)),
                  pl.BlockSpec((None, tm, d), lambda bi, i: (bi, i, 0)),
                  _mod_spec(2, d), _mod_spec(4, d), _mod_spec(3, d),
                  pl.BlockSpec((NA_WIDTH, d), lambda bi, i: (0, 0)),
                  pl.BlockSpec((RET_WIDTH, d), lambda bi, i: (1, 0)),
                  const((1, d)), const((1, d)), const((ne, d)), const((ne, d))],
        out_specs=[pl.BlockSpec((None, tm, d), lambda bi, i: (bi, i, 0)),
                   pl.BlockSpec((PACK_W, tm, LANES), lambda bi, i: (0, bi * nt + i, 0)),
                   pl.BlockSpec((ne, tm), lambda bi, i: (0, bi * nt + i))],
        out_shape=[jax.ShapeDtypeStruct((b, s, d), F32),
                   jax.ShapeDtypeStruct((PACK_W, b * s, LANES), I32),
                   jax.ShapeDtypeStruct((ne, b * s), F32)],
        compiler_params=_cparams(("parallel", "parallel")),
        name="outproj",
    )(y_na, y_r, x, mod, mod, mod, w_out_bf, w_out_bf, ln_g.reshape(1, d), ln_b.reshape(1, d), wr_hi, wr_lo)
    return x1, hfp, lg


def _route_kernel(lg_ref, rb_ref, idx_ref, w_ref, rank_ref, cnt_ref):
    t = lg_ref.shape[1]
    ninf = -jnp.inf

    @pl.when(pl.program_id(0) == 0)
    def _():
        cnt_ref[...] = jnp.zeros_like(cnt_ref)

    scores = jax.nn.sigmoid(lg_ref[...])
    sel = scores + rb_ref[...]

    io_g = lax.broadcasted_iota(I32, (GROUP_SIZE, t), 0)
    gs_rows = []
    for g in range(N_GROUPS):
        blk = sel[g * GROUP_SIZE:(g + 1) * GROUP_SIZE, :]
        m1 = jnp.max(blk, axis=0, keepdims=True)
        i1 = jnp.min(jnp.where(blk == m1, io_g, GROUP_SIZE), axis=0, keepdims=True)
        m2 = jnp.max(jnp.where(io_g == i1, ninf, blk), axis=0, keepdims=True)
        gs_rows.append(m1 + m2)
    gs = jnp.concatenate(gs_rows, axis=0)

    io8 = lax.broadcasted_iota(I32, (N_GROUPS, t), 0)
    gsel = jnp.zeros((N_GROUPS, t), F32)
    for _ in range(TOPK_GROUPS):
        m = jnp.max(gs, axis=0, keepdims=True)
        gi = jnp.min(jnp.where(gs == m, io8, N_GROUPS), axis=0, keepdims=True)
        hit = io8 == gi
        gsel = jnp.where(hit, 1.0, gsel)
        gs = jnp.where(hit, ninf, gs)

    masked = jnp.concatenate(
        [jnp.where(gsel[g:g + 1, :] > 0.0, sel[g * GROUP_SIZE:(g + 1) * GROUP_SIZE, :], ninf)
         for g in range(N_GROUPS)], axis=0)

    io_e = lax.broadcasted_iota(I32, (N_EXPERTS, t), 0)
    chosen = jnp.zeros((N_EXPERTS, t), F32)
    idx_rows, w_rows = [], []
    for _ in range(TOP_K):
        m = jnp.max(masked, axis=0, keepdims=True)
        ei = jnp.min(jnp.where(masked == m, io_e, N_EXPERTS), axis=0, keepdims=True)
        hit = io_e == ei
        w_rows.append(jnp.sum(jnp.where(hit, scores, 0.0), axis=0, keepdims=True))
        idx_rows.append(ei)
        chosen = jnp.where(hit, 1.0, chosen)
        masked = jnp.where(hit, ninf, masked)
    wk = jnp.concatenate(w_rows, axis=0)
    w_ref[...] = wk / jnp.sum(wk, axis=0, keepdims=True) * ROUTED_SCALE
    idx_ref[...] = jnp.concatenate(idx_rows, axis=0)

    upper = (lax.broadcasted_iota(I32, (t, t), 0) < lax.broadcasted_iota(I32, (t, t), 1))
    prefix = jnp.dot(chosen.astype(BF16), upper.astype(BF16), preferred_element_type=F32)
    rank_full = prefix + cnt_ref[...]
    rank_rows = [jnp.sum(jnp.where(io_e == ei, rank_full, 0.0), axis=0, keepdims=True) for ei in idx_rows]
    rank_ref[...] = jnp.concatenate(rank_rows, axis=0).astype(I32)
    cnt_ref[...] += jnp.sum(chosen, axis=1, keepdims=True)


def _route(logits_t, router_bias):
    ne, n = logits_t.shape
    t = min(T_ROUTE, n)
    kspec = pl.BlockSpec((TOP_K, t), lambda i: (0, i))
    return pl.pallas_call(
        _route_kernel,
        grid=(n // t,),
        in_specs=[pl.BlockSpec((ne, t), lambda i: (0, i)),
                  pl.BlockSpec((ne, 1), lambda i: (0, 0))],
        out_specs=[kspec, kspec, kspec, pl.BlockSpec((ne, 1), lambda i: (0, 0))],
        out_shape=[jax.ShapeDtypeStruct((TOP_K, n), I32),
                   jax.ShapeDtypeStruct((TOP_K, n), F32),
                   jax.ShapeDtypeStruct((TOP_K, n), I32),
                   jax.ShapeDtypeStruct((ne, 1), F32)],
        compiler_params=_cparams(("arbitrary",)),
        name="route",
    )(logits_t, router_bias.reshape(ne, 1))


def _block(ref, g):
    return ref.at[:, pl.ds(pl.multiple_of(g * BLK_E, BLK_E), BLK_E), :]


def _slots_kernel(idx_ref, rank_ref, ps_ref, o_ref, *, n_slots):
    t = idx_ref.shape[1]
    io = lax.broadcasted_iota(I32, (N_EXPERTS, t), 0)
    ps = ps_ref[...]
    for k in range(TOP_K):
        hit = io == idx_ref[k:k + 1, :]
        slot = jnp.sum(jnp.where(hit, ps, 0), axis=0, keepdims=True) + rank_ref[k:k + 1, :]
        for j in range(PACK_W):
            o_ref[j * TOP_K + k:j * TOP_K + k + 1, :] = slot + j * n_slots


def _slots(idx, rank, pstarts, n_slots):
    n = idx.shape[1]
    t = min(T_SLOT, n)
    kspec = pl.BlockSpec((TOP_K, t), lambda i: (0, i))
    return pl.pallas_call(
        functools.partial(_slots_kernel, n_slots=n_slots),
        grid=(n // t,),
        in_specs=[kspec, kspec, pl.BlockSpec((N_EXPERTS, 1), lambda i: (0, 0))],
        out_specs=pl.BlockSpec((PACK_W * TOP_K, t), lambda i: (0, i)),
        out_shape=jax.ShapeDtypeStruct((PACK_W * TOP_K, n), I32),
        compiler_params=_cparams(("parallel",)),
        name="slots",
    )(idx, rank, pstarts.reshape(N_EXPERTS, 1))


def _sc_mesh():
    return plsc.VectorSubcoreMesh(core_axis_name="core", subcore_axis_name="subcore")


def _scatter_rows_sc(rows, dest_rows, n_out):
    n_rows = rows.shape[0]
    wpp = n_rows // PACK_W // SC_WINDOW

    @functools.partial(pl.kernel, mesh=_sc_mesh(), scratch_types=[pltpu.SemaphoreType.DMA],
                       out_type=jax.ShapeDtypeStruct((n_out, LANES), I32))
    def scatter_rows(x_hbm, i_hbm, o_hbm, sem):
        def body(x_vmem, i_vmem):
            copies = [pltpu.async_copy(x_vmem, o_hbm.at[i_vmem.at[k]], sem) for k in range(TOP_K)]
            for c in copies:
                c.wait()

        pltpu.emit_pipeline(
            body,
            grid=(n_rows // SC_WINDOW,),
            in_specs=[pl.BlockSpec((SC_WINDOW, LANES), lambda i: (i, 0)),
                      pl.BlockSpec((TOP_K, SC_WINDOW), lambda i: (i // wpp, i % wpp))],
            out_specs=[],
            core_axis_name=("core", "subcore"),
            dimension_semantics=(pltpu.PARALLEL,),
        )(x_hbm, i_hbm)

    return scatter_rows(rows, dest_rows)


def _gather_rows_sc(rows, slot_rows, tok0, nc):
    nr = slot_rows.shape[0]
    span = SC_GATHER_WINDOWS * SC_WINDOW
    spr = nc // span
    w0 = tok0 // SC_WINDOW

    @functools.partial(pl.kernel, mesh=_sc_mesh(), scratch_types=[pltpu.SemaphoreType.DMA],
                       out_type=jax.ShapeDtypeStruct((nr * nc, LANES), I32))
    def gather_rows(x_hbm, i_hbm, o_hbm, sem):
        def body(*refs):
            o_vmem = refs[-1]
            copies = [pltpu.async_copy(x_hbm.at[i_vmem.at[0]], o_vmem.at[pl.ds(w * SC_WINDOW, SC_WINDOW)], sem)
                      for w, i_vmem in enumerate(refs[:-1])]
            for c in copies:
                c.wait()

        def idx_spec(w):
            return pl.BlockSpec((1, SC_WINDOW),
                                lambda i: (i // spr, w0 + (i % spr) * SC_GATHER_WINDOWS + w))

        pltpu.emit_pipeline(
            body,
            grid=(nr * spr,),
            in_specs=[idx_spec(w) for w in range(SC_GATHER_WINDOWS)],
            out_specs=[pl.BlockSpec((span, LANES), lambda i: (i, 0))],
            core_axis_name=("core", "subcore"),
            dimension_semantics=(pltpu.PARALLEL,),
        )(*([i_hbm] * SC_GATHER_WINDOWS), o_hbm)

    return gather_rows(rows, slot_rows)


def _experts_kernel(nblk_ref, gstart_ref, cnt_ref, nb_ref, xs_ref, wg_ref, wu_ref, wd_ref, ys_ref,
                    xbuf_ref, ybuf_ref, wgub_ref, wdb_ref, xsem, ysem):
    e = pl.program_id(0)
    total = nb_ref[0]
    nb_max = ys_ref.shape[1] // BLK_E

    def x_copy(g):
        slot = g % X_RING
        return pltpu.make_async_copy(_block(xs_ref, g), xbuf_ref.at[slot], xsem.at[slot])

    def y_copy(g):
        slot = g % Y_RING
        return pltpu.make_async_copy(ybuf_ref.at[slot], _block(ys_ref, g), ysem.at[slot])

    @pl.when(e == 0)
    def _():
        for g in range(X_AHEAD):
            @pl.when(g < total)
            def _():
                x_copy(g).start()

    @pl.when(nblk_ref[e] > 0)
    def _():
        f = wg_ref.shape[1]
        wgub_ref[:, :f] = wg_ref[...].astype(BF16)
        wgub_ref[:, f:] = wu_ref[...].astype(BF16)
        wdb_ref[...] = wd_ref[...].astype(BF16)

    def fetch(b):
        g = gstart_ref[e] + b
        x_copy(g).wait()

        @pl.when(g + X_AHEAD < total)
        def _():
            x_copy(g + X_AHEAD).start()

        xb = _unpack_words([xbuf_ref[g % X_RING, j] for j in range(PACK_W)]).astype(BF16)
        live = lax.broadcasted_iota(I32, (BLK_E, 1), 0) < cnt_ref[e] - b * BLK_E
        return jnp.where(live, xb, jnp.zeros((), BF16))

    def hidden(xb):
        f = wg_ref.shape[1]
        hgu = jnp.dot(xb, wgub_ref[...], preferred_element_type=F32)
        return (_silu(hgu[:, :f]) * hgu[:, f:]).astype(BF16)

    def finish(b, act):
        g = gstart_ref[e] + b
        words = _pack_rows(jnp.dot(act, wdb_ref[...], preferred_element_type=F32))

        @pl.when(g >= Y_RING)
        def _():
            y_copy(g - Y_RING).wait()

        for j, w in enumerate(words):
            ybuf_ref[g % Y_RING, j] = w
        y_copy(g).start()

    def block(b, carry):
        finish(b, hidden(fetch(b)))
        return carry

    lax.fori_loop(0, nblk_ref[e], block, 0)

    @pl.when(e == pl.num_programs(0) - 1)
    def _():
        for back in range(Y_RING):
            @pl.when(total - 1 - back >= 0)
            def _():
                y_copy(total - 1 - back).wait()

        ybuf_ref[0] = jnp.zeros(ybuf_ref.shape[1:], ybuf_ref.dtype)

        def tail_copy(g):
            return pltpu.make_async_copy(ybuf_ref.at[0], _block(ys_ref, g), ysem.at[0])

        def fill(g, carry):
            tail_copy(g).start()
            return carry

        lax.fori_loop(total, nb_max, fill, 0)

        def drain(g, carry):
            tail_copy(g).wait()
            return carry

        lax.fori_loop(total, nb_max, drain, 0)


def _experts(nblk, gstart, cnt, n_blocks, xs, w_gate, w_up, w_down):
    p = xs.shape[1]
    ne, d, f = w_gate.shape
    wspec = lambda shape: pl.BlockSpec((None,) + shape, lambda e, *_: (e, 0, 0))
    grid_spec = pltpu.PrefetchScalarGridSpec(
        num_scalar_prefetch=4,
        grid=(ne,),
        in_specs=[pl.BlockSpec(memory_space=pl.ANY), wspec((d, f)), wspec((d, f)), wspec((f, d))],
        out_specs=pl.BlockSpec(memory_space=pl.ANY),
        scratch_shapes=[pltpu.VMEM((X_RING, PACK_W, BLK_E, LANES), I32),
                        pltpu.VMEM((Y_RING, PACK_W, BLK_E, LANES), I32),
                        pltpu.VMEM((d, 2 * f), BF16), pltpu.VMEM((f, d), BF16),
                        pltpu.SemaphoreType.DMA((X_RING,)), pltpu.SemaphoreType.DMA((Y_RING,))],
    )
    return pl.pallas_call(
        _experts_kernel,
        grid_spec=grid_spec,
        out_shape=jax.ShapeDtypeStruct((PACK_W, p, LANES), I32),
        compiler_params=_cparams(("arbitrary",)),
        name="experts",
    )(nblk, gstart, cnt, n_blocks, xs, w_gate, w_up, w_down)


def _combine_kernel(wt_ref, yk_ref, x1_ref, sf_ref, shf_ref, gf_ref, wsg_ref, wsu_ref, wsd_ref, g_ref, b_ref,
                    *rest, alpha):
    o_ref = rest[-1]
    x1 = x1_ref[...]
    hb = (x1 * (1.0 + sf_ref[...]) + shf_ref[...]).astype(BF16)
    sg = jnp.dot(hb, wsg_ref[...], preferred_element_type=F32)
    su = jnp.dot(hb, wsu_ref[...], preferred_element_type=F32)
    ffn = jnp.dot((_silu(sg) * su).astype(BF16), wsd_ref[...], preferred_element_type=F32)
    wt = wt_ref[...].T
    for k in range(TOP_K):
        yk = _unpack_words([yk_ref[j, k] for j in range(PACK_W)])
        ffn = ffn + wt[:, k:k + 1] * yk
    o_ref[...] = _layer_norm(alpha * x1 + gf_ref[...] * ffn, g_ref[...], b_ref[...])


def _combine(wts, yk, x1, mod, ws_gate_bf, ws_up_bf, ws_down_bf, ln_g, ln_b, alpha, tok0, earlier):
    n, d = x1.shape
    nc = yk.shape[2]
    s = n // mod.shape[1]
    tc = min(T_COMB, s, nc)
    nt = s // tc
    t0 = tok0 // tc
    fs = ws_gate_bf.shape[1]
    const = lambda shape: pl.BlockSpec(shape, lambda i: tuple(0 for _ in shape))
    mod_spec = lambda which: pl.BlockSpec((None, None, 1, d), lambda i: (which, (t0 + i) // nt, 0, 0))
    in_specs = [pl.BlockSpec((TOP_K, tc), lambda i: (0, t0 + i)),
                pl.BlockSpec((PACK_W, TOP_K, tc, LANES), lambda i: (0, 0, i, 0)),
                pl.BlockSpec((tc, d), lambda i: (t0 + i, 0)),
                mod_spec(4), mod_spec(3), mod_spec(5),
                const((d, fs)), const((d, fs)), const((fs, d)), const((1, d)), const((1, d))]
    args = [wts, yk, x1, mod, mod, mod, ws_gate_bf, ws_up_bf, ws_down_bf, ln_g.reshape(1, d), ln_b.reshape(1, d)]
    aliases = {}
    if earlier is not None:
        in_specs.append(pl.BlockSpec(memory_space=pl.ANY))
        args.append(earlier)
        aliases = {len(args) - 1: 0}
    return pl.pallas_call(
        functools.partial(_combine_kernel, alpha=alpha),
        grid=(nc // tc,),
        in_specs=in_specs,
        out_specs=pl.BlockSpec((tc, d), lambda i: (t0 + i, 0)),
        out_shape=jax.ShapeDtypeStruct((n, d), F32),
        input_output_aliases=aliases,
        compiler_params=_cparams(("parallel",)),
        name="combine",
    )(*args)


def _slot_layout(counts, n_assign):
    cnt = counts[:, 0].astype(I32)
    padded = (cnt + BLK_E - 1) // BLK_E * BLK_E
    pends = jnp.cumsum(padded)
    pstarts = pends - padded
    n_blocks_max = (n_assign + N_EXPERTS * (BLK_E - 1)) // BLK_E
    n_blocks = (pends[-1] // BLK_E).astype(I32).reshape(1)
    return pstarts, padded // BLK_E, pstarts // BLK_E, n_blocks, n_blocks_max * BLK_E


def kernel(x, c, w_ada, b_ada, w_in, w_out, na_rpb, ret_log_decay, ret_gn_g, ln1_g, ln1_b, ln2_g, ln2_b,
           w_router, router_bias, w_gate, w_up, w_down, ws_gate, ws_up, ws_down):
    b, s, d = x.shape
    depth = w_ada.shape[0]
    alpha = (2.0 * depth) ** 0.25
    t = jnp.arange(s, dtype=F32)
    inv_freq = ROPE_BASE ** (-jnp.arange(0, RET_HEAD_DIM, 2, dtype=F32) / RET_HEAD_DIM)
    ang = t[:, None] * inv_freq[None, :]
    cos, sin = jnp.cos(ang), jnp.sin(ang)
    cos2 = jnp.concatenate([cos, cos], axis=-1)
    sin2 = jnp.concatenate([-sin, sin], axis=-1)
    for l in range(depth):
        mod = _mod(c, w_ada[l], b_ada[l])
        proj = _inproj(x, mod, w_in[l].astype(BF16), cos2, sin2)
        y_na = _natten(proj, _na_bias_table(na_rpb[l]))
        y_r = _retention(proj, ret_log_decay[l], ret_gn_g[l])
        wr_t = w_router[l].T
        wr_hi = wr_t.astype(BF16)
        wr_lo = (wr_t - wr_hi.astype(F32)).astype(BF16)
        x1, hfp, logits_t = _outproj(y_na, y_r, x, mod, w_out[l].astype(BF16), ln1_g[l], ln1_b[l],
                                     wr_hi, wr_lo, alpha)
        idx, wts, rank, counts = _route(logits_t, router_bias[l])
        n = b * s
        pstarts, nblk, gstart, n_blocks, n_slots = _slot_layout(counts, n * TOP_K)
        slot_rows = _slots(idx, rank, pstarts, n_slots)
        xs = _scatter_rows_sc(hfp.reshape(PACK_W * n, LANES), slot_rows, PACK_W * n_slots)
        ys = _experts(nblk, gstart, counts[:, 0].astype(I32), n_blocks, xs.reshape(PACK_W, n_slots, LANES),
                      w_gate[l], w_up[l], w_down[l]).reshape(PACK_W * n_slots, LANES)
        ws = (ws_gate[l].astype(BF16), ws_up[l].astype(BF16), ws_down[l].astype(BF16))
        x1n = x1.reshape(n, d)
        nc = n // COMBINE_CHUNKS
        out = None
        for ci in range(COMBINE_CHUNKS):
            yk = _gather_rows_sc(ys, slot_rows, ci * nc, nc).reshape(PACK_W, TOP_K, nc, LANES)
            out = _combine(wts, yk, x1n, mod, *ws, ln2_g[l], ln2_b[l], alpha, ci * nc, out)
        x = out.reshape(b, s, d)
    return x
```

```python
import functools
import math

import jax
import jax.numpy as jnp
from jax import lax
from jax.experimental import pallas as pl
from jax.experimental.pallas import tpu as pltpu
from jax.experimental.pallas import tpu_sc as plsc

F32 = jnp.float32
BF16 = jnp.bfloat16
U32 = jnp.uint32
I32 = jnp.int32

GRID_W = 64
WIN_R = 8
WIN_C = 16
NA_HEADS = 8
NA_HEAD_DIM = 64
NA_WIDTH = NA_HEADS * NA_HEAD_DIM
RET_HEADS = 4
RET_HEAD_DIM = 128
RET_WIDTH = RET_HEADS * RET_HEAD_DIM
ROPE_BASE = 10000.0
N_EXPERTS = 256
TOP_K = 8
N_GROUPS = 8
TOPK_GROUPS = 4
GROUP_SIZE = N_EXPERTS // N_GROUPS
ROUTED_SCALE = 2.5
LOG2_E = 1.4426950408889634
LN_EPS = 1e-5
GN_EPS = 1e-6

LANES = 128
VMEM_BYTES = 64 * 1024 * 1024
VMEM_LIMIT = VMEM_BYTES * 7 // 8

TM_INPROJ = 1024
TM_PROJ = 1024
T_ROUTE = 512
T_SLOT = 1024
SC_WINDOW = 128
SC_GATHER_WINDOWS = 2
BLK_E = 512
OUTPROJ_PARTS = 2
T_COMB = 512
COMBINE_CHUNKS = 8
PACK_W = 4
NA_ROWS_PER_ITER = 32
RET_BLOCK = 256
X_RING = 4
X_AHEAD = 3
Y_RING = 3


def _cparams(sem, vmem=VMEM_LIMIT):
    return pltpu.CompilerParams(dimension_semantics=sem, vmem_limit_bytes=vmem)


def _silu(v):
    return v * jax.nn.sigmoid(v)


def _layer_norm(z, g, b):
    mu = jnp.mean(z, -1, keepdims=True)
    zc = z - mu
    var = jnp.mean(zc * zc, -1, keepdims=True)
    return zc * lax.rsqrt(var + LN_EPS) * g + b


def _pack_rows(v):
    half = v.shape[1] // 2
    vb = v.astype(BF16)
    lo = lax.bitcast_convert_type(vb[:, :half].astype(F32), U32) >> 16
    hi = lax.bitcast_convert_type(vb[:, half:].astype(F32), U32)
    w = lax.bitcast_convert_type(hi | lo, I32)
    return [w[:, j * LANES:(j + 1) * LANES] for j in range(half // LANES)]


def _unpack_words(words):
    words = [lax.bitcast_convert_type(w, U32) for w in words]
    lo = [lax.bitcast_convert_type(w << 16, F32) for w in words]
    hi = [lax.bitcast_convert_type(w & jnp.uint32(0xFFFF0000), F32) for w in words]
    return jnp.concatenate(lo + hi, axis=-1)


def _mod_kernel(c_ref, w_ref, b_ref, o_ref):
    cond = _silu(c_ref[...])
    o_ref[0] = jnp.dot(cond, w_ref[...], precision=lax.Precision.HIGHEST,
                       preferred_element_type=F32) + b_ref[0]


def _mod(c, w_ada, b_ada):
    b, d = c.shape
    n6 = w_ada.shape[1] // d
    out = pl.pallas_call(
        _mod_kernel,
        grid=(n6,),
        in_specs=[pl.BlockSpec((b, d), lambda j: (0, 0)),
                  pl.BlockSpec((d, d), lambda j: (0, j)),
                  pl.BlockSpec((1, 1, d), lambda j: (j, 0, 0))],
        out_specs=pl.BlockSpec((1, b, d), lambda j: (j, 0, 0)),
        out_shape=jax.ShapeDtypeStruct((n6, b, d), F32),
        compiler_params=_cparams(("arbitrary",)),
        name="mod",
    )(c, w_ada, b_ada.reshape(n6, 1, d))
    return out.reshape(n6, b, 1, d)


def _mod_spec(which, d):
    return pl.BlockSpec((None, None, 1, d), lambda b, i, which=which: (which, b, 0, 0))


def _inproj_kernel(x_ref, sc_ref, sh_ref, w_ref, cos_ref, sin_ref, o_ref, *, chunk, q_scale):
    h = (x_ref[...] * (1.0 + sc_ref[...]) + sh_ref[...]).astype(BF16)
    q_r, k_r = 3 * NA_WIDTH // chunk, 3 * NA_WIDTH // chunk + 1
    dh = RET_HEAD_DIM
    for j in range(o_ref.shape[1] // chunk):
        acc = jnp.dot(h, w_ref[:, j * chunk:(j + 1) * chunk], preferred_element_type=F32)
        if j == 0:
            acc = acc * q_scale
        if j in (q_r, k_r):
            cos2, sin2 = cos_ref[...], sin_ref[...]
            heads = [acc[:, hh * dh:(hh + 1) * dh] for hh in range(chunk // dh)]
            heads = [t * cos2 + pltpu.roll(t, dh // 2, 1) * sin2 for t in heads]
            acc = jnp.concatenate(heads, axis=1)
            if j == k_r:
                acc = acc * dh ** -0.5
        o_ref[:, j * chunk:(j + 1) * chunk] = acc.astype(o_ref.dtype)


def _inproj(x, mod, w_in_bf, cos2, sin2):
    b, s, d = x.shape
    e = w_in_bf.shape[1]
    tm = min(TM_INPROJ, s)
    rope = pl.BlockSpec((tm, RET_HEAD_DIM), lambda bi, i: (i, 0))
    return pl.pallas_call(
        functools.partial(_inproj_kernel, chunk=NA_WIDTH, q_scale=NA_HEAD_DIM ** -0.5 * LOG2_E),
        grid=(b, s // tm),
        in_specs=[pl.BlockSpec((None, tm, d), lambda bi, i: (bi, i, 0)),
                  _mod_spec(1, d), _mod_spec(0, d),
                  pl.BlockSpec((d, e), lambda bi, i: (0, 0)), rope, rope],
        out_specs=pl.BlockSpec((None, tm, e), lambda bi, i: (bi, i, 0)),
        out_shape=jax.ShapeDtypeStruct((b, s, e), BF16),
        compiler_params=_cparams(("parallel", "parallel")),
        name="inproj",
    )(x, mod, mod, w_in_bf, cos2, sin2)


def _natten_kernel(q_ref, k_ref, v_ref, bias_ref, o_ref, *, rows, rows_per_iter):
    kspan = WIN_R * GRID_W

    first = lax.broadcasted_iota(I32, (1, LANES), 1) < NA_HEAD_DIM
    zero = jnp.zeros((), BF16)

    def rows_body(i, carry):
        qrows, krows, scores, probs = {}, {}, {}, {}

        def stage_scores(u):
            r = i * rows_per_iter + u
            rs = jnp.clip(r - WIN_R // 2, 0, rows - WIN_R)
            vi = r - rs
            qrows[u] = pl.ds(pl.multiple_of(r * GRID_W, GRID_W), GRID_W)
            krows[u] = pl.ds(pl.multiple_of(rs * GRID_W, GRID_W), kspan)
            q = q_ref[qrows[u], :]
            qm = jnp.concatenate([jnp.where(first, q, zero), jnp.where(first, zero, q)], axis=0)
            st = lax.dot_general(k_ref[krows[u], :], qm, (((1,), (1,)), ((), ())), preferred_element_type=F32)
            scores[u] = st + bias_ref[vi]

        def stage_softmax(u):
            st = scores.pop(u)
            p = jnp.exp2(st - jnp.max(st, axis=0, keepdims=True))
            probs[u] = (p * (1.0 / jnp.sum(p, axis=0, keepdims=True))).astype(BF16)

        def stage_values(u):
            res = lax.dot_general(probs.pop(u), v_ref[krows[u], :], (((0,), (0,)), ((), ())),
                                  preferred_element_type=F32)
            o_ref[qrows[u], :] = jnp.where(first, res[:GRID_W], res[GRID_W:]).astype(o_ref.dtype)

        for step in range(rows_per_iter + 2):
            if step < rows_per_iter:
                stage_scores(step)
            if 0 <= step - 1 < rows_per_iter:
                stage_softmax(step - 1)
            if 0 <= step - 2 < rows_per_iter:
                stage_values(step - 2)
        return carry

    lax.fori_loop(0, rows // rows_per_iter, rows_body, 0)


def _na_bias_table(rpb):
    w = GRID_W
    cq = jnp.arange(w)
    cs = jnp.clip(cq - WIN_C // 2, 0, w - WIN_C)
    ck = jnp.arange(w)
    col_in = (ck[None, :] >= cs[:, None]) & (ck[None, :] < cs[:, None] + WIN_C)
    dc_idx = jnp.clip(ck[None, :] - cq[:, None] + WIN_C - 1, 0, 2 * WIN_C - 2)
    t = rpb[:, :, dc_idx]
    t = jnp.where(col_in[None, None], t, -jnp.inf)
    vi = jnp.arange(WIN_R)
    kr = jnp.arange(WIN_R)
    dr = kr[None, :] - vi[:, None] + WIN_R - 1
    tb = t[:, dr]
    hp = LANES // NA_HEAD_DIM
    tb = tb.reshape(rpb.shape[0] // hp, hp, WIN_R, WIN_R, w, w)
    tb = tb.transpose(0, 2, 3, 5, 1, 4).reshape(rpb.shape[0] // hp, WIN_R, WIN_R * w, hp * w)
    return tb.astype(F32) * LOG2_E


def _natten(proj, bias_tab):
    b, s, _ = proj.shape
    rows = s // GRID_W
    hp = LANES // NA_HEAD_DIM
    npair = NA_HEADS // hp
    blk = lambda off: pl.BlockSpec((None, s, LANES), lambda p, bi, off=off: (bi, 0, off + p))
    return pl.pallas_call(
        functools.partial(_natten_kernel, rows=rows, rows_per_iter=math.gcd(rows, NA_ROWS_PER_ITER)),
        grid=(npair, b),
        in_specs=[blk(0), blk(npair), blk(2 * npair),
                  pl.BlockSpec((None, WIN_R, WIN_R * GRID_W, hp * GRID_W), lambda p, bi: (p, 0, 0, 0))],
        out_specs=pl.BlockSpec((None, s, LANES), lambda p, bi: (bi, 0, p)),
        out_shape=jax.ShapeDtypeStruct((b, s, NA_WIDTH), BF16),
        compiler_params=_cparams(("parallel", "parallel")),
        name="natten",
    )(proj, proj, proj, bias_tab)


def _retent_kernel(ld_ref, q_ref, k_ref, v_ref, g_ref, gn_ref, o_ref, st_ref, *, nchunk):
    c = RET_BLOCK
    dh = RET_HEAD_DIM
    h = pl.program_id(1)
    lgf = ld_ref[0, h]
    lgb = ld_ref[1, h]

    ic = lax.broadcasted_iota(I32, (c, 1), 0).astype(F32)
    ir = lax.broadcasted_iota(I32, (1, c), 1).astype(F32)
    diff = ic - ir
    dmat = jnp.where(diff >= 0, jnp.exp(jnp.maximum(diff, 0.0) * lgf),
                     jnp.exp(jnp.maximum(-diff, 0.0) * lgb))
    kdec_f = jnp.exp((c - 1 - ic) * lgf)
    qdec_f = jnp.exp((ic + 1) * lgf)
    kdec_b = jnp.exp(ic * lgb)
    qdec_b = jnp.exp((c - ic) * lgb)
    one = jnp.ones((1, 1), F32)
    cdec_f = jnp.exp(one * (c * lgf))
    cdec_b = jnp.exp(one * (c * lgb))
    tn = (((0,), (0,)), ((), ()))

    chunks = [pl.ds(n * c, c) for n in range(nchunk)]
    kv = []
    for rows in chunks:
        kn = k_ref[rows, :].astype(F32)
        kd = jnp.concatenate([kn * kdec_f, kn * kdec_b], axis=1).astype(BF16)
        kv.append(lax.dot_general(kd, v_ref[rows, :], tn, preferred_element_type=F32))

    sf = [jnp.zeros((dh, dh), F32)]
    for n in range(1, nchunk):
        sf.append(cdec_f * sf[n - 1] + kv[n - 1][:dh])
    sb = [jnp.zeros((dh, dh), F32)]
    for n in range(nchunk - 2, -1, -1):
        sb.insert(0, cdec_b * sb[0] + kv[n + 1][dh:])
    for n in range(nchunk):
        st_ref[n] = jnp.concatenate([sf[n], sb[n]], axis=0).astype(BF16)

    gn = gn_ref[...]
    for n, rows in enumerate(chunks):
        qb = q_ref[rows, :]
        qn = qb.astype(F32)
        sc = lax.dot_general(qb, k_ref[rows, :], (((1,), (1,)), ((), ())), preferred_element_type=F32) * dmat
        y = jnp.dot(sc.astype(BF16), v_ref[rows, :], preferred_element_type=F32)
        qd = jnp.concatenate([qn * qdec_f, qn * qdec_b], axis=1).astype(BF16)
        y = y + jnp.dot(qd, st_ref[n], preferred_element_type=F32)
        mu = jnp.mean(y, -1, keepdims=True)
        yc = y - mu
        var = jnp.mean(yc * yc, -1, keepdims=True)
        yn = yc * lax.rsqrt(var + GN_EPS) * gn
        o_ref[rows, :] = (_silu(g_ref[rows, :].astype(F32)) * yn).astype(o_ref.dtype)


def _retention(proj, log_decay, gn_g):
    b, s, _ = proj.shape
    dh = RET_HEAD_DIM
    nchunk = s // RET_BLOCK
    base = 3 * NA_WIDTH // dh
    blk = lambda off: pl.BlockSpec((None, s, dh), lambda bi, h, off=off: (bi, 0, base + off + h))
    return pl.pallas_call(
        functools.partial(_retent_kernel, nchunk=nchunk),
        grid=(b, RET_HEADS),
        in_specs=[pl.BlockSpec(memory_space=pltpu.SMEM),
                  blk(0), blk(RET_HEADS), blk(2 * RET_HEADS), blk(3 * RET_HEADS),
                  pl.BlockSpec((1, dh), lambda bi, h: (0, h))],
        out_specs=pl.BlockSpec((None, s, dh), lambda bi, h: (bi, 0, h)),
        out_shape=jax.ShapeDtypeStruct((b, s, RET_WIDTH), BF16),
        scratch_shapes=[pltpu.VMEM((nchunk, 2 * dh, dh), BF16)],
        compiler_params=_cparams(("parallel", "parallel")),
        name="retent",
    )(log_decay, proj, proj, proj, proj, gn_g.reshape(1, RET_WIDTH))


def _outproj_kernel(yna_ref, yr_ref, x_ref, ga_ref, sf_ref, shf_ref, wo1_ref, wo2_ref, g_ref, b_ref,
                    wrh_ref, wrl_ref, x1_ref, hfp_ref, lg_ref, *, alpha):
    nt = (((1,), (1,)), ((), ()))
    tm = x_ref.shape[0]
    parts = [pl.ds(p * (tm // OUTPROJ_PARTS), tm // OUTPROJ_PARTS) for p in range(OUTPROJ_PARTS)]
    def mix_of(r):
        return (jnp.dot(yna_ref[r, :], wo1_ref[...], preferred_element_type=F32)
                + jnp.dot(yr_ref[r, :], wo2_ref[...], preferred_element_type=F32))

    nxt = mix_of(parts[0])
    for p, r in enumerate(parts):
        mix = nxt
        if p + 1 < len(parts):
            nxt = mix_of(parts[p + 1])
        x1 = _layer_norm(alpha * x_ref[r, :] + ga_ref[...] * mix, g_ref[...], b_ref[...])
        x1_ref[r, :] = x1
        hf = x1 * (1.0 + sf_ref[...]) + shf_ref[...]
        for j, w in enumerate(_pack_rows(hf)):
            hfp_ref[j, r, :] = w
        hb = hf.astype(BF16)
        hl = (hf - hb.astype(F32)).astype(BF16)
        lg = lax.dot_general(wrh_ref[...], hb, nt, preferred_element_type=F32)
        lg = lg + lax.dot_general(wrh_ref[...], hl, nt, preferred_element_type=F32)
        lg = lg + lax.dot_general(wrl_ref[...], hb, nt, preferred_element_type=F32)
        lg_ref[:, r] = lg


def _outproj(y_na, y_r, x, mod, w_out_bf, ln_g, ln_b, wr_hi, wr_lo, alpha):
    b, s, d = x.shape
    tm = min(TM_PROJ, s)
    nt = s // tm
    ne = wr_hi.shape[0]
    const = lambda shape: pl.BlockSpec(shape, lambda bi, i: tuple(0 for _ in shape))
    x1, hfp, lg = pl.pallas_call(
        functools.partial(_outproj_kernel, alpha=alpha),
        grid=(b, nt),
        in_specs=[pl.BlockSpec((None, tm, NA_WIDTH), lambda bi, i: (bi, i, 0)),
                  pl.BlockSpec((None, tm, RET_WIDTH), lambda bi, i: (bi, i, 0)),
                  pl.BlockSpec((None, tm, d), lambda bi, i: (bi, i, 0)),
                  _mod_spec(2, d), _mod_spec(4, d), _mod_spec(3, d),
                  pl.BlockSpec((NA_WIDTH, d), lambda bi, i: (0, 0)),
                  pl.BlockSpec((RET_WIDTH, d), lambda bi, i: (1, 0)),
                  const((1, d)), const((1, d)), const((ne, d)), const((ne, d))],
        out_specs=[pl.BlockSpec((None, tm, d), lambda bi, i: (bi, i, 0)),
                   pl.BlockSpec((PACK_W, tm, LANES), lambda bi, i: (0, bi * nt + i, 0)),
                   pl.BlockSpec((ne, tm), lambda bi, i: (0, bi * nt + i))],
        out_shape=[jax.ShapeDtypeStruct((b, s, d), F32),
                   jax.ShapeDtypeStruct((PACK_W, b * s, LANES), I32),
                   jax.ShapeDtypeStruct((ne, b * s), F32)],
        compiler_params=_cparams(("parallel", "parallel")),
        name="outproj",
    )(y_na, y_r, x, mod, mod, mod, w_out_bf, w_out_bf, ln_g.reshape(1, d), ln_b.reshape(1, d), wr_hi, wr_lo)
    return x1, hfp, lg


def _route_kernel(lg_ref, rb_ref, idx_ref, w_ref, rank_ref, cnt_ref):
    t = lg_ref.shape[1]
    ninf = -jnp.inf

    @pl.when(pl.program_id(0) == 0)
    def _():
        cnt_ref[...] = jnp.zeros_like(cnt_ref)

    scores = jax.nn.sigmoid(lg_ref[...])
    sel = scores + rb_ref[...]

    io_g = lax.broadcasted_iota(I32, (GROUP_SIZE, t), 0)
    gs_rows = []
    for g in range(N_GROUPS):
        blk = sel[g * GROUP_SIZE:(g + 1) * GROUP_SIZE, :]
        m1 = jnp.max(blk, axis=0, keepdims=True)
        i1 = jnp.min(jnp.where(blk == m1, io_g, GROUP_SIZE), axis=0, keepdims=True)
        m2 = jnp.max(jnp.where(io_g == i1, ninf, blk), axis=0, keepdims=True)
        gs_rows.append(m1 + m2)
    gs = jnp.concatenate(gs_rows, axis=0)

    io8 = lax.broadcasted_iota(I32, (N_GROUPS, t), 0)
    gsel = jnp.zeros((N_GROUPS, t), F32)
    for _ in range(TOPK_GROUPS):
        m = jnp.max(gs, axis=0, keepdims=True)
        gi = jnp.min(jnp.where(gs == m, io8, N_GROUPS), axis=0, keepdims=True)
        hit = io8 == gi
        gsel = jnp.where(hit, 1.0, gsel)
        gs = jnp.where(hit, ninf, gs)

    masked = jnp.concatenate(
        [jnp.where(gsel[g:g + 1, :] > 0.0, sel[g * GROUP_SIZE:(g + 1) * GROUP_SIZE, :], ninf)
         for g in range(N_GROUPS)], axis=0)

    io_e = lax.broadcasted_iota(I32, (N_EXPERTS, t), 0)
    chosen = jnp.zeros((N_EXPERTS, t), F32)
    idx_rows, w_rows = [], []
    for _ in range(TOP_K):
        m = jnp.max(masked, axis=0, keepdims=True)
        ei = jnp.min(jnp.where(masked == m, io_e, N_EXPERTS), axis=0, keepdims=True)
        hit = io_e == ei
        w_rows.append(jnp.sum(jnp.where(hit, scores, 0.0), axis=0, keepdims=True))
        idx_rows.append(ei)
        chosen = jnp.where(hit, 1.0, chosen)
        masked = jnp.where(hit, ninf, masked)
    wk = jnp.concatenate(w_rows, axis=0)
    w_ref[...] = wk / jnp.sum(wk, axis=0, keepdims=True) * ROUTED_SCALE
    idx_ref[...] = jnp.concatenate(idx_rows, axis=0)

    upper = (lax.broadcasted_iota(I32, (t, t), 0) < lax.broadcasted_iota(I32, (t, t), 1))
    prefix = jnp.dot(chosen.astype(BF16), upper.astype(BF16), preferred_element_type=F32)
    rank_full = prefix + cnt_ref[...]
    rank_rows = [jnp.sum(jnp.where(io_e == ei, rank_full, 0.0), axis=0, keepdims=True) for ei in idx_rows]
    rank_ref[...] = jnp.concatenate(rank_rows, axis=0).astype(I32)
    cnt_ref[...] += jnp.sum(chosen, axis=1, keepdims=True)


def _route(logits_t, router_bias):
    ne, n = logits_t.shape
    t = min(T_ROUTE, n)
    kspec = pl.BlockSpec((TOP_K, t), lambda i: (0, i))
    return pl.pallas_call(
        _route_kernel,
        grid=(n // t,),
        in_specs=[pl.BlockSpec((ne, t), lambda i: (0, i)),
                  pl.BlockSpec((ne, 1), lambda i: (0, 0))],
        out_specs=[kspec, kspec, kspec, pl.BlockSpec((ne, 1), lambda i: (0, 0))],
        out_shape=[jax.ShapeDtypeStruct((TOP_K, n), I32),
                   jax.ShapeDtypeStruct((TOP_K, n), F32),
                   jax.ShapeDtypeStruct((TOP_K, n), I32),
                   jax.ShapeDtypeStruct((ne, 1), F32)],
        compiler_params=_cparams(("arbitrary",)),
        name="route",
    )(logits_t, router_bias.reshape(ne, 1))


def _block(ref, g):
    return ref.at[:, pl.ds(pl.multiple_of(g * BLK_E, BLK_E), BLK_E), :]


def _slots_kernel(idx_ref, rank_ref, ps_ref, o_ref, *, n_slots):
    t = idx_ref.shape[1]
    io = lax.broadcasted_iota(I32, (N_EXPERTS, t), 0)
    ps = ps_ref[...]
    for k in range(TOP_K):
        hit = io == idx_ref[k:k + 1, :]
        slot = jnp.sum(jnp.where(hit, ps, 0), axis=0, keepdims=True) + rank_ref[k:k + 1, :]
        for j in range(PACK_W):
            o_ref[j * TOP_K + k:j * TOP_K + k + 1, :] = slot + j * n_slots


def _slots(idx, rank, pstarts, n_slots):
    n = idx.shape[1]
    t = min(T_SLOT, n)
    kspec = pl.BlockSpec((TOP_K, t), lambda i: (0, i))
    return pl.pallas_call(
        functools.partial(_slots_kernel, n_slots=n_slots),
        grid=(n // t,),
        in_specs=[kspec, kspec, pl.BlockSpec((N_EXPERTS, 1), lambda i: (0, 0))],
        out_specs=pl.BlockSpec((PACK_W * TOP_K, t), lambda i: (0, i)),
        out_shape=jax.ShapeDtypeStruct((PACK_W * TOP_K, n), I32),
        compiler_params=_cparams(("parallel",)),
        name="slots",
    )(idx, rank, pstarts.reshape(N_EXPERTS, 1))


def _sc_mesh():
    return plsc.VectorSubcoreMesh(core_axis_name="core", subcore_axis_name="subcore")


def _scatter_rows_sc(rows, dest_rows, n_out):
    n_rows = rows.shape[0]
    wpp = n_rows // PACK_W // SC_WINDOW

    @functools.partial(pl.kernel, mesh=_sc_mesh(), scratch_types=[pltpu.SemaphoreType.DMA],
                       out_type=jax.ShapeDtypeStruct((n_out, LANES), I32))
    def scatter_rows(x_hbm, i_hbm, o_hbm, sem):
        def body(x_vmem, i_vmem):
            copies = [pltpu.async_copy(x_vmem, o_hbm.at[i_vmem.at[k]], sem) for k in range(TOP_K)]
            for c in copies:
                c.wait()

        pltpu.emit_pipeline(
            body,
            grid=(n_rows // SC_WINDOW,),
            in_specs=[pl.BlockSpec((SC_WINDOW, LANES), lambda i: (i, 0)),
                      pl.BlockSpec((TOP_K, SC_WINDOW), lambda i: (i // wpp, i % wpp))],
            out_specs=[],
            core_axis_name=("core", "subcore"),
            dimension_semantics=(pltpu.PARALLEL,),
        )(x_hbm, i_hbm)

    return scatter_rows(rows, dest_rows)


def _gather_rows_sc(rows, slot_rows, tok0, nc):
    nr = slot_rows.shape[0]
    span = SC_GATHER_WINDOWS * SC_WINDOW
    spr = nc // span
    w0 = tok0 // SC_WINDOW

    @functools.partial(pl.kernel, mesh=_sc_mesh(), scratch_types=[pltpu.SemaphoreType.DMA],
                       out_type=jax.ShapeDtypeStruct((nr * nc, LANES), I32))
    def gather_rows(x_hbm, i_hbm, o_hbm, sem):
        def body(*refs):
            o_vmem = refs[-1]
            copies = [pltpu.async_copy(x_hbm.at[i_vmem.at[0]], o_vmem.at[pl.ds(w * SC_WINDOW, SC_WINDOW)], sem)
                      for w, i_vmem in enumerate(refs[:-1])]
            for c in copies:
                c.wait()

        def idx_spec(w):
            return pl.BlockSpec((1, SC_WINDOW),
                                lambda i: (i // spr, w0 + (i % spr) * SC_GATHER_WINDOWS + w))

        pltpu.emit_pipeline(
            body,
            grid=(nr * spr,),
            in_specs=[idx_spec(w) for w in range(SC_GATHER_WINDOWS)],
            out_specs=[pl.BlockSpec((span, LANES), lambda i: (i, 0))],
            core_axis_name=("core", "subcore"),
            dimension_semantics=(pltpu.PARALLEL,),
        )(*([i_hbm] * SC_GATHER_WINDOWS), o_hbm)

    return gather_rows(rows, slot_rows)


def _experts_kernel(nblk_ref, gstart_ref, cnt_ref, nb_ref, xs_ref, wg_ref, wu_ref, wd_ref, ys_ref,
                    xbuf_ref, ybuf_ref, wgub_ref, wdb_ref, xsem, ysem):
    e = pl.program_id(0)
    total = nb_ref[0]
    nb_max = ys_ref.shape[1] // BLK_E

    def x_copy(g):
        slot = g % X_RING
        return pltpu.make_async_copy(_block(xs_ref, g), xbuf_ref.at[slot], xsem.at[slot])

    def y_copy(g):
        slot = g % Y_RING
        return pltpu.make_async_copy(ybuf_ref.at[slot], _block(ys_ref, g), ysem.at[slot])

    @pl.when(e == 0)
    def _():
        for g in range(X_AHEAD):
            @pl.when(g < total)
            def _():
                x_copy(g).start()

    @pl.when(nblk_ref[e] > 0)
    def _():
        f = wg_ref.shape[1]
        wgub_ref[:, :f] = wg_ref[...].astype(BF16)
        wgub_ref[:, f:] = wu_ref[...].astype(BF16)
        wdb_ref[...] = wd_ref[...].astype(BF16)

    def fetch(b):
        g = gstart_ref[e] + b
        x_copy(g).wait()

        @pl.when(g + X_AHEAD < total)
        def _():
            x_copy(g + X_AHEAD).start()

        xb = _unpack_words([xbuf_ref[g % X_RING, j] for j in range(PACK_W)]).astype(BF16)
        live = lax.broadcasted_iota(I32, (BLK_E, 1), 0) < cnt_ref[e] - b * BLK_E
        return jnp.where(live, xb, jnp.zeros((), BF16))

    def hidden(xb):
        f = wg_ref.shape[1]
        hgu = jnp.dot(xb, wgub_ref[...], preferred_element_type=F32)
        return (_silu(hgu[:, :f]) * hgu[:, f:]).astype(BF16)

    def finish(b, act):
        g = gstart_ref[e] + b
        words = _pack_rows(jnp.dot(act, wdb_ref[...], preferred_element_type=F32))

        @pl.when(g >= Y_RING)
        def _():
            y_copy(g - Y_RING).wait()

        for j, w in enumerate(words):
            ybuf_ref[g % Y_RING, j] = w
        y_copy(g).start()

    def block(b, carry):
        finish(b, hidden(fetch(b)))
        return carry

    lax.fori_loop(0, nblk_ref[e], block, 0)

    @pl.when(e == pl.num_programs(0) - 1)
    def _():
        for back in range(Y_RING):
            @pl.when(total - 1 - back >= 0)
            def _():
                y_copy(total - 1 - back).wait()

        ybuf_ref[0] = jnp.zeros(ybuf_ref.shape[1:], ybuf_ref.dtype)

        def tail_copy(g):
            return pltpu.make_async_copy(ybuf_ref.at[0], _block(ys_ref, g), ysem.at[0])

        def fill(g, carry):
            tail_copy(g).start()
            return carry

        lax.fori_loop(total, nb_max, fill, 0)

        def drain(g, carry):
            tail_copy(g).wait()
            return carry

        lax.fori_loop(total, nb_max, drain, 0)


def _experts(nblk, gstart, cnt, n_blocks, xs, w_gate, w_up, w_down):
    p = xs.shape[1]
    ne, d, f = w_gate.shape
    wspec = lambda shape: pl.BlockSpec((None,) + shape, lambda e, *_: (e, 0, 0))
    grid_spec = pltpu.PrefetchScalarGridSpec(
        num_scalar_prefetch=4,
        grid=(ne,),
        in_specs=[pl.BlockSpec(memory_space=pl.ANY), wspec((d, f)), wspec((d, f)), wspec((f, d))],
        out_specs=pl.BlockSpec(memory_space=pl.ANY),
        scratch_shapes=[pltpu.VMEM((X_RING, PACK_W, BLK_E, LANES), I32),
                        pltpu.VMEM((Y_RING, PACK_W, BLK_E, LANES), I32),
                        pltpu.VMEM((d, 2 * f), BF16), pltpu.VMEM((f, d), BF16),
                        pltpu.SemaphoreType.DMA((X_RING,)), pltpu.SemaphoreType.DMA((Y_RING,))],
    )
    return pl.pallas_call(
        _experts_kernel,
        grid_spec=grid_spec,
        out_shape=jax.ShapeDtypeStruct((PACK_W, p, LANES), I32),
        compiler_params=_cparams(("arbitrary",)),
        name="experts",
    )(nblk, gstart, cnt, n_blocks, xs, w_gate, w_up, w_down)


def _combine_kernel(wt_ref, yk_ref, x1_ref, sf_ref, shf_ref, gf_ref, wsg_ref, wsu_ref, wsd_ref, g_ref, b_ref,
                    *rest, alpha):
    o_ref = rest[-1]
    x1 = x1_ref[...]
    hb = (x1 * (1.0 + sf_ref[...]) + shf_ref[...]).astype(BF16)
    sg = jnp.dot(hb, wsg_ref[...], preferred_element_type=F32)
    su = jnp.dot(hb, wsu_ref[...], preferred_element_type=F32)
    ffn = jnp.dot((_silu(sg) * su).astype(BF16), wsd_ref[...], preferred_element_type=F32)
    wt = wt_ref[...].T
    for k in range(TOP_K):
        yk = _unpack_words([yk_ref[j, k] for j in range(PACK_W)])
        ffn = ffn + wt[:, k:k + 1] * yk
    o_ref[...] = _layer_norm(alpha * x1 + gf_ref[...] * ffn, g_ref[...], b_ref[...])


def _combine(wts, yk, x1, mod, ws_gate_bf, ws_up_bf, ws_down_bf, ln_g, ln_b, alpha, tok0, earlier):
    n, d = x1.shape
    nc = yk.shape[2]
    s = n // mod.shape[1]
    tc = min(T_COMB, s, nc)
    nt = s // tc
    t0 = tok0 // tc
    fs = ws_gate_bf.shape[1]
    const = lambda shape: pl.BlockSpec(shape, lambda i: tuple(0 for _ in shape))
    mod_spec = lambda which: pl.BlockSpec((None, None, 1, d), lambda i: (which, (t0 + i) // nt, 0, 0))
    in_specs = [pl.BlockSpec((TOP_K, tc), lambda i: (0, t0 + i)),
                pl.BlockSpec((PACK_W, TOP_K, tc, LANES), lambda i: (0, 0, i, 0)),
                pl.BlockSpec((tc, d), lambda i: (t0 + i, 0)),
                mod_spec(4), mod_spec(3), mod_spec(5),
                const((d, fs)), const((d, fs)), const((fs, d)), const((1, d)), const((1, d))]
    args = [wts, yk, x1, mod, mod, mod, ws_gate_bf, ws_up_bf, ws_down_bf, ln_g.reshape(1, d), ln_b.reshape(1, d)]
    aliases = {}
    if earlier is not None:
        in_specs.append(pl.BlockSpec(memory_space=pl.ANY))
        args.append(earlier)
        aliases = {len(args) - 1: 0}
    return pl.pallas_call(
        functools.partial(_combine_kernel, alpha=alpha),
        grid=(nc // tc,),
        in_specs=in_specs,
        out_specs=pl.BlockSpec((tc, d), lambda i: (t0 + i, 0)),
        out_shape=jax.ShapeDtypeStruct((n, d), F32),
        input_output_aliases=aliases,
        compiler_params=_cparams(("parallel",)),
        name="combine",
    )(*args)


def _slot_layout(counts, n_assign):
    cnt = counts[:, 0].astype(I32)
    padded = (cnt + BLK_E - 1) // BLK_E * BLK_E
    pends = jnp.cumsum(padded)
    pstarts = pends - padded
    n_blocks_max = (n_assign + N_EXPERTS * (BLK_E - 1)) // BLK_E
    n_blocks = (pends[-1] // BLK_E).astype(I32).reshape(1)
    return pstarts, padded // BLK_E, pstarts // BLK_E, n_blocks, n_blocks_max * BLK_E


def kernel(x, c, w_ada, b_ada, w_in, w_out, na_rpb, ret_log_decay, ret_gn_g, ln1_g, ln1_b, ln2_g, ln2_b,
           w_router, router_bias, w_gate, w_up, w_down, ws_gate, ws_up, ws_down):
    b, s, d = x.shape
    depth = w_ada.shape[0]
    alpha = (2.0 * depth) ** 0.25
    t = jnp.arange(s, dtype=F32)
    inv_freq = ROPE_BASE ** (-jnp.arange(0, RET_HEAD_DIM, 2, dtype=F32) / RET_HEAD_DIM)
    ang = t[:, None] * inv_freq[None, :]
    cos, sin = jnp.cos(ang), jnp.sin(ang)
    cos2 = jnp.concatenate([cos, cos], axis=-1)
    sin2 = jnp.concatenate([-sin, sin], axis=-1)
    for l in range(depth):
        mod = _mod(c, w_ada[l], b_ada[l])
        proj = _inproj(x, mod, w_in[l].astype(BF16), cos2, sin2)
        y_na = _natten(proj, _na_bias_table(na_rpb[l]))
        y_r = _retention(proj, ret_log_decay[l], ret_gn_g[l])
        wr_t = w_router[l].T
        wr_hi = wr_t.astype(BF16)
        wr_lo = (wr_t - wr_hi.astype(F32)).astype(BF16)
        x1, hfp, logits_t = _outproj(y_na, y_r, x, mod, w_out[l].astype(BF16), ln1_g[l], ln1_b[l],
                                     wr_hi, wr_lo, alpha)
        idx, wts, rank, counts = _route(logits_t, router_bias[l])
        n = b * s
        pstarts, nblk, gstart, n_blocks, n_slots = _slot_layout(counts, n * TOP_K)
        slot_rows = _slots(idx, rank, pstarts, n_slots)
        xs = _scatter_rows_sc(hfp.reshape(PACK_W * n, LANES), slot_rows, PACK_W * n_slots)
        ys = _experts(nblk, gstart, counts[:, 0].astype(I32), n_blocks, xs.reshape(PACK_W, n_slots, LANES),
                      w_gate[l], w_up[l], w_down[l]).reshape(PACK_W * n_slots, LANES)
        ws = (ws_gate[l].astype(BF16), ws_up[l].astype(BF16), ws_down[l].astype(BF16))
        x1n = x1.reshape(n, d)
        nc = n // COMBINE_CHUNKS
        out = None
        for ci in range(COMBINE_CHUNKS):
            yk = _gather_rows_sc(ys, slot_rows, ci * nc, nc).reshape(PACK_W, TOP_K, nc, LANES)
            out = _combine(wts, yk, x1n, mod, *ws, ln2_g[l], ln2_b[l], alpha, ci * nc, out)
        x = out.reshape(b, s, d)
    return x
```

```python
import functools
import math

import jax
import jax.numpy as jnp
from jax import lax
from jax.experimental import pallas as pl
from jax.experimental.pallas import tpu as pltpu
from jax.experimental.pallas import tpu_sc as plsc

F32 = jnp.float32
BF16 = jnp.bfloat16
U32 = jnp.uint32
I32 = jnp.int32

GRID_W = 64
WIN_R = 8
WIN_C = 16
NA_HEADS = 8
NA_HEAD_DIM = 64
NA_WIDTH = NA_HEADS * NA_HEAD_DIM
RET_HEADS = 4
RET_HEAD_DIM = 128
RET_WIDTH = RET_HEADS * RET_HEAD_DIM
ROPE_BASE = 10000.0
N_EXPERTS = 256
TOP_K = 8
N_GROUPS = 8
TOPK_GROUPS = 4
GROUP_SIZE = N_EXPERTS // N_GROUPS
ROUTED_SCALE = 2.5
LOG2_E = 1.4426950408889634
LN_EPS = 1e-5
GN_EPS = 1e-6

LANES = 128
VMEM_BYTES = 64 * 1024 * 1024
VMEM_LIMIT = VMEM_BYTES * 7 // 8

TM_INPROJ = 1024
TM_PROJ = 1024
T_ROUTE = 512
T_SLOT = 1024
SC_WINDOW = 128
SC_GATHER_WINDOWS = 2
BLK_E = 512
OUTPROJ_PARTS = 2
T_COMB = 512
COMBINE_CHUNKS = 8
PACK_W = 4
NA_ROWS_PER_ITER = 32
RET_BLOCK = 256
X_RING = 4
X_AHEAD = 3
Y_RING = 3


def _cparams(sem, vmem=VMEM_LIMIT):
    return pltpu.CompilerParams(dimension_semantics=sem, vmem_limit_bytes=vmem)


def _silu(v):
    return v * jax.nn.sigmoid(v)


def _layer_norm(z, g, b):
    mu = jnp.mean(z, -1, keepdims=True)
    zc = z - mu
    var = jnp.mean(zc * zc, -1, keepdims=True)
    return zc * lax.rsqrt(var + LN_EPS) * g + b


def _pack_rows(v):
    half = v.shape[1] // 2
    vb = v.astype(BF16)
    lo = lax.bitcast_convert_type(vb[:, :half].astype(F32), U32) >> 16
    hi = lax.bitcast_convert_type(vb[:, half:].astype(F32), U32)
    w = lax.bitcast_convert_type(hi | lo, I32)
    return [w[:, j * LANES:(j + 1) * LANES] for j in range(half // LANES)]


def _unpack_words(words):
    words = [lax.bitcast_convert_type(w, U32) for w in words]
    lo = [lax.bitcast_convert_type(w << 16, F32) for w in words]
    hi = [lax.bitcast_convert_type(w & jnp.uint32(0xFFFF0000), F32) for w in words]
    return jnp.concatenate(lo + hi, axis=-1)


def _mod_kernel(c_ref, w_ref, b_ref, o_ref):
    cond = _silu(c_ref[...])
    o_ref[0] = jnp.dot(cond, w_ref[...], precision=lax.Precision.HIGHEST,
                       preferred_element_type=F32) + b_ref[0]


def _mod(c, w_ada, b_ada):
    b, d = c.shape
    n6 = w_ada.shape[1] // d
    out = pl.pallas_call(
        _mod_kernel,
        grid=(n6,),
        in_specs=[pl.BlockSpec((b, d), lambda j: (0, 0)),
                  pl.BlockSpec((d, d), lambda j: (0, j)),
                  pl.BlockSpec((1, 1, d), lambda j: (j, 0, 0))],
        out_specs=pl.BlockSpec((1, b, d), lambda j: (j, 0, 0)),
        out_shape=jax.ShapeDtypeStruct((n6, b, d), F32),
        compiler_params=_cparams(("arbitrary",)),
        name="mod",
    )(c, w_ada, b_ada.reshape(n6, 1, d))
    return out.reshape(n6, b, 1, d)


def _mod_spec(which, d):
    return pl.BlockSpec((None, None, 1, d), lambda b, i, which=which: (which, b, 0, 0))


def _inproj_kernel(x_ref, sc_ref, sh_ref, w_ref, cos_ref, sin_ref, o_ref, *, chunk, q_scale):
    h = (x_ref[...] * (1.0 + sc_ref[...]) + sh_ref[...]).astype(BF16)
    q_r, k_r = 3 * NA_WIDTH // chunk, 3 * NA_WIDTH // chunk + 1
    dh = RET_HEAD_DIM
    for j in range(o_ref.shape[1] // chunk):
        acc = jnp.dot(h, w_ref[:, j * chunk:(j + 1) * chunk], preferred_element_type=F32)
        if j == 0:
            acc = acc * q_scale
        if j in (q_r, k_r):
            cos2, sin2 = cos_ref[...], sin_ref[...]
            heads = [acc[:, hh * dh:(hh + 1) * dh] for hh in range(chunk // dh)]
            heads = [t * cos2 + pltpu.roll(t, dh // 2, 1) * sin2 for t in heads]
            acc = jnp.concatenate(heads, axis=1)
            if j == k_r:
                acc = acc * dh ** -0.5
        o_ref[:, j * chunk:(j + 1) * chunk] = acc.astype(o_ref.dtype)


def _inproj(x, mod, w_in_bf, cos2, sin2):
    b, s, d = x.shape
    e = w_in_bf.shape[1]
    tm = min(TM_INPROJ, s)
    rope = pl.BlockSpec((tm, RET_HEAD_DIM), lambda bi, i: (i, 0))
    return pl.pallas_call(
        functools.partial(_inproj_kernel, chunk=NA_WIDTH, q_scale=NA_HEAD_DIM ** -0.5 * LOG2_E),
        grid=(b, s // tm),
        in_specs=[pl.BlockSpec((None, tm, d), lambda bi, i: (bi, i, 0)),
                  _mod_spec(1, d), _mod_spec(0, d),
                  pl.BlockSpec((d, e), lambda bi, i: (0, 0)), rope, rope],
        out_specs=pl.BlockSpec((None, tm, e), lambda bi, i: (bi, i, 0)),
        out_shape=jax.ShapeDtypeStruct((b, s, e), BF16),
        compiler_params=_cparams(("parallel", "parallel")),
        name="inproj",
    )(x, mod, mod, w_in_bf, cos2, sin2)


def _natten_kernel(q_ref, k_ref, v_ref, bias_ref, o_ref, *, rows, rows_per_iter):
    kspan = WIN_R * GRID_W

    first = lax.broadcasted_iota(I32, (1, LANES), 1) < NA_HEAD_DIM
    zero = jnp.zeros((), BF16)

    def rows_body(i, carry):
        qrows, krows, scores, probs = {}, {}, {}, {}

        def stage_scores(u):
            r = i * rows_per_iter + u
            rs = jnp.clip(r - WIN_R // 2, 0, rows - WIN_R)
            vi = r - rs
            qrows[u] = pl.ds(pl.multiple_of(r * GRID_W, GRID_W), GRID_W)
            krows[u] = pl.ds(pl.multiple_of(rs * GRID_W, GRID_W), kspan)
            q = q_ref[qrows[u], :]
            qm = jnp.concatenate([jnp.where(first, q, zero), jnp.where(first, zero, q)], axis=0)
            st = lax.dot_general(k_ref[krows[u], :], qm, (((1,), (1,)), ((), ())), preferred_element_type=F32)
            scores[u] = st + bias_ref[vi]

        def stage_softmax(u):
            st = scores.pop(u)
            p = jnp.exp2(st - jnp.max(st, axis=0, keepdims=True))
            probs[u] = (p * (1.0 / jnp.sum(p, axis=0, keepdims=True))).astype(BF16)

        def stage_values(u):
            res = lax.dot_general(probs.pop(u), v_ref[krows[u], :], (((0,), (0,)), ((), ())),
                                  preferred_element_type=F32)
            o_ref[qrows[u], :] = jnp.where(first, res[:GRID_W], res[GRID_W:]).astype(o_ref.dtype)

        for step in range(rows_per_iter + 2):
            if step < rows_per_iter:
                stage_scores(step)
            if 0 <= step - 1 < rows_per_iter:
                stage_softmax(step - 1)
            if 0 <= step - 2 < rows_per_iter:
                stage_values(step - 2)
        return carry

    lax.fori_loop(0, rows // rows_per_iter, rows_body, 0)


def _na_bias_table(rpb):
    w = GRID_W
    cq = jnp.arange(w)
    cs = jnp.clip(cq - WIN_C // 2, 0, w - WIN_C)
    ck = jnp.arange(w)
    col_in = (ck[None, :] >= cs[:, None]) & (ck[None, :] < cs[:, None] + WIN_C)
    dc_idx = jnp.clip(ck[None, :] - cq[:, None] + WIN_C - 1, 0, 2 * WIN_C - 2)
    t = rpb[:, :, dc_idx]
    t = jnp.where(col_in[None, None], t, -jnp.inf)
    vi = jnp.arange(WIN_R)
    kr = jnp.arange(WIN_R)
    dr = kr[None, :] - vi[:, None] + WIN_R - 1
    tb = t[:, dr]
    hp = LANES // NA_HEAD_DIM
    tb = tb.reshape(rpb.shape[0] // hp, hp, WIN_R, WIN_R, w, w)
    tb = tb.transpose(0, 2, 3, 5, 1, 4).reshape(rpb.shape[0] // hp, WIN_R, WIN_R * w, hp * w)
    return tb.astype(F32) * LOG2_E


def _natten(proj, bias_tab):
    b, s, _ = proj.shape
    rows = s // GRID_W
    hp = LANES // NA_HEAD_DIM
    npair = NA_HEADS // hp
    blk = lambda off: pl.BlockSpec((None, s, LANES), lambda p, bi, off=off: (bi, 0, off + p))
    return pl.pallas_call(
        functools.partial(_natten_kernel, rows=rows, rows_per_iter=math.gcd(rows, NA_ROWS_PER_ITER)),
        grid=(npair, b),
        in_specs=[blk(0), blk(npair), blk(2 * npair),
                  pl.BlockSpec((None, WIN_R, WIN_R * GRID_W, hp * GRID_W), lambda p, bi: (p, 0, 0, 0))],
        out_specs=pl.BlockSpec((None, s, LANES), lambda p, bi: (bi, 0, p)),
        out_shape=jax.ShapeDtypeStruct((b, s, NA_WIDTH), BF16),
        compiler_params=_cparams(("parallel", "parallel")),
        name="natten",
    )(proj, proj, proj, bias_tab)


def _retent_kernel(ld_ref, q_ref, k_ref, v_ref, g_ref, gn_ref, o_ref, st_ref, *, nchunk):
    c = RET_BLOCK
    dh = RET_HEAD_DIM
    h = pl.program_id(1)
    lgf = ld_ref[0, h]
    lgb = ld_ref[1, h]

    ic = lax.broadcasted_iota(I32, (c, 1), 0).astype(F32)
    ir = lax.broadcasted_iota(I32, (1, c), 1).astype(F32)
    diff = ic - ir
    dmat = jnp.where(diff >= 0, jnp.exp(jnp.maximum(diff, 0.0) * lgf),
                     jnp.exp(jnp.maximum(-diff, 0.0) * lgb))
    kdec_f = jnp.exp((c - 1 - ic) * lgf)
    qdec_f = jnp.exp((ic + 1) * lgf)
    kdec_b = jnp.exp(ic * lgb)
    qdec_b = jnp.exp((c - ic) * lgb)
    one = jnp.ones((1, 1), F32)
    cdec_f = jnp.exp(one * (c * lgf))
    cdec_b = jnp.exp(one * (c * lgb))
    tn = (((0,), (0,)), ((), ()))

    chunks = [pl.ds(n * c, c) for n in range(nchunk)]
    kv = []
    for rows in chunks:
        kn = k_ref[rows, :].astype(F32)
        kd = jnp.concatenate([kn * kdec_f, kn * kdec_b], axis=1).astype(BF16)
        kv.append(lax.dot_general(kd, v_ref[rows, :], tn, preferred_element_type=F32))

    sf = [jnp.zeros((dh, dh), F32)]
    for n in range(1, nchunk):
        sf.append(cdec_f * sf[n - 1] + kv[n - 1][:dh])
    sb = [jnp.zeros((dh, dh), F32)]
    for n in range(nchunk - 2, -1, -1):
        sb.insert(0, cdec_b * sb[0] + kv[n + 1][dh:])
    for n in range(nchunk):
        st_ref[n] = jnp.concatenate([sf[n], sb[n]], axis=0).astype(BF16)

    gn = gn_ref[...]
    for n, rows in enumerate(chunks):
        qb = q_ref[rows, :]
        qn = qb.astype(F32)
        sc = lax.dot_general(qb, k_ref[rows, :], (((1,), (1,)), ((), ())), preferred_element_type=F32) * dmat
        y = jnp.dot(sc.astype(BF16), v_ref[rows, :], preferred_element_type=F32)
        qd = jnp.concatenate([qn * qdec_f, qn * qdec_b], axis=1).astype(BF16)
        y = y + jnp.dot(qd, st_ref[n], preferred_element_type=F32)
        mu = jnp.mean(y, -1, keepdims=True)
        yc = y - mu
        var = jnp.mean(yc * yc, -1, keepdims=True)
        yn = yc * lax.rsqrt(var + GN_EPS) * gn
        o_ref[rows, :] = (_silu(g_ref[rows, :].astype(F32)) * yn).astype(o_ref.dtype)


def _retention(proj, log_decay, gn_g):
    b, s, _ = proj.shape
    dh = RET_HEAD_DIM
    nchunk = s // RET_BLOCK
    base = 3 * NA_WIDTH // dh
    blk = lambda off: pl.BlockSpec((None, s, dh), lambda bi, h, off=off: (bi, 0, base + off + h))
    return pl.pallas_call(
        functools.partial(_retent_kernel, nchunk=nchunk),
        grid=(b, RET_HEADS),
        in_specs=[pl.BlockSpec(memory_space=pltpu.SMEM),
                  blk(0), blk(RET_HEADS), blk(2 * RET_HEADS), blk(3 * RET_HEADS),
                  pl.BlockSpec((1, dh), lambda bi, h: (0, h))],
        out_specs=pl.BlockSpec((None, s, dh), lambda bi, h: (bi, 0, h)),
        out_shape=jax.ShapeDtypeStruct((b, s, RET_WIDTH), BF16),
        scratch_shapes=[pltpu.VMEM((nchunk, 2 * dh, dh), BF16)],
        compiler_params=_cparams(("parallel", "parallel")),
        name="retent",
    )(log_decay, proj, proj, proj, proj, gn_g.reshape(1, RET_WIDTH))


def _outproj_kernel(yna_ref, yr_ref, x_ref, ga_ref, sf_ref, shf_ref, wo1_ref, wo2_ref, g_ref, b_ref,
                    wrh_ref, wrl_ref, x1_ref, hfp_ref, lg_ref, *, alpha):
    nt = (((1,), (1,)), ((), ()))
    tm = x_ref.shape[0]
    parts = [pl.ds(p * (tm // OUTPROJ_PARTS), tm // OUTPROJ_PARTS) for p in range(OUTPROJ_PARTS)]
    def mix_of(r):
        return (jnp.dot(yna_ref[r, :], wo1_ref[...], preferred_element_type=F32)
                + jnp.dot(yr_ref[r, :], wo2_ref[...], preferred_element_type=F32))

    nxt = mix_of(parts[0])
    for p, r in enumerate(parts):
        mix = nxt
        if p + 1 < len(parts):
            nxt = mix_of(parts[p + 1])
        x1 = _layer_norm(alpha * x_ref[r, :] + ga_ref[...] * mix, g_ref[...], b_ref[...])
        x1_ref[r, :] = x1
        hf = x1 * (1.0 + sf_ref[...]) + shf_ref[...]
        for j, w in enumerate(_pack_rows(hf)):
            hfp_ref[j, r, :] = w
        hb = hf.astype(BF16)
        hl = (hf - hb.astype(F32)).astype(BF16)
        lg = lax.dot_general(wrh_ref[...], hb, nt, preferred_element_type=F32)
        lg = lg + lax.dot_general(wrh_ref[...], hl, nt, preferred_element_type=F32)
        lg = lg + lax.dot_general(wrl_ref[...], hb, nt, preferred_element_type=F32)
        lg_ref[:, r] = lg


def _outproj(y_na, y_r, x, mod, w_out_bf, ln_g, ln_b, wr_hi, wr_lo, alpha):
    b, s, d = x.shape
    tm = min(TM_PROJ, s)
    nt = s // tm
    ne = wr_hi.shape[0]
    const = lambda shape: pl.BlockSpec(shape, lambda bi, i: tuple(0 for _ in shape))
    x1, hfp, lg = pl.pallas_call(
        functools.partial(_outproj_kernel, alpha=alpha),
        grid=(b, nt),
        in_specs=[pl.BlockSpec((None, tm, NA_WIDTH), lambda bi, i: (bi, i, 0)),
                  pl.BlockSpec((None, tm, RET_WIDTH), lambda bi, i: (bi, i, 0)),
                  pl.BlockSpec((None, tm, d), lambda bi, i: (bi, i, 0)),
                  _mod_spec(2, d), _mod_spec(4, d), _mod_spec(3, d),
                  pl.BlockSpec((NA_WIDTH, d), lambda bi, i: (0, 0)),
                  pl.BlockSpec((RET_WIDTH, d), lambda bi, i: (1, 0)),
                  const((1, d)), const((1, d)), const((ne, d)), const((ne, d))],
        out_specs=[pl.BlockSpec((None, tm, d), lambda bi, i: (bi, i, 0)),
                   pl.BlockSpec((PACK_W, tm, LANES), lambda bi, i: (0, bi * nt + i, 0)),
                   pl.BlockSpec((ne, tm), lambda bi, i: (0, bi * nt + i))],
        out_shape=[jax.ShapeDtypeStruct((b, s, d), F32),
                   jax.ShapeDtypeStruct((PACK_W, b * s, LANES), I32),
                   jax.ShapeDtypeStruct((ne, b * s), F32)],
        compiler_params=_cparams(("parallel", "parallel")),
        name="outproj",
    )(y_na, y_r, x, mod, mod, mod, w_out_bf, w_out_bf, ln_g.reshape(1, d), ln_b.reshape(1, d), wr_hi, wr_lo)
    return x1, hfp, lg


def _route_kernel(lg_ref, rb_ref, idx_ref, w_ref, rank_ref, cnt_ref):
    t = lg_ref.shape[1]
    ninf = -jnp.inf

    @pl.when(pl.program_id(0) == 0)
    def _():
        cnt_ref[...] = jnp.zeros_like(cnt_ref)

    scores = jax.nn.sigmoid(lg_ref[...])
    sel = scores + rb_ref[...]

    io_g = lax.broadcasted_iota(I32, (GROUP_SIZE, t), 0)
    gs_rows = []
    for g in range(N_GROUPS):
        blk = sel[g * GROUP_SIZE:(g + 1) * GROUP_SIZE, :]
        m1 = jnp.max(blk, axis=0, keepdims=True)
        i1 = jnp.min(jnp.where(blk == m1, io_g, GROUP_SIZE), axis=0, keepdims=True)
        m2 = jnp.max(jnp.where(io_g == i1, ninf, blk), axis=0, keepdims=True)
        gs_rows.append(m1 + m2)
    gs = jnp.concatenate(gs_rows, axis=0)

    io8 = lax.broadcasted_iota(I32, (N_GROUPS, t), 0)
    gsel = jnp.zeros((N_GROUPS, t), F32)
    for _ in range(TOPK_GROUPS):
        m = jnp.max(gs, axis=0, keepdims=True)
        gi = jnp.min(jnp.where(gs == m, io8, N_GROUPS), axis=0, keepdims=True)
        hit = io8 == gi
        gsel = jnp.where(hit, 1.0, gsel)
        gs = jnp.where(hit, ninf, gs)

    masked = jnp.concatenate(
        [jnp.where(gsel[g:g + 1, :] > 0.0, sel[g * GROUP_SIZE:(g + 1) * GROUP_SIZE, :], ninf)
         for g in range(N_GROUPS)], axis=0)

    io_e = lax.broadcasted_iota(I32, (N_EXPERTS, t), 0)
    chosen = jnp.zeros((N_EXPERTS, t), F32)
    idx_rows, w_rows = [], []
    for _ in range(TOP_K):
        m = jnp.max(masked, axis=0, keepdims=True)
        ei = jnp.min(jnp.where(masked == m, io_e, N_EXPERTS), axis=0, keepdims=True)
        hit = io_e == ei
        w_rows.append(jnp.sum(jnp.where(hit, scores, 0.0), axis=0, keepdims=True))
        idx_rows.append(ei)
        chosen = jnp.where(hit, 1.0, chosen)
        masked = jnp.where(hit, ninf, masked)
    wk = jnp.concatenate(w_rows, axis=0)
    w_ref[...] = wk / jnp.sum(wk, axis=0, keepdims=True) * ROUTED_SCALE
    idx_ref[...] = jnp.concatenate(idx_rows, axis=0)

    upper = (lax.broadcasted_iota(I32, (t, t), 0) < lax.broadcasted_iota(I32, (t, t), 1))
    prefix = jnp.dot(chosen.astype(BF16), upper.astype(BF16), preferred_element_type=F32)
    rank_full = prefix + cnt_ref[...]
    rank_rows = [jnp.sum(jnp.where(io_e == ei, rank_full, 0.0), axis=0, keepdims=True) for ei in idx_rows]
    rank_ref[...] = jnp.concatenate(rank_rows, axis=0).astype(I32)
    cnt_ref[...] += jnp.sum(chosen, axis=1, keepdims=True)


def _route(logits_t, router_bias):
    ne, n = logits_t.shape
    t = min(T_ROUTE, n)
    kspec = pl.BlockSpec((TOP_K, t), lambda i: (0, i))
    return pl.pallas_call(
        _route_kernel,
        grid=(n // t,),
        in_specs=[pl.BlockSpec((ne, t), lambda i: (0, i)),
                  pl.BlockSpec((ne, 1), lambda i: (0, 0))],
        out_specs=[kspec, kspec, kspec, pl.BlockSpec((ne, 1), lambda i: (0, 0))],
        out_shape=[jax.ShapeDtypeStruct((TOP_K, n), I32),
                   jax.ShapeDtypeStruct((TOP_K, n), F32),
                   jax.ShapeDtypeStruct((TOP_K, n), I32),
                   jax.ShapeDtypeStruct((ne, 1), F32)],
        compiler_params=_cparams(("arbitrary",)),
        name="route",
    )(logits_t, router_bias.reshape(ne, 1))


def _block(ref, g):
    return ref.at[:, pl.ds(pl.multiple_of(g * BLK_E, BLK_E), BLK_E), :]


def _slots_kernel(idx_ref, rank_ref, ps_ref, o_ref, *, n_slots):
    t = idx_ref.shape[1]
    io = lax.broadcasted_iota(I32, (N_EXPERTS, t), 0)
    ps = ps_ref[...]
    for k in range(TOP_K):
        hit = io == idx_ref[k:k + 1, :]
        slot = jnp.sum(jnp.where(hit, ps, 0), axis=0, keepdims=True) + rank_ref[k:k + 1, :]
        for j in range(PACK_W):
            o_ref[j * TOP_K + k:j * TOP_K + k + 1, :] = slot + j * n_slots


def _slots(idx, rank, pstarts, n_slots):
    n = idx.shape[1]
    t = min(T_SLOT, n)
    kspec = pl.BlockSpec((TOP_K, t), lambda i: (0, i))
    return pl.pallas_call(
        functools.partial(_slots_kernel, n_slots=n_slots),
        grid=(n // t,),
        in_specs=[kspec, kspec, pl.BlockSpec((N_EXPERTS, 1), lambda i: (0, 0))],
        out_specs=pl.BlockSpec((PACK_W * TOP_K, t), lambda i: (0, i)),
        out_shape=jax.ShapeDtypeStruct((PACK_W * TOP_K, n), I32),
        compiler_params=_cparams(("parallel",)),
        name="slots",
    )(idx, rank, pstarts.reshape(N_EXPERTS, 1))


def _sc_mesh():
    return plsc.VectorSubcoreMesh(core_axis_name="core", subcore_axis_name="subcore")


def _scatter_rows_sc(rows, dest_rows, n_out):
    n_rows = rows.shape[0]
    wpp = n_rows // PACK_W // SC_WINDOW

    @functools.partial(pl.kernel, mesh=_sc_mesh(), scratch_types=[pltpu.SemaphoreType.DMA],
                       out_type=jax.ShapeDtypeStruct((n_out, LANES), I32))
    def scatter_rows(x_hbm, i_hbm, o_hbm, sem):
        def body(x_vmem, i_vmem):
            copies = [pltpu.async_copy(x_vmem, o_hbm.at[i_vmem.at[k]], sem) for k in range(TOP_K)]
            for c in copies:
                c.wait()

        pltpu.emit_pipeline(
            body,
            grid=(n_rows // SC_WINDOW,),
            in_specs=[pl.BlockSpec((SC_WINDOW, LANES), lambda i: (i, 0)),
                      pl.BlockSpec((TOP_K, SC_WINDOW), lambda i: (i // wpp, i % wpp))],
            out_specs=[],
            core_axis_name=("core", "subcore"),
            dimension_semantics=(pltpu.PARALLEL,),
        )(x_hbm, i_hbm)

    return scatter_rows(rows, dest_rows)


def _gather_rows_sc(rows, slot_rows, tok0, nc):
    nr = slot_rows.shape[0]
    span = SC_GATHER_WINDOWS * SC_WINDOW
    spr = nc // span
    w0 = tok0 // SC_WINDOW

    @functools.partial(pl.kernel, mesh=_sc_mesh(), scratch_types=[pltpu.SemaphoreType.DMA],
                       out_type=jax.ShapeDtypeStruct((nr * nc, LANES), I32))
    def gather_rows(x_hbm, i_hbm, o_hbm, sem):
        def body(*refs):
            o_vmem = refs[-1]
            copies = [pltpu.async_copy(x_hbm.at[i_vmem.at[0]], o_vmem.at[pl.ds(w * SC_WINDOW, SC_WINDOW)], sem)
                      for w, i_vmem in enumerate(refs[:-1])]
            for c in copies:
                c.wait()

        def idx_spec(w):
            return pl.BlockSpec((1, SC_WINDOW),
                                lambda i: (i // spr, w0 + (i % spr) * SC_GATHER_WINDOWS + w))

        pltpu.emit_pipeline(
            body,
            grid=(nr * spr,),
            in_specs=[idx_spec(w) for w in range(SC_GATHER_WINDOWS)],
            out_specs=[pl.BlockSpec((span, LANES), lambda i: (i, 0))],
            core_axis_name=("core", "subcore"),
            dimension_semantics=(pltpu.PARALLEL,),
        )(*([i_hbm] * SC_GATHER_WINDOWS), o_hbm)

    return gather_rows(rows, slot_rows)


def _experts_kernel(nblk_ref, gstart_ref, cnt_ref, nb_ref, xs_ref, wg_ref, wu_ref, wd_ref, ys_ref,
                    xbuf_ref, ybuf_ref, wgub_ref, wdb_ref, xsem, ysem):
    e = pl.program_id(0)
    total = nb_ref[0]
    nb_max = ys_ref.shape[1] // BLK_E

    def x_copy(g):
        slot = g % X_RING
        return pltpu.make_async_copy(_block(xs_ref, g), xbuf_ref.at[slot], xsem.at[slot])

    def y_copy(g):
        slot = g % Y_RING
        return pltpu.make_async_copy(ybuf_ref.at[slot], _block(ys_ref, g), ysem.at[slot])

    @pl.when(e == 0)
    def _():
        for g in range(X_AHEAD):
            @pl.when(g < total)
            def _():
                x_copy(g).start()

    @pl.when(nblk_ref[e] > 0)
    def _():
        f = wg_ref.shape[1]
        wgub_ref[:, :f] = wg_ref[...].astype(BF16)
        wgub_ref[:, f:] = wu_ref[...].astype(BF16)
        wdb_ref[...] = wd_ref[...].astype(BF16)

    def fetch(b):
        g = gstart_ref[e] + b
        x_copy(g).wait()

        @pl.when(g + X_AHEAD < total)
        def _():
            x_copy(g + X_AHEAD).start()

        xb = _unpack_words([xbuf_ref[g % X_RING, j] for j in range(PACK_W)]).astype(BF16)
        live = lax.broadcasted_iota(I32, (BLK_E, 1), 0) < cnt_ref[e] - b * BLK_E
        return jnp.where(live, xb, jnp.zeros((), BF16))

    def hidden(xb):
        f = wg_ref.shape[1]
        hgu = jnp.dot(xb, wgub_ref[...], preferred_element_type=F32)
        return (_silu(hgu[:, :f]) * hgu[:, f:]).astype(BF16)

    def finish(b, act):
        g = gstart_ref[e] + b
        words = _pack_rows(jnp.dot(act, wdb_ref[...], preferred_element_type=F32))

        @pl.when(g >= Y_RING)
        def _():
            y_copy(g - Y_RING).wait()

        for j, w in enumerate(words):
            ybuf_ref[g % Y_RING, j] = w
        y_copy(g).start(priority=1)

    def block(b, carry):
        finish(b, hidden(fetch(b)))
        return carry

    lax.fori_loop(0, nblk_ref[e], block, 0)

    @pl.when(e == pl.num_programs(0) - 1)
    def _():
        for back in range(Y_RING):
            @pl.when(total - 1 - back >= 0)
            def _():
                y_copy(total - 1 - back).wait()

        ybuf_ref[0] = jnp.zeros(ybuf_ref.shape[1:], ybuf_ref.dtype)

        def tail_copy(g):
            return pltpu.make_async_copy(ybuf_ref.at[0], _block(ys_ref, g), ysem.at[0])

        def fill(g, carry):
            tail_copy(g).start()
            return carry

        lax.fori_loop(total, nb_max, fill, 0)

        def drain(g, carry):
            tail_copy(g).wait()
            return carry

        lax.fori_loop(total, nb_max, drain, 0)


def _experts(nblk, gstart, cnt, n_blocks, xs, w_gate, w_up, w_down):
    p = xs.shape[1]
    ne, d, f = w_gate.shape
    wspec = lambda shape: pl.BlockSpec((None,) + shape, lambda e, *_: (e, 0, 0))
    grid_spec = pltpu.PrefetchScalarGridSpec(
        num_scalar_prefetch=4,
        grid=(ne,),
        in_specs=[pl.BlockSpec(memory_space=pl.ANY), wspec((d, f)), wspec((d, f)), wspec((f, d))],
        out_specs=pl.BlockSpec(memory_space=pl.ANY),
        scratch_shapes=[pltpu.VMEM((X_RING, PACK_W, BLK_E, LANES), I32),
                        pltpu.VMEM((Y_RING, PACK_W, BLK_E, LANES), I32),
                        pltpu.VMEM((d, 2 * f), BF16), pltpu.VMEM((f, d), BF16),
                        pltpu.SemaphoreType.DMA((X_RING,)), pltpu.SemaphoreType.DMA((Y_RING,))],
    )
    return pl.pallas_call(
        _experts_kernel,
        grid_spec=grid_spec,
        out_shape=jax.ShapeDtypeStruct((PACK_W, p, LANES), I32),
        compiler_params=_cparams(("arbitrary",)),
        name="experts",
    )(nblk, gstart, cnt, n_blocks, xs, w_gate, w_up, w_down)


def _combine_kernel(wt_ref, yk_ref, x1_ref, sf_ref, shf_ref, gf_ref, wsg_ref, wsu_ref, wsd_ref, g_ref, b_ref,
                    *rest, alpha):
    o_ref = rest[-1]
    x1 = x1_ref[...]
    hb = (x1 * (1.0 + sf_ref[...]) + shf_ref[...]).astype(BF16)
    sg = jnp.dot(hb, wsg_ref[...], preferred_element_type=F32)
    su = jnp.dot(hb, wsu_ref[...], preferred_element_type=F32)
    ffn = jnp.dot((_silu(sg) * su).astype(BF16), wsd_ref[...], preferred_element_type=F32)
    wt = wt_ref[...].T
    for k in range(TOP_K):
        yk = _unpack_words([yk_ref[j, k] for j in range(PACK_W)])
        ffn = ffn + wt[:, k:k + 1] * yk
    o_ref[...] = _layer_norm(alpha * x1 + gf_ref[...] * ffn, g_ref[...], b_ref[...])


def _combine(wts, yk, x1, mod, ws_gate_bf, ws_up_bf, ws_down_bf, ln_g, ln_b, alpha, tok0, earlier):
    n, d = x1.shape
    nc = yk.shape[2]
    s = n // mod.shape[1]
    tc = min(T_COMB, s, nc)
    nt = s // tc
    t0 = tok0 // tc
    fs = ws_gate_bf.shape[1]
    const = lambda shape: pl.BlockSpec(shape, lambda i: tuple(0 for _ in shape))
    mod_spec = lambda which: pl.BlockSpec((None, None, 1, d), lambda i: (which, (t0 + i) // nt, 0, 0))
    in_specs = [pl.BlockSpec((TOP_K, tc), lambda i: (0, t0 + i)),
                pl.BlockSpec((PACK_W, TOP_K, tc, LANES), lambda i: (0, 0, i, 0)),
                pl.BlockSpec((tc, d), lambda i: (t0 + i, 0)),
                mod_spec(4), mod_spec(3), mod_spec(5),
                const((d, fs)), const((d, fs)), const((fs, d)), const((1, d)), const((1, d))]
    args = [wts, yk, x1, mod, mod, mod, ws_gate_bf, ws_up_bf, ws_down_bf, ln_g.reshape(1, d), ln_b.reshape(1, d)]
    aliases = {}
    if earlier is not None:
        in_specs.append(pl.BlockSpec(memory_space=pl.ANY))
        args.append(earlier)
        aliases = {len(args) - 1: 0}
    return pl.pallas_call(
        functools.partial(_combine_kernel, alpha=alpha),
        grid=(nc // tc,),
        in_specs=in_specs,
        out_specs=pl.BlockSpec((tc, d), lambda i: (t0 + i, 0)),
        out_shape=jax.ShapeDtypeStruct((n, d), F32),
        input_output_aliases=aliases,
        compiler_params=_cparams(("parallel",)),
        name="combine",
    )(*args)


def _slot_layout(counts, n_assign):
    cnt = counts[:, 0].astype(I32)
    padded = (cnt + BLK_E - 1) // BLK_E * BLK_E
    pends = jnp.cumsum(padded)
    pstarts = pends - padded
    n_blocks_max = (n_assign + N_EXPERTS * (BLK_E - 1)) // BLK_E
    n_blocks = (pends[-1] // BLK_E).astype(I32).reshape(1)
    return pstarts, padded // BLK_E, pstarts // BLK_E, n_blocks, n_blocks_max * BLK_E


def kernel(x, c, w_ada, b_ada, w_in, w_out, na_rpb, ret_log_decay, ret_gn_g, ln1_g, ln1_b, ln2_g, ln2_b,
           w_router, router_bias, w_gate, w_up, w_down, ws_gate, ws_up, ws_down):
    b, s, d = x.shape
    depth = w_ada.shape[0]
    alpha = (2.0 * depth) ** 0.25
    t = jnp.arange(s, dtype=F32)
    inv_freq = ROPE_BASE ** (-jnp.arange(0, RET_HEAD_DIM, 2, dtype=F32) / RET_HEAD_DIM)
    ang = t[:, None] * inv_freq[None, :]
    cos, sin = jnp.cos(ang), jnp.sin(ang)
    cos2 = jnp.concatenate([cos, cos], axis=-1)
    sin2 = jnp.concatenate([-sin, sin], axis=-1)
    for l in range(depth):
        mod = _mod(c, w_ada[l], b_ada[l])
        proj = _inproj(x, mod, w_in[l].astype(BF16), cos2, sin2)
        y_na = _natten(proj, _na_bias_table(na_rpb[l]))
        y_r = _retention(proj, ret_log_decay[l], ret_gn_g[l])
        wr_t = w_router[l].T
        wr_hi = wr_t.astype(BF16)
        wr_lo = (wr_t - wr_hi.astype(F32)).astype(BF16)
        x1, hfp, logits_t = _outproj(y_na, y_r, x, mod, w_out[l].astype(BF16), ln1_g[l], ln1_b[l],
                                     wr_hi, wr_lo, alpha)
        idx, wts, rank, counts = _route(logits_t, router_bias[l])
        n = b * s
        pstarts, nblk, gstart, n_blocks, n_slots = _slot_layout(counts, n * TOP_K)
        slot_rows = _slots(idx, rank, pstarts, n_slots)
        xs = _scatter_rows_sc(hfp.reshape(PACK_W * n, LANES), slot_rows, PACK_W * n_slots)
        ys = _experts(nblk, gstart, counts[:, 0].astype(I32), n_blocks, xs.reshape(PACK_W, n_slots, LANES),
                      w_gate[l], w_up[l], w_down[l]).reshape(PACK_W * n_slots, LANES)
        ws = (ws_gate[l].astype(BF16), ws_up[l].astype(BF16), ws_down[l].astype(BF16))
        x1n = x1.reshape(n, d)
        nc = n // COMBINE_CHUNKS
        out = None
        for ci in range(COMBINE_CHUNKS):
            yk = _gather_rows_sc(ys, slot_rows, ci * nc, nc).reshape(PACK_W, TOP_K, nc, LANES)
            out = _combine(wts, yk, x1n, mod, *ws, ln2_g[l], ln2_b[l], alpha, ci * nc, out)
        x = out.reshape(b, s, d)
    return x
```
